```python
import jax
import jax.numpy as jnp
from jax import lax
import numpy as np


D_MODEL = 1024
BATCH = 8
SEQ = 2048
DEPTH = 1
DEC_BATCH = 128
DEC_SEQ = 4
PAST_LEN = 2048
PAGE_SIZE = 128

HEAD_DIM = 64
HEADS_PER_GROUP = 4
DIL_GROUPS = ((128, 1), (512, 4), (2048, 16))
N_DGROUPS = len(DIL_GROUPS)
ATTN_WIDTH = N_DGROUPS * HEADS_PER_GROUP * HEAD_DIM
COMB_WIDTH = HEADS_PER_GROUP * HEAD_DIM
BLOCK = 128
CHUNK = 128
SG_GROUPS = 4
SG_WIDTH = 512
SG_GROUP_DIM = SG_WIDTH // SG_GROUPS
D_FF = 4 * D_MODEL
IN_WIDTH = 3 * ATTN_WIDTH + 2 * SG_WIDTH + 2 * D_MODEL
EPS = 1e-6
NEG = -1e30
SCALE = HEAD_DIM ** -0.5

kernel_name = 'hybrid_dilated_attn_gmlp_step'


def rms_norm(x, g):
    xf = x.astype(jnp.float32)
    y = xf * lax.rsqrt(jnp.mean(xf * xf, axis=-1, keepdims=True) + EPS)
    return (y * g.astype(jnp.float32)).astype(x.dtype)


def layer_norm(x, g, b):
    xf = x.astype(jnp.float32)
    mu = jnp.mean(xf, axis=-1, keepdims=True)
    var = jnp.mean(jnp.square(xf - mu), axis=-1, keepdims=True)
    y = (xf - mu) * lax.rsqrt(var + EPS)
    return (y * g.astype(jnp.float32) + b.astype(jnp.float32)).astype(x.dtype)


def project_inputs(h, w_in, b_gate):
    p = h @ w_in
    a = ATTN_WIDTH
    zs = 3 * a
    gs = zs + 2 * SG_WIDTH
    heads = h.shape[:-1] + (N_DGROUPS, HEADS_PER_GROUP, HEAD_DIM)
    q = p[..., :a].reshape(heads)
    k = p[..., a:2 * a].reshape(heads)
    v = p[..., 2 * a:zs].reshape(heads)
    z = jax.nn.gelu(p[..., zs:gs], approximate=False)
    gates = jax.nn.sigmoid((p[..., gs:] + b_gate).astype(jnp.float32)).astype(h.dtype)
    return q, k, v, z[..., :SG_WIDTH], z[..., SG_WIDTH:], gates[..., :D_MODEL], gates[..., D_MODEL:]


def banded_attention(q, k, v, reach):
    n, L, H, hd = q.shape
    nb = -(-L // BLOCK)
    lp = nb * BLOCK
    qb = jnp.pad(q, ((0, 0), (0, lp - L), (0, 0), (0, 0))).reshape(n, nb, BLOCK, H, hd)

    def key_blocks(t):
        tp = jnp.pad(t, ((0, 0), (BLOCK, lp - L), (0, 0), (0, 0))).reshape(n, nb + 1, BLOCK, H, hd)
        return jnp.concatenate([tp[:, :-1], tp[:, 1:]], axis=2)

    kb, vb = key_blocks(k), key_blocks(v)
    s = jnp.einsum('nbqhd,nbkhd->nbhqk', qb, kb).astype(jnp.float32) * SCALE
    qi = jnp.arange(BLOCK)[:, None]
    ki = jnp.arange(2 * BLOCK)[None, :]
    dist = qi + BLOCK - ki
    kpos = (jnp.arange(nb)[:, None, None] - 1) * BLOCK + ki[None]
    mask = (dist >= 0) & (dist <= reach) & (kpos >= 0)
    s = jnp.where(mask[None, :, None], s, NEG)
    lse = jax.nn.logsumexp(s, axis=-1)
    p = jnp.exp(s - lse[..., None]).astype(v.dtype)
    o = jnp.einsum('nbhqk,nbkhd->nbqhd', p, vb).reshape(n, lp, H, hd)[:, :L]
    lse = lse.transpose(0, 1, 3, 2).reshape(n, lp, H)[:, :L]
    return o, lse


def dilated_group_prompt(q, k, v, window, dil):
    b, s, H, hd = q.shape
    L = s // dil

    def to_res(t):
        return t.reshape(b, L, dil, H, hd).transpose(0, 2, 1, 3, 4).reshape(b * dil, L, H, hd)

    o, lse = banded_attention(to_res(q), to_res(k), to_res(v), window // dil)
    o = o.reshape(b, dil, L, H, hd).transpose(0, 2, 1, 3, 4).reshape(b, s, H, hd)
    lse = lse.reshape(b, dil, L, H).transpose(0, 2, 1, 3).reshape(b, s, H)
    return o, lse


def dilated_group_sample(q, k, v, buf, window, dil):
    t = q.shape[1]
    wb = buf.shape[1]
    kc = jnp.concatenate([buf[:, :, 0], k], axis=1)
    vc = jnp.concatenate([buf[:, :, 1], v], axis=1)
    n_keys = window // dil + 1
    idx = wb + jnp.arange(t)[:, None] - dil * jnp.arange(n_keys)[None, :]
    valid = idx >= 0
    idx = jnp.maximum(idx, 0)
    kg = kc[:, idx]
    vg = vc[:, idx]
    s = jnp.einsum('bthd,btjhd->bhtj', q, kg).astype(jnp.float32) * SCALE
    s = jnp.where(valid[None, None], s, NEG)
    lse = jax.nn.logsumexp(s, axis=-1)
    p = jnp.exp(s - lse[..., None]).astype(v.dtype)
    o = jnp.einsum('bhtj,btjhd->bthd', p, vg)
    return o, lse.transpose(0, 2, 1)


def merge_dilations(outs, lses):
    alpha = jax.nn.softmax(jnp.stack(lses, 0), axis=0).astype(outs[0].dtype)
    o = jnp.einsum('gbsh,gbshd->bshd', alpha, jnp.stack(outs, 0))
    return o.reshape(o.shape[:2] + (COMB_WIDTH,))


def spatial_gate_prompt(z1, z2, ln_z_g, ln_z_b, w_spatial, b_spatial):
    b, s, _ = z1.shape
    zn = layer_norm(z2, ln_z_g, ln_z_b).reshape(b, s // CHUNK, CHUNK, SG_GROUPS, SG_GROUP_DIM)
    mix = jnp.einsum('gts,bcsgk->bctgk', jnp.tril(w_spatial), zn) + b_spatial.T[:, :, None]
    return z1 * mix.reshape(b, s, SG_WIDTH)


def spatial_gate_sample(z1, z2, ln_z_g, ln_z_b, w_spatial, b_spatial):
    b, t, _ = z1.shape
    zn = layer_norm(z2, ln_z_g, ln_z_b)
    zg = zn.reshape(b, t, SG_GROUPS, SG_GROUP_DIM)
    mix = jnp.einsum('gts,bsgk->btgk', jnp.tril(w_spatial)[:, :t, :t], zg) + b_spatial[:, :t].T[:, :, None]
    return z1 * mix.reshape(b, t, SG_WIDTH), zn


def run_layer(x, bufs, norm_pre_mix, w_in, b_gate, ln_z_g, ln_z_b, w_spatial, b_spatial,
              w_ao, w_bo, w_out, norm_post_mix, norm_pre_ffn, w_up, w_down, norm_post_ffn):
    h = rms_norm(x, norm_pre_mix)
    q, k, v, z1, z2, g_a, g_b = project_inputs(h, w_in, b_gate)
    outs, lses, kv_new = [], [], []
    for gi, (win, dil) in enumerate(DIL_GROUPS):
        qg, kg, vg = q[:, :, gi], k[:, :, gi], v[:, :, gi]
        if bufs is None:
            o, lse = dilated_group_prompt(qg, kg, vg, win, dil)
            keep = min(win, x.shape[1])
            kv_new.append(jnp.stack([kg[:, -keep:], vg[:, -keep:]], axis=2))
        else:
            o, lse = dilated_group_sample(qg, kg, vg, bufs[gi], win, dil)
            kv_new.append(jnp.stack([kg, vg], axis=2))
        outs.append(o)
        lses.append(lse)
    o_a = merge_dilations(outs, lses)
    if bufs is None:
        o_b = spatial_gate_prompt(z1, z2, ln_z_g, ln_z_b, w_spatial, b_spatial)
        sg_new = None
    else:
        o_b, sg_new = spatial_gate_sample(z1, z2, ln_z_g, ln_z_b, w_spatial, b_spatial)
    merged = g_a * (o_a @ w_ao) + g_b * (o_b @ w_bo)
    x = x + rms_norm(merged @ w_out, norm_post_mix)
    h2 = rms_norm(x, norm_pre_ffn)
    f = jnp.square(jax.nn.relu(h2 @ w_up)) @ w_down
    x = x + rms_norm(f, norm_post_ffn)
    return x, kv_new, sg_new


def setup_inputs(seed: int = 0) -> dict:
    key = jax.random.key(seed)
    ks = jax.random.split(key, 24)
    f32 = jnp.float32

    def nrm(k, shape, scale):
        return jax.random.normal(k, shape, f32) * scale

    def gain(k, n):
        return 1.0 + 0.05 * jax.random.normal(k, (DEPTH, n), f32)

    def cache(k, win):
        return nrm(k, (DEPTH, DEC_BATCH, min(win, PAST_LEN), 2, HEADS_PER_GROUP, HEAD_DIM), 1.0)

    return {
        'x_prompt': nrm(ks[0], (BATCH, SEQ, D_MODEL), 1.0),
        'x_sample': nrm(ks[1], (DEC_BATCH, DEC_SEQ, D_MODEL), 1.0),
        'cache_kv_w128': cache(ks[2], DIL_GROUPS[0][0]),
        'cache_kv_w512': cache(ks[3], DIL_GROUPS[1][0]),
        'cache_kv_w2048': cache(ks[4], DIL_GROUPS[2][0]),
        'norm_pre_mix': gain(ks[5], D_MODEL),
        'w_in': nrm(ks[6], (DEPTH, D_MODEL, IN_WIDTH), D_MODEL ** -0.5),
        'b_gate': nrm(ks[7], (DEPTH, 2 * D_MODEL), 0.02),
        'ln_z_g': gain(ks[8], SG_WIDTH),
        'ln_z_b': nrm(ks[9], (DEPTH, SG_WIDTH), 0.02),
        'w_spatial': nrm(ks[10], (DEPTH, SG_GROUPS, CHUNK, CHUNK), CHUNK ** -0.5),
        'b_spatial': 1.0 + nrm(ks[11], (DEPTH, SG_GROUPS, CHUNK), 0.1),
        'w_ao': nrm(ks[12], (DEPTH, COMB_WIDTH, D_MODEL), COMB_WIDTH ** -0.5),
        'w_bo': nrm(ks[13], (DEPTH, SG_WIDTH, D_MODEL), SG_WIDTH ** -0.5),
        'w_out': nrm(ks[14], (DEPTH, D_MODEL, D_MODEL), D_MODEL ** -0.5),
        'norm_post_mix': gain(ks[15], D_MODEL),
        'norm_pre_ffn': gain(ks[16], D_MODEL),
        'w_up': nrm(ks[17], (DEPTH, D_MODEL, D_FF), D_MODEL ** -0.5),
        'w_down': nrm(ks[18], (DEPTH, D_FF, D_MODEL), D_FF ** -0.5),
        'norm_post_ffn': gain(ks[19], D_MODEL),
    }


def reference(x_prompt, x_sample, cache_kv_w128, cache_kv_w512, cache_kv_w2048,
              norm_pre_mix, w_in, b_gate, ln_z_g, ln_z_b, w_spatial, b_spatial,
              w_ao, w_bo, w_out, norm_post_mix, norm_pre_ffn, w_up, w_down, norm_post_ffn):
    y_p, y_s = x_prompt, x_sample
    kv_p = [[] for _ in DIL_GROUPS]
    kv_s = [[] for _ in DIL_GROUPS]
    sg_s = []
    for l in range(DEPTH):
        weights = (norm_pre_mix[l], w_in[l], b_gate[l], ln_z_g[l], ln_z_b[l], w_spatial[l],
                   b_spatial[l], w_ao[l], w_bo[l], w_out[l], norm_post_mix[l], norm_pre_ffn[l],
                   w_up[l], w_down[l], norm_post_ffn[l])
        y_p, kvp, _ = run_layer(y_p, None, *weights)
        bufs = (cache_kv_w128[l], cache_kv_w512[l], cache_kv_w2048[l])
        y_s, kvs, sgn = run_layer(y_s, bufs, *weights)
        for gi in range(N_DGROUPS):
            kv_p[gi].append(kvp[gi])
            kv_s[gi].append(kvs[gi])
        sg_s.append(sgn)
    return (y_p, y_s,
            jnp.stack(kv_p[0]), jnp.stack(kv_p[1]), jnp.stack(kv_p[2]),
            jnp.stack(kv_s[0]), jnp.stack(kv_s[1]), jnp.stack(kv_s[2]),
            jnp.stack(sg_s))
```

```python
import functools
import math

import jax
import jax.numpy as jnp
from jax import lax
from jax.experimental import pallas as pl
from jax.experimental.pallas import tpu as pltpu

F32 = jnp.float32
BF16 = jnp.bfloat16

D_MODEL = 1024
HEAD_DIM = 64
HEADS = 4
GROUP_W = HEADS * HEAD_DIM
DIL_GROUPS = ((128, 1), (512, 4), (2048, 16))
N_GROUPS = len(DIL_GROUPS)
ATTN_W = N_GROUPS * GROUP_W
BLOCK = 128
CHUNK = 128
SG_GROUPS = 4
SG_W = 512
D_FF = 4 * D_MODEL
Z_OFF = 3 * ATTN_W
G_OFF = Z_OFF + 2 * SG_W
IN_W = G_OFF + 2 * D_MODEL
EPS = 1e-6
NEG = -1e30
SCALE = HEAD_DIM ** -0.5
INV_SQRT2 = 1.0 / math.sqrt(2.0)
LANES = 128
SLABS = GROUP_W // LANES
HEADS_PER_SLAB = LANES // HEAD_DIM
HEAD_SHIFT = HEAD_DIM.bit_length() - 1
VMEM_LIMIT = 56 * 1024 * 1024

TOKEN_TILE = 512
SAMPLE_DB_TILE = 8
T_PAD = 8
T_SHIFT = T_PAD.bit_length() - 1


def _params(n_axes):
    return pltpu.CompilerParams(dimension_semantics=("arbitrary",) * n_axes,
                                vmem_limit_bytes=VMEM_LIMIT)


def _const_spec(shape):
    return pl.BlockSpec(shape, lambda *_: (0,) * len(shape), pipeline_mode=pl.Buffered(1))


def _rms(x, g):
    return x * lax.rsqrt(jnp.mean(x * x, axis=-1, keepdims=True) + EPS) * g


def _proj_kernel(x_ref, g_ref, w_ref, bg_ref, lng_ref, lnb_ref,
                 qkv_ref, z1_ref, zn_ref, ga_ref, gb_ref):
    h = _rms(x_ref[...], g_ref[...]).astype(BF16)
    qkv_ref[...] = jnp.dot(h, w_ref[:, :Z_OFF], preferred_element_type=F32)

    z = jnp.dot(h, w_ref[:, Z_OFF:G_OFF], preferred_element_type=F32)
    z = 0.5 * z * (1.0 + lax.erf(z * INV_SQRT2))
    z1_ref[...] = z[:, :SG_W]
    z2 = z[:, SG_W:]
    mu = jnp.mean(z2, axis=-1, keepdims=True)
    zc = z2 - mu
    var = jnp.mean(zc * zc, axis=-1, keepdims=True)
    zn_ref[...] = zc * lax.rsqrt(var + EPS) * lng_ref[...] + lnb_ref[...]

    gates = jnp.dot(h, w_ref[:, G_OFF:], preferred_element_type=F32) + bg_ref[...]
    gates = 1.0 / (1.0 + jnp.exp(-gates))
    ga_ref[...] = gates[:, :D_MODEL].astype(BF16)
    gb_ref[...] = gates[:, D_MODEL:].astype(BF16)


def _project(x, g, w_bf, bg, lng, lnb):
    n = x.shape[0]
    tm = min(TOKEN_TILE, n)
    row = lambda w: pl.BlockSpec((tm, w), lambda i: (i, 0))
    return pl.pallas_call(
        _proj_kernel,
        grid=(n // tm,),
        in_specs=[row(D_MODEL), _const_spec((1, D_MODEL)), _const_spec((D_MODEL, IN_W)),
                  _const_spec((1, 2 * D_MODEL)), _const_spec((1, SG_W)), _const_spec((1, SG_W))],
        out_specs=[row(Z_OFF), row(SG_W), row(SG_W), row(D_MODEL), row(D_MODEL)],
        out_shape=[jax.ShapeDtypeStruct((n, Z_OFF), F32),
                   jax.ShapeDtypeStruct((n, SG_W), F32),
                   jax.ShapeDtypeStruct((n, SG_W), F32),
                   jax.ShapeDtypeStruct((n, D_MODEL), BF16),
                   jax.ShapeDtypeStruct((n, D_MODEL), BF16)],
        compiler_params=_params(1),
        name="proj",
    )(x, g, w_bf, bg, lng, lnb)


def _head_lane_mask(shape, hh):
    lane = lax.broadcasted_iota(jnp.int32, shape, len(shape) - 1)
    return (lane >> HEAD_SHIFT) == hh


def _band_attn_kernel(q_ref, kp_ref, kc_ref, vp_ref, vc_ref, o_ref, lse_ref):
    first_key = jnp.where(pl.program_id(1) > 0, 0, BLOCK)
    qi = lax.broadcasted_iota(jnp.int32, (BLOCK, 2 * BLOCK), 0)
    ki = lax.broadcasted_iota(jnp.int32, (BLOCK, 2 * BLOCK), 1)
    mask = (ki >= qi) & (ki <= qi + BLOCK) & (ki >= first_key)
    for s in range(SLABS):
        sl = slice(s * LANES, (s + 1) * LANES)
        q = q_ref[0, :, sl] * SCALE
        k2 = jnp.concatenate([kp_ref[0, :, sl], kc_ref[0, :, sl]], axis=0)
        v2 = jnp.concatenate([vp_ref[0, :, sl], vc_ref[0, :, sl]], axis=0)
        o_slab = jnp.zeros((BLOCK, LANES), F32)
        lse_slab = jnp.zeros((BLOCK, LANES), F32)
        for hh in range(HEADS_PER_SLAB):
            lm_q = _head_lane_mask((BLOCK, LANES), hh)
            lm_v = _head_lane_mask((2 * BLOCK, LANES), hh)
            qm = jnp.where(lm_q, q, jnp.zeros_like(q))
            sc = lax.dot_general(qm, k2, (((1,), (1,)), ((), ())), preferred_element_type=F32)
            sc = jnp.where(mask, sc, NEG)
            m = jnp.max(sc, axis=-1, keepdims=True)
            p = jnp.exp(sc - m)
            l = jnp.sum(p, axis=-1, keepdims=True)
            vm = jnp.where(lm_v, v2, jnp.zeros_like(v2))
            o = jnp.dot(p.astype(BF16), vm, preferred_element_type=F32)
            o_slab = o_slab + o / l
            lse_slab = jnp.where(lm_q, m + jnp.log(l), lse_slab)
        o_ref[0, :, sl] = o_slab
        lse_ref[0, :, sl] = lse_slab


def _band_attention(q, k, v):
    n_seq, L, _ = q.shape
    cur = pl.BlockSpec((1, BLOCK, GROUP_W), lambda n, i: (n, i, 0))
    prev = pl.BlockSpec((1, BLOCK, GROUP_W), lambda n, i: (n, jnp.maximum(i - 1, 0), 0))
    return pl.pallas_call(
        _band_attn_kernel,
        grid=(n_seq, L // BLOCK),
        in_specs=[cur, prev, cur, prev, cur],
        out_specs=[cur, cur],
        out_shape=[jax.ShapeDtypeStruct((n_seq, L, GROUP_W), F32)] * 2,
        compiler_params=_params(2),
        name="band_attn",
    )(q, k, k, v, v)


def _sample_attn_kernel(q_ref, kn_ref, vn_ref, k1_ref, v1_ref, k2_ref, v2_ref, k3_ref, v3_ref, o_ref):
    rows = HEADS * T_PAD
    row = lax.broadcasted_iota(jnp.int32, (rows, GROUP_W), 0)
    lane = lax.broadcasted_iota(jnp.int32, (rows, GROUP_W), 1)
    head_sel = (row >> T_SHIFT) == (lane >> HEAD_SHIFT)

    def key_masks(width, every):
        t = lax.broadcasted_iota(jnp.int32, (rows, width), 0) & (T_PAD - 1)
        i = lax.broadcasted_iota(jnp.int32, (rows, width), 1)
        tn = lax.broadcasted_iota(jnp.int32, (rows, T_PAD), 0) & (T_PAD - 1)
        j = lax.broadcasted_iota(jnp.int32, (rows, T_PAD), 1)
        if every == 1:
            return i >= t, j <= tn
        return (i & (every - 1)) == t, j == tn

    caches = ((k1_ref, v1_ref, 1), (k2_ref, v2_ref, 4), (k3_ref, v3_ref, 4))
    masks = [key_masks(c[0].shape[1], c[2]) for c in caches]

    def body(d, carry):
        outs, lses = [], []
        for g, (kc_ref, vc_ref, _) in enumerate(caches):
            gs = slice(g * GROUP_W, (g + 1) * GROUP_W)
            q8 = q_ref[d, :, gs] * SCALE
            qm = jnp.where(head_sel, jnp.concatenate([q8] * HEADS, axis=0), 0.0).astype(BF16)
            kn = kn_ref[d, :, gs].astype(BF16)
            vn = vn_ref[d, :, gs].astype(BF16)
            dn = (((1,), (1,)), ((), ()))
            sc = lax.dot_general(qm, kc_ref[d], dn, preferred_element_type=F32)
            sn = lax.dot_general(qm, kn, dn, preferred_element_type=F32)
            mc, mn = masks[g]
            sc = jnp.where(mc, sc, NEG)
            sn = jnp.where(mn, sn, NEG)
            m = jnp.maximum(jnp.max(sc, axis=-1, keepdims=True), jnp.max(sn, axis=-1, keepdims=True))
            pc = jnp.exp(sc - m)
            pn = jnp.exp(sn - m)
            l = jnp.sum(pc, axis=-1, keepdims=True) + jnp.sum(pn, axis=-1, keepdims=True)
            o = (jnp.dot(pc.astype(BF16), vc_ref[d], preferred_element_type=F32)
                 + jnp.dot(pn.astype(BF16), vn, preferred_element_type=F32))
            outs.append(o / l)
            lses.append(m + jnp.log(l))
        top = jnp.maximum(jnp.maximum(lses[0], lses[1]), lses[2])
        w = [jnp.exp(x - top) for x in lses]
        merged = (w[0] * outs[0] + w[1] * outs[1] + w[2] * outs[2]) / (w[0] + w[1] + w[2])
        merged = jnp.where(head_sel, merged, 0.0)
        o8 = merged[0:T_PAD]
        for h in range(1, HEADS):
            o8 = o8 + merged[h * T_PAD:(h + 1) * T_PAD]
        o_ref[d] = o8
        return carry

    lax.fori_loop(0, o_ref.shape[0], body, 0)


def _sample_attention(q8, kn8, vn8, caches):
    db = q8.shape[0]
    tile = SAMPLE_DB_TILE
    blk = lambda a: pl.BlockSpec((tile,) + a.shape[1:], lambda i: (i, 0, 0))
    flat = [a for kv in caches for a in kv]
    return pl.pallas_call(
        _sample_attn_kernel,
        grid=(db // tile,),
        in_specs=[blk(q8), blk(kn8), blk(vn8)] + [blk(a) for a in flat],
        out_specs=pl.BlockSpec((tile, T_PAD, GROUP_W), lambda i: (i, 0, 0)),
        out_shape=jax.ShapeDtypeStruct((db, T_PAD, GROUP_W), F32),
        compiler_params=_params(1),
        name="sample_attn",
    )(q8, kn8, vn8, *flat)


def _mix_tail(o_a, o_b, ga_ref, gb_ref, x_ref, wao_ref, wbo_ref, wout_ref, gpost_ref, x1_ref):
    a = jnp.dot(o_a.astype(BF16), wao_ref[...], preferred_element_type=F32)
    b = jnp.dot(o_b.astype(BF16), wbo_ref[...], preferred_element_type=F32)
    merged = ga_ref[...].astype(F32) * a + gb_ref[...].astype(F32) * b
    t = jnp.dot(merged.astype(BF16), wout_ref[...], preferred_element_type=F32)
    x1_ref[...] = x_ref[...] + _rms(t, gpost_ref[...])


def _mix_prompt_kernel(o1_ref, o2_ref, o3_ref, l1_ref, l2_ref, l3_ref, z1_ref, zn_ref, ga_ref, gb_ref, x_ref,
                       wsp_ref, bsp_ref, wao_ref, wbo_ref, wout_ref, gpost_ref, x1_ref):
    l1, l2, l3 = l1_ref[...], l2_ref[...], l3_ref[...]
    top = jnp.maximum(jnp.maximum(l1, l2), l3)
    e1, e2, e3 = jnp.exp(l1 - top), jnp.exp(l2 - top), jnp.exp(l3 - top)
    o_a = (e1 * o1_ref[...] + e2 * o2_ref[...] + e3 * o3_ref[...]) / (e1 + e2 + e3)

    r = lax.broadcasted_iota(jnp.int32, (CHUNK, CHUNK), 0)
    c = lax.broadcasted_iota(jnp.int32, (CHUNK, CHUNK), 1)
    tm = z1_ref.shape[0]
    chunks = []
    for ci in range(tm // CHUNK):
        rows = slice(ci * CHUNK, (ci + 1) * CHUNK)
        cols = []
        for g in range(SG_GROUPS):
            gl = slice(g * CHUNK, (g + 1) * CHUNK)
            wt = jnp.where(r >= c, wsp_ref[g], 0.0).astype(BF16)
            cols.append(jnp.dot(wt, zn_ref[rows, gl].astype(BF16), preferred_element_type=F32))
        chunks.append(jnp.concatenate(cols, axis=1) + bsp_ref[...])
    o_b = z1_ref[...] * jnp.concatenate(chunks, axis=0)
    _mix_tail(o_a, o_b, ga_ref, gb_ref, x_ref, wao_ref, wbo_ref, wout_ref, gpost_ref, x1_ref)


def _mix_sample_kernel(oa_ref, z1_ref, zn_ref, ga_ref, gb_ref, x_ref, coef_ref, bsp_ref,
                       wao_ref, wbo_ref, wout_ref, gpost_ref, x1_ref):
    zn = zn_ref[...]
    mix = coef_ref[0] * zn
    for d in range(1, coef_ref.shape[0]):
        mix = mix + coef_ref[d] * pltpu.roll(zn, d, axis=0)
    o_b = z1_ref[...] * (mix + bsp_ref[...])
    _mix_tail(oa_ref[...], o_b, ga_ref, gb_ref, x_ref, wao_ref, wbo_ref, wout_ref, gpost_ref, x1_ref)


def _mix_weight_specs():
    return [_const_spec((GROUP_W, D_MODEL)), _const_spec((SG_W, D_MODEL)), _const_spec((D_MODEL, D_MODEL)),
            _const_spec((1, D_MODEL))]


def _mix_prompt(os_, ls_, z1, zn, ga, gb, x, wsp, bsp, wao, wbo, wout, gpost):
    n = x.shape[0]
    tm = TOKEN_TILE
    row = lambda w: pl.BlockSpec((tm, w), lambda i: (i, 0))
    return pl.pallas_call(
        _mix_prompt_kernel,
        grid=(n // tm,),
        in_specs=[row(GROUP_W)] * 6 + [row(SG_W), row(SG_W), row(D_MODEL), row(D_MODEL), row(D_MODEL),
                                       _const_spec((SG_GROUPS, CHUNK, CHUNK)), _const_spec((CHUNK, SG_W))]
        + _mix_weight_specs(),
        out_specs=row(D_MODEL),
        out_shape=jax.ShapeDtypeStruct((n, D_MODEL), F32),
        compiler_params=_params(1),
        name="mix_prompt",
    )(*os_, *ls_, z1, zn, ga, gb, x, wsp, bsp, wao, wbo, wout, gpost)


def _mix_sample(o_a, z1, zn, ga, gb, x, coef, bsp, wao, wbo, wout, gpost):
    n = x.shape[0]
    full = lambda a: _const_spec(a.shape)
    return pl.pallas_call(
        _mix_sample_kernel,
        grid=(1,),
        in_specs=[full(o_a), full(z1), full(zn), full(ga), full(gb), full(x), full(coef), full(bsp)]
        + _mix_weight_specs(),
        out_specs=_const_spec((n, D_MODEL)),
        out_shape=jax.ShapeDtypeStruct((n, D_MODEL), F32),
        compiler_params=_params(1),
        name="mix_sample",
    )(o_a, z1, zn, ga, gb, x, coef, bsp, wao, wbo, wout, gpost)


FF_CHUNK = 1024


def _ffn_kernel(x_ref, gpre_ref, wup_ref, wdn_ref, gpost_ref, y_ref):
    x = x_ref[...]
    h = _rms(x, gpre_ref[...]).astype(BF16)
    f = jnp.zeros(x.shape, F32)
    for c in range(D_FF // FF_CHUNK):
        cs = slice(c * FF_CHUNK, (c + 1) * FF_CHUNK)
        u = jnp.maximum(jnp.dot(h, wup_ref[:, cs], preferred_element_type=F32), 0.0)
        f = f + jnp.dot((u * u).astype(BF16), wdn_ref[cs, :], preferred_element_type=F32)
    y_ref[...] = x + _rms(f, gpost_ref[...])


def _ffn(x, gpre, wup, wdn, gpost):
    n = x.shape[0]
    tm = min(TOKEN_TILE, n)
    row = pl.BlockSpec((tm, D_MODEL), lambda i: (i, 0))
    return pl.pallas_call(
        _ffn_kernel,
        grid=(n // tm,),
        in_specs=[row, _const_spec((1, D_MODEL)), _const_spec((D_MODEL, D_FF)), _const_spec((D_FF, D_MODEL)),
                  _const_spec((1, D_MODEL))],
        out_specs=row,
        out_shape=jax.ShapeDtypeStruct((n, D_MODEL), F32),
        compiler_params=_params(1),
        name="ffn",
    )(x, gpre, wup, wdn, gpost)


def _to_residues(a, batch, seq, dil):
    L = seq // dil
    return a.reshape(batch, L, dil, GROUP_W).transpose(0, 2, 1, 3).reshape(batch * dil, L, GROUP_W)


def _from_residues(a, batch, seq, dil):
    L = seq // dil
    return a.reshape(batch, dil, L, GROUP_W).transpose(0, 2, 1, 3).reshape(batch * seq, GROUP_W)


def kernel(x_prompt, x_sample, cache_kv_w128, cache_kv_w512, cache_kv_w2048, norm_pre_mix, w_in, b_gate,
           ln_z_g, ln_z_b, w_spatial, b_spatial, w_ao, w_bo, w_out, norm_post_mix, norm_pre_ffn, w_up, w_down,
           norm_post_ffn):
    assert w_in.shape[0] == 1, "single-layer problem"
    batch, seq, _ = x_prompt.shape
    dbatch, t_new, _ = x_sample.shape
    caches_in = (cache_kv_w128[0], cache_kv_w512[0], cache_kv_w2048[0])

    w_in_bf = w_in[0].astype(BF16)
    wao, wbo, wout = w_ao[0].astype(BF16), w_bo[0].astype(BF16), w_out[0].astype(BF16)
    wup, wdn = w_up[0].astype(BF16), w_down[0].astype(BF16)
    wsp = w_spatial[0]
    bsp_rows = jnp.repeat(b_spatial[0].T, CHUNK, axis=1)

    xp = x_prompt.reshape(batch * seq, D_MODEL)
    qkv, z1, zn, ga, gb = _project(xp, norm_pre_mix, w_in_bf, b_gate, ln_z_g, ln_z_b)
    q_all, k_all, v_all = qkv[:, :ATTN_W], qkv[:, ATTN_W:2 * ATTN_W], qkv[:, 2 * ATTN_W:]
    outs, lses, kv_prompt = [], [], []
    for gi, (win, dil) in enumerate(DIL_GROUPS):
        gs = slice(gi * GROUP_W, (gi + 1) * GROUP_W)
        qg, kg, vg = q_all[:, gs], k_all[:, gs], v_all[:, gs]
        o, lse = _band_attention(*(_to_residues(a.astype(BF16), batch, seq, dil) for a in (qg, kg, vg)))
        outs.append(_from_residues(o, batch, seq, dil))
        lses.append(_from_residues(lse, batch, seq, dil))
        keep = min(win, seq)
        k5 = kg.reshape(batch, seq, HEADS, HEAD_DIM)[:, -keep:]
        v5 = vg.reshape(batch, seq, HEADS, HEAD_DIM)[:, -keep:]
        kv_prompt.append(jnp.stack([k5, v5], axis=2)[None])
    x1 = _mix_prompt(outs, lses, z1, zn, ga, gb, xp, wsp, bsp_rows, wao, wbo, wout, norm_post_mix)
    y_prompt = _ffn(x1, norm_pre_ffn, wup, wdn, norm_post_ffn).reshape(batch, seq, D_MODEL)

    ns = dbatch * t_new
    xs = x_sample.reshape(ns, D_MODEL)
    qkv_s, z1_s, zn_s, ga_s, gb_s = _project(xs, norm_pre_mix, w_in_bf, b_gate, ln_z_g, ln_z_b)
    pad = lambda a: jnp.pad(a.reshape(dbatch, t_new, ATTN_W), ((0, 0), (0, T_PAD - t_new), (0, 0)))
    q8, kn8, vn8 = pad(qkv_s[:, :ATTN_W]), pad(qkv_s[:, ATTN_W:2 * ATTN_W]), pad(qkv_s[:, 2 * ATTN_W:])
    caches, kv_sample = [], []
    for gi, (win, dil) in enumerate(DIL_GROUPS):
        buf = caches_in[gi]
        wb = buf.shape[1]
        assert wb == win and t_new == 4, "full window buffers and four new rows"
        rows = buf.reshape(dbatch, wb, 2, GROUP_W)
        if dil > t_new:
            rows = rows.reshape(dbatch, wb // dil, dil, 2, GROUP_W)[:, :, :t_new].reshape(dbatch, -1, 2, GROUP_W)
        caches.append((rows[:, :, 0].astype(BF16), rows[:, :, 1].astype(BF16)))
        gs = slice(gi * GROUP_W, (gi + 1) * GROUP_W)
        kn = qkv_s[:, ATTN_W:2 * ATTN_W][:, gs].reshape(dbatch, t_new, HEADS, HEAD_DIM)
        vn = qkv_s[:, 2 * ATTN_W:][:, gs].reshape(dbatch, t_new, HEADS, HEAD_DIM)
        kv_sample.append(jnp.stack([kn, vn], axis=2)[None])
    o_a = _sample_attention(q8, kn8, vn8, caches)[:, :t_new].reshape(ns, GROUP_W)

    wt = jnp.tril(wsp)[:, :t_new, :t_new]
    coef = jnp.stack([jnp.where(jnp.arange(t_new)[None, :] >= d,
                                wt[:, jnp.arange(t_new), jnp.maximum(jnp.arange(t_new) - d, 0)], 0.0)
                      for d in range(t_new)])
    coef = jnp.repeat(coef.transpose(0, 2, 1), CHUNK, axis=2)
    coef = jnp.tile(coef, (1, dbatch, 1))
    bsp_s = jnp.tile(bsp_rows[:t_new], (dbatch, 1))
    x1_s = _mix_sample(o_a, z1_s, zn_s, ga_s, gb_s, xs, coef, bsp_s, wao, wbo, wout, norm_post_mix)
    y_sample = _ffn(x1_s, norm_pre_ffn, wup, wdn, norm_post_ffn).reshape(dbatch, t_new, D_MODEL)
    sg_sample = zn_s.reshape(1, dbatch, t_new, SG_W)

    return (y_prompt, y_sample, kv_prompt[0], kv_prompt[1], kv_prompt[2],
            kv_sample[0], kv_sample[1], kv_sample[2], sg_sample)
```

```python
import functools
import math

import jax
import jax.numpy as jnp
from jax import lax
from jax.experimental import pallas as pl
from jax.experimental.pallas import tpu as pltpu

F32 = jnp.float32
BF16 = jnp.bfloat16

D_MODEL = 1024
HEAD_DIM = 64
HEADS = 4
GROUP_W = HEADS * HEAD_DIM
DIL_GROUPS = ((128, 1), (512, 4), (2048, 16))
N_GROUPS = len(DIL_GROUPS)
ATTN_W = N_GROUPS * GROUP_W
BLOCK = 128
CHUNK = 128
SG_GROUPS = 4
SG_W = 512
D_FF = 4 * D_MODEL
Z_OFF = 3 * ATTN_W
G_OFF = Z_OFF + 2 * SG_W
IN_W = G_OFF + 2 * D_MODEL
EPS = 1e-6
NEG = -1e30
SCALE = HEAD_DIM ** -0.5
INV_SQRT2 = 1.0 / math.sqrt(2.0)
LANES = 128
SLABS = GROUP_W // LANES
HEADS_PER_SLAB = LANES // HEAD_DIM
HEAD_SHIFT = HEAD_DIM.bit_length() - 1
VMEM_LIMIT = 56 * 1024 * 1024

TOKEN_TILE = 512
SAMPLE_DB_TILE = 2
T_PAD = 8
T_SHIFT = T_PAD.bit_length() - 1


def _params(n_axes):
    return pltpu.CompilerParams(dimension_semantics=("arbitrary",) * n_axes,
                                vmem_limit_bytes=VMEM_LIMIT)


def _const_spec(shape):
    return pl.BlockSpec(shape, lambda *_: (0,) * len(shape), pipeline_mode=pl.Buffered(1))


def _rms(x, g):
    return x * lax.rsqrt(jnp.mean(x * x, axis=-1, keepdims=True) + EPS) * g


def _proj_kernel(x_ref, g_ref, w_ref, bg_ref, lng_ref, lnb_ref,
                 qkv_ref, z1_ref, zn_ref, ga_ref, gb_ref):
    h = _rms(x_ref[...], g_ref[...]).astype(BF16)
    qkv_ref[...] = jnp.dot(h, w_ref[:, :Z_OFF], preferred_element_type=F32)

    z = jnp.dot(h, w_ref[:, Z_OFF:G_OFF], preferred_element_type=F32)
    z = 0.5 * z * (1.0 + lax.erf(z * INV_SQRT2))
    z1_ref[...] = z[:, :SG_W]
    z2 = z[:, SG_W:]
    mu = jnp.mean(z2, axis=-1, keepdims=True)
    zc = z2 - mu
    var = jnp.mean(zc * zc, axis=-1, keepdims=True)
    zn_ref[...] = zc * lax.rsqrt(var + EPS) * lng_ref[...] + lnb_ref[...]

    gates = jnp.dot(h, w_ref[:, G_OFF:], preferred_element_type=F32) + bg_ref[...]
    gates = 1.0 / (1.0 + jnp.exp(-gates))
    ga_ref[...] = gates[:, :D_MODEL].astype(BF16)
    gb_ref[...] = gates[:, D_MODEL:].astype(BF16)


def _project(x, g, w_bf, bg, lng, lnb):
    n = x.shape[0]
    tm = min(TOKEN_TILE, n)
    row = lambda w: pl.BlockSpec((tm, w), lambda i: (i, 0))
    return pl.pallas_call(
        _proj_kernel,
        grid=(n // tm,),
        in_specs=[row(D_MODEL), _const_spec((1, D_MODEL)), _const_spec((D_MODEL, IN_W)),
                  _const_spec((1, 2 * D_MODEL)), _const_spec((1, SG_W)), _const_spec((1, SG_W))],
        out_specs=[row(Z_OFF), row(SG_W), row(SG_W), row(D_MODEL), row(D_MODEL)],
        out_shape=[jax.ShapeDtypeStruct((n, Z_OFF), F32),
                   jax.ShapeDtypeStruct((n, SG_W), F32),
                   jax.ShapeDtypeStruct((n, SG_W), F32),
                   jax.ShapeDtypeStruct((n, D_MODEL), BF16),
                   jax.ShapeDtypeStruct((n, D_MODEL), BF16)],
        compiler_params=_params(1),
        name="proj",
    )(x, g, w_bf, bg, lng, lnb)


def _head_lane_mask(shape, hh):
    lane = lax.broadcasted_iota(jnp.int32, shape, len(shape) - 1)
    return (lane >> HEAD_SHIFT) == hh


def _band_attn_kernel(q_ref, kp_ref, kc_ref, vp_ref, vc_ref, o_ref, lse_ref):
    first_key = jnp.where(pl.program_id(1) > 0, 0, BLOCK)
    qi = lax.broadcasted_iota(jnp.int32, (BLOCK, 2 * BLOCK), 0)
    ki = lax.broadcasted_iota(jnp.int32, (BLOCK, 2 * BLOCK), 1)
    mask = (ki >= qi) & (ki <= qi + BLOCK) & (ki >= first_key)
    for s in range(SLABS):
        sl = slice(s * LANES, (s + 1) * LANES)
        q = q_ref[0, :, sl] * SCALE
        k2 = jnp.concatenate([kp_ref[0, :, sl], kc_ref[0, :, sl]], axis=0)
        v2 = jnp.concatenate([vp_ref[0, :, sl], vc_ref[0, :, sl]], axis=0)
        o_slab = jnp.zeros((BLOCK, LANES), F32)
        lse_slab = jnp.zeros((BLOCK, LANES), F32)
        for hh in range(HEADS_PER_SLAB):
            lm_q = _head_lane_mask((BLOCK, LANES), hh)
            lm_v = _head_lane_mask((2 * BLOCK, LANES), hh)
            qm = jnp.where(lm_q, q, jnp.zeros_like(q))
            sc = lax.dot_general(qm, k2, (((1,), (1,)), ((), ())), preferred_element_type=F32)
            sc = jnp.where(mask, sc, NEG)
            m = jnp.max(sc, axis=-1, keepdims=True)
            p = jnp.exp(sc - m)
            l = jnp.sum(p, axis=-1, keepdims=True)
            vm = jnp.where(lm_v, v2, jnp.zeros_like(v2))
            o = jnp.dot(p.astype(BF16), vm, preferred_element_type=F32)
            o_slab = o_slab + o / l
            lse_slab = jnp.where(lm_q, m + jnp.log(l), lse_slab)
        o_ref[0, :, sl] = o_slab
        lse_ref[0, :, sl] = lse_slab


def _band_attention(q, k, v):
    n_seq, L, _ = q.shape
    cur = pl.BlockSpec((1, BLOCK, GROUP_W), lambda n, i: (n, i, 0))
    prev = pl.BlockSpec((1, BLOCK, GROUP_W), lambda n, i: (n, jnp.maximum(i - 1, 0), 0))
    return pl.pallas_call(
        _band_attn_kernel,
        grid=(n_seq, L // BLOCK),
        in_specs=[cur, prev, cur, prev, cur],
        out_specs=[cur, cur],
        out_shape=[jax.ShapeDtypeStruct((n_seq, L, GROUP_W), F32)] * 2,
        compiler_params=_params(2),
        name="band_attn",
    )(q, k, k, v, v)


def _sample_attn_kernel(q_ref, kn_ref, vn_ref, c1_ref, c2_ref, c3_ref, o_ref):
    rows = HEADS * T_PAD
    row = lax.broadcasted_iota(jnp.int32, (rows, GROUP_W), 0)
    lane = lax.broadcasted_iota(jnp.int32, (rows, GROUP_W), 1)
    head_sel = (row >> T_SHIFT) == (lane >> HEAD_SHIFT)

    def key_masks(width, dil):
        t = lax.broadcasted_iota(jnp.int32, (rows, width), 0) & (T_PAD - 1)
        i = lax.broadcasted_iota(jnp.int32, (rows, width), 1)
        tn = lax.broadcasted_iota(jnp.int32, (rows, T_PAD), 0) & (T_PAD - 1)
        j = lax.broadcasted_iota(jnp.int32, (rows, T_PAD), 1)
        return (((i - t) & (dil - 1)) == 0) & (i >= t), (((tn - j) & (dil - 1)) == 0) & (j <= tn)

    caches = tuple(zip((c1_ref, c2_ref, c3_ref), (dil for _, dil in DIL_GROUPS)))
    masks = [key_masks(c.shape[2], dil) for c, dil in caches]
    nt = (((1,), (1,)), ((), ()))

    for d in range(o_ref.shape[0]):
        outs, lses = [], []
        for g, (c_ref, _) in enumerate(caches):
            gs = slice(g * GROUP_W, (g + 1) * GROUP_W)
            q8 = q_ref[d, :, gs] * SCALE
            qm = jnp.where(head_sel, jnp.concatenate([q8] * HEADS, axis=0), 0.0).astype(BF16)
            kn = kn_ref[d, :, gs].astype(BF16)
            vn = vn_ref[d, :, gs].astype(BF16)
            kt = c_ref[d, :GROUP_W, :].astype(BF16)
            vt = c_ref[d, GROUP_W:, :].astype(BF16)
            sc = jnp.dot(qm, kt, preferred_element_type=F32)
            sn = lax.dot_general(qm, kn, nt, preferred_element_type=F32)
            mc, mn = masks[g]
            sc = jnp.where(mc, sc, NEG)
            sn = jnp.where(mn, sn, NEG)
            m = jnp.maximum(jnp.max(sc, axis=-1, keepdims=True), jnp.max(sn, axis=-1, keepdims=True))
            pc = jnp.exp(sc - m)
            pn = jnp.exp(sn - m)
            l = jnp.sum(pc, axis=-1, keepdims=True) + jnp.sum(pn, axis=-1, keepdims=True)
            o = (lax.dot_general(pc.astype(BF16), vt, nt, preferred_element_type=F32)
                 + jnp.dot(pn.astype(BF16), vn, preferred_element_type=F32))
            outs.append(o / l)
            lses.append(m + jnp.log(l))
        top = jnp.maximum(jnp.maximum(lses[0], lses[1]), lses[2])
        w = [jnp.exp(x - top) for x in lses]
        merged = (w[0] * outs[0] + w[1] * outs[1] + w[2] * outs[2]) / (w[0] + w[1] + w[2])
        merged = jnp.where(head_sel, merged, 0.0)
        o8 = merged[0:T_PAD]
        for h in range(1, HEADS):
            o8 = o8 + merged[h * T_PAD:(h + 1) * T_PAD]
        o_ref[d] = o8


def _sample_attention(q8, kn8, vn8, caches):
    db = q8.shape[0]
    tile = SAMPLE_DB_TILE
    blk = lambda a: pl.BlockSpec((tile,) + a.shape[1:], lambda i: (i, 0, 0))
    return pl.pallas_call(
        _sample_attn_kernel,
        grid=(db // tile,),
        in_specs=[blk(q8), blk(kn8), blk(vn8)] + [blk(a) for a in caches],
        out_specs=pl.BlockSpec((tile, T_PAD, GROUP_W), lambda i: (i, 0, 0)),
        out_shape=jax.ShapeDtypeStruct((db, T_PAD, GROUP_W), F32),
        compiler_params=_params(1),
        name="sample_attn",
    )(q8, kn8, vn8, *caches)


def _mix_tail(o_a, o_b, ga_ref, gb_ref, x_ref, wao_ref, wbo_ref, wout_ref, gpost_ref, x1_ref):
    a = jnp.dot(o_a.astype(BF16), wao_ref[...], preferred_element_type=F32)
    b = jnp.dot(o_b.astype(BF16), wbo_ref[...], preferred_element_type=F32)
    merged = ga_ref[...].astype(F32) * a + gb_ref[...].astype(F32) * b
    t = jnp.dot(merged.astype(BF16), wout_ref[...], preferred_element_type=F32)
    x1_ref[...] = x_ref[...] + _rms(t, gpost_ref[...])


def _mix_prompt_kernel(o1_ref, o2_ref, o3_ref, l1_ref, l2_ref, l3_ref, z1_ref, zn_ref, ga_ref, gb_ref, x_ref,
                       wsp_ref, bsp_ref, wao_ref, wbo_ref, wout_ref, gpost_ref, x1_ref):
    l1, l2, l3 = l1_ref[...], l2_ref[...], l3_ref[...]
    top = jnp.maximum(jnp.maximum(l1, l2), l3)
    e1, e2, e3 = jnp.exp(l1 - top), jnp.exp(l2 - top), jnp.exp(l3 - top)
    o_a = (e1 * o1_ref[...] + e2 * o2_ref[...] + e3 * o3_ref[...]) / (e1 + e2 + e3)

    r = lax.broadcasted_iota(jnp.int32, (CHUNK, CHUNK), 0)
    c = lax.broadcasted_iota(jnp.int32, (CHUNK, CHUNK), 1)
    tm = z1_ref.shape[0]
    chunks = []
    for ci in range(tm // CHUNK):
        rows = slice(ci * CHUNK, (ci + 1) * CHUNK)
        cols = []
        for g in range(SG_GROUPS):
            gl = slice(g * CHUNK, (g + 1) * CHUNK)
            wt = jnp.where(r >= c, wsp_ref[g], 0.0).astype(BF16)
            cols.append(jnp.dot(wt, zn_ref[rows, gl].astype(BF16), preferred_element_type=F32))
        chunks.append(jnp.concatenate(cols, axis=1) + bsp_ref[...])
    o_b = z1_ref[...] * jnp.concatenate(chunks, axis=0)
    _mix_tail(o_a, o_b, ga_ref, gb_ref, x_ref, wao_ref, wbo_ref, wout_ref, gpost_ref, x1_ref)


def _mix_sample_kernel(oa_ref, z1_ref, zn_ref, ga_ref, gb_ref, x_ref, coef_ref, bsp_ref,
                       wao_ref, wbo_ref, wout_ref, gpost_ref, x1_ref):
    zn = zn_ref[...]
    mix = coef_ref[0] * zn
    for d in range(1, coef_ref.shape[0]):
        mix = mix + coef_ref[d] * pltpu.roll(zn, d, axis=0)
    o_b = z1_ref[...] * (mix + bsp_ref[...])
    _mix_tail(oa_ref[...], o_b, ga_ref, gb_ref, x_ref, wao_ref, wbo_ref, wout_ref, gpost_ref, x1_ref)


def _mix_weight_specs():
    return [_const_spec((GROUP_W, D_MODEL)), _const_spec((SG_W, D_MODEL)), _const_spec((D_MODEL, D_MODEL)),
            _const_spec((1, D_MODEL))]


def _mix_prompt(os_, ls_, z1, zn, ga, gb, x, wsp, bsp, wao, wbo, wout, gpost):
    n = x.shape[0]
    tm = TOKEN_TILE
    row = lambda w: pl.BlockSpec((tm, w), lambda i: (i, 0))
    return pl.pallas_call(
        _mix_prompt_kernel,
        grid=(n // tm,),
        in_specs=[row(GROUP_W)] * 6 + [row(SG_W), row(SG_W), row(D_MODEL), row(D_MODEL), row(D_MODEL),
                                       _const_spec((SG_GROUPS, CHUNK, CHUNK)), _const_spec((CHUNK, SG_W))]
        + _mix_weight_specs(),
        out_specs=row(D_MODEL),
        out_shape=jax.ShapeDtypeStruct((n, D_MODEL), F32),
        compiler_params=_params(1),
        name="mix_prompt",
    )(*os_, *ls_, z1, zn, ga, gb, x, wsp, bsp, wao, wbo, wout, gpost)


def _mix_sample(o_a, z1, zn, ga, gb, x, coef, bsp, wao, wbo, wout, gpost):
    n = x.shape[0]
    full = lambda a: _const_spec(a.shape)
    return pl.pallas_call(
        _mix_sample_kernel,
        grid=(1,),
        in_specs=[full(o_a), full(z1), full(zn), full(ga), full(gb), full(x), full(coef), full(bsp)]
        + _mix_weight_specs(),
        out_specs=_const_spec((n, D_MODEL)),
        out_shape=jax.ShapeDtypeStruct((n, D_MODEL), F32),
        compiler_params=_params(1),
        name="mix_sample",
    )(o_a, z1, zn, ga, gb, x, coef, bsp, wao, wbo, wout, gpost)


FF_CHUNK = 1024


def _ffn_kernel(x_ref, gpre_ref, wup_ref, wdn_ref, gpost_ref, y_ref):
    x = x_ref[...]
    h = _rms(x, gpre_ref[...]).astype(BF16)
    f = jnp.zeros(x.shape, F32)
    for c in range(D_FF // FF_CHUNK):
        cs = slice(c * FF_CHUNK, (c + 1) * FF_CHUNK)
        u = jnp.maximum(jnp.dot(h, wup_ref[:, cs], preferred_element_type=F32), 0.0)
        f = f + jnp.dot((u * u).astype(BF16), wdn_ref[cs, :], preferred_element_type=F32)
    y_ref[...] = x + _rms(f, gpost_ref[...])


def _ffn(x, gpre, wup, wdn, gpost):
    n = x.shape[0]
    tm = min(TOKEN_TILE, n)
    row = pl.BlockSpec((tm, D_MODEL), lambda i: (i, 0))
    return pl.pallas_call(
        _ffn_kernel,
        grid=(n // tm,),
        in_specs=[row, _const_spec((1, D_MODEL)), _const_spec((D_MODEL, D_FF)), _const_spec((D_FF, D_MODEL)),
                  _const_spec((1, D_MODEL))],
        out_specs=row,
        out_shape=jax.ShapeDtypeStruct((n, D_MODEL), F32),
        compiler_params=_params(1),
        name="ffn",
    )(x, gpre, wup, wdn, gpost)


def _to_residues(a, batch, seq, dil):
    L = seq // dil
    return a.reshape(batch, L, dil, GROUP_W).transpose(0, 2, 1, 3).reshape(batch * dil, L, GROUP_W)


def _from_residues(a, batch, seq, dil):
    L = seq // dil
    return a.reshape(batch, dil, L, GROUP_W).transpose(0, 2, 1, 3).reshape(batch * seq, GROUP_W)


def kernel(x_prompt, x_sample, cache_kv_w128, cache_kv_w512, cache_kv_w2048, norm_pre_mix, w_in, b_gate,
           ln_z_g, ln_z_b, w_spatial, b_spatial, w_ao, w_bo, w_out, norm_post_mix, norm_pre_ffn, w_up, w_down,
           norm_post_ffn):
    assert w_in.shape[0] == 1, "single-layer problem"
    batch, seq, _ = x_prompt.shape
    dbatch, t_new, _ = x_sample.shape
    caches_in = (cache_kv_w128[0], cache_kv_w512[0], cache_kv_w2048[0])

    w_in_bf = w_in[0].astype(BF16)
    wao, wbo, wout = w_ao[0].astype(BF16), w_bo[0].astype(BF16), w_out[0].astype(BF16)
    wup, wdn = w_up[0].astype(BF16), w_down[0].astype(BF16)
    wsp = w_spatial[0]
    bsp_rows = jnp.repeat(b_spatial[0].T, CHUNK, axis=1)

    xp = x_prompt.reshape(batch * seq, D_MODEL)
    qkv, z1, zn, ga, gb = _project(xp, norm_pre_mix, w_in_bf, b_gate, ln_z_g, ln_z_b)
    q_all, k_all, v_all = qkv[:, :ATTN_W], qkv[:, ATTN_W:2 * ATTN_W], qkv[:, 2 * ATTN_W:]
    outs, lses, kv_prompt = [], [], []
    for gi, (win, dil) in enumerate(DIL_GROUPS):
        gs = slice(gi * GROUP_W, (gi + 1) * GROUP_W)
        qg, kg, vg = q_all[:, gs], k_all[:, gs], v_all[:, gs]
        o, lse = _band_attention(*(_to_residues(a.astype(BF16), batch, seq, dil) for a in (qg, kg, vg)))
        outs.append(_from_residues(o, batch, seq, dil))
        lses.append(_from_residues(lse, batch, seq, dil))
        keep = min(win, seq)
        k5 = kg.reshape(batch, seq, HEADS, HEAD_DIM)[:, -keep:]
        v5 = vg.reshape(batch, seq, HEADS, HEAD_DIM)[:, -keep:]
        kv_prompt.append(jnp.stack([k5, v5], axis=2)[None])
    x1 = _mix_prompt(outs, lses, z1, zn, ga, gb, xp, wsp, bsp_rows, wao, wbo, wout, norm_post_mix)
    y_prompt = _ffn(x1, norm_pre_ffn, wup, wdn, norm_post_ffn).reshape(batch, seq, D_MODEL)

    ns = dbatch * t_new
    xs = x_sample.reshape(ns, D_MODEL)
    qkv_s, z1_s, zn_s, ga_s, gb_s = _project(xs, norm_pre_mix, w_in_bf, b_gate, ln_z_g, ln_z_b)
    pad = lambda a: jnp.pad(a.reshape(dbatch, t_new, ATTN_W), ((0, 0), (0, T_PAD - t_new), (0, 0)))
    q8, kn8, vn8 = pad(qkv_s[:, :ATTN_W]), pad(qkv_s[:, ATTN_W:2 * ATTN_W]), pad(qkv_s[:, 2 * ATTN_W:])
    caches, kv_sample = [], []
    for gi, (win, dil) in enumerate(DIL_GROUPS):
        buf = caches_in[gi]
        wb = buf.shape[1]
        assert wb == win and t_new <= T_PAD, "full window buffers"
        caches.append(buf.transpose(0, 2, 3, 4, 1).reshape(dbatch, 2 * GROUP_W, wb))
        gs = slice(gi * GROUP_W, (gi + 1) * GROUP_W)
        kn = qkv_s[:, ATTN_W:2 * ATTN_W][:, gs].reshape(dbatch, t_new, HEADS, HEAD_DIM)
        vn = qkv_s[:, 2 * ATTN_W:][:, gs].reshape(dbatch, t_new, HEADS, HEAD_DIM)
        kv_sample.append(jnp.stack([kn, vn], axis=2)[None])
    o_a = _sample_attention(q8, kn8, vn8, caches)[:, :t_new].reshape(ns, GROUP_W)

    wt = jnp.tril(wsp)[:, :t_new, :t_new]
    coef = jnp.stack([jnp.where(jnp.arange(t_new)[None, :] >= d,
                                wt[:, jnp.arange(t_new), jnp.maximum(jnp.arange(t_new) - d, 0)], 0.0)
                      for d in range(t_new)])
    coef = jnp.repeat(coef.transpose(0, 2, 1), CHUNK, axis=2)
    coef = jnp.tile(coef, (1, dbatch, 1))
    bsp_s = jnp.tile(bsp_rows[:t_new], (dbatch, 1))
    x1_s = _mix_sample(o_a, z1_s, zn_s, ga_s, gb_s, xs, coef, bsp_s, wao, wbo, wout, norm_post_mix)
    y_sample = _ffn(x1_s, norm_pre_ffn, wup, wdn, norm_post_ffn).reshape(dbatch, t_new, D_MODEL)
    sg_sample = zn_s.reshape(1, dbatch, t_new, SG_W)

    return (y_prompt, y_sample, kv_prompt[0], kv_prompt[1], kv_prompt[2],
            kv_sample[0], kv_sample[1], kv_sample[2], sg_sample)
```

```python
import functools
import math

import jax
import jax.numpy as jnp
from jax import lax
from jax.experimental import pallas as pl
from jax.experimental.pallas import tpu as pltpu

F32 = jnp.float32
BF16 = jnp.bfloat16

D_MODEL = 1024
HEAD_DIM = 64
HEADS = 4
GROUP_W = HEADS * HEAD_DIM
DIL_GROUPS = ((128, 1), (512, 4), (2048, 16))
N_GROUPS = len(DIL_GROUPS)
ATTN_W = N_GROUPS * GROUP_W
BLOCK = 128
CHUNK = 128
SG_GROUPS = 4
SG_W = 512
D_FF = 4 * D_MODEL
Z_OFF = 3 * ATTN_W
G_OFF = Z_OFF + 2 * SG_W
IN_W = G_OFF + 2 * D_MODEL
EPS = 1e-6
NEG = -1e30
SCALE = HEAD_DIM ** -0.5
INV_SQRT2 = 1.0 / math.sqrt(2.0)
LANES = 128
SLABS = GROUP_W // LANES
HEADS_PER_SLAB = LANES // HEAD_DIM
HEAD_SHIFT = HEAD_DIM.bit_length() - 1
VMEM_LIMIT = 56 * 1024 * 1024

TOKEN_TILE = 512
SAMPLE_DB_TILE = 2
T_PAD = 8
T_SHIFT = T_PAD.bit_length() - 1


def _params(n_axes):
    return pltpu.CompilerParams(dimension_semantics=("arbitrary",) * n_axes,
                                vmem_limit_bytes=VMEM_LIMIT)


def _const_spec(shape):
    return pl.BlockSpec(shape, lambda *_: (0,) * len(shape), pipeline_mode=pl.Buffered(1))


def _rms(x, g):
    return x * lax.rsqrt(jnp.mean(x * x, axis=-1, keepdims=True) + EPS) * g


def _proj_gate_branches(h, w_ref, bg_ref, lng_ref, lnb_ref, z1_ref, zn_ref, ga_ref, gb_ref):
    z = jnp.dot(h, w_ref[:, Z_OFF:G_OFF], preferred_element_type=F32)
    z = 0.5 * z * (1.0 + lax.erf(z * INV_SQRT2))
    z1_ref[...] = z[:, :SG_W]
    z2 = z[:, SG_W:]
    mu = jnp.mean(z2, axis=-1, keepdims=True)
    zc = z2 - mu
    var = jnp.mean(zc * zc, axis=-1, keepdims=True)
    zn_ref[...] = zc * lax.rsqrt(var + EPS) * lng_ref[...] + lnb_ref[...]

    gates = jnp.dot(h, w_ref[:, G_OFF:], preferred_element_type=F32) + bg_ref[...]
    gates = 1.0 / (1.0 + jnp.exp(-gates))
    ga_ref[...] = gates[:, :D_MODEL].astype(BF16)
    gb_ref[...] = gates[:, D_MODEL:].astype(BF16)


def _proj_sample_kernel(x_ref, g_ref, w_ref, bg_ref, lng_ref, lnb_ref,
                        qkv_ref, z1_ref, zn_ref, ga_ref, gb_ref):
    h = _rms(x_ref[...], g_ref[...]).astype(BF16)
    qkv_ref[...] = jnp.dot(h, w_ref[:, :Z_OFF], preferred_element_type=F32)
    _proj_gate_branches(h, w_ref, bg_ref, lng_ref, lnb_ref, z1_ref, zn_ref, ga_ref, gb_ref)


def _proj_prompt_kernel(x_ref, g_ref, w_ref, bg_ref, lng_ref, lnb_ref,
                        qkv_ref, kv1_ref, kv2_ref, kv3_ref, z1_ref, zn_ref, ga_ref, gb_ref):
    h = _rms(x_ref[0], g_ref[...]).astype(BF16)
    qkv = jnp.dot(h, w_ref[:, :Z_OFF], preferred_element_type=F32)
    for g in range(N_GROUPS):
        for part in range(3):
            for s in range(SLABS):
                col = part * ATTN_W + g * GROUP_W + s * LANES
                qkv_ref[0, g, part * SLABS + s] = qkv[:, col:col + LANES]

    def kv_t(g):
        k = qkv[:, ATTN_W + g * GROUP_W:ATTN_W + (g + 1) * GROUP_W]
        v = qkv[:, 2 * ATTN_W + g * GROUP_W:2 * ATTN_W + (g + 1) * GROUP_W]
        return jnp.concatenate([k.T, v.T], axis=0)

    tm = x_ref.shape[1]
    kv3_ref[0] = kv_t(2)

    @pl.when(pl.program_id(1) == pl.num_programs(1) - 1)
    def _():
        kv2_ref[0] = kv_t(1)[:, tm - kv2_ref.shape[2]:]
        kv1_ref[0] = kv_t(0)[:, tm - kv1_ref.shape[2]:]

    _proj_gate_branches(h, w_ref, bg_ref, lng_ref, lnb_ref, z1_ref, zn_ref, ga_ref, gb_ref)


def _proj_weight_specs():
    return [_const_spec((1, D_MODEL)), _const_spec((D_MODEL, IN_W)),
            _const_spec((1, 2 * D_MODEL)), _const_spec((1, SG_W)), _const_spec((1, SG_W))]


def _gate_branch_shapes(n):
    return [jax.ShapeDtypeStruct((n, SG_W), F32), jax.ShapeDtypeStruct((n, SG_W), F32),
            jax.ShapeDtypeStruct((n, D_MODEL), BF16), jax.ShapeDtypeStruct((n, D_MODEL), BF16)]


def _project_sample(x, g, w_bf, bg, lng, lnb):
    n = x.shape[0]
    tm = min(TOKEN_TILE, n)
    row = lambda w: pl.BlockSpec((tm, w), lambda i: (i, 0))
    return pl.pallas_call(
        _proj_sample_kernel,
        grid=(n // tm,),
        in_specs=[row(D_MODEL)] + _proj_weight_specs(),
        out_specs=[row(Z_OFF), row(SG_W), row(SG_W), row(D_MODEL), row(D_MODEL)],
        out_shape=[jax.ShapeDtypeStruct((n, Z_OFF), F32)] + _gate_branch_shapes(n),
        compiler_params=_params(1),
        name="proj_sample",
    )(x, g, w_bf, bg, lng, lnb)


def _project_prompt(x, g, w_bf, bg, lng, lnb):
    batch, seq, _ = x.shape
    tm = TOKEN_TILE
    tiles = seq // tm
    keeps = [min(win, seq) for win, _ in DIL_GROUPS]
    assert seq % tm == 0 and keeps[0] <= tm and keeps[1] <= tm and keeps[2] == seq
    row = lambda w: pl.BlockSpec((tm, w), lambda b, t: (b * tiles + t, 0))
    last = lambda keep: pl.BlockSpec((1, 2 * GROUP_W, keep), lambda b, t: (b, 0, 0))
    n = batch * seq
    return pl.pallas_call(
        _proj_prompt_kernel,
        grid=(batch, tiles),
        in_specs=[pl.BlockSpec((1, tm, D_MODEL), lambda b, t: (b, t, 0))] + _proj_weight_specs(),
        out_specs=[pl.BlockSpec((1, N_GROUPS, 3 * SLABS, tm, LANES), lambda b, t: (b, 0, 0, t, 0)),
                   last(keeps[0]), last(keeps[1]),
                   pl.BlockSpec((1, 2 * GROUP_W, tm), lambda b, t: (b, 0, t)),
                   row(SG_W), row(SG_W), row(D_MODEL), row(D_MODEL)],
        out_shape=[jax.ShapeDtypeStruct((batch, N_GROUPS, 3 * SLABS, seq, LANES), F32),
                   jax.ShapeDtypeStruct((batch, 2 * GROUP_W, keeps[0]), F32),
                   jax.ShapeDtypeStruct((batch, 2 * GROUP_W, keeps[1]), F32),
                   jax.ShapeDtypeStruct((batch, 2 * GROUP_W, keeps[2]), F32)] + _gate_branch_shapes(n),
        compiler_params=_params(2),
        name="proj_prompt",
    )(x, g, w_bf, bg, lng, lnb)


def _head_lane_mask(shape, hh):
    lane = lax.broadcasted_iota(jnp.int32, shape, len(shape) - 1)
    return (lane >> HEAD_SHIFT) == hh


def _band_block(q, k, v, mask):
    nt = (((1,), (1,)), ((), ()))
    outs, lses = [], []
    for s in range(SLABS):
        o_slab = jnp.zeros((BLOCK, LANES), F32)
        lse_slab = jnp.zeros((BLOCK, LANES), F32)
        for hh in range(HEADS_PER_SLAB):
            lm_q = _head_lane_mask((BLOCK, LANES), hh)
            lm_v = _head_lane_mask(v[s].shape, hh)
            qm = jnp.where(lm_q, q[s], jnp.zeros_like(q[s]))
            sc = lax.dot_general(qm, k[s], nt, preferred_element_type=F32)
            sc = jnp.where(mask, sc, NEG)
            m = jnp.max(sc, axis=-1, keepdims=True)
            p = jnp.exp(sc - m)
            l = jnp.sum(p, axis=-1, keepdims=True)
            vm = jnp.where(lm_v, v[s], jnp.zeros_like(v[s]))
            o = jnp.dot(p.astype(BF16), vm, preferred_element_type=F32)
            o_slab = o_slab + o * (1.0 / l)
            lse_slab = jnp.where(lm_q, m + jnp.log(l), lse_slab)
        outs.append(o_slab)
        lses.append(lse_slab)
    return outs, lses


def _group_attention(qkv_ref, o_scr, lse_scr, g, dil, seq):
    n_qb = seq // (dil * BLOCK)
    qb_shift = n_qb.bit_length() - 1
    span = dil * BLOCK
    rows = (lambda start: pl.ds(start, BLOCK, stride=dil)) if dil > 1 else (lambda start: pl.ds(start, BLOCK))
    nk = 2 * BLOCK if n_qb > 1 else BLOCK
    qi = lax.broadcasted_iota(jnp.int32, (BLOCK, nk), 0)
    ki = lax.broadcasted_iota(jnp.int32, (BLOCK, nk), 1)

    def load(slab, start):
        return qkv_ref[0, 0, slab, rows(start), :]

    def body(n, carry):
        r = n >> qb_shift
        qb = n & (n_qb - 1)
        cur = r + span * qb
        if dil == 1:
            cur = pl.multiple_of(cur, BLOCK)
        q = [(load(s, cur) * SCALE).astype(BF16) for s in range(SLABS)]
        k = [load(SLABS + s, cur).astype(BF16) for s in range(SLABS)]
        v = [load(2 * SLABS + s, cur).astype(BF16) for s in range(SLABS)]
        if n_qb > 1:
            prev = r + span * jnp.maximum(qb - 1, 0)
            if dil == 1:
                prev = pl.multiple_of(prev, BLOCK)
            k = [jnp.concatenate([load(SLABS + s, prev).astype(BF16), k[s]], axis=0) for s in range(SLABS)]
            v = [jnp.concatenate([load(2 * SLABS + s, prev).astype(BF16), v[s]], axis=0) for s in range(SLABS)]
            mask = (ki >= qi) & (ki <= qi + BLOCK) & (ki >= jnp.where(qb > 0, 0, BLOCK))
        else:
            mask = ki <= qi
        outs, lses = _band_block(q, k, v, mask)
        for s in range(SLABS):
            o_scr[g, s, rows(cur), :] = outs[s]
            lse_scr[g, s, rows(cur), :] = lses[s]
        return carry

    lax.fori_loop(0, dil * n_qb, body, 0)


def _prompt_attn_kernel(qkv_ref, o_ref, o_scr, lse_scr):
    g = pl.program_id(1)
    seq = o_ref.shape[1]
    for gi, (_, dil) in enumerate(DIL_GROUPS):
        @pl.when(g == gi)
        def _(gi=gi, dil=dil):
            _group_attention(qkv_ref, o_scr, lse_scr, gi, dil, seq)

    @pl.when(g == N_GROUPS - 1)
    def _():
        def merge(i, carry):
            rows = pl.ds(pl.multiple_of(i * BLOCK, BLOCK), BLOCK)
            for s in range(SLABS):
                ls = [lse_scr[gi, s, rows, :] for gi in range(N_GROUPS)]
                top = jnp.maximum(jnp.maximum(ls[0], ls[1]), ls[2])
                e = [jnp.exp(x - top) for x in ls]
                num = e[0] * o_scr[0, s, rows, :] + e[1] * o_scr[1, s, rows, :] + e[2] * o_scr[2, s, rows, :]
                o_ref[0, rows, s * LANES:(s + 1) * LANES] = (num / (e[0] + e[1] + e[2])).astype(o_ref.dtype)
            return carry
        lax.fori_loop(0, seq // BLOCK, merge, 0)


def _prompt_attention(qkv, seq):
    batch = qkv.shape[0]
    scratch = pltpu.VMEM((N_GROUPS, SLABS, seq, LANES), F32)
    return pl.pallas_call(
        _prompt_attn_kernel,
        grid=(batch, N_GROUPS),
        in_specs=[pl.BlockSpec((1, 1, 3 * SLABS, seq, LANES), lambda b, g: (b, g, 0, 0, 0))],
        out_specs=pl.BlockSpec((1, seq, GROUP_W), lambda b, g: (b, 0, 0)),
        out_shape=jax.ShapeDtypeStruct((batch, seq, GROUP_W), BF16),
        scratch_shapes=[scratch, scratch],
        compiler_params=_params(2),
        name="prompt_attn",
    )(qkv)


def _sample_attn_kernel(q_ref, kn_ref, vn_ref, c1_ref, c2_ref, c3_ref, o_ref):
    rows = HEADS * T_PAD
    row = lax.broadcasted_iota(jnp.int32, (rows, GROUP_W), 0)
    lane = lax.broadcasted_iota(jnp.int32, (rows, GROUP_W), 1)
    head_sel = (row >> T_SHIFT) == (lane >> HEAD_SHIFT)

    def key_masks(width, dil):
        t = lax.broadcasted_iota(jnp.int32, (rows, width), 0) & (T_PAD - 1)
        i = lax.broadcasted_iota(jnp.int32, (rows, width), 1)
        tn = lax.broadcasted_iota(jnp.int32, (rows, T_PAD), 0) & (T_PAD - 1)
        j = lax.broadcasted_iota(jnp.int32, (rows, T_PAD), 1)
        return (((i - t) & (dil - 1)) == 0) & (i >= t), (((tn - j) & (dil - 1)) == 0) & (j <= tn)

    caches = tuple(zip((c1_ref, c2_ref, c3_ref), (dil for _, dil in DIL_GROUPS)))
    masks = [key_masks(c.shape[2], dil) for c, dil in caches]
    nt = (((1,), (1,)), ((), ()))

    for d in range(o_ref.shape[0]):
        outs, lses = [], []
        for g, (c_ref, _) in enumerate(caches):
            gs = slice(g * GROUP_W, (g + 1) * GROUP_W)
            q8 = q_ref[d, :, gs] * SCALE
            qm = jnp.where(head_sel, jnp.concatenate([q8] * HEADS, axis=0), 0.0).astype(BF16)
            kn = kn_ref[d, :, gs].astype(BF16)
            vn = vn_ref[d, :, gs].astype(BF16)
            kt = c_ref[d, :GROUP_W, :].astype(BF16)
            vt = c_ref[d, GROUP_W:, :].astype(BF16)
            sc = jnp.dot(qm, kt, preferred_element_type=F32)
            sn = lax.dot_general(qm, kn, nt, preferred_element_type=F32)
            mc, mn = masks[g]
            sc = jnp.where(mc, sc, NEG)
            sn = jnp.where(mn, sn, NEG)
            m = jnp.maximum(jnp.max(sc, axis=-1, keepdims=True), jnp.max(sn, axis=-1, keepdims=True))
            pc = jnp.exp(sc - m)
            pn = jnp.exp(sn - m)
            l = jnp.sum(pc, axis=-1, keepdims=True) + jnp.sum(pn, axis=-1, keepdims=True)
            o = (lax.dot_general(pc.astype(BF16), vt, nt, preferred_element_type=F32)
                 + jnp.dot(pn.astype(BF16), vn, preferred_element_type=F32))
            outs.append(o / l)
            lses.append(m + jnp.log(l))
        top = jnp.maximum(jnp.maximum(lses[0], lses[1]), lses[2])
        w = [jnp.exp(x - top) for x in lses]
        merged = (w[0] * outs[0] + w[1] * outs[1] + w[2] * outs[2]) / (w[0] + w[1] + w[2])
        merged = jnp.where(head_sel, merged, 0.0)
        o8 = merged[0:T_PAD]
        for h in range(1, HEADS):
            o8 = o8 + merged[h * T_PAD:(h + 1) * T_PAD]
        o_ref[d] = o8


def _sample_attention(q8, kn8, vn8, caches):
    db = q8.shape[0]
    tile = SAMPLE_DB_TILE
    blk = lambda a: pl.BlockSpec((tile,) + a.shape[1:], lambda i: (i, 0, 0))
    return pl.pallas_call(
        _sample_attn_kernel,
        grid=(db // tile,),
        in_specs=[blk(q8), blk(kn8), blk(vn8)] + [blk(a) for a in caches],
        out_specs=pl.BlockSpec((tile, T_PAD, GROUP_W), lambda i: (i, 0, 0)),
        out_shape=jax.ShapeDtypeStruct((db, T_PAD, GROUP_W), F32),
        compiler_params=_params(1),
        name="sample_attn",
    )(q8, kn8, vn8, *caches)


def _mix_tail(o_a, o_b, ga_ref, gb_ref, x_ref, wao_ref, wbo_ref, wout_ref, gpost_ref, x1_ref):
    a = jnp.dot(o_a.astype(BF16), wao_ref[...], preferred_element_type=F32)
    b = jnp.dot(o_b.astype(BF16), wbo_ref[...], preferred_element_type=F32)
    merged = ga_ref[...].astype(F32) * a + gb_ref[...].astype(F32) * b
    t = jnp.dot(merged.astype(BF16), wout_ref[...], preferred_element_type=F32)
    x1_ref[...] = x_ref[...] + _rms(t, gpost_ref[...])


def _mix_prompt_kernel(oa_ref, z1_ref, zn_ref, ga_ref, gb_ref, x_ref,
                       wsp_ref, bsp_ref, wao_ref, wbo_ref, wout_ref, gpost_ref, x1_ref):
    r = lax.broadcasted_iota(jnp.int32, (CHUNK, CHUNK), 0)
    c = lax.broadcasted_iota(jnp.int32, (CHUNK, CHUNK), 1)
    tm = z1_ref.shape[0]
    chunks = []
    for ci in range(tm // CHUNK):
        rows = slice(ci * CHUNK, (ci + 1) * CHUNK)
        cols = []
        for g in range(SG_GROUPS):
            gl = slice(g * CHUNK, (g + 1) * CHUNK)
            wt = jnp.where(r >= c, wsp_ref[g], 0.0).astype(BF16)
            cols.append(jnp.dot(wt, zn_ref[rows, gl].astype(BF16), preferred_element_type=F32))
        chunks.append(jnp.concatenate(cols, axis=1) + bsp_ref[...])
    o_b = z1_ref[...] * jnp.concatenate(chunks, axis=0)
    _mix_tail(oa_ref[...], o_b, ga_ref, gb_ref, x_ref, wao_ref, wbo_ref, wout_ref, gpost_ref, x1_ref)


def _mix_sample_kernel(oa_ref, z1_ref, zn_ref, ga_ref, gb_ref, x_ref, coef_ref, bsp_ref,
                       wao_ref, wbo_ref, wout_ref, gpost_ref, x1_ref):
    zn = zn_ref[...]
    mix = coef_ref[0] * zn
    for d in range(1, coef_ref.shape[0]):
        mix = mix + coef_ref[d] * pltpu.roll(zn, d, axis=0)
    o_b = z1_ref[...] * (mix + bsp_ref[...])
    _mix_tail(oa_ref[...], o_b, ga_ref, gb_ref, x_ref, wao_ref, wbo_ref, wout_ref, gpost_ref, x1_ref)


def _mix_weight_specs():
    return [_const_spec((GROUP_W, D_MODEL)), _const_spec((SG_W, D_MODEL)), _const_spec((D_MODEL, D_MODEL)),
            _const_spec((1, D_MODEL))]


def _mix_prompt(o_a, z1, zn, ga, gb, x, wsp, bsp, wao, wbo, wout, gpost):
    n = x.shape[0]
    tm = TOKEN_TILE
    row = lambda w: pl.BlockSpec((tm, w), lambda i: (i, 0))
    return pl.pallas_call(
        _mix_prompt_kernel,
        grid=(n // tm,),
        in_specs=[row(GROUP_W), row(SG_W), row(SG_W), row(D_MODEL), row(D_MODEL), row(D_MODEL),
                  _const_spec((SG_GROUPS, CHUNK, CHUNK)), _const_spec((CHUNK, SG_W))]
        + _mix_weight_specs(),
        out_specs=row(D_MODEL),
        out_shape=jax.ShapeDtypeStruct((n, D_MODEL), F32),
        compiler_params=_params(1),
        name="mix_prompt",
    )(o_a, z1, zn, ga, gb, x, wsp, bsp, wao, wbo, wout, gpost)


def _mix_sample(o_a, z1, zn, ga, gb, x, coef, bsp, wao, wbo, wout, gpost):
    n = x.shape[0]
    full = lambda a: _const_spec(a.shape)
    return pl.pallas_call(
        _mix_sample_kernel,
        grid=(1,),
        in_specs=[full(o_a), full(z1), full(zn), full(ga), full(gb), full(x), full(coef), full(bsp)]
        + _mix_weight_specs(),
        out_specs=_const_spec((n, D_MODEL)),
        out_shape=jax.ShapeDtypeStruct((n, D_MODEL), F32),
        compiler_params=_params(1),
        name="mix_sample",
    )(o_a, z1, zn, ga, gb, x, coef, bsp, wao, wbo, wout, gpost)


FF_CHUNK = 1024


def _ffn_kernel(x_ref, gpre_ref, wup_ref, wdn_ref, gpost_ref, y_ref):
    x = x_ref[...]
    h = _rms(x, gpre_ref[...]).astype(BF16)
    f = jnp.zeros(x.shape, F32)
    for c in range(D_FF // FF_CHUNK):
        cs = slice(c * FF_CHUNK, (c + 1) * FF_CHUNK)
        u = jnp.maximum(jnp.dot(h, wup_ref[:, cs], preferred_element_type=F32), 0.0)
        f = f + jnp.dot((u * u).astype(BF16), wdn_ref[cs, :], preferred_element_type=F32)
    y_ref[...] = x + _rms(f, gpost_ref[...])


def _ffn(x, gpre, wup, wdn, gpost):
    n = x.shape[0]
    tm = min(TOKEN_TILE, n)
    row = pl.BlockSpec((tm, D_MODEL), lambda i: (i, 0))
    return pl.pallas_call(
        _ffn_kernel,
        grid=(n // tm,),
        in_specs=[row, _const_spec((1, D_MODEL)), _const_spec((D_MODEL, D_FF)), _const_spec((D_FF, D_MODEL)),
                  _const_spec((1, D_MODEL))],
        out_specs=row,
        out_shape=jax.ShapeDtypeStruct((n, D_MODEL), F32),
        compiler_params=_params(1),
        name="ffn",
    )(x, gpre, wup, wdn, gpost)


def _kv_from_feature_major(kv_t):
    batch, _, keep = kv_t.shape
    return kv_t.reshape(batch, 2, HEADS, HEAD_DIM, keep).transpose(0, 4, 1, 2, 3)[None]


def kernel(x_prompt, x_sample, cache_kv_w128, cache_kv_w512, cache_kv_w2048, norm_pre_mix, w_in, b_gate,
           ln_z_g, ln_z_b, w_spatial, b_spatial, w_ao, w_bo, w_out, norm_post_mix, norm_pre_ffn, w_up, w_down,
           norm_post_ffn):
    assert w_in.shape[0] == 1, "single-layer problem"
    batch, seq, _ = x_prompt.shape
    dbatch, t_new, _ = x_sample.shape
    caches_in = (cache_kv_w128[0], cache_kv_w512[0], cache_kv_w2048[0])

    w_in_bf = w_in[0].astype(BF16)
    wao, wbo, wout = w_ao[0].astype(BF16), w_bo[0].astype(BF16), w_out[0].astype(BF16)
    wup, wdn = w_up[0].astype(BF16), w_down[0].astype(BF16)
    wsp = w_spatial[0]
    bsp_rows = jnp.repeat(b_spatial[0].T, CHUNK, axis=1)

    xp = x_prompt.reshape(batch * seq, D_MODEL)
    qkv, kv1_t, kv2_t, kv3_t, z1, zn, ga, gb = _project_prompt(
        x_prompt, norm_pre_mix, w_in_bf, b_gate, ln_z_g, ln_z_b)
    kv_prompt = [_kv_from_feature_major(a) for a in (kv1_t, kv2_t, kv3_t)]
    o_a_p = _prompt_attention(qkv, seq).reshape(batch * seq, GROUP_W)
    x1 = _mix_prompt(o_a_p, z1, zn, ga, gb, xp, wsp, bsp_rows, wao, wbo, wout, norm_post_mix)
    y_prompt = _ffn(x1, norm_pre_ffn, wup, wdn, norm_post_ffn).reshape(batch, seq, D_MODEL)

    ns = dbatch * t_new
    xs = x_sample.reshape(ns, D_MODEL)
    qkv_s, z1_s, zn_s, ga_s, gb_s = _project_sample(xs, norm_pre_mix, w_in_bf, b_gate, ln_z_g, ln_z_b)
    pad = lambda a: jnp.pad(a.reshape(dbatch, t_new, ATTN_W), ((0, 0), (0, T_PAD - t_new), (0, 0)))
    q8, kn8, vn8 = pad(qkv_s[:, :ATTN_W]), pad(qkv_s[:, ATTN_W:2 * ATTN_W]), pad(qkv_s[:, 2 * ATTN_W:])
    caches, kv_sample = [], []
    for gi, (win, dil) in enumerate(DIL_GROUPS):
        buf = caches_in[gi]
        wb = buf.shape[1]
        assert wb == win and t_new <= T_PAD, "full window buffers"
        caches.append(buf.transpose(0, 2, 3, 4, 1).reshape(dbatch, 2 * GROUP_W, wb))
        gs = slice(gi * GROUP_W, (gi + 1) * GROUP_W)
        kn = qkv_s[:, ATTN_W:2 * ATTN_W][:, gs].reshape(dbatch, t_new, HEADS, HEAD_DIM)
        vn = qkv_s[:, 2 * ATTN_W:][:, gs].reshape(dbatch, t_new, HEADS, HEAD_DIM)
        kv_sample.append(jnp.stack([kn, vn], axis=2)[None])
    o_a = _sample_attention(q8, kn8, vn8, caches)[:, :t_new].reshape(ns, GROUP_W)

    wt = jnp.tril(wsp)[:, :t_new, :t_new]
    coef = jnp.stack([jnp.where(jnp.arange(t_new)[None, :] >= d,
                                wt[:, jnp.arange(t_new), jnp.maximum(jnp.arange(t_new) - d, 0)], 0.0)
                      for d in range(t_new)])
    coef = jnp.repeat(coef.transpose(0, 2, 1), CHUNK, axis=2)
    coef = jnp.tile(coef, (1, dbatch, 1))
    bsp_s = jnp.tile(bsp_rows[:t_new], (dbatch, 1))
    x1_s = _mix_sample(o_a, z1_s, zn_s, ga_s, gb_s, xs, coef, bsp_s, wao, wbo, wout, norm_post_mix)
    y_sample = _ffn(x1_s, norm_pre_ffn, wup, wdn, norm_post_ffn).reshape(dbatch, t_new, D_MODEL)
    sg_sample = zn_s.reshape(1, dbatch, t_new, SG_W)

    return (y_prompt, y_sample, kv_prompt[0], kv_prompt[1], kv_prompt[2],
            kv_sample[0], kv_sample[1], kv_sample[2], sg_sample)
```

```python
import functools
import math

import jax
import jax.numpy as jnp
from jax import lax
from jax.experimental import pallas as pl
from jax.experimental.pallas import tpu as pltpu

F32 = jnp.float32
BF16 = jnp.bfloat16

D_MODEL = 1024
HEAD_DIM = 64
HEADS = 4
GROUP_W = HEADS * HEAD_DIM
DIL_GROUPS = ((128, 1), (512, 4), (2048, 16))
N_GROUPS = len(DIL_GROUPS)
ATTN_W = N_GROUPS * GROUP_W
BLOCK = 128
CHUNK = 128
SG_GROUPS = 4
SG_W = 512
D_FF = 4 * D_MODEL
Z_OFF = 3 * ATTN_W
G_OFF = Z_OFF + 2 * SG_W
IN_W = G_OFF + 2 * D_MODEL
EPS = 1e-6
NEG = -1e30
SCALE = HEAD_DIM ** -0.5
INV_SQRT2 = 1.0 / math.sqrt(2.0)
LANES = 128
SLABS = GROUP_W // LANES
HEADS_PER_SLAB = LANES // HEAD_DIM
HEAD_SHIFT = HEAD_DIM.bit_length() - 1
VMEM_LIMIT = 56 * 1024 * 1024

TOKEN_TILE = 512
BLOCK_UNROLL = 2
SAMPLE_DB_TILE = 2
T_PAD = 8
T_SHIFT = T_PAD.bit_length() - 1


def _params(n_axes):
    return pltpu.CompilerParams(dimension_semantics=("arbitrary",) * n_axes,
                                vmem_limit_bytes=VMEM_LIMIT)


def _const_spec(shape):
    return pl.BlockSpec(shape, lambda *_: (0,) * len(shape), pipeline_mode=pl.Buffered(1))


def _rms(x, g):
    return x * lax.rsqrt(jnp.mean(x * x, axis=-1, keepdims=True) + EPS) * g


def _proj_gate_branches(h, w_ref, bg_ref, lng_ref, lnb_ref, z1_ref, zn_ref, ga_ref, gb_ref):
    z = jnp.dot(h, w_ref[:, Z_OFF:G_OFF], preferred_element_type=F32)
    z = 0.5 * z * (1.0 + lax.erf(z * INV_SQRT2))
    z1_ref[...] = z[:, :SG_W]
    z2 = z[:, SG_W:]
    mu = jnp.mean(z2, axis=-1, keepdims=True)
    zc = z2 - mu
    var = jnp.mean(zc * zc, axis=-1, keepdims=True)
    zn_ref[...] = zc * lax.rsqrt(var + EPS) * lng_ref[...] + lnb_ref[...]

    gates = jnp.dot(h, w_ref[:, G_OFF:], preferred_element_type=F32) + bg_ref[...]
    gates = 1.0 / (1.0 + jnp.exp(-gates))
    ga_ref[...] = gates[:, :D_MODEL].astype(BF16)
    gb_ref[...] = gates[:, D_MODEL:].astype(BF16)


def _proj_sample_kernel(x_ref, g_ref, w_ref, bg_ref, lng_ref, lnb_ref,
                        qkv_ref, z1_ref, zn_ref, ga_ref, gb_ref):
    h = _rms(x_ref[...], g_ref[...]).astype(BF16)
    qkv_ref[...] = jnp.dot(h, w_ref[:, :Z_OFF], preferred_element_type=F32)
    _proj_gate_branches(h, w_ref, bg_ref, lng_ref, lnb_ref, z1_ref, zn_ref, ga_ref, gb_ref)


def _proj_prompt_kernel(x_ref, g_ref, w_ref, bg_ref, lng_ref, lnb_ref,
                        qkv_ref, kv1_ref, kv2_ref, kv3_ref, z1_ref, zn_ref, ga_ref, gb_ref):
    h = _rms(x_ref[0], g_ref[...]).astype(BF16)
    qkv = jnp.dot(h, w_ref[:, :Z_OFF], preferred_element_type=F32)
    for g in range(N_GROUPS):
        for part in range(3):
            for s in range(SLABS):
                col = part * ATTN_W + g * GROUP_W + s * LANES
                qkv_ref[0, g, part * SLABS + s] = qkv[:, col:col + LANES]

    def kv_t(g):
        k = qkv[:, ATTN_W + g * GROUP_W:ATTN_W + (g + 1) * GROUP_W]
        v = qkv[:, 2 * ATTN_W + g * GROUP_W:2 * ATTN_W + (g + 1) * GROUP_W]
        return jnp.concatenate([k.T, v.T], axis=0)

    tm = x_ref.shape[1]
    kv3_ref[0] = kv_t(2)

    @pl.when(pl.program_id(1) == pl.num_programs(1) - 1)
    def _():
        kv2_ref[0] = kv_t(1)[:, tm - kv2_ref.shape[2]:]
        kv1_ref[0] = kv_t(0)[:, tm - kv1_ref.shape[2]:]

    _proj_gate_branches(h, w_ref, bg_ref, lng_ref, lnb_ref, z1_ref, zn_ref, ga_ref, gb_ref)


def _proj_weight_specs():
    return [_const_spec((1, D_MODEL)), _const_spec((D_MODEL, IN_W)),
            _const_spec((1, 2 * D_MODEL)), _const_spec((1, SG_W)), _const_spec((1, SG_W))]


def _gate_branch_shapes(n):
    return [jax.ShapeDtypeStruct((n, SG_W), F32), jax.ShapeDtypeStruct((n, SG_W), F32),
            jax.ShapeDtypeStruct((n, D_MODEL), BF16), jax.ShapeDtypeStruct((n, D_MODEL), BF16)]


def _project_sample(x, g, w_bf, bg, lng, lnb):
    n = x.shape[0]
    tm = min(TOKEN_TILE, n)
    row = lambda w: pl.BlockSpec((tm, w), lambda i: (i, 0))
    return pl.pallas_call(
        _proj_sample_kernel,
        grid=(n // tm,),
        in_specs=[row(D_MODEL)] + _proj_weight_specs(),
        out_specs=[row(Z_OFF), row(SG_W), row(SG_W), row(D_MODEL), row(D_MODEL)],
        out_shape=[jax.ShapeDtypeStruct((n, Z_OFF), F32)] + _gate_branch_shapes(n),
        compiler_params=_params(1),
        name="proj_sample",
    )(x, g, w_bf, bg, lng, lnb)


def _project_prompt(x, g, w_bf, bg, lng, lnb):
    batch, seq, _ = x.shape
    tm = TOKEN_TILE
    tiles = seq // tm
    keeps = [min(win, seq) for win, _ in DIL_GROUPS]
    assert seq % tm == 0 and keeps[0] <= tm and keeps[1] <= tm and keeps[2] == seq
    row = lambda w: pl.BlockSpec((tm, w), lambda b, t: (b * tiles + t, 0))
    last = lambda keep: pl.BlockSpec((1, 2 * GROUP_W, keep), lambda b, t: (b, 0, 0))
    n = batch * seq
    return pl.pallas_call(
        _proj_prompt_kernel,
        grid=(batch, tiles),
        in_specs=[pl.BlockSpec((1, tm, D_MODEL), lambda b, t: (b, t, 0))] + _proj_weight_specs(),
        out_specs=[pl.BlockSpec((1, N_GROUPS, 3 * SLABS, tm, LANES), lambda b, t: (b, 0, 0, t, 0)),
                   last(keeps[0]), last(keeps[1]),
                   pl.BlockSpec((1, 2 * GROUP_W, tm), lambda b, t: (b, 0, t)),
                   row(SG_W), row(SG_W), row(D_MODEL), row(D_MODEL)],
        out_shape=[jax.ShapeDtypeStruct((batch, N_GROUPS, 3 * SLABS, seq, LANES), F32),
                   jax.ShapeDtypeStruct((batch, 2 * GROUP_W, keeps[0]), F32),
                   jax.ShapeDtypeStruct((batch, 2 * GROUP_W, keeps[1]), F32),
                   jax.ShapeDtypeStruct((batch, 2 * GROUP_W, keeps[2]), F32)] + _gate_branch_shapes(n),
        compiler_params=_params(2),
        name="proj_prompt",
    )(x, g, w_bf, bg, lng, lnb)


def _head_lane_mask(shape, hh):
    lane = lax.broadcasted_iota(jnp.int32, shape, len(shape) - 1)
    return (lane >> HEAD_SHIFT) == hh


def _band_block(q, k, v, mask):
    nt = (((1,), (1,)), ((), ()))
    assert HEADS_PER_SLAB == 2
    lm0 = _head_lane_mask((BLOCK, LANES), 0)
    mask2 = jnp.concatenate([mask, mask], axis=0)
    outs, lses = [], []
    for s in range(SLABS):
        zero = jnp.zeros_like(q[s])
        qm = jnp.concatenate([jnp.where(lm0, q[s], zero), jnp.where(lm0, zero, q[s])], axis=0)
        sc = lax.dot_general(qm, k[s], nt, preferred_element_type=F32)
        sc = jnp.where(mask2, sc, NEG)
        m = jnp.max(sc, axis=-1, keepdims=True)
        p = jnp.exp(sc - m)
        l = jnp.sum(p, axis=-1, keepdims=True)
        o = jnp.dot(p.astype(BF16), v[s], preferred_element_type=F32) * (1.0 / l)
        lse = m + jnp.log(l)
        outs.append(jnp.where(lm0, o[:BLOCK], o[BLOCK:]))
        lses.append(jnp.where(lm0, lse[:BLOCK], lse[BLOCK:]))
    return outs, lses


def _group_attention(qkv_ref, o_scr, lse_scr, g, dil, seq):
    n_qb = seq // (dil * BLOCK)
    qb_shift = n_qb.bit_length() - 1
    span = dil * BLOCK
    rows = (lambda start: pl.ds(start, BLOCK, stride=dil)) if dil > 1 else (lambda start: pl.ds(start, BLOCK))
    nk = 2 * BLOCK if n_qb > 1 else BLOCK
    qi = lax.broadcasted_iota(jnp.int32, (BLOCK, nk), 0)
    ki = lax.broadcasted_iota(jnp.int32, (BLOCK, nk), 1)

    def load(slab, start):
        return qkv_ref[0, 0, slab, rows(start), :]

    def body(n, carry):
        r = n >> qb_shift
        qb = n & (n_qb - 1)
        cur = r + span * qb
        if dil == 1:
            cur = pl.multiple_of(cur, BLOCK)
        q = [(load(s, cur) * SCALE).astype(BF16) for s in range(SLABS)]
        k = [load(SLABS + s, cur).astype(BF16) for s in range(SLABS)]
        v = [load(2 * SLABS + s, cur).astype(BF16) for s in range(SLABS)]
        if n_qb > 1:
            prev = r + span * jnp.maximum(qb - 1, 0)
            if dil == 1:
                prev = pl.multiple_of(prev, BLOCK)
            k = [jnp.concatenate([load(SLABS + s, prev).astype(BF16), k[s]], axis=0) for s in range(SLABS)]
            v = [jnp.concatenate([load(2 * SLABS + s, prev).astype(BF16), v[s]], axis=0) for s in range(SLABS)]
            mask = (ki >= qi) & (ki <= qi + BLOCK) & (ki >= jnp.where(qb > 0, 0, BLOCK))
        else:
            mask = ki <= qi
        outs, lses = _band_block(q, k, v, mask)
        for s in range(SLABS):
            o_scr[g, s, rows(cur), :] = outs[s]
            lse_scr[g, s, rows(cur), :] = lses[s]
        return carry

    lax.fori_loop(0, dil * n_qb, body, 0, unroll=BLOCK_UNROLL)


def _prompt_attn_kernel(qkv_ref, o_ref, o_scr, lse_scr):
    g = pl.program_id(1)
    seq = o_ref.shape[1]
    for gi, (_, dil) in enumerate(DIL_GROUPS):
        @pl.when(g == gi)
        def _(gi=gi, dil=dil):
            _group_attention(qkv_ref, o_scr, lse_scr, gi, dil, seq)

    @pl.when(g == N_GROUPS - 1)
    def _():
        def merge(i, carry):
            rows = pl.ds(pl.multiple_of(i * BLOCK, BLOCK), BLOCK)
            for s in range(SLABS):
                ls = [lse_scr[gi, s, rows, :] for gi in range(N_GROUPS)]
                top = jnp.maximum(jnp.maximum(ls[0], ls[1]), ls[2])
                e = [jnp.exp(x - top) for x in ls]
                num = e[0] * o_scr[0, s, rows, :] + e[1] * o_scr[1, s, rows, :] + e[2] * o_scr[2, s, rows, :]
                o_ref[0, rows, s * LANES:(s + 1) * LANES] = (num / (e[0] + e[1] + e[2])).astype(o_ref.dtype)
            return carry
        lax.fori_loop(0, seq // BLOCK, merge, 0)


def _prompt_attention(qkv, seq):
    batch = qkv.shape[0]
    scratch = pltpu.VMEM((N_GROUPS, SLABS, seq, LANES), F32)
    return pl.pallas_call(
        _prompt_attn_kernel,
        grid=(batch, N_GROUPS),
        in_specs=[pl.BlockSpec((1, 1, 3 * SLABS, seq, LANES), lambda b, g: (b, g, 0, 0, 0))],
        out_specs=pl.BlockSpec((1, seq, GROUP_W), lambda b, g: (b, 0, 0)),
        out_shape=jax.ShapeDtypeStruct((batch, seq, GROUP_W), BF16),
        scratch_shapes=[scratch, scratch],
        compiler_params=_params(2),
        name="prompt_attn",
    )(qkv)


def _sample_attn_kernel(q_ref, kn_ref, vn_ref, c1_ref, c2_ref, c3_ref, o_ref):
    rows = HEADS * T_PAD
    row = lax.broadcasted_iota(jnp.int32, (rows, GROUP_W), 0)
    lane = lax.broadcasted_iota(jnp.int32, (rows, GROUP_W), 1)
    head_sel = (row >> T_SHIFT) == (lane >> HEAD_SHIFT)

    def key_masks(width, dil):
        t = lax.broadcasted_iota(jnp.int32, (rows, width), 0) & (T_PAD - 1)
        i = lax.broadcasted_iota(jnp.int32, (rows, width), 1)
        tn = lax.broadcasted_iota(jnp.int32, (rows, T_PAD), 0) & (T_PAD - 1)
        j = lax.broadcasted_iota(jnp.int32, (rows, T_PAD), 1)
        return (((i - t) & (dil - 1)) == 0) & (i >= t), (((tn - j) & (dil - 1)) == 0) & (j <= tn)

    caches = tuple(zip((c1_ref, c2_ref, c3_ref), (dil for _, dil in DIL_GROUPS)))
    masks = [key_masks(c.shape[2], dil) for c, dil in caches]
    nt = (((1,), (1,)), ((), ()))

    for d in range(o_ref.shape[0]):
        outs, lses = [], []
        for g, (c_ref, _) in enumerate(caches):
            gs = slice(g * GROUP_W, (g + 1) * GROUP_W)
            q8 = q_ref[d, :, gs] * SCALE
            qm = jnp.where(head_sel, jnp.concatenate([q8] * HEADS, axis=0), 0.0).astype(BF16)
            kn = kn_ref[d, :, gs].astype(BF16)
            vn = vn_ref[d, :, gs].astype(BF16)
            kt = c_ref[d, :GROUP_W, :].astype(BF16)
            vt = c_ref[d, GROUP_W:, :].astype(BF16)
            sc = jnp.dot(qm, kt, preferred_element_type=F32)
            sn = lax.dot_general(qm, kn, nt, preferred_element_type=F32)
            mc, mn = masks[g]
            sc = jnp.where(mc, sc, NEG)
            sn = jnp.where(mn, sn, NEG)
            m = jnp.maximum(jnp.max(sc, axis=-1, keepdims=True), jnp.max(sn, axis=-1, keepdims=True))
            pc = jnp.exp(sc - m)
            pn = jnp.exp(sn - m)
            l = jnp.sum(pc, axis=-1, keepdims=True) + jnp.sum(pn, axis=-1, keepdims=True)
            o = (lax.dot_general(pc.astype(BF16), vt, nt, preferred_element_type=F32)
                 + jnp.dot(pn.astype(BF16), vn, preferred_element_type=F32))
            outs.append(o / l)
            lses.append(m + jnp.log(l))
        top = jnp.maximum(jnp.maximum(lses[0], lses[1]), lses[2])
        w = [jnp.exp(x - top) for x in lses]
        merged = (w[0] * outs[0] + w[1] * outs[1] + w[2] * outs[2]) / (w[0] + w[1] + w[2])
        merged = jnp.where(head_sel, merged, 0.0)
        o8 = merged[0:T_PAD]
        for h in range(1, HEADS):
            o8 = o8 + merged[h * T_PAD:(h + 1) * T_PAD]
        o_ref[d] = o8


def _sample_attention(q8, kn8, vn8, caches):
    db = q8.shape[0]
    tile = SAMPLE_DB_TILE
    blk = lambda a: pl.BlockSpec((tile,) + a.shape[1:], lambda i: (i, 0, 0))
    return pl.pallas_call(
        _sample_attn_kernel,
        grid=(db // tile,),
        in_specs=[blk(q8), blk(kn8), blk(vn8)] + [blk(a) for a in caches],
        out_specs=pl.BlockSpec((tile, T_PAD, GROUP_W), lambda i: (i, 0, 0)),
        out_shape=jax.ShapeDtypeStruct((db, T_PAD, GROUP_W), F32),
        compiler_params=_params(1),
        name="sample_attn",
    )(q8, kn8, vn8, *caches)


def _mix_tail(o_a, o_b, ga_ref, gb_ref, x_ref, wao_ref, wbo_ref, wout_ref, gpost_ref, x1_ref):
    a = jnp.dot(o_a.astype(BF16), wao_ref[...], preferred_element_type=F32)
    b = jnp.dot(o_b.astype(BF16), wbo_ref[...], preferred_element_type=F32)
    merged = ga_ref[...].astype(F32) * a + gb_ref[...].astype(F32) * b
    t = jnp.dot(merged.astype(BF16), wout_ref[...], preferred_element_type=F32)
    x1_ref[...] = x_ref[...] + _rms(t, gpost_ref[...])


def _mix_prompt_kernel(oa_ref, z1_ref, zn_ref, ga_ref, gb_ref, x_ref,
                       wsp_ref, bsp_ref, wao_ref, wbo_ref, wout_ref, gpost_ref, x1_ref):
    r = lax.broadcasted_iota(jnp.int32, (CHUNK, CHUNK), 0)
    c = lax.broadcasted_iota(jnp.int32, (CHUNK, CHUNK), 1)
    tm = z1_ref.shape[0]
    chunks = []
    for ci in range(tm // CHUNK):
        rows = slice(ci * CHUNK, (ci + 1) * CHUNK)
        cols = []
        for g in range(SG_GROUPS):
            gl = slice(g * CHUNK, (g + 1) * CHUNK)
            wt = jnp.where(r >= c, wsp_ref[g], 0.0).astype(BF16)
            cols.append(jnp.dot(wt, zn_ref[rows, gl].astype(BF16), preferred_element_type=F32))
        chunks.append(jnp.concatenate(cols, axis=1) + bsp_ref[...])
    o_b = z1_ref[...] * jnp.concatenate(chunks, axis=0)
    _mix_tail(oa_ref[...], o_b, ga_ref, gb_ref, x_ref, wao_ref, wbo_ref, wout_ref, gpost_ref, x1_ref)


def _mix_sample_kernel(oa_ref, z1_ref, zn_ref, ga_ref, gb_ref, x_ref, coef_ref, bsp_ref,
                       wao_ref, wbo_ref, wout_ref, gpost_ref, x1_ref):
    zn = zn_ref[...]
    mix = coef_ref[0] * zn
    for d in range(1, coef_ref.shape[0]):
        mix = mix + coef_ref[d] * pltpu.roll(zn, d, axis=0)
    o_b = z1_ref[...] * (mix + bsp_ref[...])
    _mix_tail(oa_ref[...], o_b, ga_ref, gb_ref, x_ref, wao_ref, wbo_ref, wout_ref, gpost_ref, x1_ref)


def _mix_weight_specs():
    return [_const_spec((GROUP_W, D_MODEL)), _const_spec((SG_W, D_MODEL)), _const_spec((D_MODEL, D_MODEL)),
            _const_spec((1, D_MODEL))]


def _mix_prompt(o_a, z1, zn, ga, gb, x, wsp, bsp, wao, wbo, wout, gpost):
    n = x.shape[0]
    tm = TOKEN_TILE
    row = lambda w: pl.BlockSpec((tm, w), lambda i: (i, 0))
    return pl.pallas_call(
        _mix_prompt_kernel,
        grid=(n // tm,),
        in_specs=[row(GROUP_W), row(SG_W), row(SG_W), row(D_MODEL), row(D_MODEL), row(D_MODEL),
                  _const_spec((SG_GROUPS, CHUNK, CHUNK)), _const_spec((CHUNK, SG_W))]
        + _mix_weight_specs(),
        out_specs=row(D_MODEL),
        out_shape=jax.ShapeDtypeStruct((n, D_MODEL), F32),
        compiler_params=_params(1),
        name="mix_prompt",
    )(o_a, z1, zn, ga, gb, x, wsp, bsp, wao, wbo, wout, gpost)


def _mix_sample(o_a, z1, zn, ga, gb, x, coef, bsp, wao, wbo, wout, gpost):
    n = x.shape[0]
    full = lambda a: _const_spec(a.shape)
    return pl.pallas_call(
        _mix_sample_kernel,
        grid=(1,),
        in_specs=[full(o_a), full(z1), full(zn), full(ga), full(gb), full(x), full(coef), full(bsp)]
        + _mix_weight_specs(),
        out_specs=_const_spec((n, D_MODEL)),
        out_shape=jax.ShapeDtypeStruct((n, D_MODEL), F32),
        compiler_params=_params(1),
        name="mix_sample",
    )(o_a, z1, zn, ga, gb, x, coef, bsp, wao, wbo, wout, gpost)


FF_CHUNK = 1024


def _ffn_kernel(x_ref, gpre_ref, wup_ref, wdn_ref, gpost_ref, y_ref):
    x = x_ref[...]
    h = _rms(x, gpre_ref[...]).astype(BF16)
    f = jnp.zeros(x.shape, F32)
    for c in range(D_FF // FF_CHUNK):
        cs = slice(c * FF_CHUNK, (c + 1) * FF_CHUNK)
        u = jnp.maximum(jnp.dot(h, wup_ref[:, cs], preferred_element_type=F32), 0.0)
        f = f + jnp.dot((u * u).astype(BF16), wdn_ref[cs, :], preferred_element_type=F32)
    y_ref[...] = x + _rms(f, gpost_ref[...])


def _ffn_weight_specs():
    return [_const_spec((1, D_MODEL)), _const_spec((D_MODEL, D_FF)), _const_spec((D_FF, D_MODEL)),
            _const_spec((1, D_MODEL))]


def _ffn(x, gpre, wup, wdn, gpost):
    n = x.shape[0]
    tm = min(TOKEN_TILE, n)
    row = pl.BlockSpec((tm, D_MODEL), lambda i: (i, 0))
    return pl.pallas_call(
        _ffn_kernel,
        grid=(n // tm,),
        in_specs=[row] + _ffn_weight_specs(),
        out_specs=row,
        out_shape=jax.ShapeDtypeStruct((n, D_MODEL), F32),
        compiler_params=_params(1),
        name="ffn",
    )(x, gpre, wup, wdn, gpost)


def _ffn_sample_attn_kernel(x_ref, gpre_ref, wup_ref, wdn_ref, gpost_ref,
                            q_ref, kn_ref, vn_ref, c1_ref, c2_ref, c3_ref, y_ref, o_ref):
    _sample_attn_kernel(q_ref, kn_ref, vn_ref, c1_ref, c2_ref, c3_ref, o_ref)
    _ffn_kernel(x_ref, gpre_ref, wup_ref, wdn_ref, gpost_ref, y_ref)


def _ffn_with_sample_attention(x, gpre, wup, wdn, gpost, q8, kn8, vn8, caches):
    n, db = x.shape[0], q8.shape[0]
    tile = SAMPLE_DB_TILE
    steps = db // tile
    tm = n // steps
    assert db % tile == 0 and n % steps == 0 and tm % 8 == 0
    row = pl.BlockSpec((tm, D_MODEL), lambda i: (i, 0))
    blk = lambda a: pl.BlockSpec((tile,) + a.shape[1:], lambda i: (i, 0, 0))
    return pl.pallas_call(
        _ffn_sample_attn_kernel,
        grid=(steps,),
        in_specs=[row] + _ffn_weight_specs() + [blk(q8), blk(kn8), blk(vn8)] + [blk(a) for a in caches],
        out_specs=[row, pl.BlockSpec((tile, T_PAD, GROUP_W), lambda i: (i, 0, 0))],
        out_shape=[jax.ShapeDtypeStruct((n, D_MODEL), F32),
                   jax.ShapeDtypeStruct((db, T_PAD, GROUP_W), F32)],
        compiler_params=_params(1),
        name="ffn_sample_attn",
    )(x, gpre, wup, wdn, gpost, q8, kn8, vn8, *caches)


def _kv_from_feature_major(kv_t):
    batch, _, keep = kv_t.shape
    return kv_t.reshape(batch, 2, HEADS, HEAD_DIM, keep).transpose(0, 4, 1, 2, 3)[None]


def kernel(x_prompt, x_sample, cache_kv_w128, cache_kv_w512, cache_kv_w2048, norm_pre_mix, w_in, b_gate,
           ln_z_g, ln_z_b, w_spatial, b_spatial, w_ao, w_bo, w_out, norm_post_mix, norm_pre_ffn, w_up, w_down,
           norm_post_ffn):
    assert w_in.shape[0] == 1, "single-layer problem"
    batch, seq, _ = x_prompt.shape
    dbatch, t_new, _ = x_sample.shape
    caches_in = (cache_kv_w128[0], cache_kv_w512[0], cache_kv_w2048[0])

    w_in_bf = w_in[0].astype(BF16)
    wao, wbo, wout = w_ao[0].astype(BF16), w_bo[0].astype(BF16), w_out[0].astype(BF16)
    wup, wdn = w_up[0].astype(BF16), w_down[0].astype(BF16)
    wsp = w_spatial[0]
    bsp_rows = jnp.repeat(b_spatial[0].T, CHUNK, axis=1)

    ns = dbatch * t_new
    xs = x_sample.reshape(ns, D_MODEL)
    qkv_s, z1_s, zn_s, ga_s, gb_s = _project_sample(xs, norm_pre_mix, w_in_bf, b_gate, ln_z_g, ln_z_b)
    pad = lambda a: jnp.pad(a.reshape(dbatch, t_new, ATTN_W), ((0, 0), (0, T_PAD - t_new), (0, 0)))
    q8, kn8, vn8 = pad(qkv_s[:, :ATTN_W]), pad(qkv_s[:, ATTN_W:2 * ATTN_W]), pad(qkv_s[:, 2 * ATTN_W:])
    caches, kv_sample = [], []
    for gi, (win, dil) in enumerate(DIL_GROUPS):
        buf = caches_in[gi]
        wb = buf.shape[1]
        assert wb == win and t_new <= T_PAD, "full window buffers"
        caches.append(buf.transpose(0, 2, 3, 4, 1).reshape(dbatch, 2 * GROUP_W, wb))
        gs = slice(gi * GROUP_W, (gi + 1) * GROUP_W)
        kn = qkv_s[:, ATTN_W:2 * ATTN_W][:, gs].reshape(dbatch, t_new, HEADS, HEAD_DIM)
        vn = qkv_s[:, 2 * ATTN_W:][:, gs].reshape(dbatch, t_new, HEADS, HEAD_DIM)
        kv_sample.append(jnp.stack([kn, vn], axis=2)[None])

    xp = x_prompt.reshape(batch * seq, D_MODEL)
    qkv, kv1_t, kv2_t, kv3_t, z1, zn, ga, gb = _project_prompt(
        x_prompt, norm_pre_mix, w_in_bf, b_gate, ln_z_g, ln_z_b)
    kv_prompt = [_kv_from_feature_major(a) for a in (kv1_t, kv2_t, kv3_t)]
    o_a_p = _prompt_attention(qkv, seq).reshape(batch * seq, GROUP_W)
    x1 = _mix_prompt(o_a_p, z1, zn, ga, gb, xp, wsp, bsp_rows, wao, wbo, wout, norm_post_mix)
    y_prompt, o_a8 = _ffn_with_sample_attention(x1, norm_pre_ffn, wup, wdn, norm_post_ffn, q8, kn8, vn8, caches)
    y_prompt = y_prompt.reshape(batch, seq, D_MODEL)

    o_a = o_a8[:, :t_new].reshape(ns, GROUP_W)

    wt = jnp.tril(wsp)[:, :t_new, :t_new]
    coef = jnp.stack([jnp.where(jnp.arange(t_new)[None, :] >= d,
                                wt[:, jnp.arange(t_new), jnp.maximum(jnp.arange(t_new) - d, 0)], 0.0)
                      for d in range(t_new)])
    coef = jnp.repeat(coef.transpose(0, 2, 1), CHUNK, axis=2)
    coef = jnp.tile(coef, (1, dbatch, 1))
    bsp_s = jnp.tile(bsp_rows[:t_new], (dbatch, 1))
    x1_s = _mix_sample(o_a, z1_s, zn_s, ga_s, gb_s, xs, coef, bsp_s, wao, wbo, wout, norm_post_mix)
    y_sample = _ffn(x1_s, norm_pre_ffn, wup, wdn, norm_post_ffn).reshape(dbatch, t_new, D_MODEL)
    sg_sample = zn_s.reshape(1, dbatch, t_new, SG_W)

    return (y_prompt, y_sample, kv_prompt[0], kv_prompt[1], kv_prompt[2],
            kv_sample[0], kv_sample[1], kv_sample[2], sg_sample)
```

```python
import math

import jax
import jax.numpy as jnp
from jax import lax
from jax.experimental import pallas as pl
from jax.experimental.pallas import tpu as pltpu

F32 = jnp.float32
BF16 = jnp.bfloat16

D_MODEL = 1024
HEAD_DIM = 64
HEADS = 4
GROUP_W = HEADS * HEAD_DIM
DIL_GROUPS = ((128, 1), (512, 4), (2048, 16))
N_GROUPS = len(DIL_GROUPS)
ATTN_W = N_GROUPS * GROUP_W
BLOCK = 128
CHUNK = 128
SG_GROUPS = 4
SG_W = 512
D_FF = 4 * D_MODEL
Z_OFF = 3 * ATTN_W
G_OFF = Z_OFF + 2 * SG_W
IN_W = G_OFF + 2 * D_MODEL
EPS = 1e-6
NEG = -1e30
SCALE = HEAD_DIM ** -0.5
INV_SQRT2 = 1.0 / math.sqrt(2.0)
LANES = 128
SLABS = GROUP_W // LANES
HEADS_PER_SLAB = LANES // HEAD_DIM
HEAD_SHIFT = HEAD_DIM.bit_length() - 1
VMEM_LIMIT = 56 * 1024 * 1024

TOKEN_TILE = 512
BLOCK_UNROLL = 4
SAMPLE_DB_TILE = 2
T_PAD = 8
T_SHIFT = T_PAD.bit_length() - 1
SAMPLE_TAIL_TILE = 128
FF_CHUNK = 1024


def _params(n_axes):
    return pltpu.CompilerParams(dimension_semantics=("arbitrary",) * n_axes,
                                vmem_limit_bytes=VMEM_LIMIT)


def _const_spec(shape):
    return pl.BlockSpec(shape, lambda *_: (0,) * len(shape), pipeline_mode=pl.Buffered(1))


def _rms(x, g):
    return x * lax.rsqrt(jnp.mean(x * x, axis=-1, keepdims=True) + EPS) * g


def _proj_gate_branches(h, w_ref, bg_ref, lng_ref, lnb_ref, z1_ref, zn_ref, ga_ref, gb_ref):
    z = jnp.dot(h, w_ref[:, Z_OFF:G_OFF], preferred_element_type=F32)
    z = 0.5 * z * (1.0 + lax.erf(z * INV_SQRT2))
    z1_ref[...] = z[:, :SG_W].astype(z1_ref.dtype)
    z2 = z[:, SG_W:]
    mu = jnp.mean(z2, axis=-1, keepdims=True)
    zc = z2 - mu
    var = jnp.mean(zc * zc, axis=-1, keepdims=True)
    zn_ref[...] = (zc * lax.rsqrt(var + EPS) * lng_ref[...] + lnb_ref[...]).astype(zn_ref.dtype)

    gates = jnp.dot(h, w_ref[:, G_OFF:], preferred_element_type=F32) + bg_ref[...]
    gates = 1.0 / (1.0 + jnp.exp(-gates))
    ga_ref[...] = gates[:, :D_MODEL].astype(BF16)
    gb_ref[...] = gates[:, D_MODEL:].astype(BF16)


def _proj_sample_kernel(x_ref, g_ref, w_ref, bg_ref, lng_ref, lnb_ref,
                        qkv_ref, z1_ref, zn_ref, ga_ref, gb_ref):
    h = _rms(x_ref[...], g_ref[...]).astype(BF16)
    qkv_ref[...] = jnp.dot(h, w_ref[:, :Z_OFF], preferred_element_type=F32)
    _proj_gate_branches(h, w_ref, bg_ref, lng_ref, lnb_ref, z1_ref, zn_ref, ga_ref, gb_ref)


def _proj_prompt_kernel(x_ref, g_ref, w_ref, bg_ref, lng_ref, lnb_ref,
                        qkv_ref, kv1_ref, kv2_ref, kv3_ref, z1_ref, zn_ref, ga_ref, gb_ref):
    h = _rms(x_ref[0], g_ref[...]).astype(BF16)
    qkv = jnp.dot(h, w_ref[:, :Z_OFF], preferred_element_type=F32)
    for g in range(N_GROUPS):
        for part in range(3):
            for s in range(SLABS):
                col = part * ATTN_W + g * GROUP_W + s * LANES
                qkv_ref[0, g, part * SLABS + s] = qkv[:, col:col + LANES]

    def kv_t(g):
        return jnp.concatenate([qkv_ref[0, g, slab].T for slab in range(SLABS, 3 * SLABS)], axis=0)

    tm = x_ref.shape[1]
    kv3_ref[0] = kv_t(2)
    _proj_gate_branches(h, w_ref, bg_ref, lng_ref, lnb_ref, z1_ref, zn_ref, ga_ref, gb_ref)

    @pl.when(pl.program_id(1) == pl.num_programs(1) - 1)
    def _():
        kv2_ref[0] = kv_t(1)[:, tm - kv2_ref.shape[2]:]
        kv1_ref[0] = kv_t(0)[:, tm - kv1_ref.shape[2]:]


def _proj_weight_specs():
    return [_const_spec((1, D_MODEL)), _const_spec((D_MODEL, IN_W)),
            _const_spec((1, 2 * D_MODEL)), _const_spec((1, SG_W)), _const_spec((1, SG_W))]


def _gate_branch_shapes(n, z_dtype):
    return [jax.ShapeDtypeStruct((n, SG_W), z_dtype), jax.ShapeDtypeStruct((n, SG_W), z_dtype),
            jax.ShapeDtypeStruct((n, D_MODEL), BF16), jax.ShapeDtypeStruct((n, D_MODEL), BF16)]


def _project_sample(x, g, w_bf, bg, lng, lnb):
    n = x.shape[0]
    tm = min(SAMPLE_TAIL_TILE, n)
    row = lambda w: pl.BlockSpec((tm, w), lambda i: (i, 0))
    return pl.pallas_call(
        _proj_sample_kernel,
        grid=(n // tm,),
        in_specs=[row(D_MODEL)] + _proj_weight_specs(),
        out_specs=[row(Z_OFF), row(SG_W), row(SG_W), row(D_MODEL), row(D_MODEL)],
        out_shape=[jax.ShapeDtypeStruct((n, Z_OFF), F32)] + _gate_branch_shapes(n, F32),
        compiler_params=_params(1),
        name="proj_sample",
    )(x, g, w_bf, bg, lng, lnb)


def _project_prompt(x, g, w_bf, bg, lng, lnb):
    batch, seq, _ = x.shape
    tm = TOKEN_TILE
    tiles = seq // tm
    keeps = [min(win, seq) for win, _ in DIL_GROUPS]
    assert seq % tm == 0 and keeps[0] <= tm and keeps[1] <= tm and keeps[2] == seq
    row = lambda w: pl.BlockSpec((tm, w), lambda b, t: (b * tiles + t, 0))
    last = lambda keep: pl.BlockSpec((1, 2 * GROUP_W, keep), lambda b, t: (b, 0, 0))
    n = batch * seq
    return pl.pallas_call(
        _proj_prompt_kernel,
        grid=(batch, tiles),
        in_specs=[pl.BlockSpec((1, tm, D_MODEL), lambda b, t: (b, t, 0))] + _proj_weight_specs(),
        out_specs=[pl.BlockSpec((1, N_GROUPS, 3 * SLABS, tm, LANES), lambda b, t: (b, 0, 0, t, 0)),
                   last(keeps[0]), last(keeps[1]),
                   pl.BlockSpec((1, 2 * GROUP_W, tm), lambda b, t: (b, 0, t)),
                   row(SG_W), row(SG_W), row(D_MODEL), row(D_MODEL)],
        out_shape=[jax.ShapeDtypeStruct((batch, N_GROUPS, 3 * SLABS, seq, LANES), F32),
                   jax.ShapeDtypeStruct((batch, 2 * GROUP_W, keeps[0]), F32),
                   jax.ShapeDtypeStruct((batch, 2 * GROUP_W, keeps[1]), F32),
                   jax.ShapeDtypeStruct((batch, 2 * GROUP_W, keeps[2]), F32)] + _gate_branch_shapes(n, BF16),
        compiler_params=_params(2),
        name="proj_prompt",
    )(x, g, w_bf, bg, lng, lnb)


def _head_lane_mask(shape, hh):
    lane = lax.broadcasted_iota(jnp.int32, shape, len(shape) - 1)
    return (lane >> HEAD_SHIFT) == hh


def _band_block(q, k, v, mask):
    nt = (((1,), (1,)), ((), ()))
    assert HEADS_PER_SLAB == 2
    lm0 = _head_lane_mask((BLOCK, LANES), 0)
    mask2 = jnp.concatenate([mask, mask], axis=0)
    outs, lses = [], []
    for s in range(SLABS):
        zero = jnp.zeros_like(q[s])
        qm = jnp.concatenate([jnp.where(lm0, q[s], zero), jnp.where(lm0, zero, q[s])], axis=0)
        sc = lax.dot_general(qm, k[s], nt, preferred_element_type=F32)
        sc = jnp.where(mask2, sc, NEG)
        m = jnp.max(sc, axis=-1, keepdims=True)
        p = jnp.exp(sc - m)
        l = jnp.sum(p, axis=-1, keepdims=True)
        o = jnp.dot(p.astype(BF16), v[s], preferred_element_type=F32) * (1.0 / l)
        lse = m + jnp.log(l)
        outs.append(jnp.where(lm0, o[:BLOCK], o[BLOCK:]))
        lses.append(jnp.where(lm0, lse[:BLOCK], lse[BLOCK:]))
    return outs, lses


def _group_attention(qkv_ref, o_scr, lse_scr, g, dil, seq):
    n_qb = seq // (dil * BLOCK)
    qb_shift = n_qb.bit_length() - 1
    span = dil * BLOCK
    rows = (lambda start: pl.ds(start, BLOCK, stride=dil)) if dil > 1 else (lambda start: pl.ds(start, BLOCK))
    nk = 2 * BLOCK if n_qb > 1 else BLOCK
    qi = lax.broadcasted_iota(jnp.int32, (BLOCK, nk), 0)
    ki = lax.broadcasted_iota(jnp.int32, (BLOCK, nk), 1)

    def load(slab, start):
        return qkv_ref[0, 0, slab, rows(start), :]

    def body(n, carry):
        r = n >> qb_shift
        qb = n & (n_qb - 1)
        cur = r + span * qb
        if dil == 1:
            cur = pl.multiple_of(cur, BLOCK)
        q = [(load(s, cur) * SCALE).astype(BF16) for s in range(SLABS)]
        k = [load(SLABS + s, cur).astype(BF16) for s in range(SLABS)]
        v = [load(2 * SLABS + s, cur).astype(BF16) for s in range(SLABS)]
        if n_qb > 1:
            prev = r + span * jnp.maximum(qb - 1, 0)
            if dil == 1:
                prev = pl.multiple_of(prev, BLOCK)
            k = [jnp.concatenate([load(SLABS + s, prev).astype(BF16), k[s]], axis=0) for s in range(SLABS)]
            v = [jnp.concatenate([load(2 * SLABS + s, prev).astype(BF16), v[s]], axis=0) for s in range(SLABS)]
            mask = (ki >= qi) & (ki <= qi + BLOCK) & (ki >= jnp.where(qb > 0, 0, BLOCK))
        else:
            mask = ki <= qi
        outs, lses = _band_block(q, k, v, mask)
        for s in range(SLABS):
            o_scr[g, s, rows(cur), :] = outs[s]
            lse_scr[g, s, rows(cur), :] = lses[s]
        return carry

    lax.fori_loop(0, dil * n_qb, body, 0, unroll=BLOCK_UNROLL)


def _prompt_attn_kernel(qkv_ref, o_ref, o_scr, lse_scr):
    g = pl.program_id(1)
    seq = o_ref.shape[1]
    for gi, (_, dil) in enumerate(DIL_GROUPS):
        @pl.when(g == gi)
        def _(gi=gi, dil=dil):
            _group_attention(qkv_ref, o_scr, lse_scr, gi, dil, seq)

    @pl.when(g == N_GROUPS - 1)
    def _():
        def merge(i, carry):
            rows = pl.ds(pl.multiple_of(i * BLOCK, BLOCK), BLOCK)
            for s in range(SLABS):
                ls = [lse_scr[gi, s, rows, :] for gi in range(N_GROUPS)]
                top = jnp.maximum(jnp.maximum(ls[0], ls[1]), ls[2])
                e = [jnp.exp(x - top) for x in ls]
                num = e[0] * o_scr[0, s, rows, :] + e[1] * o_scr[1, s, rows, :] + e[2] * o_scr[2, s, rows, :]
                o_ref[0, rows, s * LANES:(s + 1) * LANES] = (num / (e[0] + e[1] + e[2])).astype(o_ref.dtype)
            return carry
        lax.fori_loop(0, seq // BLOCK, merge, 0)


def _prompt_attention(qkv, seq):
    batch = qkv.shape[0]
    scratch = pltpu.VMEM((N_GROUPS, SLABS, seq, LANES), F32)
    return pl.pallas_call(
        _prompt_attn_kernel,
        grid=(batch, N_GROUPS),
        in_specs=[pl.BlockSpec((1, 1, 3 * SLABS, seq, LANES), lambda b, g: (b, g, 0, 0, 0))],
        out_specs=pl.BlockSpec((1, seq, GROUP_W), lambda b, g: (b, 0, 0)),
        out_shape=jax.ShapeDtypeStruct((batch, seq, GROUP_W), BF16),
        scratch_shapes=[scratch, scratch],
        compiler_params=_params(2),
        name="prompt_attn",
    )(qkv)


def _sample_attn_body(q_ref, kn_ref, vn_ref, c1_ref, c2_ref, c3_ref, o_ref):
    rows = HEADS * T_PAD
    row = lax.broadcasted_iota(jnp.int32, (rows, GROUP_W), 0)
    lane = lax.broadcasted_iota(jnp.int32, (rows, GROUP_W), 1)
    head_sel = (row >> T_SHIFT) == (lane >> HEAD_SHIFT)

    def key_masks(width, dil):
        t = lax.broadcasted_iota(jnp.int32, (rows, width), 0) & (T_PAD - 1)
        i = lax.broadcasted_iota(jnp.int32, (rows, width), 1)
        tn = lax.broadcasted_iota(jnp.int32, (rows, T_PAD), 0) & (T_PAD - 1)
        j = lax.broadcasted_iota(jnp.int32, (rows, T_PAD), 1)
        return (((i - t) & (dil - 1)) == 0) & (i >= t), (((tn - j) & (dil - 1)) == 0) & (j <= tn)

    caches = tuple(zip((c1_ref, c2_ref, c3_ref), (dil for _, dil in DIL_GROUPS)))
    masks = [key_masks(c.shape[2], dil) for c, dil in caches]
    nt = (((1,), (1,)), ((), ()))

    for d in range(o_ref.shape[0]):
        outs, lses = [], []
        for g, (c_ref, _) in enumerate(caches):
            gs = slice(g * GROUP_W, (g + 1) * GROUP_W)
            q8 = q_ref[d, :, gs] * SCALE
            qm = jnp.where(head_sel, jnp.concatenate([q8] * HEADS, axis=0), 0.0).astype(BF16)
            kn = kn_ref[d, :, gs].astype(BF16)
            vn = vn_ref[d, :, gs].astype(BF16)
            kt = c_ref[d, :GROUP_W, :].astype(BF16)
            vt = c_ref[d, GROUP_W:, :].astype(BF16)
            sc = jnp.dot(qm, kt, preferred_element_type=F32)
            sn = lax.dot_general(qm, kn, nt, preferred_element_type=F32)
            mc, mn = masks[g]
            sc = jnp.where(mc, sc, NEG)
            sn = jnp.where(mn, sn, NEG)
            m = jnp.maximum(jnp.max(sc, axis=-1, keepdims=True), jnp.max(sn, axis=-1, keepdims=True))
            pc = jnp.exp(sc - m)
            pn = jnp.exp(sn - m)
            l = jnp.sum(pc, axis=-1, keepdims=True) + jnp.sum(pn, axis=-1, keepdims=True)
            o = (lax.dot_general(pc.astype(BF16), vt, nt, preferred_element_type=F32)
                 + jnp.dot(pn.astype(BF16), vn, preferred_element_type=F32))
            outs.append(o / l)
            lses.append(m + jnp.log(l))
        top = jnp.maximum(jnp.maximum(lses[0], lses[1]), lses[2])
        w = [jnp.exp(x - top) for x in lses]
        merged = (w[0] * outs[0] + w[1] * outs[1] + w[2] * outs[2]) / (w[0] + w[1] + w[2])
        merged = jnp.where(head_sel, merged, 0.0)
        o8 = merged[0:T_PAD]
        for h in range(1, HEADS):
            o8 = o8 + merged[h * T_PAD:(h + 1) * T_PAD]
        o_ref[d] = o8


def _mix_value(o_a, o_b, ga_ref, gb_ref, x_ref, wao_ref, wbo_ref, wout_ref, gpost_ref):
    a = jnp.dot(o_a.astype(BF16), wao_ref[...], preferred_element_type=F32)
    b = jnp.dot(o_b.astype(BF16), wbo_ref[...], preferred_element_type=F32)
    merged = ga_ref[...].astype(F32) * a + gb_ref[...].astype(F32) * b
    t = jnp.dot(merged.astype(BF16), wout_ref[...], preferred_element_type=F32)
    return x_ref[...] + _rms(t, gpost_ref[...])


def _ffn_value(x, gpre_ref, wup_ref, wdn_ref, gpost_ref):
    h = _rms(x, gpre_ref[...]).astype(BF16)
    f = jnp.zeros(x.shape, F32)
    for c in range(D_FF // FF_CHUNK):
        cs = slice(c * FF_CHUNK, (c + 1) * FF_CHUNK)
        u = jnp.maximum(jnp.dot(h, wup_ref[:, cs], preferred_element_type=F32), 0.0)
        f = f + jnp.dot((u * u).astype(BF16), wdn_ref[cs, :], preferred_element_type=F32)
    return x + _rms(f, gpost_ref[...])


def _prompt_spatial_gate(z1_ref, zn_ref, wsp_ref, bsp_ref):
    r = lax.broadcasted_iota(jnp.int32, (CHUNK, CHUNK), 0)
    c = lax.broadcasted_iota(jnp.int32, (CHUNK, CHUNK), 1)
    wts = [jnp.where(r >= c, wsp_ref[g], 0.0).astype(BF16) for g in range(SG_GROUPS)]
    chunks = []
    for ci in range(z1_ref.shape[0] // CHUNK):
        rows = slice(ci * CHUNK, (ci + 1) * CHUNK)
        cols = [jnp.dot(wts[g], zn_ref[rows, g * CHUNK:(g + 1) * CHUNK].astype(BF16), preferred_element_type=F32)
                for g in range(SG_GROUPS)]
        chunks.append(jnp.concatenate(cols, axis=1) + bsp_ref[...])
    return z1_ref[...].astype(F32) * jnp.concatenate(chunks, axis=0)


def _tail_prompt_kernel(oa_ref, z1_ref, zn_ref, ga_ref, gb_ref, x_ref, wsp_ref, bsp_ref,
                        wao_ref, wbo_ref, wout_ref, gmix_ref, gpre_ref, wup_ref, wdn_ref, gpost_ref,
                        q_ref, kn_ref, vn_ref, c1_ref, c2_ref, c3_ref, y_ref, o_ref):
    _sample_attn_body(q_ref, kn_ref, vn_ref, c1_ref, c2_ref, c3_ref, o_ref)
    o_b = _prompt_spatial_gate(z1_ref, zn_ref, wsp_ref, bsp_ref)
    x1 = _mix_value(oa_ref[...], o_b, ga_ref, gb_ref, x_ref, wao_ref, wbo_ref, wout_ref, gmix_ref)
    y_ref[...] = _ffn_value(x1, gpre_ref, wup_ref, wdn_ref, gpost_ref)


def _tail_sample_kernel(oa_ref, z1_ref, zn_ref, ga_ref, gb_ref, x_ref, coef_ref, bsp_ref,
                        wao_ref, wbo_ref, wout_ref, gmix_ref, gpre_ref, wup_ref, wdn_ref, gpost_ref, y_ref):
    zn = zn_ref[...]
    mix = coef_ref[0] * zn
    for d in range(1, coef_ref.shape[0]):
        mix = mix + coef_ref[d] * pltpu.roll(zn, d, axis=0)
    o_b = z1_ref[...] * (mix + bsp_ref[...])
    x1 = _mix_value(oa_ref[...], o_b, ga_ref, gb_ref, x_ref, wao_ref, wbo_ref, wout_ref, gmix_ref)
    y_ref[...] = _ffn_value(x1, gpre_ref, wup_ref, wdn_ref, gpost_ref)


def _tail_weight_specs():
    return [_const_spec((GROUP_W, D_MODEL)), _const_spec((SG_W, D_MODEL)), _const_spec((D_MODEL, D_MODEL)),
            _const_spec((1, D_MODEL)), _const_spec((1, D_MODEL)), _const_spec((D_MODEL, D_FF)),
            _const_spec((D_FF, D_MODEL)), _const_spec((1, D_MODEL))]


def _tail_prompt(o_a, z1, zn, ga, gb, x, wsp, bsp, weights, q8, kn8, vn8, caches):
    n, db = x.shape[0], q8.shape[0]
    tile = SAMPLE_DB_TILE
    steps = db // tile
    tm = n // steps
    assert db % tile == 0 and n % steps == 0 and tm % CHUNK == 0
    row = lambda w: pl.BlockSpec((tm, w), lambda i: (i, 0))
    blk = lambda a: pl.BlockSpec((tile,) + a.shape[1:], lambda i: (i, 0, 0))
    return pl.pallas_call(
        _tail_prompt_kernel,
        grid=(steps,),
        in_specs=[row(GROUP_W), row(SG_W), row(SG_W), row(D_MODEL), row(D_MODEL), row(D_MODEL),
                  _const_spec((SG_GROUPS, CHUNK, CHUNK)), _const_spec((CHUNK, SG_W))] + _tail_weight_specs()
        + [blk(q8), blk(kn8), blk(vn8)] + [blk(a) for a in caches],
        out_specs=[row(D_MODEL), pl.BlockSpec((tile, T_PAD, GROUP_W), lambda i: (i, 0, 0))],
        out_shape=[jax.ShapeDtypeStruct((n, D_MODEL), F32),
                   jax.ShapeDtypeStruct((db, T_PAD, GROUP_W), F32)],
        compiler_params=_params(1),
        name="tail_prompt",
    )(o_a, z1, zn, ga, gb, x, wsp, bsp, *weights, q8, kn8, vn8, *caches)


def _tail_sample(o_a, z1, zn, ga, gb, x, coef, bsp, weights):
    n = x.shape[0]
    tm = coef.shape[1]
    row = lambda w: pl.BlockSpec((tm, w), lambda i: (i, 0))
    return pl.pallas_call(
        _tail_sample_kernel,
        grid=(n // tm,),
        in_specs=[row(GROUP_W), row(SG_W), row(SG_W), row(D_MODEL), row(D_MODEL), row(D_MODEL),
                  _const_spec(coef.shape), _const_spec(bsp.shape)] + _tail_weight_specs(),
        out_specs=row(D_MODEL),
        out_shape=jax.ShapeDtypeStruct((n, D_MODEL), F32),
        compiler_params=_params(1),
        name="tail_sample",
    )(o_a, z1, zn, ga, gb, x, coef, bsp, *weights)


def _kv_from_feature_major(kv_t):
    batch, _, keep = kv_t.shape
    return kv_t.reshape(batch, 2, HEADS, HEAD_DIM, keep).transpose(0, 4, 1, 2, 3)[None]


def kernel(x_prompt, x_sample, cache_kv_w128, cache_kv_w512, cache_kv_w2048, norm_pre_mix, w_in, b_gate,
           ln_z_g, ln_z_b, w_spatial, b_spatial, w_ao, w_bo, w_out, norm_post_mix, norm_pre_ffn, w_up, w_down,
           norm_post_ffn):
    assert w_in.shape[0] == 1, "single-layer problem"
    batch, seq, _ = x_prompt.shape
    dbatch, t_new, _ = x_sample.shape
    caches_in = (cache_kv_w128[0], cache_kv_w512[0], cache_kv_w2048[0])

    w_in_bf = w_in[0].astype(BF16)
    wao, wbo, wout = w_ao[0].astype(BF16), w_bo[0].astype(BF16), w_out[0].astype(BF16)
    wup, wdn = w_up[0].astype(BF16), w_down[0].astype(BF16)
    weights = (wao, wbo, wout, norm_post_mix, norm_pre_ffn, wup, wdn, norm_post_ffn)
    wsp = w_spatial[0]
    bsp_rows = jnp.repeat(b_spatial[0].T, CHUNK, axis=1)

    ns = dbatch * t_new
    xs = x_sample.reshape(ns, D_MODEL)
    qkv_s, z1_s, zn_s, ga_s, gb_s = _project_sample(xs, norm_pre_mix, w_in_bf, b_gate, ln_z_g, ln_z_b)
    pad = lambda a: jnp.pad(a.reshape(dbatch, t_new, ATTN_W), ((0, 0), (0, T_PAD - t_new), (0, 0)))
    q8, kn8, vn8 = pad(qkv_s[:, :ATTN_W]), pad(qkv_s[:, ATTN_W:2 * ATTN_W]), pad(qkv_s[:, 2 * ATTN_W:])
    caches, kv_sample = [], []
    for gi, (win, dil) in enumerate(DIL_GROUPS):
        buf = caches_in[gi]
        wb = buf.shape[1]
        assert wb == win and t_new <= T_PAD, "full window buffers"
        caches.append(buf.transpose(0, 2, 3, 4, 1).reshape(dbatch, 2 * GROUP_W, wb))
        gs = slice(gi * GROUP_W, (gi + 1) * GROUP_W)
        kn = qkv_s[:, ATTN_W:2 * ATTN_W][:, gs].reshape(dbatch, t_new, HEADS, HEAD_DIM)
        vn = qkv_s[:, 2 * ATTN_W:][:, gs].reshape(dbatch, t_new, HEADS, HEAD_DIM)
        kv_sample.append(jnp.stack([kn, vn], axis=2)[None])

    xp = x_prompt.reshape(batch * seq, D_MODEL)
    qkv, kv1_t, kv2_t, kv3_t, z1, zn, ga, gb = _project_prompt(
        x_prompt, norm_pre_mix, w_in_bf, b_gate, ln_z_g, ln_z_b)
    kv_prompt = [_kv_from_feature_major(a) for a in (kv1_t, kv2_t, kv3_t)]
    o_a_p = _prompt_attention(qkv, seq).reshape(batch * seq, GROUP_W)
    y_prompt, o_a8 = _tail_prompt(o_a_p, z1, zn, ga, gb, xp, wsp, bsp_rows, weights, q8, kn8, vn8, caches)
    y_prompt = y_prompt.reshape(batch, seq, D_MODEL)

    o_a = o_a8[:, :t_new].reshape(ns, GROUP_W)
    wt = jnp.tril(wsp)[:, :t_new, :t_new]
    coef = jnp.stack([jnp.where(jnp.arange(t_new)[None, :] >= d,
                                wt[:, jnp.arange(t_new), jnp.maximum(jnp.arange(t_new) - d, 0)], 0.0)
                      for d in range(t_new)])
    coef = jnp.repeat(coef.transpose(0, 2, 1), CHUNK, axis=2)
    reps = SAMPLE_TAIL_TILE // t_new
    coef = jnp.tile(coef, (1, reps, 1))
    bsp_s = jnp.tile(bsp_rows[:t_new], (reps, 1))
    y_sample = _tail_sample(o_a, z1_s, zn_s, ga_s, gb_s, xs, coef, bsp_s, weights).reshape(dbatch, t_new, D_MODEL)
    sg_sample = zn_s.reshape(1, dbatch, t_new, SG_W)

    return (y_prompt, y_sample, kv_prompt[0], kv_prompt[1], kv_prompt[2],
            kv_sample[0], kv_sample[1], kv_sample[2], sg_sample)
```

```python
import math

import jax
import jax.numpy as jnp
from jax import lax
from jax.experimental import pallas as pl
from jax.experimental.pallas import tpu as pltpu

F32 = jnp.float32
BF16 = jnp.bfloat16

D_MODEL = 1024
HEAD_DIM = 64
HEADS = 4
GROUP_W = HEADS * HEAD_DIM
DIL_GROUPS = ((128, 1), (512, 4), (2048, 16))
N_GROUPS = len(DIL_GROUPS)
ATTN_W = N_GROUPS * GROUP_W
BLOCK = 128
CHUNK = 128
SG_GROUPS = 4
SG_W = 512
D_FF = 4 * D_MODEL
Z_OFF = 3 * ATTN_W
G_OFF = Z_OFF + 2 * SG_W
IN_W = G_OFF + 2 * D_MODEL
EPS = 1e-6
NEG = -1e30
SCALE = HEAD_DIM ** -0.5
INV_SQRT2 = 1.0 / math.sqrt(2.0)
LANES = 128
SLABS = GROUP_W // LANES
HEADS_PER_SLAB = LANES // HEAD_DIM
HEAD_SHIFT = HEAD_DIM.bit_length() - 1
VMEM_LIMIT = 56 * 1024 * 1024

TOKEN_TILE = 512
BLOCK_UNROLL = 4
SAMPLE_DB_TILE = 2
T_PAD = 8
T_SHIFT = T_PAD.bit_length() - 1
SAMPLE_TAIL_TILE = 128
FF_CHUNK = 1024


def _params(n_axes):
    return pltpu.CompilerParams(dimension_semantics=("arbitrary",) * n_axes,
                                vmem_limit_bytes=VMEM_LIMIT)


def _const_spec(shape):
    return pl.BlockSpec(shape, lambda *_: (0,) * len(shape), pipeline_mode=pl.Buffered(1))


def _rms(x, g):
    return x * lax.rsqrt(jnp.mean(x * x, axis=-1, keepdims=True) + EPS) * g


def _proj_gate_branches(h, w_ref, bg_ref, lng_ref, lnb_ref, z1_ref, zn_ref, ga_ref, gb_ref):
    z = jnp.dot(h, w_ref[:, Z_OFF:G_OFF], preferred_element_type=F32)
    z = 0.5 * z * (1.0 + lax.erf(z * INV_SQRT2))
    z1_ref[...] = z[:, :SG_W].astype(z1_ref.dtype)
    z2 = z[:, SG_W:]
    mu = jnp.mean(z2, axis=-1, keepdims=True)
    zc = z2 - mu
    var = jnp.mean(zc * zc, axis=-1, keepdims=True)
    zn_ref[...] = (zc * lax.rsqrt(var + EPS) * lng_ref[...] + lnb_ref[...]).astype(zn_ref.dtype)

    gates = jnp.dot(h, w_ref[:, G_OFF:], preferred_element_type=F32) + bg_ref[...]
    gates = 1.0 / (1.0 + jnp.exp(-gates))
    ga_ref[...] = gates[:, :D_MODEL].astype(BF16)
    gb_ref[...] = gates[:, D_MODEL:].astype(BF16)


def _proj_sample_kernel(x_ref, g_ref, w_ref, bg_ref, lng_ref, lnb_ref,
                        qkv_ref, z1_ref, zn_ref, ga_ref, gb_ref):
    h = _rms(x_ref[...], g_ref[...]).astype(BF16)
    qkv_ref[...] = jnp.dot(h, w_ref[:, :Z_OFF], preferred_element_type=F32)
    _proj_gate_branches(h, w_ref, bg_ref, lng_ref, lnb_ref, z1_ref, zn_ref, ga_ref, gb_ref)


def _proj_prompt_kernel(x_ref, g_ref, w_ref, bg_ref, lng_ref, lnb_ref,
                        qkv_ref, kv1_ref, kv2_ref, kv3_ref, z1_ref, zn_ref, ga_ref, gb_ref):
    h = _rms(x_ref[0], g_ref[...]).astype(BF16)
    qkv = jnp.dot(h, w_ref[:, :Z_OFF], preferred_element_type=F32)
    for g in range(N_GROUPS):
        for part in range(3):
            for s in range(SLABS):
                col = part * ATTN_W + g * GROUP_W + s * LANES
                qkv_ref[0, g, part * SLABS + s] = qkv[:, col:col + LANES]

    def kv_t(g):
        return jnp.concatenate([qkv_ref[0, g, slab].T for slab in range(SLABS, 3 * SLABS)], axis=0)

    tm = x_ref.shape[1]
    kv3_ref[0] = kv_t(2)
    _proj_gate_branches(h, w_ref, bg_ref, lng_ref, lnb_ref, z1_ref, zn_ref, ga_ref, gb_ref)

    @pl.when(pl.program_id(1) == pl.num_programs(1) - 1)
    def _():
        kv2_ref[0] = kv_t(1)[:, tm - kv2_ref.shape[2]:]
        kv1_ref[0] = kv_t(0)[:, tm - kv1_ref.shape[2]:]


def _proj_weight_specs():
    return [_const_spec((1, D_MODEL)), _const_spec((D_MODEL, IN_W)),
            _const_spec((1, 2 * D_MODEL)), _const_spec((1, SG_W)), _const_spec((1, SG_W))]


def _gate_branch_shapes(n, z_dtype):
    return [jax.ShapeDtypeStruct((n, SG_W), z_dtype), jax.ShapeDtypeStruct((n, SG_W), z_dtype),
            jax.ShapeDtypeStruct((n, D_MODEL), BF16), jax.ShapeDtypeStruct((n, D_MODEL), BF16)]


def _project_sample(x, g, w_bf, bg, lng, lnb):
    n = x.shape[0]
    tm = min(SAMPLE_TAIL_TILE, n)
    row = lambda w: pl.BlockSpec((tm, w), lambda i: (i, 0))
    return pl.pallas_call(
        _proj_sample_kernel,
        grid=(n // tm,),
        in_specs=[row(D_MODEL)] + _proj_weight_specs(),
        out_specs=[row(Z_OFF), row(SG_W), row(SG_W), row(D_MODEL), row(D_MODEL)],
        out_shape=[jax.ShapeDtypeStruct((n, Z_OFF), F32)] + _gate_branch_shapes(n, F32),
        compiler_params=_params(1),
        name="proj_sample",
    )(x, g, w_bf, bg, lng, lnb)


def _project_prompt(x, g, w_bf, bg, lng, lnb):
    batch, seq, _ = x.shape
    tm = TOKEN_TILE
    tiles = seq // tm
    keeps = [min(win, seq) for win, _ in DIL_GROUPS]
    assert seq % tm == 0 and keeps[0] <= tm and keeps[1] <= tm and keeps[2] == seq
    row = lambda w: pl.BlockSpec((tm, w), lambda b, t: (b * tiles + t, 0))
    last = lambda keep: pl.BlockSpec((1, 2 * GROUP_W, keep), lambda b, t: (b, 0, 0))
    n = batch * seq
    return pl.pallas_call(
        _proj_prompt_kernel,
        grid=(batch, tiles),
        in_specs=[pl.BlockSpec((1, tm, D_MODEL), lambda b, t: (b, t, 0))] + _proj_weight_specs(),
        out_specs=[pl.BlockSpec((1, N_GROUPS, 3 * SLABS, tm, LANES), lambda b, t: (b, 0, 0, t, 0)),
                   last(keeps[0]), last(keeps[1]),
                   pl.BlockSpec((1, 2 * GROUP_W, tm), lambda b, t: (b, 0, t)),
                   row(SG_W), row(SG_W), row(D_MODEL), row(D_MODEL)],
        out_shape=[jax.ShapeDtypeStruct((batch, N_GROUPS, 3 * SLABS, seq, LANES), F32),
                   jax.ShapeDtypeStruct((batch, 2 * GROUP_W, keeps[0]), F32),
                   jax.ShapeDtypeStruct((batch, 2 * GROUP_W, keeps[1]), F32),
                   jax.ShapeDtypeStruct((batch, 2 * GROUP_W, keeps[2]), F32)] + _gate_branch_shapes(n, BF16),
        compiler_params=_params(2),
        name="proj_prompt",
    )(x, g, w_bf, bg, lng, lnb)


def _head_lane_mask(shape, hh):
    lane = lax.broadcasted_iota(jnp.int32, shape, len(shape) - 1)
    return (lane >> HEAD_SHIFT) == hh


def _band_block(q, k, v, mask):
    nt = (((1,), (1,)), ((), ()))
    assert HEADS_PER_SLAB == 2
    lm0 = _head_lane_mask((BLOCK, LANES), 0)
    mask2 = jnp.concatenate([mask, mask], axis=0)
    outs, lses = [], []
    for s in range(SLABS):
        zero = jnp.zeros_like(q[s])
        qm = jnp.concatenate([jnp.where(lm0, q[s], zero), jnp.where(lm0, zero, q[s])], axis=0)
        sc = lax.dot_general(qm, k[s], nt, preferred_element_type=F32)
        sc = jnp.where(mask2, sc, NEG)
        m = jnp.max(sc, axis=-1, keepdims=True)
        p = jnp.exp(sc - m)
        l = jnp.sum(p, axis=-1, keepdims=True)
        o = jnp.dot(p.astype(BF16), v[s], preferred_element_type=F32) * (1.0 / l)
        lse = m + jnp.log(l)
        outs.append(jnp.where(lm0, o[:BLOCK], o[BLOCK:]))
        lses.append(jnp.where(lm0, lse[:BLOCK], lse[BLOCK:]))
    return outs, lses


def _group_attention(qkv_ref, o_scr, lse_scr, g, dil, seq):
    n_qb = seq // (dil * BLOCK)
    qb_shift = n_qb.bit_length() - 1
    span = dil * BLOCK
    rows = (lambda start: pl.ds(start, BLOCK, stride=dil)) if dil > 1 else (lambda start: pl.ds(start, BLOCK))
    nk = 2 * BLOCK if n_qb > 1 else BLOCK
    qi = lax.broadcasted_iota(jnp.int32, (BLOCK, nk), 0)
    ki = lax.broadcasted_iota(jnp.int32, (BLOCK, nk), 1)

    def load(slab, start):
        return qkv_ref[0, 0, slab, rows(start), :]

    def body(n, carry):
        r = n >> qb_shift
        qb = n & (n_qb - 1)
        cur = r + span * qb
        if dil == 1:
            cur = pl.multiple_of(cur, BLOCK)
        q = [(load(s, cur) * SCALE).astype(BF16) for s in range(SLABS)]
        k = [load(SLABS + s, cur).astype(BF16) for s in range(SLABS)]
        v = [load(2 * SLABS + s, cur).astype(BF16) for s in range(SLABS)]
        if n_qb > 1:
            prev = r + span * jnp.maximum(qb - 1, 0)
            if dil == 1:
                prev = pl.multiple_of(prev, BLOCK)
            k = [jnp.concatenate([load(SLABS + s, prev).astype(BF16), k[s]], axis=0) for s in range(SLABS)]
            v = [jnp.concatenate([load(2 * SLABS + s, prev).astype(BF16), v[s]], axis=0) for s in range(SLABS)]
            mask = (ki >= qi) & (ki <= qi + BLOCK) & (ki >= jnp.where(qb > 0, 0, BLOCK))
        else:
            mask = ki <= qi
        outs, lses = _band_block(q, k, v, mask)
        for s in range(SLABS):
            o_scr[g, s, rows(cur), :] = outs[s]
            lse_scr[g, s, rows(cur), :] = lses[s]
        return carry

    lax.fori_loop(0, dil * n_qb, body, 0, unroll=BLOCK_UNROLL)


def _prompt_attn_kernel(qkv_ref, o_ref, o_scr, lse_scr):
    g = pl.program_id(1)
    seq = o_ref.shape[1]
    for gi, (_, dil) in enumerate(DIL_GROUPS):
        @pl.when(g == gi)
        def _(gi=gi, dil=dil):
            _group_attention(qkv_ref, o_scr, lse_scr, gi, dil, seq)

    @pl.when(g == N_GROUPS - 1)
    def _():
        def merge(i, carry):
            rows = pl.ds(pl.multiple_of(i * BLOCK, BLOCK), BLOCK)
            for s in range(SLABS):
                ls = [lse_scr[gi, s, rows, :] for gi in range(N_GROUPS)]
                top = jnp.maximum(jnp.maximum(ls[0], ls[1]), ls[2])
                e = [jnp.exp(x - top) for x in ls]
                num = e[0] * o_scr[0, s, rows, :] + e[1] * o_scr[1, s, rows, :] + e[2] * o_scr[2, s, rows, :]
                o_ref[0, rows, s * LANES:(s + 1) * LANES] = (num / (e[0] + e[1] + e[2])).astype(o_ref.dtype)
            return carry
        lax.fori_loop(0, seq // BLOCK, merge, 0)


def _prompt_attention(qkv, seq):
    batch = qkv.shape[0]
    scratch = pltpu.VMEM((N_GROUPS, SLABS, seq, LANES), F32)
    return pl.pallas_call(
        _prompt_attn_kernel,
        grid=(batch, N_GROUPS),
        in_specs=[pl.BlockSpec((1, 1, 3 * SLABS, seq, LANES), lambda b, g: (b, g, 0, 0, 0))],
        out_specs=pl.BlockSpec((1, seq, GROUP_W), lambda b, g: (b, 0, 0)),
        out_shape=jax.ShapeDtypeStruct((batch, seq, GROUP_W), BF16),
        scratch_shapes=[scratch, scratch],
        compiler_params=_params(2),
        name="prompt_attn",
    )(qkv)


def _residue_allreduce(x, dil, op):
    shift = LANES // 2
    while shift >= dil:
        x = op(x, pltpu.roll(x, shift, axis=1))
        shift //= 2
    return x


def _lane_expand(rows8, dil):
    lane = lax.broadcasted_iota(jnp.int32, (LANES, T_PAD), 0)
    t = lax.broadcasted_iota(jnp.int32, (LANES, T_PAD), 1)
    sel = jnp.where((lane & (dil - 1)) == t, 1.0, 0.0).astype(BF16)
    hi = rows8.astype(BF16)
    lo = (rows8 - hi.astype(F32)).astype(BF16)
    out = jnp.dot(sel, hi, preferred_element_type=F32) + jnp.dot(sel, lo, preferred_element_type=F32)
    return out.T


def _heads_to_rows(x):
    return jnp.broadcast_to(x[:, None, :], (HEADS, HEAD_DIM, LANES)).reshape(GROUP_W, LANES)


def _rows_to_heads_sum(x):
    return jnp.sum(x.reshape(HEADS, HEAD_DIM, LANES), axis=1)


def _buffer_group_lanes(c_ref, d, q8, kn8, vn8, dil):
    width = c_ref.shape[2]
    tiles = width // LANES
    ql, kl, vl = _lane_expand(q8, dil), _lane_expand(kn8, dil), _lane_expand(vn8, dil)
    s = [_rows_to_heads_sum(c_ref[d, :GROUP_W, j * LANES:(j + 1) * LANES] * ql) for j in range(tiles)]
    s_new = _rows_to_heads_sum(kl * ql)
    top = s[0]
    for sj in s[1:]:
        top = jnp.maximum(top, sj)
    top = jnp.maximum(_residue_allreduce(top, dil, jnp.maximum), s_new)
    p = [jnp.exp(sj - top) for sj in s]
    p_new = jnp.exp(s_new - top)
    den = p[0]
    for pj in p[1:]:
        den = den + pj
    den = _residue_allreduce(den, dil, jnp.add) + p_new
    acc = c_ref[d, GROUP_W:, :LANES] * _heads_to_rows(p[0])
    for j in range(1, tiles):
        acc = acc + c_ref[d, GROUP_W:, j * LANES:(j + 1) * LANES] * _heads_to_rows(p[j])
    acc = (_residue_allreduce(acc, dil, jnp.add) + vl * _heads_to_rows(p_new)) * _heads_to_rows(1.0 / den)
    lse = _heads_to_rows(top + jnp.log(den))
    return acc.T[:T_PAD], lse.T[:T_PAD]


def _buffer_group_dense(c_ref, d, q8, kn8, vn8):
    rows = HEADS * T_PAD
    width = c_ref.shape[2]
    row = lax.broadcasted_iota(jnp.int32, (rows, GROUP_W), 0)
    lane = lax.broadcasted_iota(jnp.int32, (rows, GROUP_W), 1)
    head_sel = (row >> T_SHIFT) == (lane >> HEAD_SHIFT)
    t = lax.broadcasted_iota(jnp.int32, (rows, width), 0) & (T_PAD - 1)
    i = lax.broadcasted_iota(jnp.int32, (rows, width), 1)
    tn = lax.broadcasted_iota(jnp.int32, (rows, T_PAD), 0) & (T_PAD - 1)
    j = lax.broadcasted_iota(jnp.int32, (rows, T_PAD), 1)
    nt = (((1,), (1,)), ((), ()))
    qm = jnp.where(head_sel, jnp.concatenate([q8] * HEADS, axis=0), 0.0).astype(BF16)
    kt = c_ref[d, :GROUP_W, :].astype(BF16)
    vt = c_ref[d, GROUP_W:, :].astype(BF16)
    sc = jnp.where(i >= t, jnp.dot(qm, kt, preferred_element_type=F32), NEG)
    sn = jnp.where(j <= tn, lax.dot_general(qm, kn8.astype(BF16), nt, preferred_element_type=F32), NEG)
    m = jnp.maximum(jnp.max(sc, axis=-1, keepdims=True), jnp.max(sn, axis=-1, keepdims=True))
    pc = jnp.exp(sc - m)
    pn = jnp.exp(sn - m)
    l = jnp.sum(pc, axis=-1, keepdims=True) + jnp.sum(pn, axis=-1, keepdims=True)
    o = (lax.dot_general(pc.astype(BF16), vt, nt, preferred_element_type=F32)
         + jnp.dot(pn.astype(BF16), vn8.astype(BF16), preferred_element_type=F32))
    o = jnp.where(head_sel, o / l, 0.0)
    lse = jnp.where(head_sel, m + jnp.log(l), 0.0)
    fold = lambda x: sum(x[h * T_PAD:(h + 1) * T_PAD] for h in range(HEADS))
    return fold(o), fold(lse)


def _sample_attn_body(q_ref, kn_ref, vn_ref, c1_ref, c2_ref, c3_ref, o_ref):
    for d in range(o_ref.shape[0]):
        outs, lses = [], []
        for g, (c_ref, (_, dil)) in enumerate(zip((c1_ref, c2_ref, c3_ref), DIL_GROUPS)):
            gs = slice(g * GROUP_W, (g + 1) * GROUP_W)
            q8 = q_ref[d, :, gs] * SCALE
            kn8, vn8 = kn_ref[d, :, gs], vn_ref[d, :, gs]
            if dil == 1:
                o, lse = _buffer_group_dense(c_ref, d, q8, kn8, vn8)
            else:
                o, lse = _buffer_group_lanes(c_ref, d, q8, kn8, vn8, dil)
            outs.append(o)
            lses.append(lse)
        top = jnp.maximum(jnp.maximum(lses[0], lses[1]), lses[2])
        w = [jnp.exp(x - top) for x in lses]
        o_ref[d] = (w[0] * outs[0] + w[1] * outs[1] + w[2] * outs[2]) / (w[0] + w[1] + w[2])


def _mix_value(o_a, o_b, ga_ref, gb_ref, x_ref, wao_ref, wbo_ref, wout_ref, gpost_ref):
    a = jnp.dot(o_a.astype(BF16), wao_ref[...], preferred_element_type=F32)
    b = jnp.dot(o_b.astype(BF16), wbo_ref[...], preferred_element_type=F32)
    merged = ga_ref[...].astype(F32) * a + gb_ref[...].astype(F32) * b
    t = jnp.dot(merged.astype(BF16), wout_ref[...], preferred_element_type=F32)
    return x_ref[...] + _rms(t, gpost_ref[...])


def _ffn_value(x, gpre_ref, wup_ref, wdn_ref, gpost_ref):
    h = _rms(x, gpre_ref[...]).astype(BF16)
    f = jnp.zeros(x.shape, F32)
    for c in range(D_FF // FF_CHUNK):
        cs = slice(c * FF_CHUNK, (c + 1) * FF_CHUNK)
        u = jnp.maximum(jnp.dot(h, wup_ref[:, cs], preferred_element_type=F32), 0.0)
        f = f + jnp.dot((u * u).astype(BF16), wdn_ref[cs, :], preferred_element_type=F32)
    return x + _rms(f, gpost_ref[...])


def _prompt_spatial_gate(z1_ref, zn_ref, wsp_ref, bsp_ref):
    r = lax.broadcasted_iota(jnp.int32, (CHUNK, CHUNK), 0)
    c = lax.broadcasted_iota(jnp.int32, (CHUNK, CHUNK), 1)
    wts = [jnp.where(r >= c, wsp_ref[g], 0.0).astype(BF16) for g in range(SG_GROUPS)]
    chunks = []
    for ci in range(z1_ref.shape[0] // CHUNK):
        rows = slice(ci * CHUNK, (ci + 1) * CHUNK)
        cols = [jnp.dot(wts[g], zn_ref[rows, g * CHUNK:(g + 1) * CHUNK].astype(BF16), preferred_element_type=F32)
                for g in range(SG_GROUPS)]
        chunks.append(jnp.concatenate(cols, axis=1) + bsp_ref[...])
    return z1_ref[...].astype(F32) * jnp.concatenate(chunks, axis=0)


def _tail_prompt_kernel(oa_ref, z1_ref, zn_ref, ga_ref, gb_ref, x_ref, wsp_ref, bsp_ref,
                        wao_ref, wbo_ref, wout_ref, gmix_ref, gpre_ref, wup_ref, wdn_ref, gpost_ref,
                        q_ref, kn_ref, vn_ref, c1_ref, c2_ref, c3_ref, y_ref, o_ref):
    _sample_attn_body(q_ref, kn_ref, vn_ref, c1_ref, c2_ref, c3_ref, o_ref)
    o_b = _prompt_spatial_gate(z1_ref, zn_ref, wsp_ref, bsp_ref)
    x1 = _mix_value(oa_ref[...], o_b, ga_ref, gb_ref, x_ref, wao_ref, wbo_ref, wout_ref, gmix_ref)
    y_ref[...] = _ffn_value(x1, gpre_ref, wup_ref, wdn_ref, gpost_ref)


def _tail_sample_kernel(oa_ref, z1_ref, zn_ref, ga_ref, gb_ref, x_ref, coef_ref, bsp_ref,
                        wao_ref, wbo_ref, wout_ref, gmix_ref, gpre_ref, wup_ref, wdn_ref, gpost_ref, y_ref):
    zn = zn_ref[...]
    mix = coef_ref[0] * zn
    for d in range(1, coef_ref.shape[0]):
        mix = mix + coef_ref[d] * pltpu.roll(zn, d, axis=0)
    o_b = z1_ref[...] * (mix + bsp_ref[...])
    x1 = _mix_value(oa_ref[...], o_b, ga_ref, gb_ref, x_ref, wao_ref, wbo_ref, wout_ref, gmix_ref)
    y_ref[...] = _ffn_value(x1, gpre_ref, wup_ref, wdn_ref, gpost_ref)


def _tail_weight_specs():
    return [_const_spec((GROUP_W, D_MODEL)), _const_spec((SG_W, D_MODEL)), _const_spec((D_MODEL, D_MODEL)),
            _const_spec((1, D_MODEL)), _const_spec((1, D_MODEL)), _const_spec((D_MODEL, D_FF)),
            _const_spec((D_FF, D_MODEL)), _const_spec((1, D_MODEL))]


def _tail_prompt(o_a, z1, zn, ga, gb, x, wsp, bsp, weights, q8, kn8, vn8, caches):
    n, db = x.shape[0], q8.shape[0]
    tile = SAMPLE_DB_TILE
    steps = db // tile
    tm = n // steps
    assert db % tile == 0 and n % steps == 0 and tm % CHUNK == 0
    row = lambda w: pl.BlockSpec((tm, w), lambda i: (i, 0))
    blk = lambda a: pl.BlockSpec((tile,) + a.shape[1:], lambda i: (i, 0, 0))
    return pl.pallas_call(
        _tail_prompt_kernel,
        grid=(steps,),
        in_specs=[row(GROUP_W), row(SG_W), row(SG_W), row(D_MODEL), row(D_MODEL), row(D_MODEL),
                  _const_spec((SG_GROUPS, CHUNK, CHUNK)), _const_spec((CHUNK, SG_W))] + _tail_weight_specs()
        + [blk(q8), blk(kn8), blk(vn8)] + [blk(a) for a in caches],
        out_specs=[row(D_MODEL), pl.BlockSpec((tile, T_PAD, GROUP_W), lambda i: (i, 0, 0))],
        out_shape=[jax.ShapeDtypeStruct((n, D_MODEL), F32),
                   jax.ShapeDtypeStruct((db, T_PAD, GROUP_W), F32)],
        compiler_params=_params(1),
        name="tail_prompt",
    )(o_a, z1, zn, ga, gb, x, wsp, bsp, *weights, q8, kn8, vn8, *caches)


def _tail_sample(o_a, z1, zn, ga, gb, x, coef, bsp, weights):
    n = x.shape[0]
    tm = coef.shape[1]
    row = lambda w: pl.BlockSpec((tm, w), lambda i: (i, 0))
    return pl.pallas_call(
        _tail_sample_kernel,
        grid=(n // tm,),
        in_specs=[row(GROUP_W), row(SG_W), row(SG_W), row(D_MODEL), row(D_MODEL), row(D_MODEL),
                  _const_spec(coef.shape), _const_spec(bsp.shape)] + _tail_weight_specs(),
        out_specs=row(D_MODEL),
        out_shape=jax.ShapeDtypeStruct((n, D_MODEL), F32),
        compiler_params=_params(1),
        name="tail_sample",
    )(o_a, z1, zn, ga, gb, x, coef, bsp, *weights)


def _kv_from_feature_major(kv_t):
    batch, _, keep = kv_t.shape
    return kv_t.reshape(batch, 2, HEADS, HEAD_DIM, keep).transpose(0, 4, 1, 2, 3)[None]


def kernel(x_prompt, x_sample, cache_kv_w128, cache_kv_w512, cache_kv_w2048, norm_pre_mix, w_in, b_gate,
           ln_z_g, ln_z_b, w_spatial, b_spatial, w_ao, w_bo, w_out, norm_post_mix, norm_pre_ffn, w_up, w_down,
           norm_post_ffn):
    assert w_in.shape[0] == 1, "single-layer problem"
    batch, seq, _ = x_prompt.shape
    dbatch, t_new, _ = x_sample.shape
    caches_in = (cache_kv_w128[0], cache_kv_w512[0], cache_kv_w2048[0])

    w_in_bf = w_in[0].astype(BF16)
    wao, wbo, wout = w_ao[0].astype(BF16), w_bo[0].astype(BF16), w_out[0].astype(BF16)
    wup, wdn = w_up[0].astype(BF16), w_down[0].astype(BF16)
    weights = (wao, wbo, wout, norm_post_mix, norm_pre_ffn, wup, wdn, norm_post_ffn)
    wsp = w_spatial[0]
    bsp_rows = jnp.repeat(b_spatial[0].T, CHUNK, axis=1)

    ns = dbatch * t_new
    xs = x_sample.reshape(ns, D_MODEL)
    qkv_s, z1_s, zn_s, ga_s, gb_s = _project_sample(xs, norm_pre_mix, w_in_bf, b_gate, ln_z_g, ln_z_b)
    pad = lambda a: jnp.pad(a.reshape(dbatch, t_new, ATTN_W), ((0, 0), (0, T_PAD - t_new), (0, 0)))
    q8, kn8, vn8 = pad(qkv_s[:, :ATTN_W]), pad(qkv_s[:, ATTN_W:2 * ATTN_W]), pad(qkv_s[:, 2 * ATTN_W:])
    caches, kv_sample = [], []
    for gi, (win, dil) in enumerate(DIL_GROUPS):
        buf = caches_in[gi]
        wb = buf.shape[1]
        assert wb == win and t_new <= T_PAD, "full window buffers"
        caches.append(buf.transpose(0, 2, 3, 4, 1).reshape(dbatch, 2 * GROUP_W, wb))
        gs = slice(gi * GROUP_W, (gi + 1) * GROUP_W)
        kn = qkv_s[:, ATTN_W:2 * ATTN_W][:, gs].reshape(dbatch, t_new, HEADS, HEAD_DIM)
        vn = qkv_s[:, 2 * ATTN_W:][:, gs].reshape(dbatch, t_new, HEADS, HEAD_DIM)
        kv_sample.append(jnp.stack([kn, vn], axis=2)[None])

    xp = x_prompt.reshape(batch * seq, D_MODEL)
    qkv, kv1_t, kv2_t, kv3_t, z1, zn, ga, gb = _project_prompt(
        x_prompt, norm_pre_mix, w_in_bf, b_gate, ln_z_g, ln_z_b)
    kv_prompt = [_kv_from_feature_major(a) for a in (kv1_t, kv2_t, kv3_t)]
    o_a_p = _prompt_attention(qkv, seq).reshape(batch * seq, GROUP_W)
    y_prompt, o_a8 = _tail_prompt(o_a_p, z1, zn, ga, gb, xp, wsp, bsp_rows, weights, q8, kn8, vn8, caches)
    y_prompt = y_prompt.reshape(batch, seq, D_MODEL)

    o_a = o_a8[:, :t_new].reshape(ns, GROUP_W)
    wt = jnp.tril(wsp)[:, :t_new, :t_new]
    coef = jnp.stack([jnp.where(jnp.arange(t_new)[None, :] >= d,
                                wt[:, jnp.arange(t_new), jnp.maximum(jnp.arange(t_new) - d, 0)], 0.0)
                      for d in range(t_new)])
    coef = jnp.repeat(coef.transpose(0, 2, 1), CHUNK, axis=2)
    reps = SAMPLE_TAIL_TILE // t_new
    coef = jnp.tile(coef, (1, reps, 1))
    bsp_s = jnp.tile(bsp_rows[:t_new], (reps, 1))
    y_sample = _tail_sample(o_a, z1_s, zn_s, ga_s, gb_s, xs, coef, bsp_s, weights).reshape(dbatch, t_new, D_MODEL)
    sg_sample = zn_s.reshape(1, dbatch, t_new, SG_W)

    return (y_prompt, y_sample, kv_prompt[0], kv_prompt[1], kv_prompt[2],
            kv_sample[0], kv_sample[1], kv_sample[2], sg_sample)
```

```python
import math

import jax
import jax.numpy as jnp
from jax import lax
from jax.experimental import pallas as pl
from jax.experimental.pallas import tpu as pltpu

F32 = jnp.float32
BF16 = jnp.bfloat16

D_MODEL = 1024
HEAD_DIM = 64
HEADS = 4
GROUP_W = HEADS * HEAD_DIM
DIL_GROUPS = ((128, 1), (512, 4), (2048, 16))
N_GROUPS = len(DIL_GROUPS)
ATTN_W = N_GROUPS * GROUP_W
BLOCK = 128
CHUNK = 128
SG_GROUPS = 4
SG_W = 512
D_FF = 4 * D_MODEL
Z_OFF = 3 * ATTN_W
G_OFF = Z_OFF + 2 * SG_W
IN_W = G_OFF + 2 * D_MODEL
EPS = 1e-6
NEG = -1e30
SCALE = HEAD_DIM ** -0.5
INV_SQRT2 = 1.0 / math.sqrt(2.0)
LOG2E = 1.0 / math.log(2.0)
LANES = 128
SLABS = GROUP_W // LANES
HEADS_PER_SLAB = LANES // HEAD_DIM
HEAD_SHIFT = HEAD_DIM.bit_length() - 1
VMEM_LIMIT = 56 * 1024 * 1024

TOKEN_TILE = 512
BLOCK_UNROLL = 8
SAMPLE_DB_TILE = 2
T_PAD = 8
T_SHIFT = T_PAD.bit_length() - 1
SAMPLE_TAIL_TILE = 128
FF_CHUNK = 1024


def _params(n_axes):
    return pltpu.CompilerParams(dimension_semantics=("arbitrary",) * n_axes,
                                vmem_limit_bytes=VMEM_LIMIT)


def _const_spec(shape):
    return pl.BlockSpec(shape, lambda *_: (0,) * len(shape), pipeline_mode=pl.Buffered(1))


def _rms(x, g):
    return x * lax.rsqrt(jnp.mean(x * x, axis=-1, keepdims=True) + EPS) * g


def _proj_gate_branches(h, w_ref, bg_ref, lng_ref, lnb_ref, z1_ref, zn_ref, ga_ref, gb_ref):
    z = jnp.dot(h, w_ref[:, Z_OFF:G_OFF], preferred_element_type=F32)
    z = 0.5 * z * (1.0 + lax.erf(z * INV_SQRT2))
    z1_ref[...] = z[:, :SG_W].astype(z1_ref.dtype)
    z2 = z[:, SG_W:]
    mu = jnp.mean(z2, axis=-1, keepdims=True)
    zc = z2 - mu
    var = jnp.mean(zc * zc, axis=-1, keepdims=True)
    zn_ref[...] = (zc * lax.rsqrt(var + EPS) * lng_ref[...] + lnb_ref[...]).astype(zn_ref.dtype)

    gates = jnp.dot(h, w_ref[:, G_OFF:], preferred_element_type=F32) + bg_ref[...]
    gates = 1.0 / (1.0 + jnp.exp(-gates))
    ga_ref[...] = gates[:, :D_MODEL].astype(BF16)
    gb_ref[...] = gates[:, D_MODEL:].astype(BF16)


def _proj_sample_kernel(x_ref, g_ref, w_ref, bg_ref, lng_ref, lnb_ref,
                        qkv_ref, z1_ref, zn_ref, ga_ref, gb_ref):
    h = _rms(x_ref[...], g_ref[...]).astype(BF16)
    qkv_ref[...] = jnp.dot(h, w_ref[:, :Z_OFF], preferred_element_type=F32)
    _proj_gate_branches(h, w_ref, bg_ref, lng_ref, lnb_ref, z1_ref, zn_ref, ga_ref, gb_ref)


def _proj_prompt_kernel(x_ref, g_ref, w_ref, bg_ref, lng_ref, lnb_ref,
                        qkv_ref, kv1_ref, kv2_ref, kv3_ref, z1_ref, zn_ref, ga_ref, gb_ref):
    h = _rms(x_ref[0], g_ref[...]).astype(BF16)
    _proj_gate_branches(h, w_ref, bg_ref, lng_ref, lnb_ref, z1_ref, zn_ref, ga_ref, gb_ref)
    def project(first_part, n_parts):
        cols = jnp.dot(h, w_ref[:, first_part * ATTN_W:(first_part + n_parts) * ATTN_W], preferred_element_type=F32)
        for part in range(n_parts):
            for g in range(N_GROUPS):
                for s in range(SLABS):
                    col = part * ATTN_W + g * GROUP_W + s * LANES
                    qkv_ref[0, g, (first_part + part) * SLABS + s] = cols[:, col:col + LANES]

    def kv_t(g):
        return jnp.concatenate([qkv_ref[0, g, slab].T for slab in range(SLABS, 3 * SLABS)], axis=0)

    tm = x_ref.shape[1]
    project(1, 2)
    kv3_ref[0] = kv_t(2)
    project(0, 1)

    @pl.when(pl.program_id(1) == pl.num_programs(1) - 1)
    def _():
        kv2_ref[0] = kv_t(1)[:, tm - kv2_ref.shape[2]:]
        kv1_ref[0] = kv_t(0)[:, tm - kv1_ref.shape[2]:]


def _proj_weight_specs():
    return [_const_spec((1, D_MODEL)), _const_spec((D_MODEL, IN_W)),
            _const_spec((1, 2 * D_MODEL)), _const_spec((1, SG_W)), _const_spec((1, SG_W))]


def _gate_branch_shapes(n, z_dtype):
    return [jax.ShapeDtypeStruct((n, SG_W), z_dtype), jax.ShapeDtypeStruct((n, SG_W), z_dtype),
            jax.ShapeDtypeStruct((n, D_MODEL), BF16), jax.ShapeDtypeStruct((n, D_MODEL), BF16)]


def _project_sample(x, g, w_bf, bg, lng, lnb):
    n = x.shape[0]
    tm = min(SAMPLE_TAIL_TILE, n)
    row = lambda w: pl.BlockSpec((tm, w), lambda i: (i, 0))
    return pl.pallas_call(
        _proj_sample_kernel,
        grid=(n // tm,),
        in_specs=[row(D_MODEL)] + _proj_weight_specs(),
        out_specs=[row(Z_OFF), row(SG_W), row(SG_W), row(D_MODEL), row(D_MODEL)],
        out_shape=[jax.ShapeDtypeStruct((n, Z_OFF), F32)] + _gate_branch_shapes(n, F32),
        compiler_params=_params(1),
        name="proj_sample",
    )(x, g, w_bf, bg, lng, lnb)


def _project_prompt(x, g, w_bf, bg, lng, lnb):
    batch, seq, _ = x.shape
    tm = TOKEN_TILE
    tiles = seq // tm
    keeps = [min(win, seq) for win, _ in DIL_GROUPS]
    assert seq % tm == 0 and keeps[0] <= tm and keeps[1] <= tm and keeps[2] == seq
    row = lambda w: pl.BlockSpec((tm, w), lambda b, t: (b * tiles + t, 0))
    last = lambda keep: pl.BlockSpec((1, 2 * GROUP_W, keep), lambda b, t: (b, 0, 0))
    n = batch * seq
    return pl.pallas_call(
        _proj_prompt_kernel,
        grid=(batch, tiles),
        in_specs=[pl.BlockSpec((1, tm, D_MODEL), lambda b, t: (b, t, 0))] + _proj_weight_specs(),
        out_specs=[pl.BlockSpec((1, N_GROUPS, 3 * SLABS, tm, LANES), lambda b, t: (b, 0, 0, t, 0)),
                   last(keeps[0]), last(keeps[1]),
                   pl.BlockSpec((1, 2 * GROUP_W, tm), lambda b, t: (b, 0, t)),
                   row(SG_W), row(SG_W), row(D_MODEL), row(D_MODEL)],
        out_shape=[jax.ShapeDtypeStruct((batch, N_GROUPS, 3 * SLABS, seq, LANES), F32),
                   jax.ShapeDtypeStruct((batch, 2 * GROUP_W, keeps[0]), F32),
                   jax.ShapeDtypeStruct((batch, 2 * GROUP_W, keeps[1]), F32),
                   jax.ShapeDtypeStruct((batch, 2 * GROUP_W, keeps[2]), F32)] + _gate_branch_shapes(n, BF16),
        compiler_params=_params(2),
        name="proj_prompt",
    )(x, g, w_bf, bg, lng, lnb)


def _head_lane_mask(shape, hh):
    lane = lax.broadcasted_iota(jnp.int32, shape, len(shape) - 1)
    return (lane >> HEAD_SHIFT) == hh


def _band_block(q, k, v, mask):
    nt = (((1,), (1,)), ((), ()))
    assert HEADS_PER_SLAB == 2
    lm0 = _head_lane_mask((BLOCK, LANES), 0)
    mask2 = jnp.concatenate([mask, mask], axis=0)
    outs, lses = [], []
    for s in range(SLABS):
        zero = jnp.zeros_like(q[s])
        qm = jnp.concatenate([jnp.where(lm0, q[s], zero), jnp.where(lm0, zero, q[s])], axis=0)
        sc = lax.dot_general(qm, k[s], nt, preferred_element_type=F32)
        sc = jnp.where(mask2, sc, NEG)
        m = jnp.max(sc, axis=-1, keepdims=True)
        p = jnp.exp2(sc - m)
        l = jnp.sum(p, axis=-1, keepdims=True)
        o = jnp.dot(p.astype(BF16), v[s], preferred_element_type=F32) * (1.0 / l)
        lse = m + jnp.log2(l)
        outs.append(jnp.where(lm0, o[:BLOCK], o[BLOCK:]))
        lses.append(jnp.where(lm0, lse[:BLOCK], lse[BLOCK:]))
    return outs, lses


def _group_attention(qkv_ref, o_scr, lse_scr, g, dil, seq):
    n_qb = seq // (dil * BLOCK)
    qb_shift = n_qb.bit_length() - 1
    span = dil * BLOCK
    rows = (lambda start: pl.ds(start, BLOCK, stride=dil)) if dil > 1 else (lambda start: pl.ds(start, BLOCK))
    nk = 2 * BLOCK if n_qb > 1 else BLOCK
    qi = lax.broadcasted_iota(jnp.int32, (BLOCK, nk), 0)
    ki = lax.broadcasted_iota(jnp.int32, (BLOCK, nk), 1)

    def load(slab, start):
        return qkv_ref[0, 0, slab, rows(start), :]

    def body(n, carry):
        r = n >> qb_shift
        qb = n & (n_qb - 1)
        cur = r + span * qb
        if dil == 1:
            cur = pl.multiple_of(cur, BLOCK)
        q = [(load(s, cur) * (SCALE * LOG2E)).astype(BF16) for s in range(SLABS)]
        k = [load(SLABS + s, cur).astype(BF16) for s in range(SLABS)]
        v = [load(2 * SLABS + s, cur).astype(BF16) for s in range(SLABS)]
        if n_qb > 1:
            prev = r + span * jnp.maximum(qb - 1, 0)
            if dil == 1:
                prev = pl.multiple_of(prev, BLOCK)
            k = [jnp.concatenate([load(SLABS + s, prev).astype(BF16), k[s]], axis=0) for s in range(SLABS)]
            v = [jnp.concatenate([load(2 * SLABS + s, prev).astype(BF16), v[s]], axis=0) for s in range(SLABS)]
            mask = (ki >= qi) & (ki <= qi + BLOCK) & (ki >= jnp.where(qb > 0, 0, BLOCK))
        else:
            mask = ki <= qi
        outs, lses = _band_block(q, k, v, mask)
        for s in range(SLABS):
            o_scr[g, s, rows(cur), :] = outs[s]
            lse_scr[g, s, rows(cur), :] = lses[s]
        return carry

    lax.fori_loop(0, dil * n_qb, body, 0, unroll=BLOCK_UNROLL)


def _prompt_attn_kernel(qkv_ref, o_ref, o_scr, lse_scr):
    g = pl.program_id(1)
    seq = o_ref.shape[1]
    for gi, (_, dil) in enumerate(DIL_GROUPS):
        @pl.when(g == gi)
        def _(gi=gi, dil=dil):
            _group_attention(qkv_ref, o_scr, lse_scr, gi, dil, seq)

    @pl.when(g == N_GROUPS - 1)
    def _():
        def merge(i, carry):
            rows = pl.ds(pl.multiple_of(i * BLOCK, BLOCK), BLOCK)
            for s in range(SLABS):
                ls = [lse_scr[gi, s, rows, :] for gi in range(N_GROUPS)]
                top = jnp.maximum(jnp.maximum(ls[0], ls[1]), ls[2])
                e = [jnp.exp2(x - top) for x in ls]
                num = e[0] * o_scr[0, s, rows, :] + e[1] * o_scr[1, s, rows, :] + e[2] * o_scr[2, s, rows, :]
                o_ref[0, rows, s * LANES:(s + 1) * LANES] = (num / (e[0] + e[1] + e[2])).astype(o_ref.dtype)
            return carry
        lax.fori_loop(0, seq // BLOCK, merge, 0)


def _prompt_attention(qkv, seq):
    batch = qkv.shape[0]
    scratch = pltpu.VMEM((N_GROUPS, SLABS, seq, LANES), F32)
    return pl.pallas_call(
        _prompt_attn_kernel,
        grid=(batch, N_GROUPS),
        in_specs=[pl.BlockSpec((1, 1, 3 * SLABS, seq, LANES), lambda b, g: (b, g, 0, 0, 0))],
        out_specs=pl.BlockSpec((1, seq, GROUP_W), lambda b, g: (b, 0, 0)),
        out_shape=jax.ShapeDtypeStruct((batch, seq, GROUP_W), BF16),
        scratch_shapes=[scratch, scratch],
        compiler_params=_params(2),
        name="prompt_attn",
    )(qkv)


def _residue_allreduce(x, dil, op):
    shift = LANES // 2
    while shift >= dil:
        x = op(x, pltpu.roll(x, shift, axis=1))
        shift //= 2
    return x


def _lane_expand(rows8, dil):
    lane = lax.broadcasted_iota(jnp.int32, (LANES, T_PAD), 0)
    t = lax.broadcasted_iota(jnp.int32, (LANES, T_PAD), 1)
    sel = jnp.where((lane & (dil - 1)) == t, 1.0, 0.0).astype(BF16)
    hi = rows8.astype(BF16)
    lo = (rows8 - hi.astype(F32)).astype(BF16)
    out = jnp.dot(sel, hi, preferred_element_type=F32) + jnp.dot(sel, lo, preferred_element_type=F32)
    return out.T


def _heads_to_rows(x):
    return jnp.broadcast_to(x[:, None, :], (HEADS, HEAD_DIM, LANES)).reshape(GROUP_W, LANES)


def _rows_to_heads_sum(x):
    return jnp.sum(x.reshape(HEADS, HEAD_DIM, LANES), axis=1)


def _buffer_group_lanes(c_ref, d, q8, kn8, vn8, dil):
    width = c_ref.shape[2]
    tiles = width // LANES
    ql, kl, vl = _lane_expand(q8, dil), _lane_expand(kn8, dil), _lane_expand(vn8, dil)
    s = [_rows_to_heads_sum(c_ref[d, :GROUP_W, j * LANES:(j + 1) * LANES] * ql) for j in range(tiles)]
    s_new = _rows_to_heads_sum(kl * ql)
    top = s[0]
    for sj in s[1:]:
        top = jnp.maximum(top, sj)
    top = jnp.maximum(_residue_allreduce(top, dil, jnp.maximum), s_new)
    p = [jnp.exp(sj - top) for sj in s]
    p_new = jnp.exp(s_new - top)
    den = p[0]
    for pj in p[1:]:
        den = den + pj
    den = _residue_allreduce(den, dil, jnp.add) + p_new
    acc = c_ref[d, GROUP_W:, :LANES] * _heads_to_rows(p[0])
    for j in range(1, tiles):
        acc = acc + c_ref[d, GROUP_W:, j * LANES:(j + 1) * LANES] * _heads_to_rows(p[j])
    acc = (_residue_allreduce(acc, dil, jnp.add) + vl * _heads_to_rows(p_new)) * _heads_to_rows(1.0 / den)
    lse = _heads_to_rows(top + jnp.log(den))
    return acc.T[:T_PAD], lse.T[:T_PAD]


def _buffer_group_dense(c_ref, d, q8, kn8, vn8):
    rows = HEADS * T_PAD
    width = c_ref.shape[2]
    row = lax.broadcasted_iota(jnp.int32, (rows, GROUP_W), 0)
    lane = lax.broadcasted_iota(jnp.int32, (rows, GROUP_W), 1)
    head_sel = (row >> T_SHIFT) == (lane >> HEAD_SHIFT)
    t = lax.broadcasted_iota(jnp.int32, (rows, width), 0) & (T_PAD - 1)
    i = lax.broadcasted_iota(jnp.int32, (rows, width), 1)
    tn = lax.broadcasted_iota(jnp.int32, (rows, T_PAD), 0) & (T_PAD - 1)
    j = lax.broadcasted_iota(jnp.int32, (rows, T_PAD), 1)
    nt = (((1,), (1,)), ((), ()))
    qm = jnp.where(head_sel, jnp.concatenate([q8] * HEADS, axis=0), 0.0).astype(BF16)
    kt = c_ref[d, :GROUP_W, :].astype(BF16)
    vt = c_ref[d, GROUP_W:, :].astype(BF16)
    sc = jnp.where(i >= t, jnp.dot(qm, kt, preferred_element_type=F32), NEG)
    sn = jnp.where(j <= tn, lax.dot_general(qm, kn8.astype(BF16), nt, preferred_element_type=F32), NEG)
    m = jnp.maximum(jnp.max(sc, axis=-1, keepdims=True), jnp.max(sn, axis=-1, keepdims=True))
    pc = jnp.exp(sc - m)
    pn = jnp.exp(sn - m)
    l = jnp.sum(pc, axis=-1, keepdims=True) + jnp.sum(pn, axis=-1, keepdims=True)
    o = (lax.dot_general(pc.astype(BF16), vt, nt, preferred_element_type=F32)
         + jnp.dot(pn.astype(BF16), vn8.astype(BF16), preferred_element_type=F32))
    o = jnp.where(head_sel, o / l, 0.0)
    lse = jnp.where(head_sel, m + jnp.log(l), 0.0)
    fold = lambda x: sum(x[h * T_PAD:(h + 1) * T_PAD] for h in range(HEADS))
    return fold(o), fold(lse)


def _sample_attn_body(q_ref, kn_ref, vn_ref, c1_ref, c2_ref, c3_ref, o_ref):
    for d in range(o_ref.shape[0]):
        outs, lses = [], []
        for g, (c_ref, (_, dil)) in enumerate(zip((c1_ref, c2_ref, c3_ref), DIL_GROUPS)):
            gs = slice(g * GROUP_W, (g + 1) * GROUP_W)
            q8 = q_ref[d, :, gs] * SCALE
            kn8, vn8 = kn_ref[d, :, gs], vn_ref[d, :, gs]
            if dil == 1:
                o, lse = _buffer_group_dense(c_ref, d, q8, kn8, vn8)
            else:
                o, lse = _buffer_group_lanes(c_ref, d, q8, kn8, vn8, dil)
            outs.append(o)
            lses.append(lse)
        top = jnp.maximum(jnp.maximum(lses[0], lses[1]), lses[2])
        w = [jnp.exp(x - top) for x in lses]
        o_ref[d] = (w[0] * outs[0] + w[1] * outs[1] + w[2] * outs[2]) / (w[0] + w[1] + w[2])


def _mix_value(o_a, o_b, ga_ref, gb_ref, x_ref, wao_ref, wbo_ref, wout_ref, gpost_ref):
    a = jnp.dot(o_a.astype(BF16), wao_ref[...], preferred_element_type=F32)
    b = jnp.dot(o_b.astype(BF16), wbo_ref[...], preferred_element_type=F32)
    merged = ga_ref[...].astype(F32) * a + gb_ref[...].astype(F32) * b
    t = jnp.dot(merged.astype(BF16), wout_ref[...], preferred_element_type=F32)
    return x_ref[...] + _rms(t, gpost_ref[...])


def _ffn_value(x, gpre_ref, wup_ref, wdn_ref, gpost_ref):
    h = _rms(x, gpre_ref[...]).astype(BF16)
    f = jnp.zeros(x.shape, F32)
    for c in range(D_FF // FF_CHUNK):
        cs = slice(c * FF_CHUNK, (c + 1) * FF_CHUNK)
        u = jnp.maximum(jnp.dot(h, wup_ref[:, cs], preferred_element_type=F32), 0.0)
        f = f + jnp.dot((u * u).astype(BF16), wdn_ref[cs, :], preferred_element_type=F32)
    return x + _rms(f, gpost_ref[...])


def _prompt_spatial_gate(z1_ref, zn_ref, wsp_ref, bsp_ref):
    r = lax.broadcasted_iota(jnp.int32, (CHUNK, CHUNK), 0)
    c = lax.broadcasted_iota(jnp.int32, (CHUNK, CHUNK), 1)
    wts = [jnp.where(r >= c, wsp_ref[g], 0.0).astype(BF16) for g in range(SG_GROUPS)]
    chunks = []
    for ci in range(z1_ref.shape[0] // CHUNK):
        rows = slice(ci * CHUNK, (ci + 1) * CHUNK)
        cols = [jnp.dot(wts[g], zn_ref[rows, g * CHUNK:(g + 1) * CHUNK].astype(BF16), preferred_element_type=F32)
                for g in range(SG_GROUPS)]
        chunks.append(jnp.concatenate(cols, axis=1) + bsp_ref[...])
    return z1_ref[...].astype(F32) * jnp.concatenate(chunks, axis=0)


def _tail_prompt_kernel(oa_ref, z1_ref, zn_ref, ga_ref, gb_ref, x_ref, wsp_ref, bsp_ref,
                        wao_ref, wbo_ref, wout_ref, gmix_ref, gpre_ref, wup_ref, wdn_ref, gpost_ref,
                        q_ref, kn_ref, vn_ref, c1_ref, c2_ref, c3_ref, y_ref, o_ref):
    _sample_attn_body(q_ref, kn_ref, vn_ref, c1_ref, c2_ref, c3_ref, o_ref)
    o_b = _prompt_spatial_gate(z1_ref, zn_ref, wsp_ref, bsp_ref)
    x1 = _mix_value(oa_ref[...], o_b, ga_ref, gb_ref, x_ref, wao_ref, wbo_ref, wout_ref, gmix_ref)
    y_ref[...] = _ffn_value(x1, gpre_ref, wup_ref, wdn_ref, gpost_ref)


def _tail_sample_kernel(oa_ref, z1_ref, zn_ref, ga_ref, gb_ref, x_ref, coef_ref, bsp_ref,
                        wao_ref, wbo_ref, wout_ref, gmix_ref, gpre_ref, wup_ref, wdn_ref, gpost_ref, y_ref):
    zn = zn_ref[...]
    mix = coef_ref[0] * zn
    for d in range(1, coef_ref.shape[0]):
        mix = mix + coef_ref[d] * pltpu.roll(zn, d, axis=0)
    o_b = z1_ref[...] * (mix + bsp_ref[...])
    x1 = _mix_value(oa_ref[...], o_b, ga_ref, gb_ref, x_ref, wao_ref, wbo_ref, wout_ref, gmix_ref)
    y_ref[...] = _ffn_value(x1, gpre_ref, wup_ref, wdn_ref, gpost_ref)


def _tail_weight_specs():
    return [_const_spec((GROUP_W, D_MODEL)), _const_spec((SG_W, D_MODEL)), _const_spec((D_MODEL, D_MODEL)),
            _const_spec((1, D_MODEL)), _const_spec((1, D_MODEL)), _const_spec((D_MODEL, D_FF)),
            _const_spec((D_FF, D_MODEL)), _const_spec((1, D_MODEL))]


def _tail_prompt(o_a, z1, zn, ga, gb, x, wsp, bsp, weights, q8, kn8, vn8, caches):
    n, db = x.shape[0], q8.shape[0]
    tile = SAMPLE_DB_TILE
    steps = db // tile
    tm = n // steps
    assert db % tile == 0 and n % steps == 0 and tm % CHUNK == 0
    row = lambda w: pl.BlockSpec((tm, w), lambda i: (i, 0))
    blk = lambda a: pl.BlockSpec((tile,) + a.shape[1:], lambda i: (i, 0, 0))
    return pl.pallas_call(
        _tail_prompt_kernel,
        grid=(steps,),
        in_specs=[row(GROUP_W), row(SG_W), row(SG_W), row(D_MODEL), row(D_MODEL), row(D_MODEL),
                  _const_spec((SG_GROUPS, CHUNK, CHUNK)), _const_spec((CHUNK, SG_W))] + _tail_weight_specs()
        + [blk(q8), blk(kn8), blk(vn8)] + [blk(a) for a in caches],
        out_specs=[row(D_MODEL), pl.BlockSpec((tile, T_PAD, GROUP_W), lambda i: (i, 0, 0))],
        out_shape=[jax.ShapeDtypeStruct((n, D_MODEL), F32),
                   jax.ShapeDtypeStruct((db, T_PAD, GROUP_W), F32)],
        compiler_params=_params(1),
        name="tail_prompt",
    )(o_a, z1, zn, ga, gb, x, wsp, bsp, *weights, q8, kn8, vn8, *caches)


def _tail_sample(o_a, z1, zn, ga, gb, x, coef, bsp, weights):
    n = x.shape[0]
    tm = coef.shape[1]
    row = lambda w: pl.BlockSpec((tm, w), lambda i: (i, 0))
    return pl.pallas_call(
        _tail_sample_kernel,
        grid=(n // tm,),
        in_specs=[row(GROUP_W), row(SG_W), row(SG_W), row(D_MODEL), row(D_MODEL), row(D_MODEL),
                  _const_spec(coef.shape), _const_spec(bsp.shape)] + _tail_weight_specs(),
        out_specs=row(D_MODEL),
        out_shape=jax.ShapeDtypeStruct((n, D_MODEL), F32),
        compiler_params=_params(1),
        name="tail_sample",
    )(o_a, z1, zn, ga, gb, x, coef, bsp, *weights)


def _kv_from_feature_major(kv_t):
    batch, _, keep = kv_t.shape
    return kv_t.reshape(batch, 2, HEADS, HEAD_DIM, keep).transpose(0, 4, 1, 2, 3)[None]


def kernel(x_prompt, x_sample, cache_kv_w128, cache_kv_w512, cache_kv_w2048, norm_pre_mix, w_in, b_gate,
           ln_z_g, ln_z_b, w_spatial, b_spatial, w_ao, w_bo, w_out, norm_post_mix, norm_pre_ffn, w_up, w_down,
           norm_post_ffn):
    assert w_in.shape[0] == 1, "single-layer problem"
    batch, seq, _ = x_prompt.shape
    dbatch, t_new, _ = x_sample.shape
    caches_in = (cache_kv_w128[0], cache_kv_w512[0], cache_kv_w2048[0])

    w_in_bf = w_in[0].astype(BF16)
    wao, wbo, wout = w_ao[0].astype(BF16), w_bo[0].astype(BF16), w_out[0].astype(BF16)
    wup, wdn = w_up[0].astype(BF16), w_down[0].astype(BF16)
    weights = (wao, wbo, wout, norm_post_mix, norm_pre_ffn, wup, wdn, norm_post_ffn)
    wsp = w_spatial[0]
    bsp_rows = jnp.repeat(b_spatial[0].T, CHUNK, axis=1)

    ns = dbatch * t_new
    xs = x_sample.reshape(ns, D_MODEL)
    qkv_s, z1_s, zn_s, ga_s, gb_s = _project_sample(xs, norm_pre_mix, w_in_bf, b_gate, ln_z_g, ln_z_b)
    pad = lambda a: jnp.pad(a.reshape(dbatch, t_new, ATTN_W), ((0, 0), (0, T_PAD - t_new), (0, 0)))
    q8, kn8, vn8 = pad(qkv_s[:, :ATTN_W]), pad(qkv_s[:, ATTN_W:2 * ATTN_W]), pad(qkv_s[:, 2 * ATTN_W:])
    caches, kv_sample = [], []
    for gi, (win, dil) in enumerate(DIL_GROUPS):
        buf = caches_in[gi]
        wb = buf.shape[1]
        assert wb == win and t_new <= T_PAD, "full window buffers"
        caches.append(buf.transpose(0, 2, 3, 4, 1).reshape(dbatch, 2 * GROUP_W, wb))
        gs = slice(gi * GROUP_W, (gi + 1) * GROUP_W)
        kn = qkv_s[:, ATTN_W:2 * ATTN_W][:, gs].reshape(dbatch, t_new, HEADS, HEAD_DIM)
        vn = qkv_s[:, 2 * ATTN_W:][:, gs].reshape(dbatch, t_new, HEADS, HEAD_DIM)
        kv_sample.append(jnp.stack([kn, vn], axis=2)[None])

    xp = x_prompt.reshape(batch * seq, D_MODEL)
    qkv, kv1_t, kv2_t, kv3_t, z1, zn, ga, gb = _project_prompt(
        x_prompt, norm_pre_mix, w_in_bf, b_gate, ln_z_g, ln_z_b)
    kv_prompt = [_kv_from_feature_major(a) for a in (kv1_t, kv2_t, kv3_t)]
    o_a_p = _prompt_attention(qkv, seq).reshape(batch * seq, GROUP_W)
    y_prompt, o_a8 = _tail_prompt(o_a_p, z1, zn, ga, gb, xp, wsp, bsp_rows, weights, q8, kn8, vn8, caches)
    y_prompt = y_prompt.reshape(batch, seq, D_MODEL)

    o_a = o_a8[:, :t_new].reshape(ns, GROUP_W)
    wt = jnp.tril(wsp)[:, :t_new, :t_new]
    coef = jnp.stack([jnp.where(jnp.arange(t_new)[None, :] >= d,
                                wt[:, jnp.arange(t_new), jnp.maximum(jnp.arange(t_new) - d, 0)], 0.0)
                      for d in range(t_new)])
    coef = jnp.repeat(coef.transpose(0, 2, 1), CHUNK, axis=2)
    reps = SAMPLE_TAIL_TILE // t_new
    coef = jnp.tile(coef, (1, reps, 1))
    bsp_s = jnp.tile(bsp_rows[:t_new], (reps, 1))
    y_sample = _tail_sample(o_a, z1_s, zn_s, ga_s, gb_s, xs, coef, bsp_s, weights).reshape(dbatch, t_new, D_MODEL)
    sg_sample = zn_s.reshape(1, dbatch, t_new, SG_W)

    return (y_prompt, y_sample, kv_prompt[0], kv_prompt[1], kv_prompt[2],
            kv_sample[0], kv_sample[1], kv_sample[2], sg_sample)
```

```python
import math

import jax
import jax.numpy as jnp
from jax import lax
from jax.experimental import pallas as pl
from jax.experimental.pallas import tpu as pltpu

F32 = jnp.float32
BF16 = jnp.bfloat16

D_MODEL = 1024
HEAD_DIM = 64
HEADS = 4
GROUP_W = HEADS * HEAD_DIM
DIL_GROUPS = ((128, 1), (512, 4), (2048, 16))
N_GROUPS = len(DIL_GROUPS)
ATTN_W = N_GROUPS * GROUP_W
BLOCK = 128
CHUNK = 128
SG_GROUPS = 4
SG_W = 512
D_FF = 4 * D_MODEL
Z_OFF = 3 * ATTN_W
G_OFF = Z_OFF + 2 * SG_W
IN_W = G_OFF + 2 * D_MODEL
EPS = 1e-6
NEG = -1e30
SCALE = HEAD_DIM ** -0.5
INV_SQRT2 = 1.0 / math.sqrt(2.0)
LOG2E = 1.0 / math.log(2.0)
LANES = 128
SLABS = GROUP_W // LANES
HEADS_PER_SLAB = LANES // HEAD_DIM
HEAD_SHIFT = HEAD_DIM.bit_length() - 1
VMEM_LIMIT = 56 * 1024 * 1024

TOKEN_TILE = 512
BLOCK_UNROLL = 8
SAMPLE_DB_TILE = 2
T_PAD = 8
T_SHIFT = T_PAD.bit_length() - 1
BF16_SUBLANES = 16
CAST_COLS = 768
SAMPLE_TAIL_TILE = 128
FF_CHUNK = 1024


def _params(n_axes):
    return pltpu.CompilerParams(dimension_semantics=("arbitrary",) * n_axes,
                                vmem_limit_bytes=VMEM_LIMIT)


def _const_spec(shape):
    return pl.BlockSpec(shape, lambda *_: (0,) * len(shape), pipeline_mode=pl.Buffered(1))


def _rms(x, g):
    return x * lax.rsqrt(jnp.mean(x * x, axis=-1, keepdims=True) + EPS) * g


def _proj_gate_branches(h, w_ref, bg_ref, lng_ref, lnb_ref, z1_ref, zn_ref, ga_ref, gb_ref):
    z = jnp.dot(h, w_ref[:, Z_OFF:G_OFF], preferred_element_type=F32)
    z = 0.5 * z * (1.0 + lax.erf(z * INV_SQRT2))
    z1_ref[...] = z[:, :SG_W].astype(z1_ref.dtype)
    z2 = z[:, SG_W:]
    mu = jnp.mean(z2, axis=-1, keepdims=True)
    zc = z2 - mu
    var = jnp.mean(zc * zc, axis=-1, keepdims=True)
    zn_ref[...] = (zc * lax.rsqrt(var + EPS) * lng_ref[...] + lnb_ref[...]).astype(zn_ref.dtype)

    gates = jnp.dot(h, w_ref[:, G_OFF:], preferred_element_type=F32) + bg_ref[...]
    gates = 1.0 / (1.0 + jnp.exp(-gates))
    ga_ref[...] = gates[:, :D_MODEL].astype(BF16)
    gb_ref[...] = gates[:, D_MODEL:].astype(BF16)


def _cast_blocks(pairs):
    for src_ref, dst_ref in pairs:
        dst_ref[...] = src_ref[...].astype(dst_ref.dtype)


def _proj_sample_kernel(x_ref, g_ref, wf_ref, bg_ref, lng_ref, lnb_ref, wao_ref, wbo_ref, wout_ref,
                        qkv_ref, z1_ref, zn_ref, ga_ref, gb_ref, w_ref, wao_bf_ref, wbo_bf_ref, wout_bf_ref):
    @pl.when(pl.program_id(0) == 0)
    def _():
        for c in range(0, IN_W, CAST_COLS):
            w_ref[:, c:c + CAST_COLS] = wf_ref[:, c:c + CAST_COLS].astype(BF16)

    _cast_blocks(((wao_ref, wao_bf_ref), (wbo_ref, wbo_bf_ref), (wout_ref, wout_bf_ref)))
    h = _rms(x_ref[...], g_ref[...]).astype(BF16)
    qkv_ref[...] = jnp.dot(h, w_ref[:, :Z_OFF], preferred_element_type=F32)
    _proj_gate_branches(h, w_ref, bg_ref, lng_ref, lnb_ref, z1_ref, zn_ref, ga_ref, gb_ref)


def _proj_prompt_kernel(x_ref, g_ref, w_ref, bg_ref, lng_ref, lnb_ref, wup_ref, wdn_ref,
                        qkv_ref, kv1_ref, kv2_ref, kv3_ref, z1_ref, zn_ref, ga_ref, gb_ref, wup_bf_ref, wdn_bf_ref):
    _cast_blocks(((wup_ref, wup_bf_ref), (wdn_ref, wdn_bf_ref)))
    h = _rms(x_ref[0], g_ref[...]).astype(BF16)
    _proj_gate_branches(h, w_ref, bg_ref, lng_ref, lnb_ref, z1_ref, zn_ref, ga_ref, gb_ref)
    def project(first_part, n_parts):
        cols = jnp.dot(h, w_ref[:, first_part * ATTN_W:(first_part + n_parts) * ATTN_W], preferred_element_type=F32)
        for part in range(n_parts):
            for g in range(N_GROUPS):
                for s in range(SLABS):
                    col = part * ATTN_W + g * GROUP_W + s * LANES
                    qkv_ref[0, g, (first_part + part) * SLABS + s] = cols[:, col:col + LANES]

    def kv_t(g):
        return jnp.concatenate([qkv_ref[0, g, slab].T for slab in range(SLABS, 3 * SLABS)], axis=0)

    tm = x_ref.shape[1]
    project(1, 2)
    kv3_ref[0] = kv_t(2)
    project(0, 1)

    @pl.when(pl.program_id(1) == pl.num_programs(1) - 1)
    def _():
        kv2_ref[0] = kv_t(1)[:, tm - kv2_ref.shape[2]:]
        kv1_ref[0] = kv_t(0)[:, tm - kv1_ref.shape[2]:]


def _proj_weight_specs():
    return [_const_spec((1, D_MODEL)), _const_spec((D_MODEL, IN_W)),
            _const_spec((1, 2 * D_MODEL)), _const_spec((1, SG_W)), _const_spec((1, SG_W))]


def _gate_branch_shapes(n, z_dtype):
    return [jax.ShapeDtypeStruct((n, SG_W), z_dtype), jax.ShapeDtypeStruct((n, SG_W), z_dtype),
            jax.ShapeDtypeStruct((n, D_MODEL), BF16), jax.ShapeDtypeStruct((n, D_MODEL), BF16)]


def _row_slices(arrays, steps):
    specs, shapes = [], []
    for a in arrays:
        rows = a.shape[0] // steps
        assert rows * steps == a.shape[0] and rows % BF16_SUBLANES == 0
        specs.append((rows, a.shape[1]))
        shapes.append(jax.ShapeDtypeStruct(a.shape, BF16))
    return specs, shapes


def _project_sample(x, g, w_in, bg, lng, lnb, mix_weights):
    n = x.shape[0]
    tm = min(SAMPLE_TAIL_TILE, n)
    steps = n // tm
    row = lambda w: pl.BlockSpec((tm, w), lambda i: (i, 0))
    slices, bf_shapes = _row_slices(mix_weights, steps)
    slice_specs = [pl.BlockSpec(s, lambda i: (i, 0)) for s in slices]
    return pl.pallas_call(
        _proj_sample_kernel,
        grid=(steps,),
        in_specs=[row(D_MODEL)] + _proj_weight_specs() + slice_specs,
        out_specs=[row(Z_OFF), row(SG_W), row(SG_W), row(D_MODEL), row(D_MODEL),
                   pl.BlockSpec((D_MODEL, IN_W), lambda i: (0, 0))] + slice_specs,
        out_shape=[jax.ShapeDtypeStruct((n, Z_OFF), F32)] + _gate_branch_shapes(n, F32)
        + [jax.ShapeDtypeStruct((D_MODEL, IN_W), BF16)] + bf_shapes,
        compiler_params=_params(1),
        name="proj_sample",
    )(x, g, w_in, bg, lng, lnb, *mix_weights)


def _project_prompt(x, g, w_bf, bg, lng, lnb, ffn_weights):
    batch, seq, _ = x.shape
    tm = TOKEN_TILE
    tiles = seq // tm
    keeps = [min(win, seq) for win, _ in DIL_GROUPS]
    assert seq % tm == 0 and keeps[0] <= tm and keeps[1] <= tm and keeps[2] == seq
    row = lambda w: pl.BlockSpec((tm, w), lambda b, t: (b * tiles + t, 0))
    last = lambda keep: pl.BlockSpec((1, 2 * GROUP_W, keep), lambda b, t: (b, 0, 0))
    n = batch * seq
    slices, bf_shapes = _row_slices(ffn_weights, batch * tiles)
    slice_specs = [pl.BlockSpec(s, lambda b, t: (b * tiles + t, 0)) for s in slices]
    return pl.pallas_call(
        _proj_prompt_kernel,
        grid=(batch, tiles),
        in_specs=[pl.BlockSpec((1, tm, D_MODEL), lambda b, t: (b, t, 0))] + _proj_weight_specs() + slice_specs,
        out_specs=[pl.BlockSpec((1, N_GROUPS, 3 * SLABS, tm, LANES), lambda b, t: (b, 0, 0, t, 0)),
                   last(keeps[0]), last(keeps[1]),
                   pl.BlockSpec((1, 2 * GROUP_W, tm), lambda b, t: (b, 0, t)),
                   row(SG_W), row(SG_W), row(D_MODEL), row(D_MODEL)] + slice_specs,
        out_shape=[jax.ShapeDtypeStruct((batch, N_GROUPS, 3 * SLABS, seq, LANES), F32),
                   jax.ShapeDtypeStruct((batch, 2 * GROUP_W, keeps[0]), F32),
                   jax.ShapeDtypeStruct((batch, 2 * GROUP_W, keeps[1]), F32),
                   jax.ShapeDtypeStruct((batch, 2 * GROUP_W, keeps[2]), F32)] + _gate_branch_shapes(n, BF16)
        + bf_shapes,
        compiler_params=_params(2),
        name="proj_prompt",
    )(x, g, w_bf, bg, lng, lnb, *ffn_weights)


def _head_lane_mask(shape, hh):
    lane = lax.broadcasted_iota(jnp.int32, shape, len(shape) - 1)
    return (lane >> HEAD_SHIFT) == hh


def _band_block(q, k, v, mask):
    nt = (((1,), (1,)), ((), ()))
    assert HEADS_PER_SLAB == 2
    lm0 = _head_lane_mask((BLOCK, LANES), 0)
    mask2 = jnp.concatenate([mask, mask], axis=0)
    outs, lses = [], []
    for s in range(SLABS):
        zero = jnp.zeros_like(q[s])
        qm = jnp.concatenate([jnp.where(lm0, q[s], zero), jnp.where(lm0, zero, q[s])], axis=0)
        sc = lax.dot_general(qm, k[s], nt, preferred_element_type=F32)
        sc = jnp.where(mask2, sc, NEG)
        m = jnp.max(sc, axis=-1, keepdims=True)
        p = jnp.exp2(sc - m)
        l = jnp.sum(p, axis=-1, keepdims=True)
        o = jnp.dot(p.astype(BF16), v[s], preferred_element_type=F32) * (1.0 / l)
        lse = m + jnp.log2(l)
        outs.append(jnp.where(lm0, o[:BLOCK], o[BLOCK:]))
        lses.append(jnp.where(lm0, lse[:BLOCK], lse[BLOCK:]))
    return outs, lses


def _group_attention(qkv_ref, o_scr, lse_scr, g, dil, seq):
    n_qb = seq // (dil * BLOCK)
    qb_shift = n_qb.bit_length() - 1
    span = dil * BLOCK
    rows = (lambda start: pl.ds(start, BLOCK, stride=dil)) if dil > 1 else (lambda start: pl.ds(start, BLOCK))
    nk = 2 * BLOCK if n_qb > 1 else BLOCK
    qi = lax.broadcasted_iota(jnp.int32, (BLOCK, nk), 0)
    ki = lax.broadcasted_iota(jnp.int32, (BLOCK, nk), 1)

    def load(slab, start):
        return qkv_ref[0, 0, slab, rows(start), :]

    def body(n, carry):
        r = n >> qb_shift
        qb = n & (n_qb - 1)
        cur = r + span * qb
        if dil == 1:
            cur = pl.multiple_of(cur, BLOCK)
        q = [(load(s, cur) * (SCALE * LOG2E)).astype(BF16) for s in range(SLABS)]
        k = [load(SLABS + s, cur).astype(BF16) for s in range(SLABS)]
        v = [load(2 * SLABS + s, cur).astype(BF16) for s in range(SLABS)]
        if n_qb > 1:
            prev = r + span * jnp.maximum(qb - 1, 0)
            if dil == 1:
                prev = pl.multiple_of(prev, BLOCK)
            k = [jnp.concatenate([load(SLABS + s, prev).astype(BF16), k[s]], axis=0) for s in range(SLABS)]
            v = [jnp.concatenate([load(2 * SLABS + s, prev).astype(BF16), v[s]], axis=0) for s in range(SLABS)]
            mask = (ki >= qi) & (ki <= qi + BLOCK) & (ki >= jnp.where(qb > 0, 0, BLOCK))
        else:
            mask = ki <= qi
        outs, lses = _band_block(q, k, v, mask)
        for s in range(SLABS):
            o_scr[g, s, rows(cur), :] = outs[s]
            lse_scr[g, s, rows(cur), :] = lses[s]
        return carry

    lax.fori_loop(0, dil * n_qb, body, 0, unroll=BLOCK_UNROLL)


def _prompt_attn_kernel(qkv_ref, o_ref, o_scr, lse_scr):
    g = pl.program_id(1)
    seq = o_ref.shape[1]
    for gi, (_, dil) in enumerate(DIL_GROUPS):
        @pl.when(g == gi)
        def _(gi=gi, dil=dil):
            _group_attention(qkv_ref, o_scr, lse_scr, gi, dil, seq)

    @pl.when(g == N_GROUPS - 1)
    def _():
        def merge(i, carry):
            rows = pl.ds(pl.multiple_of(i * BLOCK, BLOCK), BLOCK)
            for s in range(SLABS):
                ls = [lse_scr[gi, s, rows, :] for gi in range(N_GROUPS)]
                top = jnp.maximum(jnp.maximum(ls[0], ls[1]), ls[2])
                e = [jnp.exp2(x - top) for x in ls]
                num = e[0] * o_scr[0, s, rows, :] + e[1] * o_scr[1, s, rows, :] + e[2] * o_scr[2, s, rows, :]
                o_ref[0, rows, s * LANES:(s + 1) * LANES] = (num / (e[0] + e[1] + e[2])).astype(o_ref.dtype)
            return carry
        lax.fori_loop(0, seq // BLOCK, merge, 0)


def _prompt_attention(qkv, seq):
    batch = qkv.shape[0]
    scratch = pltpu.VMEM((N_GROUPS, SLABS, seq, LANES), F32)
    return pl.pallas_call(
        _prompt_attn_kernel,
        grid=(batch, N_GROUPS),
        in_specs=[pl.BlockSpec((1, 1, 3 * SLABS, seq, LANES), lambda b, g: (b, g, 0, 0, 0))],
        out_specs=pl.BlockSpec((1, seq, GROUP_W), lambda b, g: (b, 0, 0)),
        out_shape=jax.ShapeDtypeStruct((batch, seq, GROUP_W), BF16),
        scratch_shapes=[scratch, scratch],
        compiler_params=_params(2),
        name="prompt_attn",
    )(qkv)


def _residue_allreduce(x, dil, op):
    shift = LANES // 2
    while shift >= dil:
        x = op(x, pltpu.roll(x, shift, axis=1))
        shift //= 2
    return x


def _lane_expand(rows8, dil, r0, t_new):
    lane = lax.broadcasted_iota(jnp.int32, (LANES, T_PAD), 0)
    t = lax.broadcasted_iota(jnp.int32, (LANES, T_PAD), 1)
    sel = jnp.where(((lane & (dil - 1)) == t - r0) & (t < r0 + t_new), 1.0, 0.0).astype(BF16)
    hi = rows8.astype(BF16)
    lo = (rows8 - hi.astype(F32)).astype(BF16)
    out = jnp.dot(sel, hi, preferred_element_type=F32) + jnp.dot(sel, lo, preferred_element_type=F32)
    return out.T


def _heads_to_rows(x):
    return jnp.broadcast_to(x[:, None, :], (HEADS, HEAD_DIM, LANES)).reshape(GROUP_W, LANES)


def _rows_to_heads_sum(x):
    return jnp.sum(x.reshape(HEADS, HEAD_DIM, LANES), axis=1)


def _buffer_group_lanes(c_ref, d, q8, kn8, vn8, dil, r0, t_new):
    width = c_ref.shape[2]
    tiles = width // LANES
    ql, kl, vl = (_lane_expand(a, dil, r0, t_new) for a in (q8, kn8, vn8))
    s = [_rows_to_heads_sum(c_ref[d, :GROUP_W, j * LANES:(j + 1) * LANES] * ql) for j in range(tiles)]
    s_new = _rows_to_heads_sum(kl * ql)
    top = s[0]
    for sj in s[1:]:
        top = jnp.maximum(top, sj)
    top = jnp.maximum(_residue_allreduce(top, dil, jnp.maximum), s_new)
    p = [jnp.exp(sj - top) for sj in s]
    p_new = jnp.exp(s_new - top)
    den = p[0]
    for pj in p[1:]:
        den = den + pj
    den = _residue_allreduce(den, dil, jnp.add) + p_new
    acc = c_ref[d, GROUP_W:, :LANES] * _heads_to_rows(p[0])
    for j in range(1, tiles):
        acc = acc + c_ref[d, GROUP_W:, j * LANES:(j + 1) * LANES] * _heads_to_rows(p[j])
    acc = (_residue_allreduce(acc, dil, jnp.add) + vl * _heads_to_rows(p_new)) * _heads_to_rows(1.0 / den)
    lse = _heads_to_rows(top + jnp.log(den))
    first = -(-r0 // dil) * dil
    return acc.T[first - r0:first - r0 + T_PAD], lse.T[first - r0:first - r0 + T_PAD]


def _buffer_group_dense(c_ref, d, q8, kn8, vn8, r0, t_new):
    rows = HEADS * T_PAD
    width = c_ref.shape[2]
    row = lax.broadcasted_iota(jnp.int32, (rows, GROUP_W), 0)
    lane = lax.broadcasted_iota(jnp.int32, (rows, GROUP_W), 1)
    head_sel = (row >> T_SHIFT) == (lane >> HEAD_SHIFT)
    t = (lax.broadcasted_iota(jnp.int32, (rows, width), 0) & (T_PAD - 1)) - r0
    i = lax.broadcasted_iota(jnp.int32, (rows, width), 1)
    tn = lax.broadcasted_iota(jnp.int32, (rows, T_PAD), 0) & (T_PAD - 1)
    j = lax.broadcasted_iota(jnp.int32, (rows, T_PAD), 1)
    nt = (((1,), (1,)), ((), ()))
    qm = jnp.where(head_sel, jnp.concatenate([q8] * HEADS, axis=0), 0.0).astype(BF16)
    kt = c_ref[d, :GROUP_W, :].astype(BF16)
    vt = c_ref[d, GROUP_W:, :].astype(BF16)
    sc = jnp.where(i >= t, jnp.dot(qm, kt, preferred_element_type=F32), NEG)
    own = (j <= tn) & (j >= r0) & (j < r0 + t_new)
    sn = jnp.where(own, lax.dot_general(qm, kn8.astype(BF16), nt, preferred_element_type=F32), NEG)
    m = jnp.maximum(jnp.max(sc, axis=-1, keepdims=True), jnp.max(sn, axis=-1, keepdims=True))
    pc = jnp.exp(sc - m)
    pn = jnp.exp(sn - m)
    l = jnp.sum(pc, axis=-1, keepdims=True) + jnp.sum(pn, axis=-1, keepdims=True)
    o = (lax.dot_general(pc.astype(BF16), vt, nt, preferred_element_type=F32)
         + jnp.dot(pn.astype(BF16), vn8.astype(BF16), preferred_element_type=F32))
    o = jnp.where(head_sel, o / l, 0.0)
    lse = jnp.where(head_sel, m + jnp.log(l), 0.0)
    fold = lambda x: sum(x[h * T_PAD:(h + 1) * T_PAD] for h in range(HEADS))
    return fold(o), fold(lse)


def _sample_attn_body(qkv_ref, c1_ref, c2_ref, c3_ref, o_ref):
    n_db = c1_ref.shape[0]
    t_new = T_PAD // n_db
    assert qkv_ref.shape[0] == T_PAD and n_db * t_new == T_PAD
    new_rows = qkv_ref[...]
    row = lax.broadcasted_iota(jnp.int32, o_ref.shape, 0)
    merged = jnp.zeros(o_ref.shape, F32)
    for d in range(n_db):
        r0 = d * t_new
        outs, lses = [], []
        for g, (c_ref, (_, dil)) in enumerate(zip((c1_ref, c2_ref, c3_ref), DIL_GROUPS)):
            cols = lambda part: slice(part * ATTN_W + g * GROUP_W, part * ATTN_W + (g + 1) * GROUP_W)
            q8 = new_rows[:, cols(0)] * SCALE
            kn8, vn8 = new_rows[:, cols(1)], new_rows[:, cols(2)]
            if dil == 1:
                o, lse = _buffer_group_dense(c_ref, d, q8, kn8, vn8, r0, t_new)
            else:
                o, lse = _buffer_group_lanes(c_ref, d, q8, kn8, vn8, dil, r0, t_new)
            outs.append(o)
            lses.append(lse)
        top = jnp.maximum(jnp.maximum(lses[0], lses[1]), lses[2])
        w = [jnp.exp(x - top) for x in lses]
        o8 = (w[0] * outs[0] + w[1] * outs[1] + w[2] * outs[2]) / (w[0] + w[1] + w[2])
        merged = jnp.where((row >= r0) & (row < r0 + t_new), o8, merged)
    o_ref[...] = merged


def _mix_value(o_a, o_b, ga_ref, gb_ref, x_ref, wao_ref, wbo_ref, wout_ref, gpost_ref):
    a = jnp.dot(o_a.astype(BF16), wao_ref[...], preferred_element_type=F32)
    b = jnp.dot(o_b.astype(BF16), wbo_ref[...], preferred_element_type=F32)
    merged = ga_ref[...].astype(F32) * a + gb_ref[...].astype(F32) * b
    t = jnp.dot(merged.astype(BF16), wout_ref[...], preferred_element_type=F32)
    return x_ref[...] + _rms(t, gpost_ref[...])


def _ffn_value(x, gpre_ref, wup_ref, wdn_ref, gpost_ref):
    h = _rms(x, gpre_ref[...]).astype(BF16)
    f = jnp.zeros(x.shape, F32)
    for c in range(D_FF // FF_CHUNK):
        cs = slice(c * FF_CHUNK, (c + 1) * FF_CHUNK)
        u = jnp.maximum(jnp.dot(h, wup_ref[:, cs], preferred_element_type=F32), 0.0)
        f = f + jnp.dot((u * u).astype(BF16), wdn_ref[cs, :], preferred_element_type=F32)
    return x + _rms(f, gpost_ref[...])


def _prompt_spatial_gate(z1_ref, zn_ref, wsp_ref, bsp_ref):
    r = lax.broadcasted_iota(jnp.int32, (CHUNK, CHUNK), 0)
    c = lax.broadcasted_iota(jnp.int32, (CHUNK, CHUNK), 1)
    wts = [jnp.where(r >= c, wsp_ref[g], 0.0).astype(BF16) for g in range(SG_GROUPS)]
    chunks = []
    for ci in range(z1_ref.shape[0] // CHUNK):
        rows = slice(ci * CHUNK, (ci + 1) * CHUNK)
        cols = [jnp.dot(wts[g], zn_ref[rows, g * CHUNK:(g + 1) * CHUNK].astype(BF16), preferred_element_type=F32)
                for g in range(SG_GROUPS)]
        chunks.append(jnp.concatenate(cols, axis=1) + bsp_ref[...])
    return z1_ref[...].astype(F32) * jnp.concatenate(chunks, axis=0)


def _tail_prompt_kernel(oa_ref, z1_ref, zn_ref, ga_ref, gb_ref, x_ref, wsp_ref, bsp_ref,
                        wao_ref, wbo_ref, wout_ref, gmix_ref, gpre_ref, wup_ref, wdn_ref, gpost_ref,
                        qkv_ref, c1_ref, c2_ref, c3_ref, y_ref, o_ref):
    _sample_attn_body(qkv_ref, c1_ref, c2_ref, c3_ref, o_ref)
    o_b = _prompt_spatial_gate(z1_ref, zn_ref, wsp_ref, bsp_ref)
    x1 = _mix_value(oa_ref[...], o_b, ga_ref, gb_ref, x_ref, wao_ref, wbo_ref, wout_ref, gmix_ref)
    y_ref[...] = _ffn_value(x1, gpre_ref, wup_ref, wdn_ref, gpost_ref)


def _tail_sample_kernel(oa_ref, z1_ref, zn_ref, ga_ref, gb_ref, x_ref, coef_ref, bsp_ref,
                        wao_ref, wbo_ref, wout_ref, gmix_ref, gpre_ref, wup_ref, wdn_ref, gpost_ref, y_ref):
    zn = zn_ref[...]
    mix = coef_ref[0] * zn
    for d in range(1, coef_ref.shape[0]):
        mix = mix + coef_ref[d] * pltpu.roll(zn, d, axis=0)
    o_b = z1_ref[...] * (mix + bsp_ref[...])
    x1 = _mix_value(oa_ref[...], o_b, ga_ref, gb_ref, x_ref, wao_ref, wbo_ref, wout_ref, gmix_ref)
    y_ref[...] = _ffn_value(x1, gpre_ref, wup_ref, wdn_ref, gpost_ref)


def _tail_weight_specs():
    return [_const_spec((GROUP_W, D_MODEL)), _const_spec((SG_W, D_MODEL)), _const_spec((D_MODEL, D_MODEL)),
            _const_spec((1, D_MODEL)), _const_spec((1, D_MODEL)), _const_spec((D_MODEL, D_FF)),
            _const_spec((D_FF, D_MODEL)), _const_spec((1, D_MODEL))]


def _tail_prompt(o_a, z1, zn, ga, gb, x, wsp, bsp, weights, qkv_s, caches):
    n, db, ns = x.shape[0], caches[0].shape[0], qkv_s.shape[0]
    tile = SAMPLE_DB_TILE
    steps = db // tile
    tm = n // steps
    assert db % tile == 0 and n % steps == 0 and tm % CHUNK == 0 and ns == steps * T_PAD
    row = lambda w: pl.BlockSpec((tm, w), lambda i: (i, 0))
    blk = lambda a: pl.BlockSpec((tile,) + a.shape[1:], lambda i: (i, 0, 0))
    new_rows = lambda w: pl.BlockSpec((T_PAD, w), lambda i: (i, 0))
    return pl.pallas_call(
        _tail_prompt_kernel,
        grid=(steps,),
        in_specs=[row(GROUP_W), row(SG_W), row(SG_W), row(D_MODEL), row(D_MODEL), row(D_MODEL),
                  _const_spec((SG_GROUPS, CHUNK, CHUNK)), _const_spec((CHUNK, SG_W))] + _tail_weight_specs()
        + [new_rows(Z_OFF)] + [blk(a) for a in caches],
        out_specs=[row(D_MODEL), new_rows(GROUP_W)],
        out_shape=[jax.ShapeDtypeStruct((n, D_MODEL), F32),
                   jax.ShapeDtypeStruct((ns, GROUP_W), F32)],
        compiler_params=_params(1),
        name="tail_prompt",
    )(o_a, z1, zn, ga, gb, x, wsp, bsp, *weights, qkv_s, *caches)


def _tail_sample(o_a, z1, zn, ga, gb, x, coef, bsp, weights):
    n = x.shape[0]
    tm = coef.shape[1]
    row = lambda w: pl.BlockSpec((tm, w), lambda i: (i, 0))
    return pl.pallas_call(
        _tail_sample_kernel,
        grid=(n // tm,),
        in_specs=[row(GROUP_W), row(SG_W), row(SG_W), row(D_MODEL), row(D_MODEL), row(D_MODEL),
                  _const_spec(coef.shape), _const_spec(bsp.shape)] + _tail_weight_specs(),
        out_specs=row(D_MODEL),
        out_shape=jax.ShapeDtypeStruct((n, D_MODEL), F32),
        compiler_params=_params(1),
        name="tail_sample",
    )(o_a, z1, zn, ga, gb, x, coef, bsp, *weights)


def _kv_from_feature_major(kv_t):
    batch, _, keep = kv_t.shape
    return kv_t.reshape(batch, 2, HEADS, HEAD_DIM, keep).transpose(0, 4, 1, 2, 3)[None]


def kernel(x_prompt, x_sample, cache_kv_w128, cache_kv_w512, cache_kv_w2048, norm_pre_mix, w_in, b_gate,
           ln_z_g, ln_z_b, w_spatial, b_spatial, w_ao, w_bo, w_out, norm_post_mix, norm_pre_ffn, w_up, w_down,
           norm_post_ffn):
    assert w_in.shape[0] == 1, "single-layer problem"
    batch, seq, _ = x_prompt.shape
    dbatch, t_new, _ = x_sample.shape
    caches_in = (cache_kv_w128[0], cache_kv_w512[0], cache_kv_w2048[0])

    wsp = w_spatial[0]
    bsp_rows = jnp.repeat(b_spatial[0].T, CHUNK, axis=1)

    ns = dbatch * t_new
    xs = x_sample.reshape(ns, D_MODEL)
    qkv_s, z1_s, zn_s, ga_s, gb_s, w_in_bf, wao, wbo, wout = _project_sample(
        xs, norm_pre_mix, w_in[0], b_gate, ln_z_g, ln_z_b, (w_ao[0], w_bo[0], w_out[0]))
    caches, kv_sample = [], []
    for gi, (win, dil) in enumerate(DIL_GROUPS):
        buf = caches_in[gi]
        wb = buf.shape[1]
        assert wb == win and t_new <= T_PAD, "full window buffers"
        caches.append(buf.transpose(0, 2, 3, 4, 1).reshape(dbatch, 2 * GROUP_W, wb))
        gs = slice(gi * GROUP_W, (gi + 1) * GROUP_W)
        kn = qkv_s[:, ATTN_W:2 * ATTN_W][:, gs].reshape(dbatch, t_new, HEADS, HEAD_DIM)
        vn = qkv_s[:, 2 * ATTN_W:][:, gs].reshape(dbatch, t_new, HEADS, HEAD_DIM)
        kv_sample.append(jnp.stack([kn, vn], axis=2)[None])

    xp = x_prompt.reshape(batch * seq, D_MODEL)
    qkv, kv1_t, kv2_t, kv3_t, z1, zn, ga, gb, wup, wdn = _project_prompt(
        x_prompt, norm_pre_mix, w_in_bf, b_gate, ln_z_g, ln_z_b, (w_up[0], w_down[0]))
    weights = (wao, wbo, wout, norm_post_mix, norm_pre_ffn, wup, wdn, norm_post_ffn)
    kv_prompt = [_kv_from_feature_major(a) for a in (kv1_t, kv2_t, kv3_t)]
    o_a_p = _prompt_attention(qkv, seq).reshape(batch * seq, GROUP_W)
    assert SAMPLE_DB_TILE * t_new == T_PAD, "a tail step takes T_PAD new rows"
    y_prompt, o_a = _tail_prompt(o_a_p, z1, zn, ga, gb, xp, wsp, bsp_rows, weights, qkv_s, caches)
    y_prompt = y_prompt.reshape(batch, seq, D_MODEL)

    wt = jnp.tril(wsp)[:, :t_new, :t_new]
    coef = jnp.stack([jnp.where(jnp.arange(t_new)[None, :] >= d,
                                wt[:, jnp.arange(t_new), jnp.maximum(jnp.arange(t_new) - d, 0)], 0.0)
                      for d in range(t_new)])
    coef = jnp.repeat(coef.transpose(0, 2, 1), CHUNK, axis=2)
    reps = SAMPLE_TAIL_TILE // t_new
    coef = jnp.tile(coef, (1, reps, 1))
    bsp_s = jnp.tile(bsp_rows[:t_new], (reps, 1))
    y_sample = _tail_sample(o_a, z1_s, zn_s, ga_s, gb_s, xs, coef, bsp_s, weights).reshape(dbatch, t_new, D_MODEL)
    sg_sample = zn_s.reshape(1, dbatch, t_new, SG_W)

    return (y_prompt, y_sample, kv_prompt[0], kv_prompt[1], kv_prompt[2],
            kv_sample[0], kv_sample[1], kv_sample[2], sg_sample)
```

```python
import math

import jax
import jax.numpy as jnp
from jax import lax
from jax.experimental import pallas as pl
from jax.experimental.pallas import tpu as pltpu

F32 = jnp.float32
BF16 = jnp.bfloat16

D_MODEL = 1024
HEAD_DIM = 64
HEADS = 4
GROUP_W = HEADS * HEAD_DIM
DIL_GROUPS = ((128, 1), (512, 4), (2048, 16))
N_GROUPS = len(DIL_GROUPS)
ATTN_W = N_GROUPS * GROUP_W
BLOCK = 128
CHUNK = 128
SG_GROUPS = 4
SG_W = 512
D_FF = 4 * D_MODEL
Z_OFF = 3 * ATTN_W
G_OFF = Z_OFF + 2 * SG_W
IN_W = G_OFF + 2 * D_MODEL
EPS = 1e-6
NEG = -1e30
SCALE = HEAD_DIM ** -0.5
INV_SQRT2 = 1.0 / math.sqrt(2.0)
LOG2E = 1.0 / math.log(2.0)
LANES = 128
SLABS = GROUP_W // LANES
HEADS_PER_SLAB = LANES // HEAD_DIM
HEAD_SHIFT = HEAD_DIM.bit_length() - 1
VMEM_LIMIT = 56 * 1024 * 1024

TOKEN_TILE = 512
BLOCK_UNROLL = 8
SAMPLE_DB_TILE = 2
T_PAD = 8
T_SHIFT = T_PAD.bit_length() - 1
BF16_SUBLANES = 16
CAST_COLS = 768
SAMPLE_TAIL_TILE = 256
FF_CHUNK = 1024


def _params(n_axes):
    return pltpu.CompilerParams(dimension_semantics=("arbitrary",) * n_axes,
                                vmem_limit_bytes=VMEM_LIMIT)


def _const_spec(shape):
    return pl.BlockSpec(shape, lambda *_: (0,) * len(shape), pipeline_mode=pl.Buffered(1))


def _rms(x, g):
    return x * lax.rsqrt(jnp.mean(x * x, axis=-1, keepdims=True) + EPS) * g


def _proj_gate_branches(h, w_ref, bg_ref, lng_ref, lnb_ref, z1_ref, zn_ref, ga_ref, gb_ref):
    z = jnp.dot(h, w_ref[:, Z_OFF:G_OFF], preferred_element_type=F32)
    z = 0.5 * z * (1.0 + lax.erf(z * INV_SQRT2))
    z1_ref[...] = z[:, :SG_W].astype(z1_ref.dtype)
    z2 = z[:, SG_W:]
    mu = jnp.mean(z2, axis=-1, keepdims=True)
    zc = z2 - mu
    var = jnp.mean(zc * zc, axis=-1, keepdims=True)
    zn_ref[...] = (zc * lax.rsqrt(var + EPS) * lng_ref[...] + lnb_ref[...]).astype(zn_ref.dtype)

    gates = jnp.dot(h, w_ref[:, G_OFF:], preferred_element_type=F32) + bg_ref[...]
    gates = 1.0 / (1.0 + jnp.exp(-gates))
    ga_ref[...] = gates[:, :D_MODEL].astype(BF16)
    gb_ref[...] = gates[:, D_MODEL:].astype(BF16)


def _cast_blocks(pairs):
    for src_ref, dst_ref in pairs:
        dst_ref[...] = src_ref[...].astype(dst_ref.dtype)


def _proj_sample_kernel(x_ref, g_ref, wf_ref, bg_ref, lng_ref, lnb_ref, wao_ref, wbo_ref, wout_ref,
                        qkv_ref, z1_ref, zn_ref, ga_ref, gb_ref, w_ref, wao_bf_ref, wbo_bf_ref, wout_bf_ref):
    @pl.when(pl.program_id(0) == 0)
    def _():
        for c in range(0, IN_W, CAST_COLS):
            w_ref[:, c:c + CAST_COLS] = wf_ref[:, c:c + CAST_COLS].astype(BF16)

    _cast_blocks(((wao_ref, wao_bf_ref), (wbo_ref, wbo_bf_ref), (wout_ref, wout_bf_ref)))
    h = _rms(x_ref[...], g_ref[...]).astype(BF16)
    qkv_ref[...] = jnp.dot(h, w_ref[:, :Z_OFF], preferred_element_type=F32)
    _proj_gate_branches(h, w_ref, bg_ref, lng_ref, lnb_ref, z1_ref, zn_ref, ga_ref, gb_ref)


def _proj_prompt_kernel(x_ref, g_ref, w_ref, bg_ref, lng_ref, lnb_ref, wup_ref, wdn_ref,
                        qkv_ref, kv1_ref, kv2_ref, kv3_ref, z1_ref, zn_ref, ga_ref, gb_ref, wup_bf_ref, wdn_bf_ref):
    _cast_blocks(((wup_ref, wup_bf_ref), (wdn_ref, wdn_bf_ref)))
    h = _rms(x_ref[0], g_ref[...]).astype(BF16)
    _proj_gate_branches(h, w_ref, bg_ref, lng_ref, lnb_ref, z1_ref, zn_ref, ga_ref, gb_ref)
    def project(first_part, n_parts):
        cols = jnp.dot(h, w_ref[:, first_part * ATTN_W:(first_part + n_parts) * ATTN_W], preferred_element_type=F32)
        for part in range(n_parts):
            for g in range(N_GROUPS):
                for s in range(SLABS):
                    col = part * ATTN_W + g * GROUP_W + s * LANES
                    qkv_ref[0, g, (first_part + part) * SLABS + s] = cols[:, col:col + LANES]

    def kv_t(g):
        return jnp.concatenate([qkv_ref[0, g, slab].T for slab in range(SLABS, 3 * SLABS)], axis=0)

    tm = x_ref.shape[1]
    project(1, 2)
    kv3_ref[0] = kv_t(2)
    project(0, 1)

    @pl.when(pl.program_id(1) == pl.num_programs(1) - 1)
    def _():
        kv2_ref[0] = kv_t(1)[:, tm - kv2_ref.shape[2]:]
        kv1_ref[0] = kv_t(0)[:, tm - kv1_ref.shape[2]:]


def _proj_weight_specs():
    return [_const_spec((1, D_MODEL)), _const_spec((D_MODEL, IN_W)),
            _const_spec((1, 2 * D_MODEL)), _const_spec((1, SG_W)), _const_spec((1, SG_W))]


def _gate_branch_shapes(n, z_dtype):
    return [jax.ShapeDtypeStruct((n, SG_W), z_dtype), jax.ShapeDtypeStruct((n, SG_W), z_dtype),
            jax.ShapeDtypeStruct((n, D_MODEL), BF16), jax.ShapeDtypeStruct((n, D_MODEL), BF16)]


def _row_slices(arrays, steps):
    specs, shapes = [], []
    for a in arrays:
        rows = a.shape[0] // steps
        assert rows * steps == a.shape[0] and rows % BF16_SUBLANES == 0
        specs.append((rows, a.shape[1]))
        shapes.append(jax.ShapeDtypeStruct(a.shape, BF16))
    return specs, shapes


def _project_sample(x, g, w_in, bg, lng, lnb, mix_weights):
    n = x.shape[0]
    tm = min(SAMPLE_TAIL_TILE, n)
    steps = n // tm
    row = lambda w: pl.BlockSpec((tm, w), lambda i: (i, 0))
    slices, bf_shapes = _row_slices(mix_weights, steps)
    slice_specs = [pl.BlockSpec(s, lambda i: (i, 0)) for s in slices]
    return pl.pallas_call(
        _proj_sample_kernel,
        grid=(steps,),
        in_specs=[row(D_MODEL)] + _proj_weight_specs() + slice_specs,
        out_specs=[row(Z_OFF), row(SG_W), row(SG_W), row(D_MODEL), row(D_MODEL),
                   pl.BlockSpec((D_MODEL, IN_W), lambda i: (0, 0))] + slice_specs,
        out_shape=[jax.ShapeDtypeStruct((n, Z_OFF), F32)] + _gate_branch_shapes(n, F32)
        + [jax.ShapeDtypeStruct((D_MODEL, IN_W), BF16)] + bf_shapes,
        compiler_params=_params(1),
        name="proj_sample",
    )(x, g, w_in, bg, lng, lnb, *mix_weights)


def _project_prompt(x, g, w_bf, bg, lng, lnb, ffn_weights):
    batch, seq, _ = x.shape
    tm = TOKEN_TILE
    tiles = seq // tm
    keeps = [min(win, seq) for win, _ in DIL_GROUPS]
    assert seq % tm == 0 and keeps[0] <= tm and keeps[1] <= tm and keeps[2] == seq
    row = lambda w: pl.BlockSpec((tm, w), lambda b, t: (b * tiles + t, 0))
    last = lambda keep: pl.BlockSpec((1, 2 * GROUP_W, keep), lambda b, t: (b, 0, 0))
    n = batch * seq
    slices, bf_shapes = _row_slices(ffn_weights, batch * tiles)
    slice_specs = [pl.BlockSpec(s, lambda b, t: (b * tiles + t, 0)) for s in slices]
    return pl.pallas_call(
        _proj_prompt_kernel,
        grid=(batch, tiles),
        in_specs=[pl.BlockSpec((1, tm, D_MODEL), lambda b, t: (b, t, 0))] + _proj_weight_specs() + slice_specs,
        out_specs=[pl.BlockSpec((1, N_GROUPS, 3 * SLABS, tm, LANES), lambda b, t: (b, 0, 0, t, 0)),
                   last(keeps[0]), last(keeps[1]),
                   pl.BlockSpec((1, 2 * GROUP_W, tm), lambda b, t: (b, 0, t)),
                   row(SG_W), row(SG_W), row(D_MODEL), row(D_MODEL)] + slice_specs,
        out_shape=[jax.ShapeDtypeStruct((batch, N_GROUPS, 3 * SLABS, seq, LANES), F32),
                   jax.ShapeDtypeStruct((batch, 2 * GROUP_W, keeps[0]), F32),
                   jax.ShapeDtypeStruct((batch, 2 * GROUP_W, keeps[1]), F32),
                   jax.ShapeDtypeStruct((batch, 2 * GROUP_W, keeps[2]), F32)] + _gate_branch_shapes(n, BF16)
        + bf_shapes,
        compiler_params=_params(2),
        name="proj_prompt",
    )(x, g, w_bf, bg, lng, lnb, *ffn_weights)


def _head_lane_mask(shape, hh):
    lane = lax.broadcasted_iota(jnp.int32, shape, len(shape) - 1)
    return (lane >> HEAD_SHIFT) == hh


def _band_block(q, k, v, mask):
    nt = (((1,), (1,)), ((), ()))
    assert HEADS_PER_SLAB == 2
    lm0 = _head_lane_mask((BLOCK, LANES), 0)
    mask2 = jnp.concatenate([mask, mask], axis=0)
    outs, lses = [], []
    for s in range(SLABS):
        zero = jnp.zeros_like(q[s])
        qm = jnp.concatenate([jnp.where(lm0, q[s], zero), jnp.where(lm0, zero, q[s])], axis=0)
        sc = lax.dot_general(qm, k[s], nt, preferred_element_type=F32)
        sc = jnp.where(mask2, sc, NEG)
        m = jnp.max(sc, axis=-1, keepdims=True)
        p = jnp.exp2(sc - m)
        l = jnp.sum(p, axis=-1, keepdims=True)
        o = jnp.dot(p.astype(BF16), v[s], preferred_element_type=F32) * (1.0 / l)
        lse = m + jnp.log2(l)
        outs.append(jnp.where(lm0, o[:BLOCK], o[BLOCK:]))
        lses.append(jnp.where(lm0, lse[:BLOCK], lse[BLOCK:]))
    return outs, lses


def _group_attention(qkv_ref, o_scr, lse_scr, g, dil, seq):
    n_qb = seq // (dil * BLOCK)
    qb_shift = n_qb.bit_length() - 1
    span = dil * BLOCK
    rows = (lambda start: pl.ds(start, BLOCK, stride=dil)) if dil > 1 else (lambda start: pl.ds(start, BLOCK))
    nk = 2 * BLOCK if n_qb > 1 else BLOCK
    qi = lax.broadcasted_iota(jnp.int32, (BLOCK, nk), 0)
    ki = lax.broadcasted_iota(jnp.int32, (BLOCK, nk), 1)

    def load(slab, start):
        return qkv_ref[0, 0, slab, rows(start), :]

    def body(n, carry):
        r = n >> qb_shift
        qb = n & (n_qb - 1)
        cur = r + span * qb
        if dil == 1:
            cur = pl.multiple_of(cur, BLOCK)
        q = [(load(s, cur) * (SCALE * LOG2E)).astype(BF16) for s in range(SLABS)]
        k = [load(SLABS + s, cur).astype(BF16) for s in range(SLABS)]
        v = [load(2 * SLABS + s, cur).astype(BF16) for s in range(SLABS)]
        if n_qb > 1:
            prev = r + span * jnp.maximum(qb - 1, 0)
            if dil == 1:
                prev = pl.multiple_of(prev, BLOCK)
            k = [jnp.concatenate([load(SLABS + s, prev).astype(BF16), k[s]], axis=0) for s in range(SLABS)]
            v = [jnp.concatenate([load(2 * SLABS + s, prev).astype(BF16), v[s]], axis=0) for s in range(SLABS)]
            mask = (ki >= qi) & (ki <= qi + BLOCK) & (ki >= jnp.where(qb > 0, 0, BLOCK))
        else:
            mask = ki <= qi
        outs, lses = _band_block(q, k, v, mask)
        for s in range(SLABS):
            o_scr[g, s, rows(cur), :] = outs[s]
            lse_scr[g, s, rows(cur), :] = lses[s]
        return carry

    lax.fori_loop(0, dil * n_qb, body, 0, unroll=BLOCK_UNROLL)


def _prompt_attn_kernel(qkv_ref, o_ref, o_scr, lse_scr):
    g = pl.program_id(1)
    seq = o_ref.shape[1]
    for gi, (_, dil) in enumerate(DIL_GROUPS):
        @pl.when(g == gi)
        def _(gi=gi, dil=dil):
            _group_attention(qkv_ref, o_scr, lse_scr, gi, dil, seq)

    @pl.when(g == N_GROUPS - 1)
    def _():
        def merge(i, carry):
            rows = pl.ds(pl.multiple_of(i * BLOCK, BLOCK), BLOCK)
            for s in range(SLABS):
                ls = [lse_scr[gi, s, rows, :] for gi in range(N_GROUPS)]
                top = jnp.maximum(jnp.maximum(ls[0], ls[1]), ls[2])
                e = [jnp.exp2(x - top) for x in ls]
                num = e[0] * o_scr[0, s, rows, :] + e[1] * o_scr[1, s, rows, :] + e[2] * o_scr[2, s, rows, :]
                o_ref[0, rows, s * LANES:(s + 1) * LANES] = (num / (e[0] + e[1] + e[2])).astype(o_ref.dtype)
            return carry
        lax.fori_loop(0, seq // BLOCK, merge, 0)


def _prompt_attention(qkv, seq):
    batch = qkv.shape[0]
    scratch = pltpu.VMEM((N_GROUPS, SLABS, seq, LANES), F32)
    return pl.pallas_call(
        _prompt_attn_kernel,
        grid=(batch, N_GROUPS),
        in_specs=[pl.BlockSpec((1, 1, 3 * SLABS, seq, LANES), lambda b, g: (b, g, 0, 0, 0))],
        out_specs=pl.BlockSpec((1, seq, GROUP_W), lambda b, g: (b, 0, 0)),
        out_shape=jax.ShapeDtypeStruct((batch, seq, GROUP_W), BF16),
        scratch_shapes=[scratch, scratch],
        compiler_params=_params(2),
        name="prompt_attn",
    )(qkv)


def _residue_allreduce(x, dil, op):
    shift = LANES // 2
    while shift >= dil:
        x = op(x, pltpu.roll(x, shift, axis=1))
        shift //= 2
    return x


def _lane_expand(rows8, dil, r0, t_new):
    lane = lax.broadcasted_iota(jnp.int32, (LANES, T_PAD), 0)
    t = lax.broadcasted_iota(jnp.int32, (LANES, T_PAD), 1)
    sel = jnp.where(((lane & (dil - 1)) == t - r0) & (t < r0 + t_new), 1.0, 0.0).astype(BF16)
    hi = rows8.astype(BF16)
    lo = (rows8 - hi.astype(F32)).astype(BF16)
    out = jnp.dot(sel, hi, preferred_element_type=F32) + jnp.dot(sel, lo, preferred_element_type=F32)
    return out.T


def _heads_to_rows(x):
    return jnp.broadcast_to(x[:, None, :], (HEADS, HEAD_DIM, LANES)).reshape(GROUP_W, LANES)


def _rows_to_heads_sum(x):
    return jnp.sum(x.reshape(HEADS, HEAD_DIM, LANES), axis=1)


def _buffer_group_lanes(c_ref, d, q8, kn8, vn8, dil, r0, t_new):
    width = c_ref.shape[2]
    tiles = width // LANES
    ql, kl, vl = (_lane_expand(a, dil, r0, t_new) for a in (q8, kn8, vn8))
    s = [_rows_to_heads_sum(c_ref[d, :GROUP_W, j * LANES:(j + 1) * LANES] * ql) for j in range(tiles)]
    s_new = _rows_to_heads_sum(kl * ql)
    top = s[0]
    for sj in s[1:]:
        top = jnp.maximum(top, sj)
    top = jnp.maximum(_residue_allreduce(top, dil, jnp.maximum), s_new)
    p = [jnp.exp(sj - top) for sj in s]
    p_new = jnp.exp(s_new - top)
    den = p[0]
    for pj in p[1:]:
        den = den + pj
    den = _residue_allreduce(den, dil, jnp.add) + p_new
    acc = c_ref[d, GROUP_W:, :LANES] * _heads_to_rows(p[0])
    for j in range(1, tiles):
        acc = acc + c_ref[d, GROUP_W:, j * LANES:(j + 1) * LANES] * _heads_to_rows(p[j])
    acc = (_residue_allreduce(acc, dil, jnp.add) + vl * _heads_to_rows(p_new)) * _heads_to_rows(1.0 / den)
    lse = _heads_to_rows(top + jnp.log(den))
    first = -(-r0 // dil) * dil
    return acc.T[first - r0:first - r0 + T_PAD], lse.T[first - r0:first - r0 + T_PAD]


def _buffer_group_dense(c_ref, d, q8, kn8, vn8, r0, t_new):
    rows = HEADS * T_PAD
    width = c_ref.shape[2]
    row = lax.broadcasted_iota(jnp.int32, (rows, GROUP_W), 0)
    lane = lax.broadcasted_iota(jnp.int32, (rows, GROUP_W), 1)
    head_sel = (row >> T_SHIFT) == (lane >> HEAD_SHIFT)
    t = (lax.broadcasted_iota(jnp.int32, (rows, width), 0) & (T_PAD - 1)) - r0
    i = lax.broadcasted_iota(jnp.int32, (rows, width), 1)
    tn = lax.broadcasted_iota(jnp.int32, (rows, T_PAD), 0) & (T_PAD - 1)
    j = lax.broadcasted_iota(jnp.int32, (rows, T_PAD), 1)
    nt = (((1,), (1,)), ((), ()))
    qm = jnp.where(head_sel, jnp.concatenate([q8] * HEADS, axis=0), 0.0).astype(BF16)
    kt = c_ref[d, :GROUP_W, :].astype(BF16)
    vt = c_ref[d, GROUP_W:, :].astype(BF16)
    sc = jnp.where(i >= t, jnp.dot(qm, kt, preferred_element_type=F32), NEG)
    own = (j <= tn) & (j >= r0) & (j < r0 + t_new)
    sn = jnp.where(own, lax.dot_general(qm, kn8.astype(BF16), nt, preferred_element_type=F32), NEG)
    m = jnp.maximum(jnp.max(sc, axis=-1, keepdims=True), jnp.max(sn, axis=-1, keepdims=True))
    pc = jnp.exp(sc - m)
    pn = jnp.exp(sn - m)
    l = jnp.sum(pc, axis=-1, keepdims=True) + jnp.sum(pn, axis=-1, keepdims=True)
    o = (lax.dot_general(pc.astype(BF16), vt, nt, preferred_element_type=F32)
         + jnp.dot(pn.astype(BF16), vn8.astype(BF16), preferred_element_type=F32))
    o = jnp.where(head_sel, o / l, 0.0)
    lse = jnp.where(head_sel, m + jnp.log(l), 0.0)
    fold = lambda x: sum(x[h * T_PAD:(h + 1) * T_PAD] for h in range(HEADS))
    return fold(o), fold(lse)


def _sample_attn_body(qkv_ref, c1_ref, c2_ref, c3_ref, o_ref):
    n_db = c1_ref.shape[0]
    t_new = T_PAD // n_db
    assert qkv_ref.shape[0] == T_PAD and n_db * t_new == T_PAD
    pad_rows = jnp.zeros((T_PAD - t_new, qkv_ref.shape[1]), F32)
    for d in range(n_db):
        new_rows = jnp.concatenate([qkv_ref[d * t_new:(d + 1) * t_new, :], pad_rows], axis=0)
        r0 = 0
        outs, lses = [], []
        for g, (c_ref, (_, dil)) in enumerate(zip((c1_ref, c2_ref, c3_ref), DIL_GROUPS)):
            cols = lambda part: slice(part * ATTN_W + g * GROUP_W, part * ATTN_W + (g + 1) * GROUP_W)
            q8 = new_rows[:, cols(0)] * SCALE
            kn8, vn8 = new_rows[:, cols(1)], new_rows[:, cols(2)]
            if dil == 1:
                o, lse = _buffer_group_dense(c_ref, d, q8, kn8, vn8, r0, t_new)
            else:
                o, lse = _buffer_group_lanes(c_ref, d, q8, kn8, vn8, dil, r0, t_new)
            outs.append(o)
            lses.append(lse)
        top = jnp.maximum(jnp.maximum(lses[0], lses[1]), lses[2])
        w = [jnp.exp(x - top) for x in lses]
        o8 = (w[0] * outs[0] + w[1] * outs[1] + w[2] * outs[2]) / (w[0] + w[1] + w[2])
        o_ref[d * t_new:(d + 1) * t_new, :] = o8[:t_new]


def _mix_value(o_a, o_b, ga_ref, gb_ref, x_ref, wao_ref, wbo_ref, wout_ref, gpost_ref):
    a = jnp.dot(o_a.astype(BF16), wao_ref[...], preferred_element_type=F32)
    b = jnp.dot(o_b.astype(BF16), wbo_ref[...], preferred_element_type=F32)
    merged = ga_ref[...].astype(F32) * a + gb_ref[...].astype(F32) * b
    t = jnp.dot(merged.astype(BF16), wout_ref[...], preferred_element_type=F32)
    return x_ref[...] + _rms(t, gpost_ref[...])


def _ffn_value(x, gpre_ref, wup_ref, wdn_ref, gpost_ref):
    h = _rms(x, gpre_ref[...]).astype(BF16)
    f = jnp.zeros(x.shape, F32)
    for c in range(D_FF // FF_CHUNK):
        cs = slice(c * FF_CHUNK, (c + 1) * FF_CHUNK)
        u = jnp.maximum(jnp.dot(h, wup_ref[:, cs], preferred_element_type=F32), 0.0)
        f = f + jnp.dot((u * u).astype(BF16), wdn_ref[cs, :], preferred_element_type=F32)
    return x + _rms(f, gpost_ref[...])


def _prompt_spatial_gate(z1_ref, zn_ref, wsp_ref, bsp_ref):
    r = lax.broadcasted_iota(jnp.int32, (CHUNK, CHUNK), 0)
    c = lax.broadcasted_iota(jnp.int32, (CHUNK, CHUNK), 1)
    n_chunks = z1_ref.shape[0] // CHUNK
    mixed = []
    for g in range(SG_GROUPS):
        wt = jnp.where(r >= c, wsp_ref[g], 0.0).astype(BF16)
        zg = jnp.concatenate([zn_ref[ci * CHUNK:(ci + 1) * CHUNK, g * CHUNK:(g + 1) * CHUNK].astype(BF16)
                              for ci in range(n_chunks)], axis=1)
        mixed.append(jnp.dot(wt, zg, preferred_element_type=F32))
    chunks = [jnp.concatenate([m[:, ci * CHUNK:(ci + 1) * CHUNK] for m in mixed], axis=1) + bsp_ref[...]
              for ci in range(n_chunks)]
    return z1_ref[...].astype(F32) * jnp.concatenate(chunks, axis=0)


def _tail_prompt_kernel(oa_ref, z1_ref, zn_ref, ga_ref, gb_ref, x_ref, wsp_ref, bsp_ref,
                        wao_ref, wbo_ref, wout_ref, gmix_ref, gpre_ref, wup_ref, wdn_ref, gpost_ref,
                        qkv_ref, c1_ref, c2_ref, c3_ref, y_ref, o_ref):
    _sample_attn_body(qkv_ref, c1_ref, c2_ref, c3_ref, o_ref)
    o_b = _prompt_spatial_gate(z1_ref, zn_ref, wsp_ref, bsp_ref)
    x1 = _mix_value(oa_ref[...], o_b, ga_ref, gb_ref, x_ref, wao_ref, wbo_ref, wout_ref, gmix_ref)
    y_ref[...] = _ffn_value(x1, gpre_ref, wup_ref, wdn_ref, gpost_ref)


def _tail_sample_kernel(oa_ref, z1_ref, zn_ref, ga_ref, gb_ref, x_ref, coef_ref, bsp_ref,
                        wao_ref, wbo_ref, wout_ref, gmix_ref, gpre_ref, wup_ref, wdn_ref, gpost_ref, y_ref):
    zn = zn_ref[...]
    mix = coef_ref[0] * zn
    for d in range(1, coef_ref.shape[0]):
        mix = mix + coef_ref[d] * pltpu.roll(zn, d, axis=0)
    o_b = z1_ref[...] * (mix + bsp_ref[...])
    x1 = _mix_value(oa_ref[...], o_b, ga_ref, gb_ref, x_ref, wao_ref, wbo_ref, wout_ref, gmix_ref)
    y_ref[...] = _ffn_value(x1, gpre_ref, wup_ref, wdn_ref, gpost_ref)


def _tail_weight_specs():
    return [_const_spec((GROUP_W, D_MODEL)), _const_spec((SG_W, D_MODEL)), _const_spec((D_MODEL, D_MODEL)),
            _const_spec((1, D_MODEL)), _const_spec((1, D_MODEL)), _const_spec((D_MODEL, D_FF)),
            _const_spec((D_FF, D_MODEL)), _const_spec((1, D_MODEL))]


def _tail_prompt(o_a, z1, zn, ga, gb, x, wsp, bsp, weights, qkv_s, caches):
    n, db, ns = x.shape[0], caches[0].shape[0], qkv_s.shape[0]
    tile = SAMPLE_DB_TILE
    steps = db // tile
    tm = n // steps
    assert db % tile == 0 and n % steps == 0 and tm % CHUNK == 0 and ns == steps * T_PAD
    row = lambda w: pl.BlockSpec((tm, w), lambda i: (i, 0))
    blk = lambda a: pl.BlockSpec((tile,) + a.shape[1:], lambda i: (i, 0, 0))
    new_rows = lambda w: pl.BlockSpec((T_PAD, w), lambda i: (i, 0))
    return pl.pallas_call(
        _tail_prompt_kernel,
        grid=(steps,),
        in_specs=[row(GROUP_W), row(SG_W), row(SG_W), row(D_MODEL), row(D_MODEL), row(D_MODEL),
                  _const_spec((SG_GROUPS, CHUNK, CHUNK)), _const_spec((CHUNK, SG_W))] + _tail_weight_specs()
        + [new_rows(Z_OFF)] + [blk(a) for a in caches],
        out_specs=[row(D_MODEL), new_rows(GROUP_W)],
        out_shape=[jax.ShapeDtypeStruct((n, D_MODEL), F32),
                   jax.ShapeDtypeStruct((ns, GROUP_W), F32)],
        compiler_params=_params(1),
        name="tail_prompt",
    )(o_a, z1, zn, ga, gb, x, wsp, bsp, *weights, qkv_s, *caches)


def _tail_sample(o_a, z1, zn, ga, gb, x, coef, bsp, weights):
    n = x.shape[0]
    tm = coef.shape[1]
    row = lambda w: pl.BlockSpec((tm, w), lambda i: (i, 0))
    return pl.pallas_call(
        _tail_sample_kernel,
        grid=(n // tm,),
        in_specs=[row(GROUP_W), row(SG_W), row(SG_W), row(D_MODEL), row(D_MODEL), row(D_MODEL),
                  _const_spec(coef.shape), _const_spec(bsp.shape)] + _tail_weight_specs(),
        out_specs=row(D_MODEL),
        out_shape=jax.ShapeDtypeStruct((n, D_MODEL), F32),
        compiler_params=_params(1),
        name="tail_sample",
    )(o_a, z1, zn, ga, gb, x, coef, bsp, *weights)


def _kv_from_feature_major(kv_t):
    batch, _, keep = kv_t.shape
    return kv_t.reshape(batch, 2, HEADS, HEAD_DIM, keep).transpose(0, 4, 1, 2, 3)[None]


def kernel(x_prompt, x_sample, cache_kv_w128, cache_kv_w512, cache_kv_w2048, norm_pre_mix, w_in, b_gate,
           ln_z_g, ln_z_b, w_spatial, b_spatial, w_ao, w_bo, w_out, norm_post_mix, norm_pre_ffn, w_up, w_down,
           norm_post_ffn):
    assert w_in.shape[0] == 1, "single-layer problem"
    batch, seq, _ = x_prompt.shape
    dbatch, t_new, _ = x_sample.shape
    caches_in = (cache_kv_w128[0], cache_kv_w512[0], cache_kv_w2048[0])

    wsp = w_spatial[0]
    bsp_rows = jnp.repeat(b_spatial[0].T, CHUNK, axis=1)

    ns = dbatch * t_new
    xs = x_sample.reshape(ns, D_MODEL)
    qkv_s, z1_s, zn_s, ga_s, gb_s, w_in_bf, wao, wbo, wout = _project_sample(
        xs, norm_pre_mix, w_in[0], b_gate, ln_z_g, ln_z_b, (w_ao[0], w_bo[0], w_out[0]))
    caches, kv_sample = [], []
    for gi, (win, dil) in enumerate(DIL_GROUPS):
        buf = caches_in[gi]
        wb = buf.shape[1]
        assert wb == win and t_new <= T_PAD, "full window buffers"
        caches.append(buf.transpose(0, 2, 3, 4, 1).reshape(dbatch, 2 * GROUP_W, wb))
        gs = slice(gi * GROUP_W, (gi + 1) * GROUP_W)
        kn = qkv_s[:, ATTN_W:2 * ATTN_W][:, gs].reshape(dbatch, t_new, HEADS, HEAD_DIM)
        vn = qkv_s[:, 2 * ATTN_W:][:, gs].reshape(dbatch, t_new, HEADS, HEAD_DIM)
        kv_sample.append(jnp.stack([kn, vn], axis=2)[None])

    xp = x_prompt.reshape(batch * seq, D_MODEL)
    qkv, kv1_t, kv2_t, kv3_t, z1, zn, ga, gb, wup, wdn = _project_prompt(
        x_prompt, norm_pre_mix, w_in_bf, b_gate, ln_z_g, ln_z_b, (w_up[0], w_down[0]))
    weights = (wao, wbo, wout, norm_post_mix, norm_pre_ffn, wup, wdn, norm_post_ffn)
    kv_prompt = [_kv_from_feature_major(a) for a in (kv1_t, kv2_t, kv3_t)]
    o_a_p = _prompt_attention(qkv, seq).reshape(batch * seq, GROUP_W)
    assert SAMPLE_DB_TILE * t_new == T_PAD, "a tail step takes T_PAD new rows"
    y_prompt, o_a = _tail_prompt(o_a_p, z1, zn, ga, gb, xp, wsp, bsp_rows, weights, qkv_s, caches)
    y_prompt = y_prompt.reshape(batch, seq, D_MODEL)

    wt = jnp.tril(wsp)[:, :t_new, :t_new]
    coef = jnp.stack([jnp.where(jnp.arange(t_new)[None, :] >= d,
                                wt[:, jnp.arange(t_new), jnp.maximum(jnp.arange(t_new) - d, 0)], 0.0)
                      for d in range(t_new)])
    coef = jnp.repeat(coef.transpose(0, 2, 1), CHUNK, axis=2)
    reps = SAMPLE_TAIL_TILE // t_new
    coef = jnp.tile(coef, (1, reps, 1))
    bsp_s = jnp.tile(bsp_rows[:t_new], (reps, 1))
    y_sample = _tail_sample(o_a, z1_s, zn_s, ga_s, gb_s, xs, coef, bsp_s, weights).reshape(dbatch, t_new, D_MODEL)
    sg_sample = zn_s.reshape(1, dbatch, t_new, SG_W)

    return (y_prompt, y_sample, kv_prompt[0], kv_prompt[1], kv_prompt[2],
            kv_sample[0], kv_sample[1], kv_sample[2], sg_sample)
```

```python
import math

import jax
import jax.numpy as jnp
from jax import lax
from jax.experimental import pallas as pl
from jax.experimental.pallas import tpu as pltpu

F32 = jnp.float32
BF16 = jnp.bfloat16

D_MODEL = 1024
HEAD_DIM = 64
HEADS = 4
GROUP_W = HEADS * HEAD_DIM
DIL_GROUPS = ((128, 1), (512, 4), (2048, 16))
N_GROUPS = len(DIL_GROUPS)
ATTN_W = N_GROUPS * GROUP_W
BLOCK = 128
CHUNK = 128
SG_GROUPS = 4
SG_W = 512
D_FF = 4 * D_MODEL
Z_OFF = 3 * ATTN_W
G_OFF = Z_OFF + 2 * SG_W
IN_W = G_OFF + 2 * D_MODEL
EPS = 1e-6
NEG = -1e30
SCALE = HEAD_DIM ** -0.5
INV_SQRT2 = 1.0 / math.sqrt(2.0)
LOG2E = 1.0 / math.log(2.0)
LANES = 128
SLABS = GROUP_W // LANES
HEADS_PER_SLAB = LANES // HEAD_DIM
HEAD_SHIFT = HEAD_DIM.bit_length() - 1
VMEM_LIMIT = 56 * 1024 * 1024

TOKEN_TILE = 512
BLOCK_UNROLL = 8
SAMPLE_DB_TILE = 2
T_PAD = 8
T_SHIFT = T_PAD.bit_length() - 1
BF16_SUBLANES = 16
CAST_COLS = 768
SAMPLE_TAIL_TILE = 256
FF_CHUNK = 1024


def _params(n_axes):
    return pltpu.CompilerParams(dimension_semantics=("arbitrary",) * n_axes,
                                vmem_limit_bytes=VMEM_LIMIT)


def _const_spec(shape):
    return pl.BlockSpec(shape, lambda *_: (0,) * len(shape), pipeline_mode=pl.Buffered(1))


def _rms(x, g):
    return x * lax.rsqrt(jnp.mean(x * x, axis=-1, keepdims=True) + EPS) * g


def _proj_gate_branches(h, w_ref, bg_ref, lng_ref, lnb_ref, z1_ref, zn_ref, ga_ref, gb_ref):
    z = jnp.dot(h, w_ref[:, Z_OFF:G_OFF], preferred_element_type=F32)
    z = 0.5 * z * (1.0 + lax.erf(z * INV_SQRT2))
    z1_ref[...] = z[:, :SG_W].astype(z1_ref.dtype)
    z2 = z[:, SG_W:]
    mu = jnp.mean(z2, axis=-1, keepdims=True)
    zc = z2 - mu
    var = jnp.mean(zc * zc, axis=-1, keepdims=True)
    zn_ref[...] = (zc * lax.rsqrt(var + EPS) * lng_ref[...] + lnb_ref[...]).astype(zn_ref.dtype)

    gates = jnp.dot(h, w_ref[:, G_OFF:], preferred_element_type=F32) + bg_ref[...]
    gates = 1.0 / (1.0 + jnp.exp(-gates))
    ga_ref[...] = gates[:, :D_MODEL].astype(BF16)
    gb_ref[...] = gates[:, D_MODEL:].astype(BF16)


def _cast_blocks(pairs):
    for src_ref, dst_ref in pairs:
        dst_ref[...] = src_ref[...].astype(dst_ref.dtype)


def _proj_sample_kernel(x_ref, g_ref, wf_ref, bg_ref, lng_ref, lnb_ref, wao_ref, wbo_ref, wout_ref,
                        qkv_ref, z1_ref, zn_ref, ga_ref, gb_ref, w_ref, wao_bf_ref, wbo_bf_ref, wout_bf_ref):
    @pl.when(pl.program_id(0) == 0)
    def _():
        for c in range(0, IN_W, CAST_COLS):
            w_ref[:, c:c + CAST_COLS] = wf_ref[:, c:c + CAST_COLS].astype(BF16)

    _cast_blocks(((wao_ref, wao_bf_ref), (wbo_ref, wbo_bf_ref), (wout_ref, wout_bf_ref)))
    h = _rms(x_ref[...], g_ref[...]).astype(BF16)
    qkv_ref[...] = jnp.dot(h, w_ref[:, :Z_OFF], preferred_element_type=F32)
    _proj_gate_branches(h, w_ref, bg_ref, lng_ref, lnb_ref, z1_ref, zn_ref, ga_ref, gb_ref)


def _proj_prompt_kernel(x_ref, g_ref, w_ref, bg_ref, lng_ref, lnb_ref, wup_ref, wdn_ref,
                        qkv_ref, kv1_ref, kv2_ref, kv3_ref, z1_ref, zn_ref, ga_ref, gb_ref, wup_bf_ref, wdn_bf_ref):
    _cast_blocks(((wup_ref, wup_bf_ref), (wdn_ref, wdn_bf_ref)))
    h = _rms(x_ref[0], g_ref[...]).astype(BF16)
    _proj_gate_branches(h, w_ref, bg_ref, lng_ref, lnb_ref, z1_ref, zn_ref, ga_ref, gb_ref)
    def project(first_part, n_parts):
        cols = jnp.dot(h, w_ref[:, first_part * ATTN_W:(first_part + n_parts) * ATTN_W], preferred_element_type=F32)
        for part in range(n_parts):
            for g in range(N_GROUPS):
                for s in range(SLABS):
                    col = part * ATTN_W + g * GROUP_W + s * LANES
                    qkv_ref[0, g, (first_part + part) * SLABS + s] = cols[:, col:col + LANES]

    def kv_t(g):
        return jnp.concatenate([qkv_ref[0, g, slab].T for slab in range(SLABS, 3 * SLABS)], axis=0)

    tm = x_ref.shape[1]
    project(1, 2)
    kv3_ref[0] = kv_t(2)
    project(0, 1)

    @pl.when(pl.program_id(1) == pl.num_programs(1) - 1)
    def _():
        kv2_ref[0] = kv_t(1)[:, tm - kv2_ref.shape[2]:]
        kv1_ref[0] = kv_t(0)[:, tm - kv1_ref.shape[2]:]


def _proj_weight_specs():
    return [_const_spec((1, D_MODEL)), _const_spec((D_MODEL, IN_W)),
            _const_spec((1, 2 * D_MODEL)), _const_spec((1, SG_W)), _const_spec((1, SG_W))]


def _gate_branch_shapes(n, z_dtype):
    return [jax.ShapeDtypeStruct((n, SG_W), z_dtype), jax.ShapeDtypeStruct((n, SG_W), z_dtype),
            jax.ShapeDtypeStruct((n, D_MODEL), BF16), jax.ShapeDtypeStruct((n, D_MODEL), BF16)]


def _row_slices(arrays, steps):
    specs, shapes = [], []
    for a in arrays:
        rows = a.shape[0] // steps
        assert rows * steps == a.shape[0] and rows % BF16_SUBLANES == 0
        specs.append((rows, a.shape[1]))
        shapes.append(jax.ShapeDtypeStruct(a.shape, BF16))
    return specs, shapes


def _project_sample(x, g, w_in, bg, lng, lnb, mix_weights):
    n = x.shape[0]
    tm = min(SAMPLE_TAIL_TILE, n)
    steps = n // tm
    row = lambda w: pl.BlockSpec((tm, w), lambda i: (i, 0))
    slices, bf_shapes = _row_slices(mix_weights, steps)
    slice_specs = [pl.BlockSpec(s, lambda i: (i, 0)) for s in slices]
    return pl.pallas_call(
        _proj_sample_kernel,
        grid=(steps,),
        in_specs=[row(D_MODEL)] + _proj_weight_specs() + slice_specs,
        out_specs=[row(Z_OFF), row(SG_W), row(SG_W), row(D_MODEL), row(D_MODEL),
                   pl.BlockSpec((D_MODEL, IN_W), lambda i: (0, 0))] + slice_specs,
        out_shape=[jax.ShapeDtypeStruct((n, Z_OFF), F32)] + _gate_branch_shapes(n, F32)
        + [jax.ShapeDtypeStruct((D_MODEL, IN_W), BF16)] + bf_shapes,
        compiler_params=_params(1),
        name="proj_sample",
    )(x, g, w_in, bg, lng, lnb, *mix_weights)


def _project_prompt(x, g, w_bf, bg, lng, lnb, ffn_weights):
    batch, seq, _ = x.shape
    tm = TOKEN_TILE
    tiles = seq // tm
    keeps = [min(win, seq) for win, _ in DIL_GROUPS]
    assert seq % tm == 0 and keeps[0] <= tm and keeps[1] <= tm and keeps[2] == seq
    row = lambda w: pl.BlockSpec((tm, w), lambda b, t: (b * tiles + t, 0))
    last = lambda keep: pl.BlockSpec((1, 2 * GROUP_W, keep), lambda b, t: (b, 0, 0))
    n = batch * seq
    slices, bf_shapes = _row_slices(ffn_weights, batch * tiles)
    slice_specs = [pl.BlockSpec(s, lambda b, t: (b * tiles + t, 0)) for s in slices]
    return pl.pallas_call(
        _proj_prompt_kernel,
        grid=(batch, tiles),
        in_specs=[pl.BlockSpec((1, tm, D_MODEL), lambda b, t: (b, t, 0))] + _proj_weight_specs() + slice_specs,
        out_specs=[pl.BlockSpec((1, N_GROUPS, 3 * SLABS, tm, LANES), lambda b, t: (b, 0, 0, t, 0)),
                   last(keeps[0]), last(keeps[1]),
                   pl.BlockSpec((1, 2 * GROUP_W, tm), lambda b, t: (b, 0, t)),
                   row(SG_W), row(SG_W), row(D_MODEL), row(D_MODEL)] + slice_specs,
        out_shape=[jax.ShapeDtypeStruct((batch, N_GROUPS, 3 * SLABS, seq, LANES), F32),
                   jax.ShapeDtypeStruct((batch, 2 * GROUP_W, keeps[0]), F32),
                   jax.ShapeDtypeStruct((batch, 2 * GROUP_W, keeps[1]), F32),
                   jax.ShapeDtypeStruct((batch, 2 * GROUP_W, keeps[2]), F32)] + _gate_branch_shapes(n, BF16)
        + bf_shapes,
        compiler_params=_params(2),
        name="proj_prompt",
    )(x, g, w_bf, bg, lng, lnb, *ffn_weights)


def _head_lane_mask(shape, hh):
    lane = lax.broadcasted_iota(jnp.int32, shape, len(shape) - 1)
    return (lane >> HEAD_SHIFT) == hh


def _band_block(q, k, v, mask):
    nt = (((1,), (1,)), ((), ()))
    assert HEADS_PER_SLAB == 2
    lm0 = _head_lane_mask((BLOCK, LANES), 0)
    mask2 = jnp.concatenate([mask, mask], axis=0)
    outs, lses = [], []
    for s in range(SLABS):
        zero = jnp.zeros_like(q[s])
        qm = jnp.concatenate([jnp.where(lm0, q[s], zero), jnp.where(lm0, zero, q[s])], axis=0)
        sc = lax.dot_general(qm, k[s], nt, preferred_element_type=F32)
        sc = jnp.where(mask2, sc, NEG)
        m = jnp.max(sc, axis=-1, keepdims=True)
        p = jnp.exp2(sc - m)
        l = jnp.sum(p, axis=-1, keepdims=True)
        o = jnp.dot(p.astype(BF16), v[s], preferred_element_type=F32) * (1.0 / l)
        lse = m + jnp.log2(l)
        outs.append(jnp.where(lm0, o[:BLOCK], o[BLOCK:]))
        lses.append(jnp.where(lm0, lse[:BLOCK], lse[BLOCK:]))
    return outs, lses


def _group_attention(qkv_ref, o_scr, lse_scr, g, dil, seq):
    n_qb = seq // (dil * BLOCK)
    qb_shift = n_qb.bit_length() - 1
    span = dil * BLOCK
    rows = (lambda start: pl.ds(start, BLOCK, stride=dil)) if dil > 1 else (lambda start: pl.ds(start, BLOCK))
    nk = 2 * BLOCK if n_qb > 1 else BLOCK
    qi = lax.broadcasted_iota(jnp.int32, (BLOCK, nk), 0)
    ki = lax.broadcasted_iota(jnp.int32, (BLOCK, nk), 1)

    def load(slab, start):
        return qkv_ref[0, 0, slab, rows(start), :]

    def body(n, carry):
        r = n >> qb_shift
        qb = n & (n_qb - 1)
        cur = r + span * qb
        if dil == 1:
            cur = pl.multiple_of(cur, BLOCK)
        q = [(load(s, cur) * (SCALE * LOG2E)).astype(BF16) for s in range(SLABS)]
        k = [load(SLABS + s, cur).astype(BF16) for s in range(SLABS)]
        v = [load(2 * SLABS + s, cur).astype(BF16) for s in range(SLABS)]
        if n_qb > 1:
            prev = r + span * jnp.maximum(qb - 1, 0)
            if dil == 1:
                prev = pl.multiple_of(prev, BLOCK)
            k = [jnp.concatenate([load(SLABS + s, prev).astype(BF16), k[s]], axis=0) for s in range(SLABS)]
            v = [jnp.concatenate([load(2 * SLABS + s, prev).astype(BF16), v[s]], axis=0) for s in range(SLABS)]
            mask = (ki >= qi) & (ki <= qi + BLOCK) & (ki >= jnp.where(qb > 0, 0, BLOCK))
        else:
            mask = ki <= qi
        outs, lses = _band_block(q, k, v, mask)
        for s in range(SLABS):
            o_scr[g, s, rows(cur), :] = outs[s]
            lse_scr[g, s, rows(cur), :] = lses[s]
        return carry

    lax.fori_loop(0, dil * n_qb, body, 0, unroll=BLOCK_UNROLL)


def _prompt_attn_kernel(qkv_ref, o_ref, o_scr, lse_scr):
    g = pl.program_id(1)
    seq = o_ref.shape[1]
    for gi, (_, dil) in enumerate(DIL_GROUPS):
        @pl.when(g == gi)
        def _(gi=gi, dil=dil):
            _group_attention(qkv_ref, o_scr, lse_scr, gi, dil, seq)

    @pl.when(g == N_GROUPS - 1)
    def _():
        def merge(i, carry):
            rows = pl.ds(pl.multiple_of(i * BLOCK, BLOCK), BLOCK)
            for s in range(SLABS):
                ls = [lse_scr[gi, s, rows, :] for gi in range(N_GROUPS)]
                top = jnp.maximum(jnp.maximum(ls[0], ls[1]), ls[2])
                e = [jnp.exp2(x - top) for x in ls]
                num = e[0] * o_scr[0, s, rows, :] + e[1] * o_scr[1, s, rows, :] + e[2] * o_scr[2, s, rows, :]
                o_ref[0, rows, s * LANES:(s + 1) * LANES] = (num / (e[0] + e[1] + e[2])).astype(o_ref.dtype)
            return carry
        lax.fori_loop(0, seq // BLOCK, merge, 0)


def _prompt_attention(qkv, seq):
    batch = qkv.shape[0]
    scratch = pltpu.VMEM((N_GROUPS, SLABS, seq, LANES), F32)
    return pl.pallas_call(
        _prompt_attn_kernel,
        grid=(batch, N_GROUPS),
        in_specs=[pl.BlockSpec((1, 1, 3 * SLABS, seq, LANES), lambda b, g: (b, g, 0, 0, 0))],
        out_specs=pl.BlockSpec((1, seq, GROUP_W), lambda b, g: (b, 0, 0)),
        out_shape=jax.ShapeDtypeStruct((batch, seq, GROUP_W), BF16),
        scratch_shapes=[scratch, scratch],
        compiler_params=_params(2),
        name="prompt_attn",
    )(qkv)


def _residue_allreduce(x, dil, op):
    shift = LANES // 2
    while shift >= dil:
        x = op(x, pltpu.roll(x, shift, axis=1))
        shift //= 2
    return x


def _lane_expand(rows8, dil, r0, t_new):
    lane = lax.broadcasted_iota(jnp.int32, (LANES, T_PAD), 0)
    t = lax.broadcasted_iota(jnp.int32, (LANES, T_PAD), 1)
    sel = jnp.where(((lane & (dil - 1)) == t - r0) & (t < r0 + t_new), 1.0, 0.0).astype(BF16)
    hi = rows8.astype(BF16)
    lo = (rows8 - hi.astype(F32)).astype(BF16)
    out = jnp.dot(sel, hi, preferred_element_type=F32) + jnp.dot(sel, lo, preferred_element_type=F32)
    return out.T


def _heads_to_rows(x):
    return jnp.broadcast_to(x[:, None, :], (HEADS, HEAD_DIM, LANES)).reshape(GROUP_W, LANES)


def _rows_to_heads_sum(x):
    return jnp.sum(x.reshape(HEADS, HEAD_DIM, LANES), axis=1)


def _buffer_group_lanes(c_ref, d, q8, kn8, vn8, dil, r0, t_new):
    width = c_ref.shape[2]
    tiles = width // LANES
    ql, kl, vl = (_lane_expand(a, dil, r0, t_new) for a in (q8, kn8, vn8))
    s = [_rows_to_heads_sum(c_ref[d, :GROUP_W, j * LANES:(j + 1) * LANES] * ql) for j in range(tiles)]
    s_new = _rows_to_heads_sum(kl * ql)
    top = s[0]
    for sj in s[1:]:
        top = jnp.maximum(top, sj)
    top = jnp.maximum(_residue_allreduce(top, dil, jnp.maximum), s_new)
    p = [jnp.exp(sj - top) for sj in s]
    p_new = jnp.exp(s_new - top)
    den = p[0]
    for pj in p[1:]:
        den = den + pj
    den = _residue_allreduce(den, dil, jnp.add) + p_new
    acc = c_ref[d, GROUP_W:, :LANES] * _heads_to_rows(p[0])
    for j in range(1, tiles):
        acc = acc + c_ref[d, GROUP_W:, j * LANES:(j + 1) * LANES] * _heads_to_rows(p[j])
    acc = (_residue_allreduce(acc, dil, jnp.add) + vl * _heads_to_rows(p_new)) * _heads_to_rows(1.0 / den)
    lse = _heads_to_rows(top + jnp.log(den))
    first = -(-r0 // dil) * dil
    return acc.T[first - r0:first - r0 + T_PAD], lse.T[first - r0:first - r0 + T_PAD]


def _buffer_group_dense(c_ref, d, q8, kn8, vn8, r0, t_new):
    rows = HEADS * T_PAD
    width = c_ref.shape[2]
    row = lax.broadcasted_iota(jnp.int32, (rows, GROUP_W), 0)
    lane = lax.broadcasted_iota(jnp.int32, (rows, GROUP_W), 1)
    head_sel = (row >> T_SHIFT) == (lane >> HEAD_SHIFT)
    t = (lax.broadcasted_iota(jnp.int32, (rows, width), 0) & (T_PAD - 1)) - r0
    i = lax.broadcasted_iota(jnp.int32, (rows, width), 1)
    tn = lax.broadcasted_iota(jnp.int32, (rows, T_PAD), 0) & (T_PAD - 1)
    j = lax.broadcasted_iota(jnp.int32, (rows, T_PAD), 1)
    nt = (((1,), (1,)), ((), ()))
    qm = jnp.where(head_sel, jnp.concatenate([q8] * HEADS, axis=0), 0.0).astype(BF16)
    kt = c_ref[d, :GROUP_W, :].astype(BF16)
    vt = c_ref[d, GROUP_W:, :].astype(BF16)
    sc = jnp.where(i >= t, jnp.dot(qm, kt, preferred_element_type=F32), NEG)
    own = (j <= tn) & (j >= r0) & (j < r0 + t_new)
    sn = jnp.where(own, lax.dot_general(qm, kn8.astype(BF16), nt, preferred_element_type=F32), NEG)
    m = jnp.maximum(jnp.max(sc, axis=-1, keepdims=True), jnp.max(sn, axis=-1, keepdims=True))
    pc = jnp.exp(sc - m)
    pn = jnp.exp(sn - m)
    l = jnp.sum(pc, axis=-1, keepdims=True) + jnp.sum(pn, axis=-1, keepdims=True)

    def finish():
        o = (lax.dot_general(pc.astype(BF16), vt, nt, preferred_element_type=F32)
             + jnp.dot(pn.astype(BF16), vn8.astype(BF16), preferred_element_type=F32))
        o = jnp.where(head_sel, o / l, 0.0)
        lse = jnp.where(head_sel, m + jnp.log(l), 0.0)
        fold = lambda x: sum(x[h * T_PAD:(h + 1) * T_PAD] for h in range(HEADS))
        return fold(o), fold(lse)
    return finish


def _sample_attn_start(qkv_ref, c1_ref, c2_ref, c3_ref, o_ref):
    n_db = c1_ref.shape[0]
    t_new = T_PAD // n_db
    assert qkv_ref.shape[0] == T_PAD and n_db * t_new == T_PAD
    pad_rows = jnp.zeros((T_PAD - t_new, qkv_ref.shape[1]), F32)
    pending = []
    for d in range(n_db):
        new_rows = jnp.concatenate([qkv_ref[d * t_new:(d + 1) * t_new, :], pad_rows], axis=0)
        groups = []
        for g, (c_ref, (_, dil)) in enumerate(zip((c1_ref, c2_ref, c3_ref), DIL_GROUPS)):
            cols = lambda part: slice(part * ATTN_W + g * GROUP_W, part * ATTN_W + (g + 1) * GROUP_W)
            q8 = new_rows[:, cols(0)] * SCALE
            kn8, vn8 = new_rows[:, cols(1)], new_rows[:, cols(2)]
            if dil == 1:
                groups.append(_buffer_group_dense(c_ref, d, q8, kn8, vn8, 0, t_new))
            else:
                result = _buffer_group_lanes(c_ref, d, q8, kn8, vn8, dil, 0, t_new)
                groups.append(lambda result=result: result)
        pending.append(groups)

    def finish():
        for d, groups in enumerate(pending):
            outs, lses = zip(*(group() for group in groups))
            top = jnp.maximum(jnp.maximum(lses[0], lses[1]), lses[2])
            w = [jnp.exp(x - top) for x in lses]
            o8 = (w[0] * outs[0] + w[1] * outs[1] + w[2] * outs[2]) / (w[0] + w[1] + w[2])
            o_ref[d * t_new:(d + 1) * t_new, :] = o8[:t_new]
    return finish


def _mix_value(o_a, o_b, ga_ref, gb_ref, x_ref, wao_ref, wbo_ref, wout_ref, gpost_ref):
    a = jnp.dot(o_a.astype(BF16), wao_ref[...], preferred_element_type=F32)
    b = jnp.dot(o_b.astype(BF16), wbo_ref[...], preferred_element_type=F32)
    merged = ga_ref[...].astype(F32) * a + gb_ref[...].astype(F32) * b
    t = jnp.dot(merged.astype(BF16), wout_ref[...], preferred_element_type=F32)
    return x_ref[...] + _rms(t, gpost_ref[...])


def _ffn_value(x, gpre_ref, wup_ref, wdn_ref, gpost_ref):
    h = _rms(x, gpre_ref[...]).astype(BF16)
    f = jnp.zeros(x.shape, F32)
    for c in range(D_FF // FF_CHUNK):
        cs = slice(c * FF_CHUNK, (c + 1) * FF_CHUNK)
        u = jnp.maximum(jnp.dot(h, wup_ref[:, cs], preferred_element_type=F32), 0.0)
        f = f + jnp.dot((u * u).astype(BF16), wdn_ref[cs, :], preferred_element_type=F32)
    return x + _rms(f, gpost_ref[...])


def _prompt_spatial_gate(z1_ref, zn_ref, wsp_ref, bsp_ref):
    r = lax.broadcasted_iota(jnp.int32, (CHUNK, CHUNK), 0)
    c = lax.broadcasted_iota(jnp.int32, (CHUNK, CHUNK), 1)
    n_chunks = z1_ref.shape[0] // CHUNK
    mixed = []
    for g in range(SG_GROUPS):
        wt = jnp.where(r >= c, wsp_ref[g], 0.0).astype(BF16)
        zg = jnp.concatenate([zn_ref[ci * CHUNK:(ci + 1) * CHUNK, g * CHUNK:(g + 1) * CHUNK].astype(BF16)
                              for ci in range(n_chunks)], axis=1)
        mixed.append(jnp.dot(wt, zg, preferred_element_type=F32))
    chunks = [jnp.concatenate([m[:, ci * CHUNK:(ci + 1) * CHUNK] for m in mixed], axis=1) + bsp_ref[...]
              for ci in range(n_chunks)]
    return z1_ref[...].astype(F32) * jnp.concatenate(chunks, axis=0)


def _tail_prompt_kernel(oa_ref, z1_ref, zn_ref, ga_ref, gb_ref, x_ref, wsp_ref, bsp_ref,
                        wao_ref, wbo_ref, wout_ref, gmix_ref, gpre_ref, wup_ref, wdn_ref, gpost_ref,
                        qkv_ref, c1_ref, c2_ref, c3_ref, y_ref, o_ref):
    finish_sample_attn = _sample_attn_start(qkv_ref, c1_ref, c2_ref, c3_ref, o_ref)
    o_b = _prompt_spatial_gate(z1_ref, zn_ref, wsp_ref, bsp_ref)
    x1 = _mix_value(oa_ref[...], o_b, ga_ref, gb_ref, x_ref, wao_ref, wbo_ref, wout_ref, gmix_ref)
    y_ref[...] = _ffn_value(x1, gpre_ref, wup_ref, wdn_ref, gpost_ref)
    finish_sample_attn()


def _tail_sample_kernel(oa_ref, z1_ref, zn_ref, ga_ref, gb_ref, x_ref, coef_ref, bsp_ref,
                        wao_ref, wbo_ref, wout_ref, gmix_ref, gpre_ref, wup_ref, wdn_ref, gpost_ref, y_ref):
    zn = zn_ref[...]
    mix = coef_ref[0] * zn
    for d in range(1, coef_ref.shape[0]):
        mix = mix + coef_ref[d] * pltpu.roll(zn, d, axis=0)
    o_b = z1_ref[...] * (mix + bsp_ref[...])
    x1 = _mix_value(oa_ref[...], o_b, ga_ref, gb_ref, x_ref, wao_ref, wbo_ref, wout_ref, gmix_ref)
    y_ref[...] = _ffn_value(x1, gpre_ref, wup_ref, wdn_ref, gpost_ref)


def _tail_weight_specs():
    return [_const_spec((GROUP_W, D_MODEL)), _const_spec((SG_W, D_MODEL)), _const_spec((D_MODEL, D_MODEL)),
            _const_spec((1, D_MODEL)), _const_spec((1, D_MODEL)), _const_spec((D_MODEL, D_FF)),
            _const_spec((D_FF, D_MODEL)), _const_spec((1, D_MODEL))]


def _tail_prompt(o_a, z1, zn, ga, gb, x, wsp, bsp, weights, qkv_s, caches):
    n, db, ns = x.shape[0], caches[0].shape[0], qkv_s.shape[0]
    tile = SAMPLE_DB_TILE
    steps = db // tile
    tm = n // steps
    assert db % tile == 0 and n % steps == 0 and tm % CHUNK == 0 and ns == steps * T_PAD
    row = lambda w: pl.BlockSpec((tm, w), lambda i: (i, 0))
    blk = lambda a: pl.BlockSpec((tile,) + a.shape[1:], lambda i: (i, 0, 0))
    new_rows = lambda w: pl.BlockSpec((T_PAD, w), lambda i: (i, 0))
    return pl.pallas_call(
        _tail_prompt_kernel,
        grid=(steps,),
        in_specs=[row(GROUP_W), row(SG_W), row(SG_W), row(D_MODEL), row(D_MODEL), row(D_MODEL),
                  _const_spec((SG_GROUPS, CHUNK, CHUNK)), _const_spec((CHUNK, SG_W))] + _tail_weight_specs()
        + [new_rows(Z_OFF)] + [blk(a) for a in caches],
        out_specs=[row(D_MODEL), new_rows(GROUP_W)],
        out_shape=[jax.ShapeDtypeStruct((n, D_MODEL), F32),
                   jax.ShapeDtypeStruct((ns, GROUP_W), F32)],
        compiler_params=_params(1),
        name="tail_prompt",
    )(o_a, z1, zn, ga, gb, x, wsp, bsp, *weights, qkv_s, *caches)


def _tail_sample(o_a, z1, zn, ga, gb, x, coef, bsp, weights):
    n = x.shape[0]
    tm = coef.shape[1]
    row = lambda w: pl.BlockSpec((tm, w), lambda i: (i, 0))
    return pl.pallas_call(
        _tail_sample_kernel,
        grid=(n // tm,),
        in_specs=[row(GROUP_W), row(SG_W), row(SG_W), row(D_MODEL), row(D_MODEL), row(D_MODEL),
                  _const_spec(coef.shape), _const_spec(bsp.shape)] + _tail_weight_specs(),
        out_specs=row(D_MODEL),
        out_shape=jax.ShapeDtypeStruct((n, D_MODEL), F32),
        compiler_params=_params(1),
        name="tail_sample",
    )(o_a, z1, zn, ga, gb, x, coef, bsp, *weights)


def _kv_from_feature_major(kv_t):
    batch, _, keep = kv_t.shape
    return kv_t.reshape(batch, 2, HEADS, HEAD_DIM, keep).transpose(0, 4, 1, 2, 3)[None]


def kernel(x_prompt, x_sample, cache_kv_w128, cache_kv_w512, cache_kv_w2048, norm_pre_mix, w_in, b_gate,
           ln_z_g, ln_z_b, w_spatial, b_spatial, w_ao, w_bo, w_out, norm_post_mix, norm_pre_ffn, w_up, w_down,
           norm_post_ffn):
    assert w_in.shape[0] == 1, "single-layer problem"
    batch, seq, _ = x_prompt.shape
    dbatch, t_new, _ = x_sample.shape
    caches_in = (cache_kv_w128[0], cache_kv_w512[0], cache_kv_w2048[0])

    wsp = w_spatial[0]
    bsp_rows = jnp.repeat(b_spatial[0].T, CHUNK, axis=1)

    ns = dbatch * t_new
    xs = x_sample.reshape(ns, D_MODEL)
    qkv_s, z1_s, zn_s, ga_s, gb_s, w_in_bf, wao, wbo, wout = _project_sample(
        xs, norm_pre_mix, w_in[0], b_gate, ln_z_g, ln_z_b, (w_ao[0], w_bo[0], w_out[0]))
    caches, kv_sample = [], []
    for gi, (win, dil) in enumerate(DIL_GROUPS):
        buf = caches_in[gi]
        wb = buf.shape[1]
        assert wb == win and t_new <= T_PAD, "full window buffers"
        caches.append(buf.transpose(0, 2, 3, 4, 1).reshape(dbatch, 2 * GROUP_W, wb))
        gs = slice(gi * GROUP_W, (gi + 1) * GROUP_W)
        kn = qkv_s[:, ATTN_W:2 * ATTN_W][:, gs].reshape(dbatch, t_new, HEADS, HEAD_DIM)
        vn = qkv_s[:, 2 * ATTN_W:][:, gs].reshape(dbatch, t_new, HEADS, HEAD_DIM)
        kv_sample.append(jnp.stack([kn, vn], axis=2)[None])

    xp = x_prompt.reshape(batch * seq, D_MODEL)
    qkv, kv1_t, kv2_t, kv3_t, z1, zn, ga, gb, wup, wdn = _project_prompt(
        x_prompt, norm_pre_mix, w_in_bf, b_gate, ln_z_g, ln_z_b, (w_up[0], w_down[0]))
    weights = (wao, wbo, wout, norm_post_mix, norm_pre_ffn, wup, wdn, norm_post_ffn)
    kv_prompt = [_kv_from_feature_major(a) for a in (kv1_t, kv2_t, kv3_t)]
    o_a_p = _prompt_attention(qkv, seq).reshape(batch * seq, GROUP_W)
    assert SAMPLE_DB_TILE * t_new == T_PAD, "a tail step takes T_PAD new rows"
    y_prompt, o_a = _tail_prompt(o_a_p, z1, zn, ga, gb, xp, wsp, bsp_rows, weights, qkv_s, caches)
    y_prompt = y_prompt.reshape(batch, seq, D_MODEL)

    wt = jnp.tril(wsp)[:, :t_new, :t_new]
    coef = jnp.stack([jnp.where(jnp.arange(t_new)[None, :] >= d,
                                wt[:, jnp.arange(t_new), jnp.maximum(jnp.arange(t_new) - d, 0)], 0.0)
                      for d in range(t_new)])
    coef = jnp.repeat(coef.transpose(0, 2, 1), CHUNK, axis=2)
    reps = SAMPLE_TAIL_TILE // t_new
    coef = jnp.tile(coef, (1, reps, 1))
    bsp_s = jnp.tile(bsp_rows[:t_new], (reps, 1))
    y_sample = _tail_sample(o_a, z1_s, zn_s, ga_s, gb_s, xs, coef, bsp_s, weights).reshape(dbatch, t_new, D_MODEL)
    sg_sample = zn_s.reshape(1, dbatch, t_new, SG_W)

    return (y_prompt, y_sample, kv_prompt[0], kv_prompt[1], kv_prompt[2],
            kv_sample[0], kv_sample[1], kv_sample[2], sg_sample)
```

```python
import math

import jax
import jax.numpy as jnp
from jax import lax
from jax.experimental import pallas as pl
from jax.experimental.pallas import tpu as pltpu

F32 = jnp.float32
BF16 = jnp.bfloat16

D_MODEL = 1024
HEAD_DIM = 64
HEADS = 4
GROUP_W = HEADS * HEAD_DIM
DIL_GROUPS = ((128, 1), (512, 4), (2048, 16))
N_GROUPS = len(DIL_GROUPS)
ATTN_W = N_GROUPS * GROUP_W
BLOCK = 128
CHUNK = 128
SG_GROUPS = 4
SG_W = 512
D_FF = 4 * D_MODEL
Z_OFF = 3 * ATTN_W
G_OFF = Z_OFF + 2 * SG_W
IN_W = G_OFF + 2 * D_MODEL
EPS = 1e-6
NEG = -1e30
SCALE = HEAD_DIM ** -0.5
INV_SQRT2 = 1.0 / math.sqrt(2.0)
LOG2E = 1.0 / math.log(2.0)
LANES = 128
SLABS = GROUP_W // LANES
HEADS_PER_SLAB = LANES // HEAD_DIM
HEAD_SHIFT = HEAD_DIM.bit_length() - 1
VMEM_LIMIT = 56 * 1024 * 1024

TOKEN_TILE = 512
BLOCK_UNROLL = 8
SAMPLE_DB_TILE = 2
T_PAD = 8
T_SHIFT = T_PAD.bit_length() - 1
BF16_SUBLANES = 16
CAST_COLS = 768
SAMPLE_TAIL_TILE = 256
FF_CHUNK = 1024


def _params(n_axes):
    return pltpu.CompilerParams(dimension_semantics=("arbitrary",) * n_axes,
                                vmem_limit_bytes=VMEM_LIMIT)


def _const_spec(shape):
    return pl.BlockSpec(shape, lambda *_: (0,) * len(shape), pipeline_mode=pl.Buffered(1))


def _rms(x, g):
    return x * lax.rsqrt(jnp.mean(x * x, axis=-1, keepdims=True) + EPS) * g


def _proj_gate_branches(h, w_ref, bg_ref, lng_ref, lnb_ref, z1_ref, zn_ref, ga_ref, gb_ref):
    z = jnp.dot(h, w_ref[:, Z_OFF:G_OFF], preferred_element_type=F32)
    z = 0.5 * z * (1.0 + lax.erf(z * INV_SQRT2))
    z1_ref[...] = z[:, :SG_W].astype(z1_ref.dtype)
    z2 = z[:, SG_W:]
    mu = jnp.mean(z2, axis=-1, keepdims=True)
    zc = z2 - mu
    var = jnp.mean(zc * zc, axis=-1, keepdims=True)
    zn_ref[...] = (zc * lax.rsqrt(var + EPS) * lng_ref[...] + lnb_ref[...]).astype(zn_ref.dtype)

    gates = jnp.dot(h, w_ref[:, G_OFF:], preferred_element_type=F32) + bg_ref[...]
    gates = 1.0 / (1.0 + jnp.exp(-gates))
    ga_ref[...] = gates[:, :D_MODEL].astype(BF16)
    gb_ref[...] = gates[:, D_MODEL:].astype(BF16)


def _cast_blocks(pairs):
    for src_ref, dst_ref in pairs:
        dst_ref[...] = src_ref[...].astype(dst_ref.dtype)


def _proj_sample_kernel(x_ref, g_ref, wf_ref, bg_ref, lng_ref, lnb_ref, wao_ref, wbo_ref, wout_ref,
                        qkv_ref, z1_ref, zn_ref, ga_ref, gb_ref, w_ref, wao_bf_ref, wbo_bf_ref, wout_bf_ref):
    @pl.when(pl.program_id(0) == 0)
    def _():
        for c in range(0, IN_W, CAST_COLS):
            w_ref[:, c:c + CAST_COLS] = wf_ref[:, c:c + CAST_COLS].astype(BF16)

    _cast_blocks(((wao_ref, wao_bf_ref), (wbo_ref, wbo_bf_ref), (wout_ref, wout_bf_ref)))
    x = x_ref[...]
    h = _rms(x.reshape(x.shape[0] * x.shape[1], x.shape[2]), g_ref[...]).astype(BF16)
    qkv_ref[...] = jnp.dot(h, w_ref[:, :Z_OFF], preferred_element_type=F32)
    _proj_gate_branches(h, w_ref, bg_ref, lng_ref, lnb_ref, z1_ref, zn_ref, ga_ref, gb_ref)


def _proj_prompt_kernel(x_ref, g_ref, w_ref, bg_ref, lng_ref, lnb_ref, wup_ref, wdn_ref,
                        qkv_ref, kv1_ref, kv2_ref, kv3_ref, z1_ref, zn_ref, ga_ref, gb_ref, wup_bf_ref, wdn_bf_ref):
    _cast_blocks(((wup_ref, wup_bf_ref), (wdn_ref, wdn_bf_ref)))
    h = _rms(x_ref[0], g_ref[...]).astype(BF16)
    _proj_gate_branches(h, w_ref, bg_ref, lng_ref, lnb_ref, z1_ref, zn_ref, ga_ref, gb_ref)
    def project(first_part, n_parts, scale=None):
        cols = jnp.dot(h, w_ref[:, first_part * ATTN_W:(first_part + n_parts) * ATTN_W], preferred_element_type=F32)
        if scale is not None:
            cols = cols * scale
        for part in range(n_parts):
            for g in range(N_GROUPS):
                for s in range(SLABS):
                    col = part * ATTN_W + g * GROUP_W + s * LANES
                    qkv_ref[0, g, (first_part + part) * SLABS + s] = cols[:, col:col + LANES]

    def kv_t(g):
        return jnp.concatenate([qkv_ref[0, g, slab].T for slab in range(SLABS, 3 * SLABS)], axis=0)

    tm = x_ref.shape[1]
    project(1, 2)
    kv3_ref[0] = kv_t(2)
    project(0, 1, scale=SCALE * LOG2E)

    @pl.when(pl.program_id(1) == pl.num_programs(1) - 1)
    def _():
        kv2_ref[0] = kv_t(1)[:, tm - kv2_ref.shape[2]:]
        kv1_ref[0] = kv_t(0)[:, tm - kv1_ref.shape[2]:]


def _proj_weight_specs():
    return [_const_spec((1, D_MODEL)), _const_spec((D_MODEL, IN_W)),
            _const_spec((1, 2 * D_MODEL)), _const_spec((1, SG_W)), _const_spec((1, SG_W))]


def _gate_branch_shapes(n, z_dtype):
    return [jax.ShapeDtypeStruct((n, SG_W), z_dtype), jax.ShapeDtypeStruct((n, SG_W), z_dtype),
            jax.ShapeDtypeStruct((n, D_MODEL), BF16), jax.ShapeDtypeStruct((n, D_MODEL), BF16)]


def _row_slices(arrays, steps):
    specs, shapes = [], []
    for a in arrays:
        rows = a.shape[0] // steps
        assert rows * steps == a.shape[0] and rows % BF16_SUBLANES == 0
        specs.append((rows, a.shape[1]))
        shapes.append(jax.ShapeDtypeStruct(a.shape, BF16))
    return specs, shapes


def _project_sample(x3, g, w_in, bg, lng, lnb, mix_weights):
    n = x3.shape[0] * x3.shape[1]
    tm = min(SAMPLE_TAIL_TILE, n)
    steps = n // tm
    row = lambda w: pl.BlockSpec((tm, w), lambda i: (i, 0))
    slices, bf_shapes = _row_slices(mix_weights, steps)
    slice_specs = [pl.BlockSpec(s, lambda i: (i, 0)) for s in slices]
    return pl.pallas_call(
        _proj_sample_kernel,
        grid=(steps,),
        in_specs=[pl.BlockSpec((tm // x3.shape[1],) + x3.shape[1:], lambda i: (i, 0, 0))] + _proj_weight_specs()
        + slice_specs,
        out_specs=[row(Z_OFF), row(SG_W), row(SG_W), row(D_MODEL), row(D_MODEL),
                   pl.BlockSpec((D_MODEL, IN_W), lambda i: (0, 0))] + slice_specs,
        out_shape=[jax.ShapeDtypeStruct((n, Z_OFF), F32)] + _gate_branch_shapes(n, F32)
        + [jax.ShapeDtypeStruct((D_MODEL, IN_W), BF16)] + bf_shapes,
        compiler_params=_params(1),
        name="proj_sample",
    )(x3, g, w_in, bg, lng, lnb, *mix_weights)


def _project_prompt(x, g, w_bf, bg, lng, lnb, ffn_weights):
    batch, seq, _ = x.shape
    tm = TOKEN_TILE
    tiles = seq // tm
    keeps = [min(win, seq) for win, _ in DIL_GROUPS]
    assert seq % tm == 0 and keeps[0] <= tm and keeps[1] <= tm and keeps[2] == seq
    row = lambda w: pl.BlockSpec((tm, w), lambda b, t: (b * tiles + t, 0))
    last = lambda keep: pl.BlockSpec((1, 2 * GROUP_W, keep), lambda b, t: (b, 0, 0))
    n = batch * seq
    slices, bf_shapes = _row_slices(ffn_weights, batch * tiles)
    slice_specs = [pl.BlockSpec(s, lambda b, t: (b * tiles + t, 0)) for s in slices]
    return pl.pallas_call(
        _proj_prompt_kernel,
        grid=(batch, tiles),
        in_specs=[pl.BlockSpec((1, tm, D_MODEL), lambda b, t: (b, t, 0))] + _proj_weight_specs() + slice_specs,
        out_specs=[pl.BlockSpec((1, N_GROUPS, 3 * SLABS, tm, LANES), lambda b, t: (b, 0, 0, t, 0)),
                   last(keeps[0]), last(keeps[1]),
                   pl.BlockSpec((1, 2 * GROUP_W, tm), lambda b, t: (b, 0, t)),
                   row(SG_W), row(SG_W), row(D_MODEL), row(D_MODEL)] + slice_specs,
        out_shape=[jax.ShapeDtypeStruct((batch, N_GROUPS, 3 * SLABS, seq, LANES), F32),
                   jax.ShapeDtypeStruct((batch, 2 * GROUP_W, keeps[0]), F32),
                   jax.ShapeDtypeStruct((batch, 2 * GROUP_W, keeps[1]), F32),
                   jax.ShapeDtypeStruct((batch, 2 * GROUP_W, keeps[2]), F32)] + _gate_branch_shapes(n, BF16)
        + bf_shapes,
        compiler_params=_params(2),
        name="proj_prompt",
    )(x, g, w_bf, bg, lng, lnb, *ffn_weights)


def _head_lane_mask(shape, hh):
    lane = lax.broadcasted_iota(jnp.int32, shape, len(shape) - 1)
    return (lane >> HEAD_SHIFT) == hh


def _band_block(q, k, v, mask):
    nt = (((1,), (1,)), ((), ()))
    assert HEADS_PER_SLAB == 2
    lm0 = _head_lane_mask((BLOCK, LANES), 0)
    mask2 = jnp.concatenate([mask, mask], axis=0)
    outs, lses = [], []
    for s in range(SLABS):
        zero = jnp.zeros_like(q[s])
        qm = jnp.concatenate([jnp.where(lm0, q[s], zero), jnp.where(lm0, zero, q[s])], axis=0)
        sc = lax.dot_general(qm, k[s], nt, preferred_element_type=F32)
        sc = jnp.where(mask2, sc, NEG)
        m = jnp.max(sc, axis=-1, keepdims=True)
        p = jnp.exp2(sc - m)
        l = jnp.sum(p, axis=-1, keepdims=True)
        o = jnp.dot(p.astype(BF16), v[s], preferred_element_type=F32) * (1.0 / l)
        lse = m + jnp.log2(l)
        outs.append(jnp.where(lm0, o[:BLOCK], o[BLOCK:]))
        lses.append(jnp.where(lm0, lse[:BLOCK], lse[BLOCK:]))
    return outs, lses


def _group_attention(qkv_ref, o_scr, lse_scr, g, dil, seq):
    n_qb = seq // (dil * BLOCK)
    qb_shift = n_qb.bit_length() - 1
    span = dil * BLOCK
    rows = (lambda start: pl.ds(start, BLOCK, stride=dil)) if dil > 1 else (lambda start: pl.ds(start, BLOCK))
    nk = 2 * BLOCK if n_qb > 1 else BLOCK
    qi = lax.broadcasted_iota(jnp.int32, (BLOCK, nk), 0)
    ki = lax.broadcasted_iota(jnp.int32, (BLOCK, nk), 1)

    def load(slab, start):
        return qkv_ref[0, 0, slab, rows(start), :]

    def body(n, carry):
        r = n >> qb_shift
        qb = n & (n_qb - 1)
        cur = r + span * qb
        if dil == 1:
            cur = pl.multiple_of(cur, BLOCK)
        q = [load(s, cur).astype(BF16) for s in range(SLABS)]
        k = [load(SLABS + s, cur).astype(BF16) for s in range(SLABS)]
        v = [load(2 * SLABS + s, cur).astype(BF16) for s in range(SLABS)]
        if n_qb > 1:
            prev = r + span * jnp.maximum(qb - 1, 0)
            if dil == 1:
                prev = pl.multiple_of(prev, BLOCK)
            k = [jnp.concatenate([load(SLABS + s, prev).astype(BF16), k[s]], axis=0) for s in range(SLABS)]
            v = [jnp.concatenate([load(2 * SLABS + s, prev).astype(BF16), v[s]], axis=0) for s in range(SLABS)]
            mask = (ki >= qi) & (ki <= qi + BLOCK) & (ki >= jnp.where(qb > 0, 0, BLOCK))
        else:
            mask = ki <= qi
        outs, lses = _band_block(q, k, v, mask)
        for s in range(SLABS):
            o_scr[g, s, rows(cur), :] = outs[s]
            lse_scr[g, s, rows(cur), :] = lses[s]
        return carry

    lax.fori_loop(0, dil * n_qb, body, 0, unroll=BLOCK_UNROLL)


def _prompt_attn_kernel(qkv_ref, o_ref, o_scr, lse_scr):
    g = pl.program_id(1)
    seq = o_ref.shape[1]
    for gi, (_, dil) in enumerate(DIL_GROUPS):
        @pl.when(g == gi)
        def _(gi=gi, dil=dil):
            _group_attention(qkv_ref, o_scr, lse_scr, gi, dil, seq)

    @pl.when(g == N_GROUPS - 1)
    def _():
        def merge(i, carry):
            rows = pl.ds(pl.multiple_of(i * BLOCK, BLOCK), BLOCK)
            for s in range(SLABS):
                ls = [lse_scr[gi, s, rows, :] for gi in range(N_GROUPS)]
                top = jnp.maximum(jnp.maximum(ls[0], ls[1]), ls[2])
                e = [jnp.exp2(x - top) for x in ls]
                num = e[0] * o_scr[0, s, rows, :] + e[1] * o_scr[1, s, rows, :] + e[2] * o_scr[2, s, rows, :]
                o_ref[0, rows, s * LANES:(s + 1) * LANES] = (num / (e[0] + e[1] + e[2])).astype(o_ref.dtype)
            return carry
        lax.fori_loop(0, seq // BLOCK, merge, 0)


def _prompt_attention(qkv, seq):
    batch = qkv.shape[0]
    scratch = pltpu.VMEM((N_GROUPS, SLABS, seq, LANES), F32)
    return pl.pallas_call(
        _prompt_attn_kernel,
        grid=(batch, N_GROUPS),
        in_specs=[pl.BlockSpec((1, 1, 3 * SLABS, seq, LANES), lambda b, g: (b, g, 0, 0, 0))],
        out_specs=pl.BlockSpec((1, seq, GROUP_W), lambda b, g: (b, 0, 0)),
        out_shape=jax.ShapeDtypeStruct((batch, seq, GROUP_W), BF16),
        scratch_shapes=[scratch, scratch],
        compiler_params=_params(2),
        name="prompt_attn",
    )(qkv)


def _residue_allreduce(x, dil, op):
    shift = LANES // 2
    while shift >= dil:
        x = op(x, pltpu.roll(x, shift, axis=1))
        shift //= 2
    return x


def _lane_expand(rows8, dil, r0, t_new):
    lane = lax.broadcasted_iota(jnp.int32, (LANES, T_PAD), 0)
    t = lax.broadcasted_iota(jnp.int32, (LANES, T_PAD), 1)
    sel = jnp.where(((lane & (dil - 1)) == t - r0) & (t < r0 + t_new), 1.0, 0.0).astype(BF16)
    hi = rows8.astype(BF16)
    lo = (rows8 - hi.astype(F32)).astype(BF16)
    out = jnp.dot(sel, hi, preferred_element_type=F32) + jnp.dot(sel, lo, preferred_element_type=F32)
    return out.T


def _heads_to_rows(x):
    return jnp.broadcast_to(x[:, None, :], (HEADS, HEAD_DIM, LANES)).reshape(GROUP_W, LANES)


def _rows_to_heads_sum(x):
    return jnp.sum(x.reshape(HEADS, HEAD_DIM, LANES), axis=1)


def _buffer_group_lanes(c_ref, d, q8, kn8, vn8, dil, r0, t_new):
    width = c_ref.shape[2]
    tiles = width // LANES
    ql, kl, vl = (_lane_expand(a, dil, r0, t_new) for a in (q8, kn8, vn8))
    s = [_rows_to_heads_sum(c_ref[d, :GROUP_W, j * LANES:(j + 1) * LANES] * ql) for j in range(tiles)]
    s_new = _rows_to_heads_sum(kl * ql)
    top = s[0]
    for sj in s[1:]:
        top = jnp.maximum(top, sj)
    top = jnp.maximum(_residue_allreduce(top, dil, jnp.maximum), s_new)
    p = [jnp.exp(sj - top) for sj in s]
    p_new = jnp.exp(s_new - top)
    den = p[0]
    for pj in p[1:]:
        den = den + pj
    den = _residue_allreduce(den, dil, jnp.add) + p_new
    acc = c_ref[d, GROUP_W:, :LANES] * _heads_to_rows(p[0])
    for j in range(1, tiles):
        acc = acc + c_ref[d, GROUP_W:, j * LANES:(j + 1) * LANES] * _heads_to_rows(p[j])
    acc = (_residue_allreduce(acc, dil, jnp.add) + vl * _heads_to_rows(p_new)) * _heads_to_rows(1.0 / den)
    lse = _heads_to_rows(top + jnp.log(den))
    first = -(-r0 // dil) * dil
    return acc.T[first - r0:first - r0 + T_PAD], lse.T[first - r0:first - r0 + T_PAD]


def _buffer_group_dense(c_ref, d, q8, kn8, vn8, r0, t_new):
    rows = HEADS * T_PAD
    width = c_ref.shape[2]
    row = lax.broadcasted_iota(jnp.int32, (rows, GROUP_W), 0)
    lane = lax.broadcasted_iota(jnp.int32, (rows, GROUP_W), 1)
    head_sel = (row >> T_SHIFT) == (lane >> HEAD_SHIFT)
    t = (lax.broadcasted_iota(jnp.int32, (rows, width), 0) & (T_PAD - 1)) - r0
    i = lax.broadcasted_iota(jnp.int32, (rows, width), 1)
    tn = lax.broadcasted_iota(jnp.int32, (rows, T_PAD), 0) & (T_PAD - 1)
    j = lax.broadcasted_iota(jnp.int32, (rows, T_PAD), 1)
    nt = (((1,), (1,)), ((), ()))
    qm = jnp.where(head_sel, jnp.concatenate([q8] * HEADS, axis=0), 0.0).astype(BF16)
    kt = c_ref[d, :GROUP_W, :].astype(BF16)
    vt = c_ref[d, GROUP_W:, :].astype(BF16)
    sc = jnp.where(i >= t, jnp.dot(qm, kt, preferred_element_type=F32), NEG)
    own = (j <= tn) & (j >= r0) & (j < r0 + t_new)
    sn = jnp.where(own, lax.dot_general(qm, kn8.astype(BF16), nt, preferred_element_type=F32), NEG)
    m = jnp.maximum(jnp.max(sc, axis=-1, keepdims=True), jnp.max(sn, axis=-1, keepdims=True))
    pc = jnp.exp(sc - m)
    pn = jnp.exp(sn - m)
    l = jnp.sum(pc, axis=-1, keepdims=True) + jnp.sum(pn, axis=-1, keepdims=True)

    def finish():
        o = (lax.dot_general(pc.astype(BF16), vt, nt, preferred_element_type=F32)
             + jnp.dot(pn.astype(BF16), vn8.astype(BF16), preferred_element_type=F32))
        o = jnp.where(head_sel, o / l, 0.0)
        lse = jnp.where(head_sel, m + jnp.log(l), 0.0)
        fold = lambda x: sum(x[h * T_PAD:(h + 1) * T_PAD] for h in range(HEADS))
        return fold(o), fold(lse)
    return finish


def _sample_attn_start(qkv_ref, c1_ref, c2_ref, c3_ref, o_ref):
    n_db = c1_ref.shape[0]
    t_new = T_PAD // n_db
    assert qkv_ref.shape[0] == T_PAD and n_db * t_new == T_PAD
    pad_rows = jnp.zeros((T_PAD - t_new, qkv_ref.shape[1]), F32)
    pending = []
    for d in range(n_db):
        new_rows = jnp.concatenate([qkv_ref[d * t_new:(d + 1) * t_new, :], pad_rows], axis=0)
        groups = []
        for g, (c_ref, (_, dil)) in enumerate(zip((c1_ref, c2_ref, c3_ref), DIL_GROUPS)):
            cols = lambda part: slice(part * ATTN_W + g * GROUP_W, part * ATTN_W + (g + 1) * GROUP_W)
            q8 = new_rows[:, cols(0)] * SCALE
            kn8, vn8 = new_rows[:, cols(1)], new_rows[:, cols(2)]
            if dil == 1:
                groups.append(_buffer_group_dense(c_ref, d, q8, kn8, vn8, 0, t_new))
            else:
                result = _buffer_group_lanes(c_ref, d, q8, kn8, vn8, dil, 0, t_new)
                groups.append(lambda result=result: result)
        pending.append(groups)

    def finish():
        for d, groups in enumerate(pending):
            outs, lses = zip(*(group() for group in groups))
            top = jnp.maximum(jnp.maximum(lses[0], lses[1]), lses[2])
            w = [jnp.exp(x - top) for x in lses]
            o8 = (w[0] * outs[0] + w[1] * outs[1] + w[2] * outs[2]) / (w[0] + w[1] + w[2])
            o_ref[d * t_new:(d + 1) * t_new, :] = o8[:t_new]
    return finish


def _mix_value(o_a, o_b, ga_ref, gb_ref, x, wao_ref, wbo_ref, wout_ref, gpost_ref):
    a = jnp.dot(o_a.astype(BF16), wao_ref[...], preferred_element_type=F32)
    b = jnp.dot(o_b.astype(BF16), wbo_ref[...], preferred_element_type=F32)
    merged = ga_ref[...].astype(F32) * a + gb_ref[...].astype(F32) * b
    t = jnp.dot(merged.astype(BF16), wout_ref[...], preferred_element_type=F32)
    return x + _rms(t, gpost_ref[...])


def _ffn_value(x, gpre_ref, wup_ref, wdn_ref, gpost_ref):
    h = _rms(x, gpre_ref[...]).astype(BF16)
    f = jnp.zeros(x.shape, F32)
    for c in range(D_FF // FF_CHUNK):
        cs = slice(c * FF_CHUNK, (c + 1) * FF_CHUNK)
        u = jnp.maximum(jnp.dot(h, wup_ref[:, cs], preferred_element_type=F32), 0.0)
        f = f + jnp.dot((u * u).astype(BF16), wdn_ref[cs, :], preferred_element_type=F32)
    return x + _rms(f, gpost_ref[...])


def _prompt_spatial_gate(z1_ref, zn_ref, wsp_ref, bsp_ref):
    r = lax.broadcasted_iota(jnp.int32, (CHUNK, CHUNK), 0)
    c = lax.broadcasted_iota(jnp.int32, (CHUNK, CHUNK), 1)
    n_chunks = z1_ref.shape[0] // CHUNK
    mixed = []
    for g in range(SG_GROUPS):
        wt = jnp.where(r >= c, wsp_ref[g], 0.0).astype(BF16)
        zg = jnp.concatenate([zn_ref[ci * CHUNK:(ci + 1) * CHUNK, g * CHUNK:(g + 1) * CHUNK].astype(BF16)
                              for ci in range(n_chunks)], axis=1)
        mixed.append(jnp.dot(wt, zg, preferred_element_type=F32))
    chunks = [jnp.concatenate([m[:, ci * CHUNK:(ci + 1) * CHUNK] for m in mixed], axis=1) + bsp_ref[...]
              for ci in range(n_chunks)]
    return z1_ref[...].astype(F32) * jnp.concatenate(chunks, axis=0)


def _tail_prompt_kernel(oa_ref, z1_ref, zn_ref, ga_ref, gb_ref, x_ref, wsp_ref, bsp_ref,
                        wao_ref, wbo_ref, wout_ref, gmix_ref, gpre_ref, wup_ref, wdn_ref, gpost_ref,
                        qkv_ref, c1_ref, c2_ref, c3_ref, y_ref, o_ref):
    finish_sample_attn = _sample_attn_start(qkv_ref, c1_ref, c2_ref, c3_ref, o_ref)
    o_b = _prompt_spatial_gate(z1_ref, zn_ref, wsp_ref, bsp_ref)
    x1 = _mix_value(oa_ref[...], o_b, ga_ref, gb_ref, x_ref[...], wao_ref, wbo_ref, wout_ref, gmix_ref)
    y_ref[...] = _ffn_value(x1, gpre_ref, wup_ref, wdn_ref, gpost_ref)
    finish_sample_attn()


def _tail_sample_kernel(oa_ref, z1_ref, zn_ref, ga_ref, gb_ref, x_ref, coef_ref, bsp_ref,
                        wao_ref, wbo_ref, wout_ref, gmix_ref, gpre_ref, wup_ref, wdn_ref, gpost_ref, y_ref):
    zn = zn_ref[...]
    mix = coef_ref[0] * zn
    for d in range(1, coef_ref.shape[0]):
        mix = mix + coef_ref[d] * pltpu.roll(zn, d, axis=0)
    o_b = z1_ref[...] * (mix + bsp_ref[...])
    x = x_ref[...].reshape(zn.shape[0], D_MODEL)
    x1 = _mix_value(oa_ref[...], o_b, ga_ref, gb_ref, x, wao_ref, wbo_ref, wout_ref, gmix_ref)
    y_ref[...] = _ffn_value(x1, gpre_ref, wup_ref, wdn_ref, gpost_ref).reshape(y_ref.shape)


def _tail_weight_specs():
    return [_const_spec((GROUP_W, D_MODEL)), _const_spec((SG_W, D_MODEL)), _const_spec((D_MODEL, D_MODEL)),
            _const_spec((1, D_MODEL)), _const_spec((1, D_MODEL)), _const_spec((D_MODEL, D_FF)),
            _const_spec((D_FF, D_MODEL)), _const_spec((1, D_MODEL))]


def _tail_prompt(o_a, z1, zn, ga, gb, x, wsp, bsp, weights, qkv_s, caches):
    n, db, ns = x.shape[0], caches[0].shape[0], qkv_s.shape[0]
    tile = SAMPLE_DB_TILE
    steps = db // tile
    tm = n // steps
    assert db % tile == 0 and n % steps == 0 and tm % CHUNK == 0 and ns == steps * T_PAD
    row = lambda w: pl.BlockSpec((tm, w), lambda i: (i, 0))
    blk = lambda a: pl.BlockSpec((tile,) + a.shape[1:], lambda i: (i, 0, 0))
    new_rows = lambda w: pl.BlockSpec((T_PAD, w), lambda i: (i, 0))
    return pl.pallas_call(
        _tail_prompt_kernel,
        grid=(steps,),
        in_specs=[row(GROUP_W), row(SG_W), row(SG_W), row(D_MODEL), row(D_MODEL), row(D_MODEL),
                  _const_spec((SG_GROUPS, CHUNK, CHUNK)), _const_spec((CHUNK, SG_W))] + _tail_weight_specs()
        + [new_rows(Z_OFF)] + [blk(a) for a in caches],
        out_specs=[row(D_MODEL), new_rows(GROUP_W)],
        out_shape=[jax.ShapeDtypeStruct((n, D_MODEL), F32),
                   jax.ShapeDtypeStruct((ns, GROUP_W), F32)],
        compiler_params=_params(1),
        name="tail_prompt",
    )(o_a, z1, zn, ga, gb, x, wsp, bsp, *weights, qkv_s, *caches)


def _tail_sample(o_a, z1, zn, ga, gb, x3, coef, bsp, weights):
    n = x3.shape[0] * x3.shape[1]
    tm = coef.shape[1]
    row = lambda w: pl.BlockSpec((tm, w), lambda i: (i, 0))
    rows3 = pl.BlockSpec((tm // x3.shape[1],) + x3.shape[1:], lambda i: (i, 0, 0))
    return pl.pallas_call(
        _tail_sample_kernel,
        grid=(n // tm,),
        in_specs=[row(GROUP_W), row(SG_W), row(SG_W), row(D_MODEL), row(D_MODEL), rows3,
                  _const_spec(coef.shape), _const_spec(bsp.shape)] + _tail_weight_specs(),
        out_specs=rows3,
        out_shape=jax.ShapeDtypeStruct(x3.shape, F32),
        compiler_params=_params(1),
        name="tail_sample",
    )(o_a, z1, zn, ga, gb, x3, coef, bsp, *weights)


def _kv_from_feature_major(kv_t):
    batch, _, keep = kv_t.shape
    return kv_t.reshape(batch, 2, HEADS, HEAD_DIM, keep).transpose(0, 4, 1, 2, 3)[None]


def kernel(x_prompt, x_sample, cache_kv_w128, cache_kv_w512, cache_kv_w2048, norm_pre_mix, w_in, b_gate,
           ln_z_g, ln_z_b, w_spatial, b_spatial, w_ao, w_bo, w_out, norm_post_mix, norm_pre_ffn, w_up, w_down,
           norm_post_ffn):
    assert w_in.shape[0] == 1, "single-layer problem"
    batch, seq, _ = x_prompt.shape
    dbatch, t_new, _ = x_sample.shape
    caches_in = (cache_kv_w128[0], cache_kv_w512[0], cache_kv_w2048[0])

    wsp = w_spatial[0]
    bsp_rows = jnp.repeat(b_spatial[0].T, CHUNK, axis=1)

    ns = dbatch * t_new
    qkv_s, z1_s, zn_s, ga_s, gb_s, w_in_bf, wao, wbo, wout = _project_sample(
        x_sample, norm_pre_mix, w_in[0], b_gate, ln_z_g, ln_z_b, (w_ao[0], w_bo[0], w_out[0]))
    caches, kv_sample = [], []
    for gi, (win, dil) in enumerate(DIL_GROUPS):
        buf = caches_in[gi]
        wb = buf.shape[1]
        assert wb == win and t_new <= T_PAD, "full window buffers"
        caches.append(buf.transpose(0, 2, 3, 4, 1).reshape(dbatch, 2 * GROUP_W, wb))
        gs = slice(gi * GROUP_W, (gi + 1) * GROUP_W)
        kn = qkv_s[:, ATTN_W:2 * ATTN_W][:, gs].reshape(dbatch, t_new, HEADS, HEAD_DIM)
        vn = qkv_s[:, 2 * ATTN_W:][:, gs].reshape(dbatch, t_new, HEADS, HEAD_DIM)
        kv_sample.append(jnp.stack([kn, vn], axis=2)[None])

    xp = x_prompt.reshape(batch * seq, D_MODEL)
    qkv, kv1_t, kv2_t, kv3_t, z1, zn, ga, gb, wup, wdn = _project_prompt(
        x_prompt, norm_pre_mix, w_in_bf, b_gate, ln_z_g, ln_z_b, (w_up[0], w_down[0]))
    weights = (wao, wbo, wout, norm_post_mix, norm_pre_ffn, wup, wdn, norm_post_ffn)
    kv_prompt = [_kv_from_feature_major(a) for a in (kv1_t, kv2_t, kv3_t)]
    o_a_p = _prompt_attention(qkv, seq).reshape(batch * seq, GROUP_W)
    assert SAMPLE_DB_TILE * t_new == T_PAD, "a tail step takes T_PAD new rows"
    y_prompt, o_a = _tail_prompt(o_a_p, z1, zn, ga, gb, xp, wsp, bsp_rows, weights, qkv_s, caches)
    y_prompt = y_prompt.reshape(batch, seq, D_MODEL)

    wt = jnp.tril(wsp)[:, :t_new, :t_new]
    coef = jnp.stack([jnp.where(jnp.arange(t_new)[None, :] >= d,
                                wt[:, jnp.arange(t_new), jnp.maximum(jnp.arange(t_new) - d, 0)], 0.0)
                      for d in range(t_new)])
    coef = jnp.repeat(coef.transpose(0, 2, 1), CHUNK, axis=2)
    reps = SAMPLE_TAIL_TILE // t_new
    coef = jnp.tile(coef, (1, reps, 1))
    bsp_s = jnp.tile(bsp_rows[:t_new], (reps, 1))
    y_sample = _tail_sample(o_a, z1_s, zn_s, ga_s, gb_s, x_sample, coef, bsp_s, weights)
    sg_sample = zn_s.reshape(1, dbatch, t_new, SG_W)

    return (y_prompt, y_sample, kv_prompt[0], kv_prompt[1], kv_prompt[2],
            kv_sample[0], kv_sample[1], kv_sample[2], sg_sample)
```

```python
import math

import jax
import jax.numpy as jnp
from jax import lax
from jax.experimental import pallas as pl
from jax.experimental.pallas import tpu as pltpu

F32 = jnp.float32
BF16 = jnp.bfloat16

D_MODEL = 1024
HEAD_DIM = 64
HEADS = 4
GROUP_W = HEADS * HEAD_DIM
DIL_GROUPS = ((128, 1), (512, 4), (2048, 16))
N_GROUPS = len(DIL_GROUPS)
ATTN_W = N_GROUPS * GROUP_W
BLOCK = 128
CHUNK = 128
SG_GROUPS = 4
SG_W = 512
D_FF = 4 * D_MODEL
Z_OFF = 3 * ATTN_W
G_OFF = Z_OFF + 2 * SG_W
IN_W = G_OFF + 2 * D_MODEL
EPS = 1e-6
NEG = -1e30
SCALE = HEAD_DIM ** -0.5
INV_SQRT2 = 1.0 / math.sqrt(2.0)
LOG2E = 1.0 / math.log(2.0)
LANES = 128
SLABS = GROUP_W // LANES
HEADS_PER_SLAB = LANES // HEAD_DIM
HEAD_SHIFT = HEAD_DIM.bit_length() - 1
VMEM_LIMIT = 56 * 1024 * 1024

TOKEN_TILE = 512
BLOCK_UNROLL = 16
SAMPLE_DB_TILE = 2
T_PAD = 8
T_SHIFT = T_PAD.bit_length() - 1
BF16_SUBLANES = 16
CAST_COLS = 768
SAMPLE_TAIL_TILE = 256
FF_CHUNK = 1024


def _params(n_axes):
    return pltpu.CompilerParams(dimension_semantics=("arbitrary",) * n_axes,
                                vmem_limit_bytes=VMEM_LIMIT)


def _const_spec(shape):
    return pl.BlockSpec(shape, lambda *_: (0,) * len(shape), pipeline_mode=pl.Buffered(1))


def _rms(x, g):
    return x * lax.rsqrt(jnp.mean(x * x, axis=-1, keepdims=True) + EPS) * g


def _proj_gate_branches(h, w_ref, bg_ref, lng_ref, lnb_ref, z1_ref, zn_ref, ga_ref, gb_ref):
    z = jnp.dot(h, w_ref[:, Z_OFF:G_OFF], preferred_element_type=F32)
    z = 0.5 * z * (1.0 + lax.erf(z * INV_SQRT2))
    z1_ref[...] = z[:, :SG_W].astype(z1_ref.dtype)
    z2 = z[:, SG_W:]
    mu = jnp.mean(z2, axis=-1, keepdims=True)
    zc = z2 - mu
    var = jnp.mean(zc * zc, axis=-1, keepdims=True)
    zn_ref[...] = (zc * lax.rsqrt(var + EPS) * lng_ref[...] + lnb_ref[...]).astype(zn_ref.dtype)

    gates = jnp.dot(h, w_ref[:, G_OFF:], preferred_element_type=F32) + bg_ref[...]
    gates = 1.0 / (1.0 + jnp.exp(-gates))
    ga_ref[...] = gates[:, :D_MODEL].astype(BF16)
    gb_ref[...] = gates[:, D_MODEL:].astype(BF16)


def _cast_blocks(pairs):
    for src_ref, dst_ref in pairs:
        dst_ref[...] = src_ref[...].astype(dst_ref.dtype)


def _proj_sample_kernel(x_ref, g_ref, wf_ref, bg_ref, lng_ref, lnb_ref, wao_ref, wbo_ref, wout_ref,
                        qkv_ref, z1_ref, zn_ref, ga_ref, gb_ref, w_ref, wao_bf_ref, wbo_bf_ref, wout_bf_ref):
    @pl.when(pl.program_id(0) == 0)
    def _():
        for c in range(0, IN_W, CAST_COLS):
            w_ref[:, c:c + CAST_COLS] = wf_ref[:, c:c + CAST_COLS].astype(BF16)

    _cast_blocks(((wao_ref, wao_bf_ref), (wbo_ref, wbo_bf_ref), (wout_ref, wout_bf_ref)))
    x = x_ref[...]
    h = _rms(x.reshape(x.shape[0] * x.shape[1], x.shape[2]), g_ref[...]).astype(BF16)
    qkv_ref[...] = jnp.dot(h, w_ref[:, :Z_OFF], preferred_element_type=F32)
    _proj_gate_branches(h, w_ref, bg_ref, lng_ref, lnb_ref, z1_ref, zn_ref, ga_ref, gb_ref)


def _proj_prompt_kernel(x_ref, g_ref, w_ref, bg_ref, lng_ref, lnb_ref, wup_ref, wdn_ref,
                        qkv_ref, kv1_ref, kv2_ref, kv3_ref, z1_ref, zn_ref, ga_ref, gb_ref, wup_bf_ref, wdn_bf_ref):
    _cast_blocks(((wup_ref, wup_bf_ref), (wdn_ref, wdn_bf_ref)))
    h = _rms(x_ref[0], g_ref[...]).astype(BF16)
    _proj_gate_branches(h, w_ref, bg_ref, lng_ref, lnb_ref, z1_ref, zn_ref, ga_ref, gb_ref)
    def project(first_part, n_parts, scale=None):
        cols = jnp.dot(h, w_ref[:, first_part * ATTN_W:(first_part + n_parts) * ATTN_W], preferred_element_type=F32)
        if scale is not None:
            cols = cols * scale
        for part in range(n_parts):
            for g in range(N_GROUPS):
                for s in range(SLABS):
                    col = part * ATTN_W + g * GROUP_W + s * LANES
                    qkv_ref[0, g, (first_part + part) * SLABS + s] = cols[:, col:col + LANES]

    def kv_t(g):
        return jnp.concatenate([qkv_ref[0, g, slab].T for slab in range(SLABS, 3 * SLABS)], axis=0)

    tm = x_ref.shape[1]
    project(1, 2)
    kv3_ref[0] = kv_t(2)
    project(0, 1, scale=SCALE * LOG2E)

    @pl.when(pl.program_id(1) == pl.num_programs(1) - 1)
    def _():
        kv2_ref[0] = kv_t(1)[:, tm - kv2_ref.shape[2]:]
        kv1_ref[0] = kv_t(0)[:, tm - kv1_ref.shape[2]:]


def _proj_weight_specs():
    return [_const_spec((1, D_MODEL)), _const_spec((D_MODEL, IN_W)),
            _const_spec((1, 2 * D_MODEL)), _const_spec((1, SG_W)), _const_spec((1, SG_W))]


def _gate_branch_shapes(n, z_dtype):
    return [jax.ShapeDtypeStruct((n, SG_W), z_dtype), jax.ShapeDtypeStruct((n, SG_W), z_dtype),
            jax.ShapeDtypeStruct((n, D_MODEL), BF16), jax.ShapeDtypeStruct((n, D_MODEL), BF16)]


def _row_slices(arrays, steps):
    specs, shapes = [], []
    for a in arrays:
        rows = a.shape[0] // steps
        assert rows * steps == a.shape[0] and rows % BF16_SUBLANES == 0
        specs.append((rows, a.shape[1]))
        shapes.append(jax.ShapeDtypeStruct(a.shape, BF16))
    return specs, shapes


def _project_sample(x3, g, w_in, bg, lng, lnb, mix_weights):
    n = x3.shape[0] * x3.shape[1]
    tm = min(SAMPLE_TAIL_TILE, n)
    steps = n // tm
    row = lambda w: pl.BlockSpec((tm, w), lambda i: (i, 0))
    slices, bf_shapes = _row_slices(mix_weights, steps)
    slice_specs = [pl.BlockSpec(s, lambda i: (i, 0)) for s in slices]
    return pl.pallas_call(
        _proj_sample_kernel,
        grid=(steps,),
        in_specs=[pl.BlockSpec((tm // x3.shape[1],) + x3.shape[1:], lambda i: (i, 0, 0))] + _proj_weight_specs()
        + slice_specs,
        out_specs=[row(Z_OFF), row(SG_W), row(SG_W), row(D_MODEL), row(D_MODEL),
                   pl.BlockSpec((D_MODEL, IN_W), lambda i: (0, 0))] + slice_specs,
        out_shape=[jax.ShapeDtypeStruct((n, Z_OFF), F32)] + _gate_branch_shapes(n, F32)
        + [jax.ShapeDtypeStruct((D_MODEL, IN_W), BF16)] + bf_shapes,
        compiler_params=_params(1),
        name="proj_sample",
    )(x3, g, w_in, bg, lng, lnb, *mix_weights)


def _project_prompt(x, g, w_bf, bg, lng, lnb, ffn_weights):
    batch, seq, _ = x.shape
    tm = TOKEN_TILE
    tiles = seq // tm
    keeps = [min(win, seq) for win, _ in DIL_GROUPS]
    assert seq % tm == 0 and keeps[0] <= tm and keeps[1] <= tm and keeps[2] == seq
    row = lambda w: pl.BlockSpec((tm, w), lambda b, t: (b * tiles + t, 0))
    last = lambda keep: pl.BlockSpec((1, 2 * GROUP_W, keep), lambda b, t: (b, 0, 0))
    n = batch * seq
    slices, bf_shapes = _row_slices(ffn_weights, batch * tiles)
    slice_specs = [pl.BlockSpec(s, lambda b, t: (b * tiles + t, 0)) for s in slices]
    return pl.pallas_call(
        _proj_prompt_kernel,
        grid=(batch, tiles),
        in_specs=[pl.BlockSpec((1, tm, D_MODEL), lambda b, t: (b, t, 0))] + _proj_weight_specs() + slice_specs,
        out_specs=[pl.BlockSpec((1, N_GROUPS, 3 * SLABS, tm, LANES), lambda b, t: (b, 0, 0, t, 0)),
                   last(keeps[0]), last(keeps[1]),
                   pl.BlockSpec((1, 2 * GROUP_W, tm), lambda b, t: (b, 0, t)),
                   row(SG_W), row(SG_W), row(D_MODEL), row(D_MODEL)] + slice_specs,
        out_shape=[jax.ShapeDtypeStruct((batch, N_GROUPS, 3 * SLABS, seq, LANES), F32),
                   jax.ShapeDtypeStruct((batch, 2 * GROUP_W, keeps[0]), F32),
                   jax.ShapeDtypeStruct((batch, 2 * GROUP_W, keeps[1]), F32),
                   jax.ShapeDtypeStruct((batch, 2 * GROUP_W, keeps[2]), F32)] + _gate_branch_shapes(n, BF16)
        + bf_shapes,
        compiler_params=_params(2),
        name="proj_prompt",
    )(x, g, w_bf, bg, lng, lnb, *ffn_weights)


def _head_lane_mask(shape, hh):
    lane = lax.broadcasted_iota(jnp.int32, shape, len(shape) - 1)
    return (lane >> HEAD_SHIFT) == hh


def _band_block(q, k, v, mask):
    nt = (((1,), (1,)), ((), ()))
    assert HEADS_PER_SLAB == 2
    lm0 = _head_lane_mask((BLOCK, LANES), 0)
    mask2 = jnp.concatenate([mask, mask], axis=0)
    outs, lses = [], []
    for s in range(SLABS):
        zero = jnp.zeros_like(q[s])
        qm = jnp.concatenate([jnp.where(lm0, q[s], zero), jnp.where(lm0, zero, q[s])], axis=0)
        sc = lax.dot_general(qm, k[s], nt, preferred_element_type=F32)
        sc = jnp.where(mask2, sc, NEG)
        m = jnp.max(sc, axis=-1, keepdims=True)
        p = jnp.exp2(sc - m)
        l = jnp.sum(p, axis=-1, keepdims=True)
        o = jnp.dot(p.astype(BF16), v[s], preferred_element_type=F32) * (1.0 / l)
        lse = m + jnp.log2(l)
        outs.append(jnp.where(lm0, o[:BLOCK], o[BLOCK:]))
        lses.append(jnp.where(lm0, lse[:BLOCK], lse[BLOCK:]))
    return outs, lses


def _group_attention(qkv_ref, o_scr, lse_scr, g, dil, seq):
    n_qb = seq // (dil * BLOCK)
    qb_shift = n_qb.bit_length() - 1
    span = dil * BLOCK
    rows = (lambda start: pl.ds(start, BLOCK, stride=dil)) if dil > 1 else (lambda start: pl.ds(start, BLOCK))
    nk = 2 * BLOCK if n_qb > 1 else BLOCK
    qi = lax.broadcasted_iota(jnp.int32, (BLOCK, nk), 0)
    ki = lax.broadcasted_iota(jnp.int32, (BLOCK, nk), 1)

    def load(slab, start):
        return qkv_ref[0, 0, slab, rows(start), :]

    def body(n, carry):
        r = n >> qb_shift
        qb = n & (n_qb - 1)
        cur = r + span * qb
        if dil == 1:
            cur = pl.multiple_of(cur, BLOCK)
        q = [load(s, cur).astype(BF16) for s in range(SLABS)]
        k = [load(SLABS + s, cur).astype(BF16) for s in range(SLABS)]
        v = [load(2 * SLABS + s, cur).astype(BF16) for s in range(SLABS)]
        if n_qb > 1:
            prev = r + span * jnp.maximum(qb - 1, 0)
            if dil == 1:
                prev = pl.multiple_of(prev, BLOCK)
            k = [jnp.concatenate([load(SLABS + s, prev).astype(BF16), k[s]], axis=0) for s in range(SLABS)]
            v = [jnp.concatenate([load(2 * SLABS + s, prev).astype(BF16), v[s]], axis=0) for s in range(SLABS)]
            mask = (ki >= qi) & (ki <= qi + BLOCK) & (ki >= jnp.where(qb > 0, 0, BLOCK))
        else:
            mask = ki <= qi
        outs, lses = _band_block(q, k, v, mask)
        for s in range(SLABS):
            o_scr[g, s, rows(cur), :] = outs[s]
            lse_scr[g, s, rows(cur), :] = lses[s]
        return carry

    lax.fori_loop(0, dil * n_qb, body, 0, unroll=BLOCK_UNROLL)


def _prompt_attn_kernel(qkv_ref, o_ref, o_scr, lse_scr):
    g = pl.program_id(1)
    seq = o_ref.shape[1]
    for gi, (_, dil) in enumerate(DIL_GROUPS):
        @pl.when(g == gi)
        def _(gi=gi, dil=dil):
            _group_attention(qkv_ref, o_scr, lse_scr, gi, dil, seq)

    @pl.when(g == N_GROUPS - 1)
    def _():
        def merge(i, carry):
            rows = pl.ds(pl.multiple_of(i * BLOCK, BLOCK), BLOCK)
            for s in range(SLABS):
                ls = [lse_scr[gi, s, rows, :] for gi in range(N_GROUPS)]
                top = jnp.maximum(jnp.maximum(ls[0], ls[1]), ls[2])
                e = [jnp.exp2(x - top) for x in ls]
                num = e[0] * o_scr[0, s, rows, :] + e[1] * o_scr[1, s, rows, :] + e[2] * o_scr[2, s, rows, :]
                o_ref[0, rows, s * LANES:(s + 1) * LANES] = (num / (e[0] + e[1] + e[2])).astype(o_ref.dtype)
            return carry
        lax.fori_loop(0, seq // BLOCK, merge, 0)


def _prompt_attention(qkv, seq):
    batch = qkv.shape[0]
    scratch = pltpu.VMEM((N_GROUPS, SLABS, seq, LANES), F32)
    return pl.pallas_call(
        _prompt_attn_kernel,
        grid=(batch, N_GROUPS),
        in_specs=[pl.BlockSpec((1, 1, 3 * SLABS, seq, LANES), lambda b, g: (b, g, 0, 0, 0))],
        out_specs=pl.BlockSpec((1, seq, GROUP_W), lambda b, g: (b, 0, 0)),
        out_shape=jax.ShapeDtypeStruct((batch, seq, GROUP_W), BF16),
        scratch_shapes=[scratch, scratch],
        compiler_params=_params(2),
        name="prompt_attn",
    )(qkv)


def _residue_allreduce(x, dil, op):
    shift = LANES // 2
    while shift >= dil:
        x = op(x, pltpu.roll(x, shift, axis=1))
        shift //= 2
    return x


def _lane_expand(rows8, dil, r0, t_new):
    lane = lax.broadcasted_iota(jnp.int32, (LANES, T_PAD), 0)
    t = lax.broadcasted_iota(jnp.int32, (LANES, T_PAD), 1)
    sel = jnp.where(((lane & (dil - 1)) == t - r0) & (t < r0 + t_new), 1.0, 0.0).astype(BF16)
    hi = rows8.astype(BF16)
    lo = (rows8 - hi.astype(F32)).astype(BF16)
    out = jnp.dot(sel, hi, preferred_element_type=F32) + jnp.dot(sel, lo, preferred_element_type=F32)
    return out.T


def _heads_to_rows(x):
    return jnp.broadcast_to(x[:, None, :], (HEADS, HEAD_DIM, LANES)).reshape(GROUP_W, LANES)


def _rows_to_heads_sum(x):
    return jnp.sum(x.reshape(HEADS, HEAD_DIM, LANES), axis=1)


def _buffer_group_lanes(c_ref, d, q8, kn8, vn8, dil, r0, t_new):
    width = c_ref.shape[2]
    tiles = width // LANES
    ql, kl, vl = (_lane_expand(a, dil, r0, t_new) for a in (q8, kn8, vn8))
    s = [_rows_to_heads_sum(c_ref[d, :GROUP_W, j * LANES:(j + 1) * LANES] * ql) for j in range(tiles)]
    s_new = _rows_to_heads_sum(kl * ql)
    top = s[0]
    for sj in s[1:]:
        top = jnp.maximum(top, sj)
    top = jnp.maximum(_residue_allreduce(top, dil, jnp.maximum), s_new)
    p = [jnp.exp(sj - top) for sj in s]
    p_new = jnp.exp(s_new - top)
    den = p[0]
    for pj in p[1:]:
        den = den + pj
    den = _residue_allreduce(den, dil, jnp.add) + p_new
    acc = c_ref[d, GROUP_W:, :LANES] * _heads_to_rows(p[0])
    for j in range(1, tiles):
        acc = acc + c_ref[d, GROUP_W:, j * LANES:(j + 1) * LANES] * _heads_to_rows(p[j])
    acc = (_residue_allreduce(acc, dil, jnp.add) + vl * _heads_to_rows(p_new)) * _heads_to_rows(1.0 / den)
    lse = _heads_to_rows(top + jnp.log(den))
    first = -(-r0 // dil) * dil
    return acc.T[first - r0:first - r0 + T_PAD], lse.T[first - r0:first - r0 + T_PAD]


def _buffer_group_dense(c_ref, d, q8, kn8, vn8, r0, t_new):
    rows = HEADS * T_PAD
    width = c_ref.shape[2]
    row = lax.broadcasted_iota(jnp.int32, (rows, GROUP_W), 0)
    lane = lax.broadcasted_iota(jnp.int32, (rows, GROUP_W), 1)
    head_sel = (row >> T_SHIFT) == (lane >> HEAD_SHIFT)
    t = (lax.broadcasted_iota(jnp.int32, (rows, width), 0) & (T_PAD - 1)) - r0
    i = lax.broadcasted_iota(jnp.int32, (rows, width), 1)
    tn = lax.broadcasted_iota(jnp.int32, (rows, T_PAD), 0) & (T_PAD - 1)
    j = lax.broadcasted_iota(jnp.int32, (rows, T_PAD), 1)
    nt = (((1,), (1,)), ((), ()))
    qm = jnp.where(head_sel, jnp.concatenate([q8] * HEADS, axis=0), 0.0).astype(BF16)
    kt = c_ref[d, :GROUP_W, :].astype(BF16)
    vt = c_ref[d, GROUP_W:, :].astype(BF16)
    sc = jnp.where(i >= t, jnp.dot(qm, kt, preferred_element_type=F32), NEG)
    own = (j <= tn) & (j >= r0) & (j < r0 + t_new)
    sn = jnp.where(own, lax.dot_general(qm, kn8.astype(BF16), nt, preferred_element_type=F32), NEG)
    m = jnp.maximum(jnp.max(sc, axis=-1, keepdims=True), jnp.max(sn, axis=-1, keepdims=True))
    pc = jnp.exp(sc - m)
    pn = jnp.exp(sn - m)
    l = jnp.sum(pc, axis=-1, keepdims=True) + jnp.sum(pn, axis=-1, keepdims=True)

    def finish():
        o = (lax.dot_general(pc.astype(BF16), vt, nt, preferred_element_type=F32)
             + jnp.dot(pn.astype(BF16), vn8.astype(BF16), preferred_element_type=F32))
        o = jnp.where(head_sel, o / l, 0.0)
        lse = jnp.where(head_sel, m + jnp.log(l), 0.0)
        fold = lambda x: sum(x[h * T_PAD:(h + 1) * T_PAD] for h in range(HEADS))
        return fold(o), fold(lse)
    return finish


def _sample_attn_start(qkv_ref, c1_ref, c2_ref, c3_ref, o_ref):
    n_db = c1_ref.shape[0]
    t_new = T_PAD // n_db
    assert qkv_ref.shape[0] == T_PAD and n_db * t_new == T_PAD
    pad_rows = jnp.zeros((T_PAD - t_new, qkv_ref.shape[1]), F32)
    pending = []
    for d in range(n_db):
        new_rows = jnp.concatenate([qkv_ref[d * t_new:(d + 1) * t_new, :], pad_rows], axis=0)
        groups = []
        for g, (c_ref, (_, dil)) in enumerate(zip((c1_ref, c2_ref, c3_ref), DIL_GROUPS)):
            cols = lambda part: slice(part * ATTN_W + g * GROUP_W, part * ATTN_W + (g + 1) * GROUP_W)
            q8 = new_rows[:, cols(0)] * SCALE
            kn8, vn8 = new_rows[:, cols(1)], new_rows[:, cols(2)]
            if dil == 1:
                groups.append(_buffer_group_dense(c_ref, d, q8, kn8, vn8, 0, t_new))
            else:
                result = _buffer_group_lanes(c_ref, d, q8, kn8, vn8, dil, 0, t_new)
                groups.append(lambda result=result: result)
        pending.append(groups)

    def finish():
        for d, groups in enumerate(pending):
            outs, lses = zip(*(group() for group in groups))
            top = jnp.maximum(jnp.maximum(lses[0], lses[1]), lses[2])
            w = [jnp.exp(x - top) for x in lses]
            o8 = (w[0] * outs[0] + w[1] * outs[1] + w[2] * outs[2]) / (w[0] + w[1] + w[2])
            o_ref[d * t_new:(d + 1) * t_new, :] = o8[:t_new]
    return finish


def _mix_value(o_a, o_b, ga_ref, gb_ref, x, wao_ref, wbo_ref, wout_ref, gpost_ref):
    a = jnp.dot(o_a.astype(BF16), wao_ref[...], preferred_element_type=F32)
    b = jnp.dot(o_b.astype(BF16), wbo_ref[...], preferred_element_type=F32)
    merged = ga_ref[...].astype(F32) * a + gb_ref[...].astype(F32) * b
    t = jnp.dot(merged.astype(BF16), wout_ref[...], preferred_element_type=F32)
    return x + _rms(t, gpost_ref[...])


def _ffn_value(x, gpre_ref, wup_ref, wdn_ref, gpost_ref):
    h = _rms(x, gpre_ref[...]).astype(BF16)
    f = jnp.zeros(x.shape, F32)
    for c in range(D_FF // FF_CHUNK):
        cs = slice(c * FF_CHUNK, (c + 1) * FF_CHUNK)
        u = jnp.maximum(jnp.dot(h, wup_ref[:, cs], preferred_element_type=F32), 0.0)
        f = f + jnp.dot((u * u).astype(BF16), wdn_ref[cs, :], preferred_element_type=F32)
    return x + _rms(f, gpost_ref[...])


def _prompt_spatial_gate(z1_ref, zn_ref, wsp_ref, bsp_ref):
    r = lax.broadcasted_iota(jnp.int32, (CHUNK, CHUNK), 0)
    c = lax.broadcasted_iota(jnp.int32, (CHUNK, CHUNK), 1)
    n_chunks = z1_ref.shape[0] // CHUNK
    mixed = []
    for g in range(SG_GROUPS):
        wt = jnp.where(r >= c, wsp_ref[g], 0.0).astype(BF16)
        zg = jnp.concatenate([zn_ref[ci * CHUNK:(ci + 1) * CHUNK, g * CHUNK:(g + 1) * CHUNK].astype(BF16)
                              for ci in range(n_chunks)], axis=1)
        mixed.append(jnp.dot(wt, zg, preferred_element_type=F32))
    chunks = [jnp.concatenate([m[:, ci * CHUNK:(ci + 1) * CHUNK] for m in mixed], axis=1) + bsp_ref[...]
              for ci in range(n_chunks)]
    return z1_ref[...].astype(F32) * jnp.concatenate(chunks, axis=0)


def _tail_prompt_kernel(oa_ref, z1_ref, zn_ref, ga_ref, gb_ref, x_ref, wsp_ref, bsp_ref,
                        wao_ref, wbo_ref, wout_ref, gmix_ref, gpre_ref, wup_ref, wdn_ref, gpost_ref,
                        qkv_ref, c1_ref, c2_ref, c3_ref, y_ref, o_ref):
    finish_sample_attn = _sample_attn_start(qkv_ref, c1_ref, c2_ref, c3_ref, o_ref)
    o_b = _prompt_spatial_gate(z1_ref, zn_ref, wsp_ref, bsp_ref)
    x1 = _mix_value(oa_ref[...], o_b, ga_ref, gb_ref, x_ref[...], wao_ref, wbo_ref, wout_ref, gmix_ref)
    y_ref[...] = _ffn_value(x1, gpre_ref, wup_ref, wdn_ref, gpost_ref)
    finish_sample_attn()


def _tail_sample_kernel(oa_ref, z1_ref, zn_ref, ga_ref, gb_ref, x_ref, coef_ref, bsp_ref,
                        wao_ref, wbo_ref, wout_ref, gmix_ref, gpre_ref, wup_ref, wdn_ref, gpost_ref, y_ref):
    zn = zn_ref[...]
    mix = coef_ref[0] * zn
    for d in range(1, coef_ref.shape[0]):
        mix = mix + coef_ref[d] * pltpu.roll(zn, d, axis=0)
    o_b = z1_ref[...] * (mix + bsp_ref[...])
    x = x_ref[...].reshape(zn.shape[0], D_MODEL)
    x1 = _mix_value(oa_ref[...], o_b, ga_ref, gb_ref, x, wao_ref, wbo_ref, wout_ref, gmix_ref)
    y_ref[...] = _ffn_value(x1, gpre_ref, wup_ref, wdn_ref, gpost_ref).reshape(y_ref.shape)


def _tail_weight_specs():
    return [_const_spec((GROUP_W, D_MODEL)), _const_spec((SG_W, D_MODEL)), _const_spec((D_MODEL, D_MODEL)),
            _const_spec((1, D_MODEL)), _const_spec((1, D_MODEL)), _const_spec((D_MODEL, D_FF)),
            _const_spec((D_FF, D_MODEL)), _const_spec((1, D_MODEL))]


def _tail_prompt(o_a, z1, zn, ga, gb, x, wsp, bsp, weights, qkv_s, caches):
    n, db, ns = x.shape[0], caches[0].shape[0], qkv_s.shape[0]
    tile = SAMPLE_DB_TILE
    steps = db // tile
    tm = n // steps
    assert db % tile == 0 and n % steps == 0 and tm % CHUNK == 0 and ns == steps * T_PAD
    row = lambda w: pl.BlockSpec((tm, w), lambda i: (i, 0))
    blk = lambda a: pl.BlockSpec((tile,) + a.shape[1:], lambda i: (i, 0, 0))
    new_rows = lambda w: pl.BlockSpec((T_PAD, w), lambda i: (i, 0))
    return pl.pallas_call(
        _tail_prompt_kernel,
        grid=(steps,),
        in_specs=[row(GROUP_W), row(SG_W), row(SG_W), row(D_MODEL), row(D_MODEL), row(D_MODEL),
                  _const_spec((SG_GROUPS, CHUNK, CHUNK)), _const_spec((CHUNK, SG_W))] + _tail_weight_specs()
        + [new_rows(Z_OFF)] + [blk(a) for a in caches],
        out_specs=[row(D_MODEL), new_rows(GROUP_W)],
        out_shape=[jax.ShapeDtypeStruct((n, D_MODEL), F32),
                   jax.ShapeDtypeStruct((ns, GROUP_W), F32)],
        compiler_params=_params(1),
        name="tail_prompt",
    )(o_a, z1, zn, ga, gb, x, wsp, bsp, *weights, qkv_s, *caches)


def _tail_sample(o_a, z1, zn, ga, gb, x3, coef, bsp, weights):
    n = x3.shape[0] * x3.shape[1]
    tm = coef.shape[1]
    row = lambda w: pl.BlockSpec((tm, w), lambda i: (i, 0))
    rows3 = pl.BlockSpec((tm // x3.shape[1],) + x3.shape[1:], lambda i: (i, 0, 0))
    return pl.pallas_call(
        _tail_sample_kernel,
        grid=(n // tm,),
        in_specs=[row(GROUP_W), row(SG_W), row(SG_W), row(D_MODEL), row(D_MODEL), rows3,
                  _const_spec(coef.shape), _const_spec(bsp.shape)] + _tail_weight_specs(),
        out_specs=rows3,
        out_shape=jax.ShapeDtypeStruct(x3.shape, F32),
        compiler_params=_params(1),
        name="tail_sample",
    )(o_a, z1, zn, ga, gb, x3, coef, bsp, *weights)


def _kv_from_feature_major(kv_t):
    batch, _, keep = kv_t.shape
    return kv_t.reshape(batch, 2, HEADS, HEAD_DIM, keep).transpose(0, 4, 1, 2, 3)[None]


def kernel(x_prompt, x_sample, cache_kv_w128, cache_kv_w512, cache_kv_w2048, norm_pre_mix, w_in, b_gate,
           ln_z_g, ln_z_b, w_spatial, b_spatial, w_ao, w_bo, w_out, norm_post_mix, norm_pre_ffn, w_up, w_down,
           norm_post_ffn):
    assert w_in.shape[0] == 1, "single-layer problem"
    batch, seq, _ = x_prompt.shape
    dbatch, t_new, _ = x_sample.shape
    caches_in = (cache_kv_w128[0], cache_kv_w512[0], cache_kv_w2048[0])

    wsp = w_spatial[0]
    bsp_rows = jnp.repeat(b_spatial[0].T, CHUNK, axis=1)

    ns = dbatch * t_new
    qkv_s, z1_s, zn_s, ga_s, gb_s, w_in_bf, wao, wbo, wout = _project_sample(
        x_sample, norm_pre_mix, w_in[0], b_gate, ln_z_g, ln_z_b, (w_ao[0], w_bo[0], w_out[0]))
    caches, kv_sample = [], []
    for gi, (win, dil) in enumerate(DIL_GROUPS):
        buf = caches_in[gi]
        wb = buf.shape[1]
        assert wb == win and t_new <= T_PAD, "full window buffers"
        caches.append(buf.transpose(0, 2, 3, 4, 1).reshape(dbatch, 2 * GROUP_W, wb))
        gs = slice(gi * GROUP_W, (gi + 1) * GROUP_W)
        kn = qkv_s[:, ATTN_W:2 * ATTN_W][:, gs].reshape(dbatch, t_new, HEADS, HEAD_DIM)
        vn = qkv_s[:, 2 * ATTN_W:][:, gs].reshape(dbatch, t_new, HEADS, HEAD_DIM)
        kv_sample.append(jnp.stack([kn, vn], axis=2)[None])

    xp = x_prompt.reshape(batch * seq, D_MODEL)
    qkv, kv1_t, kv2_t, kv3_t, z1, zn, ga, gb, wup, wdn = _project_prompt(
        x_prompt, norm_pre_mix, w_in_bf, b_gate, ln_z_g, ln_z_b, (w_up[0], w_down[0]))
    weights = (wao, wbo, wout, norm_post_mix, norm_pre_ffn, wup, wdn, norm_post_ffn)
    kv_prompt = [_kv_from_feature_major(a) for a in (kv1_t, kv2_t, kv3_t)]
    o_a_p = _prompt_attention(qkv, seq).reshape(batch * seq, GROUP_W)
    assert SAMPLE_DB_TILE * t_new == T_PAD, "a tail step takes T_PAD new rows"
    y_prompt, o_a = _tail_prompt(o_a_p, z1, zn, ga, gb, xp, wsp, bsp_rows, weights, qkv_s, caches)
    y_prompt = y_prompt.reshape(batch, seq, D_MODEL)

    wt = jnp.tril(wsp)[:, :t_new, :t_new]
    coef = jnp.stack([jnp.where(jnp.arange(t_new)[None, :] >= d,
                                wt[:, jnp.arange(t_new), jnp.maximum(jnp.arange(t_new) - d, 0)], 0.0)
                      for d in range(t_new)])
    coef = jnp.repeat(coef.transpose(0, 2, 1), CHUNK, axis=2)
    reps = SAMPLE_TAIL_TILE // t_new
    coef = jnp.tile(coef, (1, reps, 1))
    bsp_s = jnp.tile(bsp_rows[:t_new], (reps, 1))
    y_sample = _tail_sample(o_a, z1_s, zn_s, ga_s, gb_s, x_sample, coef, bsp_s, weights)
    sg_sample = zn_s.reshape(1, dbatch, t_new, SG_W)

    return (y_prompt, y_sample, kv_prompt[0], kv_prompt[1], kv_prompt[2],
            kv_sample[0], kv_sample[1], kv_sample[2], sg_sample)
```

```python
import math

import jax
import jax.numpy as jnp
from jax import lax
from jax.experimental import pallas as pl
from jax.experimental.pallas import tpu as pltpu

F32 = jnp.float32
BF16 = jnp.bfloat16

D_MODEL = 1024
HEAD_DIM = 64
HEADS = 4
GROUP_W = HEADS * HEAD_DIM
DIL_GROUPS = ((128, 1), (512, 4), (2048, 16))
N_GROUPS = len(DIL_GROUPS)
ATTN_W = N_GROUPS * GROUP_W
BLOCK = 128
CHUNK = 128
SG_GROUPS = 4
SG_W = 512
D_FF = 4 * D_MODEL
Z_OFF = 3 * ATTN_W
G_OFF = Z_OFF + 2 * SG_W
IN_W = G_OFF + 2 * D_MODEL
EPS = 1e-6
NEG = -1e30
SCALE = HEAD_DIM ** -0.5
INV_SQRT2 = 1.0 / math.sqrt(2.0)
LOG2E = 1.0 / math.log(2.0)
LANES = 128
SLABS = GROUP_W // LANES
HEADS_PER_SLAB = LANES // HEAD_DIM
HEAD_SHIFT = HEAD_DIM.bit_length() - 1
VMEM_LIMIT = 56 * 1024 * 1024

TOKEN_TILE = 512
MERGE_UNROLL = 4
BLOCK_UNROLL = 16
SAMPLE_DB_TILE = 2
T_PAD = 8
T_SHIFT = T_PAD.bit_length() - 1
BF16_SUBLANES = 16
CAST_COLS = 768
SAMPLE_TAIL_TILE = 256
FF_CHUNK = 1024


def _params(n_axes):
    return pltpu.CompilerParams(dimension_semantics=("arbitrary",) * n_axes,
                                vmem_limit_bytes=VMEM_LIMIT)


def _const_spec(shape):
    return pl.BlockSpec(shape, lambda *_: (0,) * len(shape), pipeline_mode=pl.Buffered(1))


def _rms(x, g):
    return x * lax.rsqrt(jnp.mean(x * x, axis=-1, keepdims=True) + EPS) * g


def _proj_gate_branches(h, w_ref, bg_ref, lng_ref, lnb_ref, z1_ref, zn_ref, ga_ref, gb_ref):
    z = jnp.dot(h, w_ref[:, Z_OFF:G_OFF], preferred_element_type=F32)
    z = 0.5 * z * (1.0 + lax.erf(z * INV_SQRT2))
    z1_ref[...] = z[:, :SG_W].astype(z1_ref.dtype)
    z2 = z[:, SG_W:]
    mu = jnp.mean(z2, axis=-1, keepdims=True)
    zc = z2 - mu
    var = jnp.mean(zc * zc, axis=-1, keepdims=True)
    zn_ref[...] = (zc * lax.rsqrt(var + EPS) * lng_ref[...] + lnb_ref[...]).astype(zn_ref.dtype)

    gates = jnp.dot(h, w_ref[:, G_OFF:], preferred_element_type=F32) + bg_ref[...]
    gates = 1.0 / (1.0 + jnp.exp(-gates))
    ga_ref[...] = gates[:, :D_MODEL].astype(BF16)
    gb_ref[...] = gates[:, D_MODEL:].astype(BF16)


def _cast_blocks(pairs):
    for src_ref, dst_ref in pairs:
        dst_ref[...] = src_ref[...].astype(dst_ref.dtype)


def _proj_sample_kernel(x_ref, g_ref, wf_ref, bg_ref, lng_ref, lnb_ref, wao_ref, wbo_ref, wout_ref,
                        qkv_ref, z1_ref, zn_ref, ga_ref, gb_ref, w_ref, wao_bf_ref, wbo_bf_ref, wout_bf_ref):
    @pl.when(pl.program_id(0) == 0)
    def _():
        for c in range(0, IN_W, CAST_COLS):
            w_ref[:, c:c + CAST_COLS] = wf_ref[:, c:c + CAST_COLS].astype(BF16)

    _cast_blocks(((wao_ref, wao_bf_ref), (wbo_ref, wbo_bf_ref), (wout_ref, wout_bf_ref)))
    x = x_ref[...]
    h = _rms(x.reshape(x.shape[0] * x.shape[1], x.shape[2]), g_ref[...]).astype(BF16)
    qkv_ref[...] = jnp.dot(h, w_ref[:, :Z_OFF], preferred_element_type=F32)
    _proj_gate_branches(h, w_ref, bg_ref, lng_ref, lnb_ref, z1_ref, zn_ref, ga_ref, gb_ref)


def _proj_prompt_kernel(x_ref, g_ref, w_ref, bg_ref, lng_ref, lnb_ref, wup_ref, wdn_ref,
                        qkv_ref, kv1_ref, kv2_ref, kv3_ref, z1_ref, zn_ref, ga_ref, gb_ref, wup_bf_ref, wdn_bf_ref):
    _cast_blocks(((wup_ref, wup_bf_ref), (wdn_ref, wdn_bf_ref)))
    h = _rms(x_ref[0], g_ref[...]).astype(BF16)
    _proj_gate_branches(h, w_ref, bg_ref, lng_ref, lnb_ref, z1_ref, zn_ref, ga_ref, gb_ref)
    def project(first_part, n_parts, scale=None):
        cols = jnp.dot(h, w_ref[:, first_part * ATTN_W:(first_part + n_parts) * ATTN_W], preferred_element_type=F32)
        if scale is not None:
            cols = cols * scale
        for part in range(n_parts):
            for g in range(N_GROUPS):
                for s in range(SLABS):
                    col = part * ATTN_W + g * GROUP_W + s * LANES
                    qkv_ref[0, g, (first_part + part) * SLABS + s] = cols[:, col:col + LANES]

    def kv_t(g):
        return jnp.concatenate([qkv_ref[0, g, slab].T for slab in range(SLABS, 3 * SLABS)], axis=0)

    tm = x_ref.shape[1]
    project(1, 2)
    kv3_ref[0] = kv_t(2)
    project(0, 1, scale=SCALE * LOG2E)

    @pl.when(pl.program_id(1) == pl.num_programs(1) - 1)
    def _():
        kv2_ref[0] = kv_t(1)[:, tm - kv2_ref.shape[2]:]
        kv1_ref[0] = kv_t(0)[:, tm - kv1_ref.shape[2]:]


def _proj_weight_specs():
    return [_const_spec((1, D_MODEL)), _const_spec((D_MODEL, IN_W)),
            _const_spec((1, 2 * D_MODEL)), _const_spec((1, SG_W)), _const_spec((1, SG_W))]


def _gate_branch_shapes(n, z_dtype):
    return [jax.ShapeDtypeStruct((n, SG_W), z_dtype), jax.ShapeDtypeStruct((n, SG_W), z_dtype),
            jax.ShapeDtypeStruct((n, D_MODEL), BF16), jax.ShapeDtypeStruct((n, D_MODEL), BF16)]


def _row_slices(arrays, steps):
    specs, shapes = [], []
    for a in arrays:
        rows = a.shape[0] // steps
        assert rows * steps == a.shape[0] and rows % BF16_SUBLANES == 0
        specs.append((rows, a.shape[1]))
        shapes.append(jax.ShapeDtypeStruct(a.shape, BF16))
    return specs, shapes


def _project_sample(x3, g, w_in, bg, lng, lnb, mix_weights):
    n = x3.shape[0] * x3.shape[1]
    tm = min(SAMPLE_TAIL_TILE, n)
    steps = n // tm
    row = lambda w: pl.BlockSpec((tm, w), lambda i: (i, 0))
    slices, bf_shapes = _row_slices(mix_weights, steps)
    slice_specs = [pl.BlockSpec(s, lambda i: (i, 0)) for s in slices]
    return pl.pallas_call(
        _proj_sample_kernel,
        grid=(steps,),
        in_specs=[pl.BlockSpec((tm // x3.shape[1],) + x3.shape[1:], lambda i: (i, 0, 0))] + _proj_weight_specs()
        + slice_specs,
        out_specs=[row(Z_OFF), row(SG_W), row(SG_W), row(D_MODEL), row(D_MODEL),
                   pl.BlockSpec((D_MODEL, IN_W), lambda i: (0, 0))] + slice_specs,
        out_shape=[jax.ShapeDtypeStruct((n, Z_OFF), F32)] + _gate_branch_shapes(n, F32)
        + [jax.ShapeDtypeStruct((D_MODEL, IN_W), BF16)] + bf_shapes,
        compiler_params=_params(1),
        name="proj_sample",
    )(x3, g, w_in, bg, lng, lnb, *mix_weights)


def _project_prompt(x, g, w_bf, bg, lng, lnb, ffn_weights):
    batch, seq, _ = x.shape
    tm = TOKEN_TILE
    tiles = seq // tm
    keeps = [min(win, seq) for win, _ in DIL_GROUPS]
    assert seq % tm == 0 and keeps[0] <= tm and keeps[1] <= tm and keeps[2] == seq
    row = lambda w: pl.BlockSpec((tm, w), lambda b, t: (b * tiles + t, 0))
    last = lambda keep: pl.BlockSpec((1, 2 * GROUP_W, keep), lambda b, t: (b, 0, 0))
    n = batch * seq
    slices, bf_shapes = _row_slices(ffn_weights, batch * tiles)
    slice_specs = [pl.BlockSpec(s, lambda b, t: (b * tiles + t, 0)) for s in slices]
    return pl.pallas_call(
        _proj_prompt_kernel,
        grid=(batch, tiles),
        in_specs=[pl.BlockSpec((1, tm, D_MODEL), lambda b, t: (b, t, 0))] + _proj_weight_specs() + slice_specs,
        out_specs=[pl.BlockSpec((1, N_GROUPS, 3 * SLABS, tm, LANES), lambda b, t: (b, 0, 0, t, 0)),
                   last(keeps[0]), last(keeps[1]),
                   pl.BlockSpec((1, 2 * GROUP_W, tm), lambda b, t: (b, 0, t)),
                   row(SG_W), row(SG_W), row(D_MODEL), row(D_MODEL)] + slice_specs,
        out_shape=[jax.ShapeDtypeStruct((batch, N_GROUPS, 3 * SLABS, seq, LANES), F32),
                   jax.ShapeDtypeStruct((batch, 2 * GROUP_W, keeps[0]), F32),
                   jax.ShapeDtypeStruct((batch, 2 * GROUP_W, keeps[1]), F32),
                   jax.ShapeDtypeStruct((batch, 2 * GROUP_W, keeps[2]), F32)] + _gate_branch_shapes(n, BF16)
        + bf_shapes,
        compiler_params=_params(2),
        name="proj_prompt",
    )(x, g, w_bf, bg, lng, lnb, *ffn_weights)


def _head_lane_mask(shape, hh):
    lane = lax.broadcasted_iota(jnp.int32, shape, len(shape) - 1)
    return (lane >> HEAD_SHIFT) == hh


def _band_block(q, k, v, mask):
    nt = (((1,), (1,)), ((), ()))
    assert HEADS_PER_SLAB == 2
    lm0 = _head_lane_mask((BLOCK, LANES), 0)
    mask2 = jnp.concatenate([mask, mask], axis=0)
    outs, lses = [], []
    for s in range(SLABS):
        zero = jnp.zeros_like(q[s])
        qm = jnp.concatenate([jnp.where(lm0, q[s], zero), jnp.where(lm0, zero, q[s])], axis=0)
        sc = lax.dot_general(qm, k[s], nt, preferred_element_type=F32)
        sc = jnp.where(mask2, sc, NEG)
        m = jnp.max(sc, axis=-1, keepdims=True)
        p = jnp.exp2(sc - m)
        l = jnp.sum(p, axis=-1, keepdims=True)
        o = jnp.dot(p.astype(BF16), v[s], preferred_element_type=F32) * (1.0 / l)
        lse = m + jnp.log2(l)
        outs.append(jnp.where(lm0, o[:BLOCK], o[BLOCK:]))
        lses.append(jnp.where(lm0, lse[:BLOCK], lse[BLOCK:]))
    return outs, lses


def _group_attention(qkv_ref, o_scr, lse_scr, g, dil, seq):
    n_qb = seq // (dil * BLOCK)
    qb_shift = n_qb.bit_length() - 1
    span = dil * BLOCK
    rows = (lambda start: pl.ds(start, BLOCK, stride=dil)) if dil > 1 else (lambda start: pl.ds(start, BLOCK))
    nk = 2 * BLOCK if n_qb > 1 else BLOCK
    qi = lax.broadcasted_iota(jnp.int32, (BLOCK, nk), 0)
    ki = lax.broadcasted_iota(jnp.int32, (BLOCK, nk), 1)

    def load(slab, start):
        return qkv_ref[0, 0, slab, rows(start), :]

    def body(n, carry):
        r = n >> qb_shift
        qb = n & (n_qb - 1)
        cur = r + span * qb
        if dil == 1:
            cur = pl.multiple_of(cur, BLOCK)
        q = [load(s, cur).astype(BF16) for s in range(SLABS)]
        k = [load(SLABS + s, cur).astype(BF16) for s in range(SLABS)]
        v = [load(2 * SLABS + s, cur).astype(BF16) for s in range(SLABS)]
        if n_qb > 1:
            prev = r + span * jnp.maximum(qb - 1, 0)
            if dil == 1:
                prev = pl.multiple_of(prev, BLOCK)
            k = [jnp.concatenate([load(SLABS + s, prev).astype(BF16), k[s]], axis=0) for s in range(SLABS)]
            v = [jnp.concatenate([load(2 * SLABS + s, prev).astype(BF16), v[s]], axis=0) for s in range(SLABS)]
            mask = (ki >= qi) & (ki <= qi + BLOCK) & (ki >= jnp.where(qb > 0, 0, BLOCK))
        else:
            mask = ki <= qi
        outs, lses = _band_block(q, k, v, mask)
        for s in range(SLABS):
            o_scr[g, s, rows(cur), :] = outs[s]
            lse_scr[g, s, rows(cur), :] = lses[s]
        return carry

    lax.fori_loop(0, dil * n_qb, body, 0, unroll=BLOCK_UNROLL)


def _prompt_attn_kernel(qkv_ref, o_ref, o_scr, lse_scr):
    g = pl.program_id(1)
    seq = o_ref.shape[1]
    for gi, (_, dil) in enumerate(DIL_GROUPS):
        @pl.when(g == gi)
        def _(gi=gi, dil=dil):
            _group_attention(qkv_ref, o_scr, lse_scr, gi, dil, seq)

    @pl.when(g == N_GROUPS - 1)
    def _():
        def merge(i, carry):
            rows = pl.ds(pl.multiple_of(i * BLOCK, BLOCK), BLOCK)
            for s in range(SLABS):
                ls = [lse_scr[gi, s, rows, :] for gi in range(N_GROUPS)]
                top = jnp.maximum(jnp.maximum(ls[0], ls[1]), ls[2])
                e = [jnp.exp2(x - top) for x in ls]
                num = e[0] * o_scr[0, s, rows, :] + e[1] * o_scr[1, s, rows, :] + e[2] * o_scr[2, s, rows, :]
                o_ref[0, rows, s * LANES:(s + 1) * LANES] = (num / (e[0] + e[1] + e[2])).astype(o_ref.dtype)
            return carry
        lax.fori_loop(0, seq // BLOCK, merge, 0, unroll=MERGE_UNROLL)


def _prompt_attention(qkv, seq):
    batch = qkv.shape[0]
    scratch = pltpu.VMEM((N_GROUPS, SLABS, seq, LANES), F32)
    return pl.pallas_call(
        _prompt_attn_kernel,
        grid=(batch, N_GROUPS),
        in_specs=[pl.BlockSpec((1, 1, 3 * SLABS, seq, LANES), lambda b, g: (b, g, 0, 0, 0))],
        out_specs=pl.BlockSpec((1, seq, GROUP_W), lambda b, g: (b, 0, 0)),
        out_shape=jax.ShapeDtypeStruct((batch, seq, GROUP_W), BF16),
        scratch_shapes=[scratch, scratch],
        compiler_params=_params(2),
        name="prompt_attn",
    )(qkv)


def _residue_allreduce(x, dil, op):
    shift = LANES // 2
    while shift >= dil:
        x = op(x, pltpu.roll(x, shift, axis=1))
        shift //= 2
    return x


def _lane_expand(rows8, dil, r0, t_new):
    assert r0 == 0 and t_new <= dil and (dil % T_PAD == 0 or T_PAD % dil == 0)
    if dil >= T_PAD:
        period = [rows8] + [jnp.zeros_like(rows8)] * (dil // T_PAD - 1)
    else:
        period = [jnp.concatenate([rows8[:dil]] * (T_PAD // dil), axis=0)]
    rows = jnp.concatenate(period * (LANES // max(dil, T_PAD)), axis=0)
    return rows.T


def _heads_to_rows(x):
    return jnp.broadcast_to(x[:, None, :], (HEADS, HEAD_DIM, LANES)).reshape(GROUP_W, LANES)


def _rows_to_heads_sum(x):
    return jnp.sum(x.reshape(HEADS, HEAD_DIM, LANES), axis=1)


def _buffer_group_lanes(c_ref, d, q8, kn8, vn8, dil, r0, t_new):
    width = c_ref.shape[2]
    tiles = width // LANES
    ql, kl, vl = (_lane_expand(a, dil, r0, t_new) for a in (q8, kn8, vn8))
    s = [_rows_to_heads_sum(c_ref[d, :GROUP_W, j * LANES:(j + 1) * LANES] * ql) for j in range(tiles)]
    s_new = _rows_to_heads_sum(kl * ql)
    top = s[0]
    for sj in s[1:]:
        top = jnp.maximum(top, sj)
    top = jnp.maximum(_residue_allreduce(top, dil, jnp.maximum), s_new)
    p = [jnp.exp(sj - top) for sj in s]
    p_new = jnp.exp(s_new - top)
    den = p[0]
    for pj in p[1:]:
        den = den + pj
    den = _residue_allreduce(den, dil, jnp.add) + p_new
    acc = c_ref[d, GROUP_W:, :LANES] * _heads_to_rows(p[0])
    for j in range(1, tiles):
        acc = acc + c_ref[d, GROUP_W:, j * LANES:(j + 1) * LANES] * _heads_to_rows(p[j])
    acc = (_residue_allreduce(acc, dil, jnp.add) + vl * _heads_to_rows(p_new)) * _heads_to_rows(1.0 / den)
    lse = _heads_to_rows(top + jnp.log(den))
    first = -(-r0 // dil) * dil
    return acc.T[first - r0:first - r0 + T_PAD], lse.T[first - r0:first - r0 + T_PAD]


def _buffer_group_dense(c_ref, d, q8, kn8, vn8, r0, t_new):
    rows = HEADS * T_PAD
    width = c_ref.shape[2]
    row = lax.broadcasted_iota(jnp.int32, (rows, GROUP_W), 0)
    lane = lax.broadcasted_iota(jnp.int32, (rows, GROUP_W), 1)
    head_sel = (row >> T_SHIFT) == (lane >> HEAD_SHIFT)
    t = (lax.broadcasted_iota(jnp.int32, (rows, width), 0) & (T_PAD - 1)) - r0
    i = lax.broadcasted_iota(jnp.int32, (rows, width), 1)
    tn = lax.broadcasted_iota(jnp.int32, (rows, T_PAD), 0) & (T_PAD - 1)
    j = lax.broadcasted_iota(jnp.int32, (rows, T_PAD), 1)
    nt = (((1,), (1,)), ((), ()))
    qm = jnp.where(head_sel, jnp.concatenate([q8] * HEADS, axis=0), 0.0).astype(BF16)
    kt = c_ref[d, :GROUP_W, :].astype(BF16)
    vt = c_ref[d, GROUP_W:, :].astype(BF16)
    sc = jnp.where(i >= t, jnp.dot(qm, kt, preferred_element_type=F32), NEG)
    own = (j <= tn) & (j >= r0) & (j < r0 + t_new)
    sn = jnp.where(own, lax.dot_general(qm, kn8.astype(BF16), nt, preferred_element_type=F32), NEG)
    m = jnp.maximum(jnp.max(sc, axis=-1, keepdims=True), jnp.max(sn, axis=-1, keepdims=True))
    pc = jnp.exp(sc - m)
    pn = jnp.exp(sn - m)
    l = jnp.sum(pc, axis=-1, keepdims=True) + jnp.sum(pn, axis=-1, keepdims=True)

    def finish():
        o = (lax.dot_general(pc.astype(BF16), vt, nt, preferred_element_type=F32)
             + jnp.dot(pn.astype(BF16), vn8.astype(BF16), preferred_element_type=F32))
        o = jnp.where(head_sel, o / l, 0.0)
        lse = jnp.where(head_sel, m + jnp.log(l), 0.0)
        fold = lambda x: sum(x[h * T_PAD:(h + 1) * T_PAD] for h in range(HEADS))
        return fold(o), fold(lse)
    return finish


def _sample_attn_start(qkv_ref, c1_ref, c2_ref, c3_ref, o_ref):
    n_db = c1_ref.shape[0]
    t_new = T_PAD // n_db
    assert qkv_ref.shape[0] == T_PAD and n_db * t_new == T_PAD
    pad_rows = jnp.zeros((T_PAD - t_new, qkv_ref.shape[1]), F32)
    pending = []
    for d in range(n_db):
        new_rows = jnp.concatenate([qkv_ref[d * t_new:(d + 1) * t_new, :], pad_rows], axis=0)
        groups = []
        for g, (c_ref, (_, dil)) in enumerate(zip((c1_ref, c2_ref, c3_ref), DIL_GROUPS)):
            cols = lambda part: slice(part * ATTN_W + g * GROUP_W, part * ATTN_W + (g + 1) * GROUP_W)
            q8 = new_rows[:, cols(0)] * SCALE
            kn8, vn8 = new_rows[:, cols(1)], new_rows[:, cols(2)]
            if dil == 1:
                groups.append(_buffer_group_dense(c_ref, d, q8, kn8, vn8, 0, t_new))
            else:
                result = _buffer_group_lanes(c_ref, d, q8, kn8, vn8, dil, 0, t_new)
                groups.append(lambda result=result: result)
        pending.append(groups)

    def finish():
        for d, groups in enumerate(pending):
            outs, lses = zip(*(group() for group in groups))
            top = jnp.maximum(jnp.maximum(lses[0], lses[1]), lses[2])
            w = [jnp.exp(x - top) for x in lses]
            o8 = (w[0] * outs[0] + w[1] * outs[1] + w[2] * outs[2]) / (w[0] + w[1] + w[2])
            o_ref[d * t_new:(d + 1) * t_new, :] = o8[:t_new]
    return finish


def _mix_value(o_a, o_b, ga_ref, gb_ref, x, wao_ref, wbo_ref, wout_ref, gpost_ref):
    a = jnp.dot(o_a.astype(BF16), wao_ref[...], preferred_element_type=F32)
    b = jnp.dot(o_b.astype(BF16), wbo_ref[...], preferred_element_type=F32)
    merged = ga_ref[...].astype(F32) * a + gb_ref[...].astype(F32) * b
    t = jnp.dot(merged.astype(BF16), wout_ref[...], preferred_element_type=F32)
    return x + _rms(t, gpost_ref[...])


def _ffn_value(x, gpre_ref, wup_ref, wdn_ref, gpost_ref):
    h = _rms(x, gpre_ref[...]).astype(BF16)
    f = jnp.zeros(x.shape, F32)
    for c in range(D_FF // FF_CHUNK):
        cs = slice(c * FF_CHUNK, (c + 1) * FF_CHUNK)
        u = jnp.maximum(jnp.dot(h, wup_ref[:, cs], preferred_element_type=F32), 0.0)
        f = f + jnp.dot((u * u).astype(BF16), wdn_ref[cs, :], preferred_element_type=F32)
    return x + _rms(f, gpost_ref[...])


def _prompt_spatial_gate(z1_ref, zn_ref, wsp_ref, bsp_ref):
    r = lax.broadcasted_iota(jnp.int32, (CHUNK, CHUNK), 0)
    c = lax.broadcasted_iota(jnp.int32, (CHUNK, CHUNK), 1)
    n_chunks = z1_ref.shape[0] // CHUNK
    mixed = []
    for g in range(SG_GROUPS):
        wt = jnp.where(r >= c, wsp_ref[g], 0.0).astype(BF16)
        zg = jnp.concatenate([zn_ref[ci * CHUNK:(ci + 1) * CHUNK, g * CHUNK:(g + 1) * CHUNK].astype(BF16)
                              for ci in range(n_chunks)], axis=1)
        mixed.append(jnp.dot(wt, zg, preferred_element_type=F32))
    chunks = [jnp.concatenate([m[:, ci * CHUNK:(ci + 1) * CHUNK] for m in mixed], axis=1) + bsp_ref[...]
              for ci in range(n_chunks)]
    return z1_ref[...].astype(F32) * jnp.concatenate(chunks, axis=0)


def _tail_prompt_kernel(oa_ref, z1_ref, zn_ref, ga_ref, gb_ref, x_ref, wsp_ref, bsp_ref,
                        wao_ref, wbo_ref, wout_ref, gmix_ref, gpre_ref, wup_ref, wdn_ref, gpost_ref,
                        qkv_ref, c1_ref, c2_ref, c3_ref, y_ref, o_ref):
    finish_sample_attn = _sample_attn_start(qkv_ref, c1_ref, c2_ref, c3_ref, o_ref)
    o_b = _prompt_spatial_gate(z1_ref, zn_ref, wsp_ref, bsp_ref)
    x1 = _mix_value(oa_ref[...], o_b, ga_ref, gb_ref, x_ref[...], wao_ref, wbo_ref, wout_ref, gmix_ref)
    y_ref[...] = _ffn_value(x1, gpre_ref, wup_ref, wdn_ref, gpost_ref)
    finish_sample_attn()


def _tail_sample_kernel(oa_ref, z1_ref, zn_ref, ga_ref, gb_ref, x_ref, coef_ref, bsp_ref,
                        wao_ref, wbo_ref, wout_ref, gmix_ref, gpre_ref, wup_ref, wdn_ref, gpost_ref, y_ref):
    zn = zn_ref[...]
    mix = coef_ref[0] * zn
    for d in range(1, coef_ref.shape[0]):
        mix = mix + coef_ref[d] * pltpu.roll(zn, d, axis=0)
    o_b = z1_ref[...] * (mix + bsp_ref[...])
    x = x_ref[...].reshape(zn.shape[0], D_MODEL)
    x1 = _mix_value(oa_ref[...], o_b, ga_ref, gb_ref, x, wao_ref, wbo_ref, wout_ref, gmix_ref)
    y_ref[...] = _ffn_value(x1, gpre_ref, wup_ref, wdn_ref, gpost_ref).reshape(y_ref.shape)


def _tail_weight_specs():
    return [_const_spec((GROUP_W, D_MODEL)), _const_spec((SG_W, D_MODEL)), _const_spec((D_MODEL, D_MODEL)),
            _const_spec((1, D_MODEL)), _const_spec((1, D_MODEL)), _const_spec((D_MODEL, D_FF)),
            _const_spec((D_FF, D_MODEL)), _const_spec((1, D_MODEL))]


def _tail_prompt(o_a, z1, zn, ga, gb, x, wsp, bsp, weights, qkv_s, caches):
    n, db, ns = x.shape[0], caches[0].shape[0], qkv_s.shape[0]
    tile = SAMPLE_DB_TILE
    steps = db // tile
    tm = n // steps
    assert db % tile == 0 and n % steps == 0 and tm % CHUNK == 0 and ns == steps * T_PAD
    row = lambda w: pl.BlockSpec((tm, w), lambda i: (i, 0))
    blk = lambda a: pl.BlockSpec((tile,) + a.shape[1:], lambda i: (i, 0, 0))
    new_rows = lambda w: pl.BlockSpec((T_PAD, w), lambda i: (i, 0))
    return pl.pallas_call(
        _tail_prompt_kernel,
        grid=(steps,),
        in_specs=[row(GROUP_W), row(SG_W), row(SG_W), row(D_MODEL), row(D_MODEL), row(D_MODEL),
                  _const_spec((SG_GROUPS, CHUNK, CHUNK)), _const_spec((CHUNK, SG_W))] + _tail_weight_specs()
        + [new_rows(Z_OFF)] + [blk(a) for a in caches],
        out_specs=[row(D_MODEL), new_rows(GROUP_W)],
        out_shape=[jax.ShapeDtypeStruct((n, D_MODEL), F32),
                   jax.ShapeDtypeStruct((ns, GROUP_W), F32)],
        compiler_params=_params(1),
        name="tail_prompt",
    )(o_a, z1, zn, ga, gb, x, wsp, bsp, *weights, qkv_s, *caches)


def _tail_sample(o_a, z1, zn, ga, gb, x3, coef, bsp, weights):
    n = x3.shape[0] * x3.shape[1]
    tm = coef.shape[1]
    row = lambda w: pl.BlockSpec((tm, w), lambda i: (i, 0))
    rows3 = pl.BlockSpec((tm // x3.shape[1],) + x3.shape[1:], lambda i: (i, 0, 0))
    return pl.pallas_call(
        _tail_sample_kernel,
        grid=(n // tm,),
        in_specs=[row(GROUP_W), row(SG_W), row(SG_W), row(D_MODEL), row(D_MODEL), rows3,
                  _const_spec(coef.shape), _const_spec(bsp.shape)] + _tail_weight_specs(),
        out_specs=rows3,
        out_shape=jax.ShapeDtypeStruct(x3.shape, F32),
        compiler_params=_params(1),
        name="tail_sample",
    )(o_a, z1, zn, ga, gb, x3, coef, bsp, *weights)


def _kv_from_feature_major(kv_t):
    batch, _, keep = kv_t.shape
    return kv_t.reshape(batch, 2, HEADS, HEAD_DIM, keep).transpose(0, 4, 1, 2, 3)[None]


def kernel(x_prompt, x_sample, cache_kv_w128, cache_kv_w512, cache_kv_w2048, norm_pre_mix, w_in, b_gate,
           ln_z_g, ln_z_b, w_spatial, b_spatial, w_ao, w_bo, w_out, norm_post_mix, norm_pre_ffn, w_up, w_down,
           norm_post_ffn):
    assert w_in.shape[0] == 1, "single-layer problem"
    batch, seq, _ = x_prompt.shape
    dbatch, t_new, _ = x_sample.shape
    caches_in = (cache_kv_w128[0], cache_kv_w512[0], cache_kv_w2048[0])

    wsp = w_spatial[0]
    bsp_rows = jnp.repeat(b_spatial[0].T, CHUNK, axis=1)

    ns = dbatch * t_new
    qkv_s, z1_s, zn_s, ga_s, gb_s, w_in_bf, wao, wbo, wout = _project_sample(
        x_sample, norm_pre_mix, w_in[0], b_gate, ln_z_g, ln_z_b, (w_ao[0], w_bo[0], w_out[0]))
    caches, kv_sample = [], []
    for gi, (win, dil) in enumerate(DIL_GROUPS):
        buf = caches_in[gi]
        wb = buf.shape[1]
        assert wb == win and t_new <= T_PAD, "full window buffers"
        caches.append(buf.transpose(0, 2, 3, 4, 1).reshape(dbatch, 2 * GROUP_W, wb))
        gs = slice(gi * GROUP_W, (gi + 1) * GROUP_W)
        kn = qkv_s[:, ATTN_W:2 * ATTN_W][:, gs].reshape(dbatch, t_new, HEADS, HEAD_DIM)
        vn = qkv_s[:, 2 * ATTN_W:][:, gs].reshape(dbatch, t_new, HEADS, HEAD_DIM)
        kv_sample.append(jnp.stack([kn, vn], axis=2)[None])

    xp = x_prompt.reshape(batch * seq, D_MODEL)
    qkv, kv1_t, kv2_t, kv3_t, z1, zn, ga, gb, wup, wdn = _project_prompt(
        x_prompt, norm_pre_mix, w_in_bf, b_gate, ln_z_g, ln_z_b, (w_up[0], w_down[0]))
    weights = (wao, wbo, wout, norm_post_mix, norm_pre_ffn, wup, wdn, norm_post_ffn)
    kv_prompt = [_kv_from_feature_major(a) for a in (kv1_t, kv2_t, kv3_t)]
    o_a_p = _prompt_attention(qkv, seq).reshape(batch * seq, GROUP_W)
    assert SAMPLE_DB_TILE * t_new == T_PAD, "a tail step takes T_PAD new rows"
    y_prompt, o_a = _tail_prompt(o_a_p, z1, zn, ga, gb, xp, wsp, bsp_rows, weights, qkv_s, caches)
    y_prompt = y_prompt.reshape(batch, seq, D_MODEL)

    wt = jnp.tril(wsp)[:, :t_new, :t_new]
    coef = jnp.stack([jnp.where(jnp.arange(t_new)[None, :] >= d,
                                wt[:, jnp.arange(t_new), jnp.maximum(jnp.arange(t_new) - d, 0)], 0.0)
                      for d in range(t_new)])
    coef = jnp.repeat(coef.transpose(0, 2, 1), CHUNK, axis=2)
    reps = SAMPLE_TAIL_TILE // t_new
    coef = jnp.tile(coef, (1, reps, 1))
    bsp_s = jnp.tile(bsp_rows[:t_new], (reps, 1))
    y_sample = _tail_sample(o_a, z1_s, zn_s, ga_s, gb_s, x_sample, coef, bsp_s, weights)
    sg_sample = zn_s.reshape(1, dbatch, t_new, SG_W)

    return (y_prompt, y_sample, kv_prompt[0], kv_prompt[1], kv_prompt[2],
            kv_sample[0], kv_sample[1], kv_sample[2], sg_sample)
```

```python
import math

import jax
import jax.numpy as jnp
from jax import lax
from jax.experimental import pallas as pl
from jax.experimental.pallas import tpu as pltpu

F32 = jnp.float32
BF16 = jnp.bfloat16

D_MODEL = 1024
HEAD_DIM = 64
HEADS = 4
GROUP_W = HEADS * HEAD_DIM
DIL_GROUPS = ((128, 1), (512, 4), (2048, 16))
N_GROUPS = len(DIL_GROUPS)
ATTN_W = N_GROUPS * GROUP_W
BLOCK = 128
CHUNK = 128
SG_GROUPS = 4
SG_W = 512
D_FF = 4 * D_MODEL
Z_OFF = 3 * ATTN_W
G_OFF = Z_OFF + 2 * SG_W
IN_W = G_OFF + 2 * D_MODEL
EPS = 1e-6
NEG = -1e30
SCALE = HEAD_DIM ** -0.5
INV_SQRT2 = 1.0 / math.sqrt(2.0)
LOG2E = 1.0 / math.log(2.0)
LANES = 128
SLABS = GROUP_W // LANES
HEADS_PER_SLAB = LANES // HEAD_DIM
HEAD_SHIFT = HEAD_DIM.bit_length() - 1
VMEM_LIMIT = 56 * 1024 * 1024

TOKEN_TILE = 512
MERGE_UNROLL = 4
BLOCK_UNROLL = 16
SAMPLE_DB_TILE = 2
T_PAD = 8
T_SHIFT = T_PAD.bit_length() - 1
BF16_SUBLANES = 16
CAST_COLS = 768
SAMPLE_TAIL_TILE = 256
FF_CHUNK = 1024


def _params(n_axes):
    return pltpu.CompilerParams(dimension_semantics=("arbitrary",) * n_axes,
                                vmem_limit_bytes=VMEM_LIMIT)


def _const_spec(shape):
    return pl.BlockSpec(shape, lambda *_: (0,) * len(shape), pipeline_mode=pl.Buffered(1))


def _rms(x, g):
    return x * lax.rsqrt(jnp.mean(x * x, axis=-1, keepdims=True) + EPS) * g


def _proj_gate_branches(h, w_ref, bg_ref, lng_ref, lnb_ref, z1_ref, zn_ref, ga_ref, gb_ref):
    z = jnp.dot(h, w_ref[:, Z_OFF:G_OFF], preferred_element_type=F32)
    z = 0.5 * z * (1.0 + lax.erf(z * INV_SQRT2))
    z1_ref[...] = z[:, :SG_W].astype(z1_ref.dtype)
    z2 = z[:, SG_W:]
    mu = jnp.mean(z2, axis=-1, keepdims=True)
    zc = z2 - mu
    var = jnp.mean(zc * zc, axis=-1, keepdims=True)
    zn_ref[...] = (zc * lax.rsqrt(var + EPS) * lng_ref[...] + lnb_ref[...]).astype(zn_ref.dtype)

    gates = jnp.dot(h, w_ref[:, G_OFF:], preferred_element_type=F32) + bg_ref[...]
    gates = 1.0 / (1.0 + jnp.exp(-gates))
    ga_ref[...] = gates[:, :D_MODEL].astype(BF16)
    gb_ref[...] = gates[:, D_MODEL:].astype(BF16)


def _cast_blocks(pairs):
    for src_ref, dst_ref in pairs:
        dst_ref[...] = src_ref[...].astype(dst_ref.dtype)


def _proj_sample_kernel(x_ref, g_ref, wf_ref, bg_ref, lng_ref, lnb_ref, wao_ref, wbo_ref, wout_ref,
                        qkv_ref, z1_ref, zn_ref, ga_ref, gb_ref, w_ref, wao_bf_ref, wbo_bf_ref, wout_bf_ref):
    @pl.when(pl.program_id(0) == 0)
    def _():
        for c in range(0, IN_W, CAST_COLS):
            w_ref[:, c:c + CAST_COLS] = wf_ref[:, c:c + CAST_COLS].astype(BF16)

    _cast_blocks(((wao_ref, wao_bf_ref), (wbo_ref, wbo_bf_ref), (wout_ref, wout_bf_ref)))
    x = x_ref[...]
    h = _rms(x.reshape(x.shape[0] * x.shape[1], x.shape[2]), g_ref[...]).astype(BF16)
    qkv_ref[...] = jnp.dot(h, w_ref[:, :Z_OFF], preferred_element_type=F32)
    _proj_gate_branches(h, w_ref, bg_ref, lng_ref, lnb_ref, z1_ref, zn_ref, ga_ref, gb_ref)


def _proj_prompt_kernel(x_ref, g_ref, w_ref, bg_ref, lng_ref, lnb_ref, wup_ref, wdn_ref,
                        qkv_ref, kv1_ref, kv2_ref, kv3_ref, z1_ref, zn_ref, ga_ref, gb_ref, wup_bf_ref, wdn_bf_ref):
    _cast_blocks(((wup_ref, wup_bf_ref), (wdn_ref, wdn_bf_ref)))
    h = _rms(x_ref[0], g_ref[...]).astype(BF16)
    _proj_gate_branches(h, w_ref, bg_ref, lng_ref, lnb_ref, z1_ref, zn_ref, ga_ref, gb_ref)
    def project(first_part, n_parts, scale=None):
        cols = jnp.dot(h, w_ref[:, first_part * ATTN_W:(first_part + n_parts) * ATTN_W], preferred_element_type=F32)
        if scale is not None:
            cols = cols * scale
        for part in range(n_parts):
            for g in range(N_GROUPS):
                for s in range(SLABS):
                    col = part * ATTN_W + g * GROUP_W + s * LANES
                    qkv_ref[0, g, (first_part + part) * SLABS + s] = cols[:, col:col + LANES]

    def kv_t(g):
        return jnp.concatenate([qkv_ref[0, g, slab].T for slab in range(SLABS, 3 * SLABS)], axis=0)

    tm = x_ref.shape[1]
    project(1, 2)
    kv3_ref[0] = kv_t(2)
    project(0, 1, scale=SCALE * LOG2E)

    @pl.when(pl.program_id(1) == pl.num_programs(1) - 1)
    def _():
        kv2_ref[0] = kv_t(1)[:, tm - kv2_ref.shape[2]:]
        kv1_ref[0] = kv_t(0)[:, tm - kv1_ref.shape[2]:]


def _proj_weight_specs():
    return [_const_spec((1, D_MODEL)), _const_spec((D_MODEL, IN_W)),
            _const_spec((1, 2 * D_MODEL)), _const_spec((1, SG_W)), _const_spec((1, SG_W))]


def _gate_branch_shapes(n, z_dtype):
    return [jax.ShapeDtypeStruct((n, SG_W), z_dtype), jax.ShapeDtypeStruct((n, SG_W), z_dtype),
            jax.ShapeDtypeStruct((n, D_MODEL), BF16), jax.ShapeDtypeStruct((n, D_MODEL), BF16)]


def _row_slices(arrays, steps):
    specs, shapes = [], []
    for a in arrays:
        rows = a.shape[0] // steps
        assert rows * steps == a.shape[0] and rows % BF16_SUBLANES == 0
        specs.append((rows, a.shape[1]))
        shapes.append(jax.ShapeDtypeStruct(a.shape, BF16))
    return specs, shapes


def _project_sample(x3, g, w_in, bg, lng, lnb, mix_weights):
    n = x3.shape[0] * x3.shape[1]
    tm = min(SAMPLE_TAIL_TILE, n)
    steps = n // tm
    row = lambda w: pl.BlockSpec((tm, w), lambda i: (i, 0))
    slices, bf_shapes = _row_slices(mix_weights, steps)
    slice_specs = [pl.BlockSpec(s, lambda i: (i, 0)) for s in slices]
    return pl.pallas_call(
        _proj_sample_kernel,
        grid=(steps,),
        in_specs=[pl.BlockSpec((tm // x3.shape[1],) + x3.shape[1:], lambda i: (i, 0, 0))] + _proj_weight_specs()
        + slice_specs,
        out_specs=[row(Z_OFF), row(SG_W), row(SG_W), row(D_MODEL), row(D_MODEL),
                   pl.BlockSpec((D_MODEL, IN_W), lambda i: (0, 0))] + slice_specs,
        out_shape=[jax.ShapeDtypeStruct((n, Z_OFF), F32)] + _gate_branch_shapes(n, F32)
        + [jax.ShapeDtypeStruct((D_MODEL, IN_W), BF16)] + bf_shapes,
        compiler_params=_params(1),
        name="proj_sample",
    )(x3, g, w_in, bg, lng, lnb, *mix_weights)


def _project_prompt(x, g, w_bf, bg, lng, lnb, ffn_weights):
    batch, seq, _ = x.shape
    tm = TOKEN_TILE
    tiles = seq // tm
    keeps = [min(win, seq) for win, _ in DIL_GROUPS]
    assert seq % tm == 0 and keeps[0] <= tm and keeps[1] <= tm and keeps[2] == seq
    row = lambda w: pl.BlockSpec((tm, w), lambda b, t: (b * tiles + t, 0))
    last = lambda keep: pl.BlockSpec((1, 2 * GROUP_W, keep), lambda b, t: (b, 0, 0))
    n = batch * seq
    slices, bf_shapes = _row_slices(ffn_weights, batch * tiles)
    slice_specs = [pl.BlockSpec(s, lambda b, t: (b * tiles + t, 0)) for s in slices]
    return pl.pallas_call(
        _proj_prompt_kernel,
        grid=(batch, tiles),
        in_specs=[pl.BlockSpec((1, tm, D_MODEL), lambda b, t: (b, t, 0))] + _proj_weight_specs() + slice_specs,
        out_specs=[pl.BlockSpec((1, N_GROUPS, 3 * SLABS, tm, LANES), lambda b, t: (b, 0, 0, t, 0)),
                   last(keeps[0]), last(keeps[1]),
                   pl.BlockSpec((1, 2 * GROUP_W, tm), lambda b, t: (b, 0, t)),
                   row(SG_W), row(SG_W), row(D_MODEL), row(D_MODEL)] + slice_specs,
        out_shape=[jax.ShapeDtypeStruct((batch, N_GROUPS, 3 * SLABS, seq, LANES), F32),
                   jax.ShapeDtypeStruct((batch, 2 * GROUP_W, keeps[0]), F32),
                   jax.ShapeDtypeStruct((batch, 2 * GROUP_W, keeps[1]), F32),
                   jax.ShapeDtypeStruct((batch, 2 * GROUP_W, keeps[2]), F32)] + _gate_branch_shapes(n, BF16)
        + bf_shapes,
        compiler_params=_params(2),
        name="proj_prompt",
    )(x, g, w_bf, bg, lng, lnb, *ffn_weights)


def _head_lane_mask(shape, hh):
    lane = lax.broadcasted_iota(jnp.int32, shape, len(shape) - 1)
    return (lane >> HEAD_SHIFT) == hh


def _band_block(q, k, v, mask):
    nt = (((1,), (1,)), ((), ()))
    assert HEADS_PER_SLAB == 2
    lm0 = _head_lane_mask((BLOCK, LANES), 0)
    mask2 = jnp.concatenate([mask, mask], axis=0)
    outs, lses = [], []
    for s in range(SLABS):
        zero = jnp.zeros_like(q[s])
        qm = jnp.concatenate([jnp.where(lm0, q[s], zero), jnp.where(lm0, zero, q[s])], axis=0)
        sc = lax.dot_general(qm, k[s], nt, preferred_element_type=F32)
        sc = jnp.where(mask2, sc, NEG)
        m = jnp.max(sc, axis=-1, keepdims=True)
        p = jnp.exp2(sc - m)
        l = jnp.sum(p, axis=-1, keepdims=True)
        o = jnp.dot(p.astype(BF16), v[s], preferred_element_type=F32) * (1.0 / l)
        lse = m + jnp.log2(l)
        outs.append(jnp.where(lm0, o[:BLOCK], o[BLOCK:]))
        lses.append(jnp.where(lm0, lse[:BLOCK], lse[BLOCK:]))
    return outs, lses


def _group_attention(qkv_ref, o_scr, lse_scr, g, dil, seq):
    n_qb = seq // (dil * BLOCK)
    qb_shift = n_qb.bit_length() - 1
    span = dil * BLOCK
    rows = (lambda start: pl.ds(start, BLOCK, stride=dil)) if dil > 1 else (lambda start: pl.ds(start, BLOCK))
    nk = 2 * BLOCK if n_qb > 1 else BLOCK
    qi = lax.broadcasted_iota(jnp.int32, (BLOCK, nk), 0)
    ki = lax.broadcasted_iota(jnp.int32, (BLOCK, nk), 1)

    def load(slab, start):
        return qkv_ref[0, 0, slab, rows(start), :]

    def body(n, carry):
        r = n >> qb_shift
        qb = n & (n_qb - 1)
        cur = r + span * qb
        if dil == 1:
            cur = pl.multiple_of(cur, BLOCK)
        q = [load(s, cur).astype(BF16) for s in range(SLABS)]
        k = [load(SLABS + s, cur).astype(BF16) for s in range(SLABS)]
        v = [load(2 * SLABS + s, cur).astype(BF16) for s in range(SLABS)]
        if n_qb > 1:
            prev = r + span * jnp.maximum(qb - 1, 0)
            if dil == 1:
                prev = pl.multiple_of(prev, BLOCK)
            k = [jnp.concatenate([load(SLABS + s, prev).astype(BF16), k[s]], axis=0) for s in range(SLABS)]
            v = [jnp.concatenate([load(2 * SLABS + s, prev).astype(BF16), v[s]], axis=0) for s in range(SLABS)]
            mask = (ki >= qi) & (ki <= qi + BLOCK) & (ki >= jnp.where(qb > 0, 0, BLOCK))
        else:
            mask = ki <= qi
        outs, lses = _band_block(q, k, v, mask)
        for s in range(SLABS):
            o_scr[g, s, rows(cur), :] = outs[s]
            lse_scr[g, s, rows(cur), :] = lses[s]
        return carry

    lax.fori_loop(0, dil * n_qb, body, 0, unroll=BLOCK_UNROLL)


def _prompt_attn_kernel(qkv_ref, o_ref, o_scr, lse_scr):
    g = pl.program_id(1)
    seq = o_ref.shape[1]
    for gi, (_, dil) in enumerate(DIL_GROUPS):
        @pl.when(g == gi)
        def _(gi=gi, dil=dil):
            _group_attention(qkv_ref, o_scr, lse_scr, gi, dil, seq)

    @pl.when(g == N_GROUPS - 1)
    def _():
        def merge(i, carry):
            rows = pl.ds(pl.multiple_of(i * BLOCK, BLOCK), BLOCK)
            for s in range(SLABS):
                ls = [lse_scr[gi, s, rows, :] for gi in range(N_GROUPS)]
                top = jnp.maximum(jnp.maximum(ls[0], ls[1]), ls[2])
                e = [jnp.exp2(x - top) for x in ls]
                num = e[0] * o_scr[0, s, rows, :] + e[1] * o_scr[1, s, rows, :] + e[2] * o_scr[2, s, rows, :]
                o_ref[0, rows, s * LANES:(s + 1) * LANES] = (num / (e[0] + e[1] + e[2])).astype(o_ref.dtype)
            return carry
        lax.fori_loop(0, seq // BLOCK, merge, 0, unroll=MERGE_UNROLL)


def _prompt_attention(qkv, seq):
    batch = qkv.shape[0]
    scratch = pltpu.VMEM((N_GROUPS, SLABS, seq, LANES), F32)
    return pl.pallas_call(
        _prompt_attn_kernel,
        grid=(batch, N_GROUPS),
        in_specs=[pl.BlockSpec((1, 1, 3 * SLABS, seq, LANES), lambda b, g: (b, g, 0, 0, 0))],
        out_specs=pl.BlockSpec((1, seq, GROUP_W), lambda b, g: (b, 0, 0)),
        out_shape=jax.ShapeDtypeStruct((batch, seq, GROUP_W), BF16),
        scratch_shapes=[scratch, scratch],
        compiler_params=_params(2),
        name="prompt_attn",
    )(qkv)


def _residue_allreduce(x, dil, op):
    shift = LANES // 2
    while shift >= dil:
        x = op(x, pltpu.roll(x, shift, axis=1))
        shift //= 2
    return x


def _lane_expand(rows8, dil, r0, t_new):
    assert r0 == 0 and t_new <= dil and (dil % T_PAD == 0 or T_PAD % dil == 0)
    if dil >= T_PAD:
        period = [rows8] + [jnp.zeros_like(rows8)] * (dil // T_PAD - 1)
    else:
        period = [jnp.concatenate([rows8[:dil]] * (T_PAD // dil), axis=0)]
    rows = jnp.concatenate(period * (LANES // max(dil, T_PAD)), axis=0)
    return rows.T


def _heads_to_rows(x):
    return jnp.broadcast_to(x[:, None, :], (HEADS, HEAD_DIM, LANES)).reshape(GROUP_W, LANES)


def _rows_to_heads_sum(x):
    return jnp.sum(x.reshape(HEADS, HEAD_DIM, LANES), axis=1)


def _buffer_group_lanes(c_ref, d, q8, kn8, vn8, dil, r0, t_new):
    width = c_ref.shape[2]
    tiles = width // LANES
    ql, kl, vl = (_lane_expand(a, dil, r0, t_new) for a in (q8, kn8, vn8))
    s = [_rows_to_heads_sum(c_ref[d, :GROUP_W, j * LANES:(j + 1) * LANES] * ql) for j in range(tiles)]
    s_new = _rows_to_heads_sum(kl * ql)
    top = s[0]
    for sj in s[1:]:
        top = jnp.maximum(top, sj)
    top = jnp.maximum(_residue_allreduce(top, dil, jnp.maximum), s_new)
    p = [jnp.exp(sj - top) for sj in s]
    p_new = jnp.exp(s_new - top)
    den = p[0]
    for pj in p[1:]:
        den = den + pj
    den = _residue_allreduce(den, dil, jnp.add) + p_new
    acc = c_ref[d, GROUP_W:, :LANES] * _heads_to_rows(p[0])
    for j in range(1, tiles):
        acc = acc + c_ref[d, GROUP_W:, j * LANES:(j + 1) * LANES] * _heads_to_rows(p[j])
    acc = (_residue_allreduce(acc, dil, jnp.add) + vl * _heads_to_rows(p_new)) * _heads_to_rows(1.0 / den)
    lse = _heads_to_rows(top + jnp.log(den))
    first = -(-r0 // dil) * dil
    return acc.T[first - r0:first - r0 + T_PAD], lse.T[first - r0:first - r0 + T_PAD]


def _buffer_group_dense(c_ref, d, q8, kn8, vn8, r0, t_new):
    rows = HEADS * T_PAD
    width = c_ref.shape[2]
    row = lax.broadcasted_iota(jnp.int32, (rows, GROUP_W), 0)
    lane = lax.broadcasted_iota(jnp.int32, (rows, GROUP_W), 1)
    head_sel = (row >> T_SHIFT) == (lane >> HEAD_SHIFT)
    t = (lax.broadcasted_iota(jnp.int32, (rows, width), 0) & (T_PAD - 1)) - r0
    i = lax.broadcasted_iota(jnp.int32, (rows, width), 1)
    tn = lax.broadcasted_iota(jnp.int32, (rows, T_PAD), 0) & (T_PAD - 1)
    j = lax.broadcasted_iota(jnp.int32, (rows, T_PAD), 1)
    nt = (((1,), (1,)), ((), ()))
    qm = jnp.where(head_sel, jnp.concatenate([q8] * HEADS, axis=0), 0.0).astype(BF16)
    kt = c_ref[d, :GROUP_W, :].astype(BF16)
    vt = c_ref[d, GROUP_W:, :].astype(BF16)
    sc = jnp.where(i >= t, jnp.dot(qm, kt, preferred_element_type=F32), NEG)
    own = (j <= tn) & (j >= r0) & (j < r0 + t_new)
    sn = jnp.where(own, lax.dot_general(qm, kn8.astype(BF16), nt, preferred_element_type=F32), NEG)
    m = jnp.maximum(jnp.max(sc, axis=-1, keepdims=True), jnp.max(sn, axis=-1, keepdims=True))
    pc = jnp.exp(sc - m)
    pn = jnp.exp(sn - m)
    l = jnp.sum(pc, axis=-1, keepdims=True) + jnp.sum(pn, axis=-1, keepdims=True)

    def finish():
        o = (lax.dot_general(pc.astype(BF16), vt, nt, preferred_element_type=F32)
             + jnp.dot(pn.astype(BF16), vn8.astype(BF16), preferred_element_type=F32))
        o = jnp.where(head_sel, o / l, 0.0)
        lse = jnp.where(head_sel, m + jnp.log(l), 0.0)
        fold = lambda x: sum(x[h * T_PAD:(h + 1) * T_PAD] for h in range(HEADS))
        return fold(o), fold(lse)
    return finish


def _sample_attn_start(qkv_ref, c1_ref, c2_ref, c3_ref, o_ref):
    n_db = c1_ref.shape[0]
    t_new = T_PAD // n_db
    assert qkv_ref.shape[0] == T_PAD and n_db * t_new == T_PAD
    pad_rows = jnp.zeros((T_PAD - t_new, qkv_ref.shape[1]), F32)
    pending = []
    for d in range(n_db):
        new_rows = jnp.concatenate([qkv_ref[d * t_new:(d + 1) * t_new, :], pad_rows], axis=0)
        groups = []
        for g, (c_ref, (_, dil)) in enumerate(zip((c1_ref, c2_ref, c3_ref), DIL_GROUPS)):
            cols = lambda part: slice(part * ATTN_W + g * GROUP_W, part * ATTN_W + (g + 1) * GROUP_W)
            q8 = new_rows[:, cols(0)] * SCALE
            kn8, vn8 = new_rows[:, cols(1)], new_rows[:, cols(2)]
            if dil == 1:
                groups.append(_buffer_group_dense(c_ref, d, q8, kn8, vn8, 0, t_new))
            else:
                result = _buffer_group_lanes(c_ref, d, q8, kn8, vn8, dil, 0, t_new)
                groups.append(lambda result=result: result)
        pending.append(groups)

    def finish():
        for d, groups in enumerate(pending):
            outs, lses = zip(*(group() for group in groups))
            top = jnp.maximum(jnp.maximum(lses[0], lses[1]), lses[2])
            w = [jnp.exp(x - top) for x in lses]
            o8 = (w[0] * outs[0] + w[1] * outs[1] + w[2] * outs[2]) / (w[0] + w[1] + w[2])
            o_ref[d * t_new:(d + 1) * t_new, :] = o8[:t_new]
    return finish


def _mix_value(o_a, o_b, ga_ref, gb_ref, x, wao_ref, wbo_ref, wout_ref, gpost_ref):
    a = jnp.dot(o_a.astype(BF16), wao_ref[...], preferred_element_type=F32)
    b = jnp.dot(o_b.astype(BF16), wbo_ref[...], preferred_element_type=F32)
    merged = ga_ref[...].astype(F32) * a + gb_ref[...].astype(F32) * b
    t = jnp.dot(merged.astype(BF16), wout_ref[...], preferred_element_type=F32)
    return x + _rms(t, gpost_ref[...])


def _ffn_value(x, gpre_ref, wup_ref, wdn_ref, gpost_ref):
    h = _rms(x, gpre_ref[...]).astype(BF16)
    f = jnp.zeros(x.shape, F32)
    for c in range(D_FF // FF_CHUNK):
        cs = slice(c * FF_CHUNK, (c + 1) * FF_CHUNK)
        u = jnp.maximum(jnp.dot(h, wup_ref[:, cs], preferred_element_type=F32), 0.0)
        f = f + jnp.dot((u * u).astype(BF16), wdn_ref[cs, :], preferred_element_type=F32)
    return x + _rms(f, gpost_ref[...])


def _prompt_spatial_gate(z1_ref, zn_ref, wsp_ref, bsp_ref):
    r = lax.broadcasted_iota(jnp.int32, (CHUNK, CHUNK), 0)
    c = lax.broadcasted_iota(jnp.int32, (CHUNK, CHUNK), 1)
    n_chunks = z1_ref.shape[0] // CHUNK
    mixed = []
    for g in range(SG_GROUPS):
        wt = jnp.where(r >= c, wsp_ref[g], 0.0).astype(BF16)
        zg = jnp.concatenate([zn_ref[ci * CHUNK:(ci + 1) * CHUNK, g * CHUNK:(g + 1) * CHUNK].astype(BF16)
                              for ci in range(n_chunks)], axis=1)
        mixed.append(jnp.dot(wt, zg, preferred_element_type=F32))
    chunks = [jnp.concatenate([m[:, ci * CHUNK:(ci + 1) * CHUNK] for m in mixed], axis=1) + bsp_ref[...]
              for ci in range(n_chunks)]
    return z1_ref[...].astype(F32) * jnp.concatenate(chunks, axis=0)


def _tail_prompt_kernel(oa_ref, z1_ref, zn_ref, ga_ref, gb_ref, x_ref, wsp_ref, bsp_ref,
                        wao_ref, wbo_ref, wout_ref, gmix_ref, gpre_ref, wup_ref, wdn_ref, gpost_ref,
                        qkv_ref, c1_ref, c2_ref, c3_ref, y_ref, o_ref):
    o_b = _prompt_spatial_gate(z1_ref, zn_ref, wsp_ref, bsp_ref)
    x1 = _mix_value(oa_ref[...], o_b, ga_ref, gb_ref, x_ref[...], wao_ref, wbo_ref, wout_ref, gmix_ref)
    finish_sample_attn = _sample_attn_start(qkv_ref, c1_ref, c2_ref, c3_ref, o_ref)
    y_ref[...] = _ffn_value(x1, gpre_ref, wup_ref, wdn_ref, gpost_ref)
    finish_sample_attn()


def _tail_sample_kernel(oa_ref, z1_ref, zn_ref, ga_ref, gb_ref, x_ref, coef_ref, bsp_ref,
                        wao_ref, wbo_ref, wout_ref, gmix_ref, gpre_ref, wup_ref, wdn_ref, gpost_ref, y_ref,
                        x1_scr, h_scr, f_scr):
    c = pl.program_id(0)

    @pl.when(c == 0)
    def _():
        zn = zn_ref[...]
        mix = coef_ref[0] * zn
        for d in range(1, coef_ref.shape[0]):
            mix = mix + coef_ref[d] * pltpu.roll(zn, d, axis=0)
        o_b = z1_ref[...] * (mix + bsp_ref[...])
        x = x_ref[...].reshape(zn.shape[0], D_MODEL)
        x1 = _mix_value(oa_ref[...], o_b, ga_ref, gb_ref, x, wao_ref, wbo_ref, wout_ref, gmix_ref)
        x1_scr[...] = x1
        h_scr[...] = _rms(x1, gpre_ref[...]).astype(BF16)
        f_scr[...] = jnp.zeros_like(f_scr)

    u = jnp.maximum(jnp.dot(h_scr[...], wup_ref[...], preferred_element_type=F32), 0.0)
    f_scr[...] += jnp.dot((u * u).astype(BF16), wdn_ref[...], preferred_element_type=F32)

    @pl.when(c == pl.num_programs(0) - 1)
    def _():
        y_ref[...] = (x1_scr[...] + _rms(f_scr[...], gpost_ref[...])).reshape(y_ref.shape)


def _tail_weight_specs():
    return [_const_spec((GROUP_W, D_MODEL)), _const_spec((SG_W, D_MODEL)), _const_spec((D_MODEL, D_MODEL)),
            _const_spec((1, D_MODEL)), _const_spec((1, D_MODEL)), _const_spec((D_MODEL, D_FF)),
            _const_spec((D_FF, D_MODEL)), _const_spec((1, D_MODEL))]


def _tail_prompt(o_a, z1, zn, ga, gb, x, wsp, bsp, weights, qkv_s, caches):
    n, db, ns = x.shape[0], caches[0].shape[0], qkv_s.shape[0]
    tile = SAMPLE_DB_TILE
    steps = db // tile
    tm = n // steps
    assert db % tile == 0 and n % steps == 0 and tm % CHUNK == 0 and ns == steps * T_PAD
    row = lambda w: pl.BlockSpec((tm, w), lambda i: (i, 0))
    blk = lambda a: pl.BlockSpec((tile,) + a.shape[1:], lambda i: (i, 0, 0))
    new_rows = lambda w: pl.BlockSpec((T_PAD, w), lambda i: (i, 0))
    return pl.pallas_call(
        _tail_prompt_kernel,
        grid=(steps,),
        in_specs=[row(GROUP_W), row(SG_W), row(SG_W), row(D_MODEL), row(D_MODEL), row(D_MODEL),
                  _const_spec((SG_GROUPS, CHUNK, CHUNK)), _const_spec((CHUNK, SG_W))] + _tail_weight_specs()
        + [new_rows(Z_OFF)] + [blk(a) for a in caches],
        out_specs=[row(D_MODEL), new_rows(GROUP_W)],
        out_shape=[jax.ShapeDtypeStruct((n, D_MODEL), F32),
                   jax.ShapeDtypeStruct((ns, GROUP_W), F32)],
        compiler_params=_params(1),
        name="tail_prompt",
    )(o_a, z1, zn, ga, gb, x, wsp, bsp, *weights, qkv_s, *caches)


def _tail_sample(o_a, z1, zn, ga, gb, x3, coef, bsp, weights):
    n = x3.shape[0] * x3.shape[1]
    wao, wbo, wout, gmix, gpre, wup, wdn, gpost = weights
    full = lambda a: _const_spec(a.shape)
    return pl.pallas_call(
        _tail_sample_kernel,
        grid=(D_FF // FF_CHUNK,),
        in_specs=[full(o_a), full(z1), full(zn), full(ga), full(gb), full(x3), full(coef), full(bsp),
                  full(wao), full(wbo), full(wout), full(gmix), full(gpre),
                  pl.BlockSpec((D_MODEL, FF_CHUNK), lambda c: (0, c)),
                  pl.BlockSpec((FF_CHUNK, D_MODEL), lambda c: (c, 0)), full(gpost)],
        out_specs=pl.BlockSpec(x3.shape, lambda c: (0, 0, 0)),
        out_shape=jax.ShapeDtypeStruct(x3.shape, F32),
        scratch_shapes=[pltpu.VMEM((n, D_MODEL), F32), pltpu.VMEM((n, D_MODEL), BF16), pltpu.VMEM((n, D_MODEL), F32)],
        compiler_params=_params(1),
        name="tail_sample",
    )(o_a, z1, zn, ga, gb, x3, coef, bsp, *weights)


def _kv_from_feature_major(kv_t):
    batch, _, keep = kv_t.shape
    return kv_t.reshape(batch, 2, HEADS, HEAD_DIM, keep).transpose(0, 4, 1, 2, 3)[None]


def kernel(x_prompt, x_sample, cache_kv_w128, cache_kv_w512, cache_kv_w2048, norm_pre_mix, w_in, b_gate,
           ln_z_g, ln_z_b, w_spatial, b_spatial, w_ao, w_bo, w_out, norm_post_mix, norm_pre_ffn, w_up, w_down,
           norm_post_ffn):
    assert w_in.shape[0] == 1, "single-layer problem"
    batch, seq, _ = x_prompt.shape
    dbatch, t_new, _ = x_sample.shape
    caches_in = (cache_kv_w128[0], cache_kv_w512[0], cache_kv_w2048[0])

    wsp = w_spatial[0]
    bsp_rows = jnp.repeat(b_spatial[0].T, CHUNK, axis=1)

    ns = dbatch * t_new
    qkv_s, z1_s, zn_s, ga_s, gb_s, w_in_bf, wao, wbo, wout = _project_sample(
        x_sample, norm_pre_mix, w_in[0], b_gate, ln_z_g, ln_z_b, (w_ao[0], w_bo[0], w_out[0]))
    caches, kv_sample = [], []
    for gi, (win, dil) in enumerate(DIL_GROUPS):
        buf = caches_in[gi]
        wb = buf.shape[1]
        assert wb == win and t_new <= T_PAD, "full window buffers"
        caches.append(buf.transpose(0, 2, 3, 4, 1).reshape(dbatch, 2 * GROUP_W, wb))
        gs = slice(gi * GROUP_W, (gi + 1) * GROUP_W)
        kn = qkv_s[:, ATTN_W:2 * ATTN_W][:, gs].reshape(dbatch, t_new, HEADS, HEAD_DIM)
        vn = qkv_s[:, 2 * ATTN_W:][:, gs].reshape(dbatch, t_new, HEADS, HEAD_DIM)
        kv_sample.append(jnp.stack([kn, vn], axis=2)[None])

    xp = x_prompt.reshape(batch * seq, D_MODEL)
    qkv, kv1_t, kv2_t, kv3_t, z1, zn, ga, gb, wup, wdn = _project_prompt(
        x_prompt, norm_pre_mix, w_in_bf, b_gate, ln_z_g, ln_z_b, (w_up[0], w_down[0]))
    weights = (wao, wbo, wout, norm_post_mix, norm_pre_ffn, wup, wdn, norm_post_ffn)
    kv_prompt = [_kv_from_feature_major(a) for a in (kv1_t, kv2_t, kv3_t)]
    o_a_p = _prompt_attention(qkv, seq).reshape(batch * seq, GROUP_W)
    assert SAMPLE_DB_TILE * t_new == T_PAD, "a tail step takes T_PAD new rows"
    y_prompt, o_a = _tail_prompt(o_a_p, z1, zn, ga, gb, xp, wsp, bsp_rows, weights, qkv_s, caches)
    y_prompt = y_prompt.reshape(batch, seq, D_MODEL)

    wt = jnp.tril(wsp)[:, :t_new, :t_new]
    coef = jnp.stack([jnp.where(jnp.arange(t_new)[None, :] >= d,
                                wt[:, jnp.arange(t_new), jnp.maximum(jnp.arange(t_new) - d, 0)], 0.0)
                      for d in range(t_new)])
    coef = jnp.repeat(coef.transpose(0, 2, 1), CHUNK, axis=2)
    reps = ns // t_new
    coef = jnp.tile(coef, (1, reps, 1))
    bsp_s = jnp.tile(bsp_rows[:t_new], (reps, 1))
    y_sample = _tail_sample(o_a, z1_s, zn_s, ga_s, gb_s, x_sample, coef, bsp_s, weights)
    sg_sample = zn_s.reshape(1, dbatch, t_new, SG_W)

    return (y_prompt, y_sample, kv_prompt[0], kv_prompt[1], kv_prompt[2],
            kv_sample[0], kv_sample[1], kv_sample[2], sg_sample)
```

```python
import math

import jax
import jax.numpy as jnp
from jax import lax
from jax.experimental import pallas as pl
from jax.experimental.pallas import tpu as pltpu

F32 = jnp.float32
BF16 = jnp.bfloat16

D_MODEL = 1024
HEAD_DIM = 64
HEADS = 4
GROUP_W = HEADS * HEAD_DIM
DIL_GROUPS = ((128, 1), (512, 4), (2048, 16))
N_GROUPS = len(DIL_GROUPS)
ATTN_W = N_GROUPS * GROUP_W
BLOCK = 128
CHUNK = 128
SG_GROUPS = 4
SG_W = 512
D_FF = 4 * D_MODEL
Z_OFF = 3 * ATTN_W
G_OFF = Z_OFF + 2 * SG_W
IN_W = G_OFF + 2 * D_MODEL
EPS = 1e-6
NEG = -1e30
SCALE = HEAD_DIM ** -0.5
INV_SQRT2 = 1.0 / math.sqrt(2.0)
LOG2E = 1.0 / math.log(2.0)
LANES = 128
SLABS = GROUP_W // LANES
HEADS_PER_SLAB = LANES // HEAD_DIM
HEAD_SHIFT = HEAD_DIM.bit_length() - 1
VMEM_LIMIT = 56 * 1024 * 1024

TOKEN_TILE = 512
MERGE_UNROLL = 4
BLOCK_UNROLL = 16
SAMPLE_DB_TILE = 2
T_PAD = 8
T_SHIFT = T_PAD.bit_length() - 1
BF16_SUBLANES = 16
CAST_COLS = 768
SAMPLE_TAIL_TILE = 256
FF_CHUNK = 1024


def _params(n_axes):
    return pltpu.CompilerParams(dimension_semantics=("arbitrary",) * n_axes,
                                vmem_limit_bytes=VMEM_LIMIT)


def _const_spec(shape):
    return pl.BlockSpec(shape, lambda *_: (0,) * len(shape), pipeline_mode=pl.Buffered(1))


def _rms(x, g):
    return x * lax.rsqrt(jnp.mean(x * x, axis=-1, keepdims=True) + EPS) * g


def _proj_gate_branches(h, w_ref, bg_ref, lng_ref, lnb_ref, z1_ref, zn_ref, ga_ref, gb_ref):
    z = jnp.dot(h, w_ref[:, Z_OFF:G_OFF], preferred_element_type=F32)
    z = 0.5 * z * (1.0 + lax.erf(z * INV_SQRT2))
    z1_ref[...] = z[:, :SG_W].astype(z1_ref.dtype)
    z2 = z[:, SG_W:]
    mu = jnp.mean(z2, axis=-1, keepdims=True)
    zc = z2 - mu
    var = jnp.mean(zc * zc, axis=-1, keepdims=True)
    zn_ref[...] = (zc * lax.rsqrt(var + EPS) * lng_ref[...] + lnb_ref[...]).astype(zn_ref.dtype)

    gates = jnp.dot(h, w_ref[:, G_OFF:], preferred_element_type=F32) + bg_ref[...]
    gates = 1.0 / (1.0 + jnp.exp(-gates))
    ga_ref[...] = gates[:, :D_MODEL].astype(BF16)
    gb_ref[...] = gates[:, D_MODEL:].astype(BF16)


def _cast_blocks(pairs):
    for src_ref, dst_ref in pairs:
        dst_ref[...] = src_ref[...].astype(dst_ref.dtype)


def _proj_sample_kernel(x_ref, g_ref, wf_ref, bg_ref, lng_ref, lnb_ref, wao_ref, wbo_ref, wout_ref,
                        qkv_ref, z1_ref, zn_ref, ga_ref, gb_ref, w_ref, wao_bf_ref, wbo_bf_ref, wout_bf_ref):
    @pl.when(pl.program_id(0) == 0)
    def _():
        for c in range(0, IN_W, CAST_COLS):
            w_ref[:, c:c + CAST_COLS] = wf_ref[:, c:c + CAST_COLS].astype(BF16)

    _cast_blocks(((wao_ref, wao_bf_ref), (wbo_ref, wbo_bf_ref), (wout_ref, wout_bf_ref)))
    x = x_ref[...]
    h = _rms(x.reshape(x.shape[0] * x.shape[1], x.shape[2]), g_ref[...]).astype(BF16)
    qkv_ref[...] = jnp.dot(h, w_ref[:, :Z_OFF], preferred_element_type=F32)
    _proj_gate_branches(h, w_ref, bg_ref, lng_ref, lnb_ref, z1_ref, zn_ref, ga_ref, gb_ref)


def _proj_prompt_kernel(x_ref, g_ref, w_ref, bg_ref, lng_ref, lnb_ref, wup_ref, wdn_ref,
                        qkv_ref, kv1_ref, kv2_ref, kv3_ref, z1_ref, zn_ref, ga_ref, gb_ref, wup_bf_ref, wdn_bf_ref):
    _cast_blocks(((wup_ref, wup_bf_ref), (wdn_ref, wdn_bf_ref)))
    h = _rms(x_ref[0], g_ref[...]).astype(BF16)
    _proj_gate_branches(h, w_ref, bg_ref, lng_ref, lnb_ref, z1_ref, zn_ref, ga_ref, gb_ref)
    def project(first_part, n_parts, scale=None):
        cols = jnp.dot(h, w_ref[:, first_part * ATTN_W:(first_part + n_parts) * ATTN_W], preferred_element_type=F32)
        if scale is not None:
            cols = cols * scale
        for part in range(n_parts):
            for g in range(N_GROUPS):
                for s in range(SLABS):
                    col = part * ATTN_W + g * GROUP_W + s * LANES
                    qkv_ref[0, g, (first_part + part) * SLABS + s] = cols[:, col:col + LANES]

    def kv_t(g):
        return jnp.concatenate([qkv_ref[0, g, slab].T for slab in range(SLABS, 3 * SLABS)], axis=0)

    tm = x_ref.shape[1]
    project(1, 2)
    kv3_ref[0] = kv_t(2)
    project(0, 1, scale=SCALE * LOG2E)

    @pl.when(pl.program_id(1) == pl.num_programs(1) - 1)
    def _():
        kv2_ref[0] = kv_t(1)[:, tm - kv2_ref.shape[2]:]
        kv1_ref[0] = kv_t(0)[:, tm - kv1_ref.shape[2]:]


def _proj_weight_specs():
    return [_const_spec((1, D_MODEL)), _const_spec((D_MODEL, IN_W)),
            _const_spec((1, 2 * D_MODEL)), _const_spec((1, SG_W)), _const_spec((1, SG_W))]


def _gate_branch_shapes(n, z_dtype):
    return [jax.ShapeDtypeStruct((n, SG_W), z_dtype), jax.ShapeDtypeStruct((n, SG_W), z_dtype),
            jax.ShapeDtypeStruct((n, D_MODEL), BF16), jax.ShapeDtypeStruct((n, D_MODEL), BF16)]


def _row_slices(arrays, steps):
    specs, shapes = [], []
    for a in arrays:
        rows = a.shape[0] // steps
        assert rows * steps == a.shape[0] and rows % BF16_SUBLANES == 0
        specs.append((rows, a.shape[1]))
        shapes.append(jax.ShapeDtypeStruct(a.shape, BF16))
    return specs, shapes


def _project_sample(x3, g, w_in, bg, lng, lnb, mix_weights):
    n = x3.shape[0] * x3.shape[1]
    tm = min(SAMPLE_TAIL_TILE, n)
    steps = n // tm
    row = lambda w: pl.BlockSpec((tm, w), lambda i: (i, 0))
    slices, bf_shapes = _row_slices(mix_weights, steps)
    slice_specs = [pl.BlockSpec(s, lambda i: (i, 0)) for s in slices]
    return pl.pallas_call(
        _proj_sample_kernel,
        grid=(steps,),
        in_specs=[pl.BlockSpec((tm // x3.shape[1],) + x3.shape[1:], lambda i: (i, 0, 0))] + _proj_weight_specs()
        + slice_specs,
        out_specs=[row(Z_OFF), row(SG_W), row(SG_W), row(D_MODEL), row(D_MODEL),
                   pl.BlockSpec((D_MODEL, IN_W), lambda i: (0, 0))] + slice_specs,
        out_shape=[jax.ShapeDtypeStruct((n, Z_OFF), F32)] + _gate_branch_shapes(n, F32)
        + [jax.ShapeDtypeStruct((D_MODEL, IN_W), BF16)] + bf_shapes,
        compiler_params=_params(1),
        name="proj_sample",
    )(x3, g, w_in, bg, lng, lnb, *mix_weights)


def _project_prompt(x, g, w_bf, bg, lng, lnb, ffn_weights):
    batch, seq, _ = x.shape
    tm = TOKEN_TILE
    tiles = seq // tm
    keeps = [min(win, seq) for win, _ in DIL_GROUPS]
    assert seq % tm == 0 and keeps[0] <= tm and keeps[1] <= tm and keeps[2] == seq
    row = lambda w: pl.BlockSpec((tm, w), lambda b, t: (b * tiles + t, 0))
    last = lambda keep: pl.BlockSpec((1, 2 * GROUP_W, keep), lambda b, t: (b, 0, 0))
    n = batch * seq
    slices, bf_shapes = _row_slices(ffn_weights, batch * tiles)
    slice_specs = [pl.BlockSpec(s, lambda b, t: (b * tiles + t, 0)) for s in slices]
    return pl.pallas_call(
        _proj_prompt_kernel,
        grid=(batch, tiles),
        in_specs=[pl.BlockSpec((1, tm, D_MODEL), lambda b, t: (b, t, 0))] + _proj_weight_specs() + slice_specs,
        out_specs=[pl.BlockSpec((1, N_GROUPS, 3 * SLABS, tm, LANES), lambda b, t: (b, 0, 0, t, 0)),
                   last(keeps[0]), last(keeps[1]),
                   pl.BlockSpec((1, 2 * GROUP_W, tm), lambda b, t: (b, 0, t)),
                   row(SG_W), row(SG_W), row(D_MODEL), row(D_MODEL)] + slice_specs,
        out_shape=[jax.ShapeDtypeStruct((batch, N_GROUPS, 3 * SLABS, seq, LANES), F32),
                   jax.ShapeDtypeStruct((batch, 2 * GROUP_W, keeps[0]), F32),
                   jax.ShapeDtypeStruct((batch, 2 * GROUP_W, keeps[1]), F32),
                   jax.ShapeDtypeStruct((batch, 2 * GROUP_W, keeps[2]), F32)] + _gate_branch_shapes(n, BF16)
        + bf_shapes,
        compiler_params=_params(2),
        name="proj_prompt",
    )(x, g, w_bf, bg, lng, lnb, *ffn_weights)


def _head_lane_mask(shape, hh):
    lane = lax.broadcasted_iota(jnp.int32, shape, len(shape) - 1)
    return (lane >> HEAD_SHIFT) == hh


def _band_block(q, k, v, mask):
    nt = (((1,), (1,)), ((), ()))
    assert HEADS_PER_SLAB == 2
    lm0 = _head_lane_mask((BLOCK, LANES), 0)
    mask2 = jnp.concatenate([mask, mask], axis=0)
    outs, lses = [], []
    for s in range(SLABS):
        zero = jnp.zeros_like(q[s])
        qm = jnp.concatenate([jnp.where(lm0, q[s], zero), jnp.where(lm0, zero, q[s])], axis=0)
        sc = lax.dot_general(qm, k[s], nt, preferred_element_type=F32)
        sc = jnp.where(mask2, sc, NEG)
        m = jnp.max(sc, axis=-1, keepdims=True)
        p = jnp.exp2(sc - m)
        l = jnp.sum(p, axis=-1, keepdims=True)
        o = jnp.dot(p.astype(BF16), v[s], preferred_element_type=F32) * (1.0 / l)
        lse = m + jnp.log2(l)
        outs.append(jnp.where(lm0, o[:BLOCK], o[BLOCK:]))
        lses.append(jnp.where(lm0, lse[:BLOCK], lse[BLOCK:]))
    return outs, lses


def _group_attention(qkv_ref, o_scr, lse_scr, g, dil, seq):
    n_qb = seq // (dil * BLOCK)
    qb_shift = n_qb.bit_length() - 1
    span = dil * BLOCK
    rows = (lambda start: pl.ds(start, BLOCK, stride=dil)) if dil > 1 else (lambda start: pl.ds(start, BLOCK))
    nk = 2 * BLOCK if n_qb > 1 else BLOCK
    qi = lax.broadcasted_iota(jnp.int32, (BLOCK, nk), 0)
    ki = lax.broadcasted_iota(jnp.int32, (BLOCK, nk), 1)

    def load(slab, start):
        return qkv_ref[0, 0, slab, rows(start), :]

    def body(n, carry):
        r = n >> qb_shift
        qb = n & (n_qb - 1)
        cur = r + span * qb
        if dil == 1:
            cur = pl.multiple_of(cur, BLOCK)
        q = [load(s, cur).astype(BF16) for s in range(SLABS)]
        k = [load(SLABS + s, cur).astype(BF16) for s in range(SLABS)]
        v = [load(2 * SLABS + s, cur).astype(BF16) for s in range(SLABS)]
        if n_qb > 1:
            prev = r + span * jnp.maximum(qb - 1, 0)
            if dil == 1:
                prev = pl.multiple_of(prev, BLOCK)
            k = [jnp.concatenate([load(SLABS + s, prev).astype(BF16), k[s]], axis=0) for s in range(SLABS)]
            v = [jnp.concatenate([load(2 * SLABS + s, prev).astype(BF16), v[s]], axis=0) for s in range(SLABS)]
            mask = (ki >= qi) & (ki <= qi + BLOCK) & (ki >= jnp.where(qb > 0, 0, BLOCK))
        else:
            mask = ki <= qi
        outs, lses = _band_block(q, k, v, mask)
        for s in range(SLABS):
            o_scr[g, s, rows(cur), :] = outs[s]
            lse_scr[g, s, rows(cur), :] = lses[s]
        return carry

    lax.fori_loop(0, dil * n_qb, body, 0, unroll=BLOCK_UNROLL)


def _prompt_attn_kernel(qkv_ref, o_ref, o_scr, lse_scr):
    g = pl.program_id(1)
    seq = o_ref.shape[1]
    for gi, (_, dil) in enumerate(DIL_GROUPS):
        @pl.when(g == gi)
        def _(gi=gi, dil=dil):
            _group_attention(qkv_ref, o_scr, lse_scr, gi, dil, seq)

    @pl.when(g == N_GROUPS - 1)
    def _():
        def merge(i, carry):
            rows = pl.ds(pl.multiple_of(i * BLOCK, BLOCK), BLOCK)
            for s in range(SLABS):
                ls = [lse_scr[gi, s, rows, :] for gi in range(N_GROUPS)]
                top = jnp.maximum(jnp.maximum(ls[0], ls[1]), ls[2])
                e = [jnp.exp2(x - top) for x in ls]
                num = e[0] * o_scr[0, s, rows, :] + e[1] * o_scr[1, s, rows, :] + e[2] * o_scr[2, s, rows, :]
                o_ref[0, rows, s * LANES:(s + 1) * LANES] = (num / (e[0] + e[1] + e[2])).astype(o_ref.dtype)
            return carry
        lax.fori_loop(0, seq // BLOCK, merge, 0, unroll=MERGE_UNROLL)


def _prompt_attention(qkv, seq):
    batch = qkv.shape[0]
    scratch = pltpu.VMEM((N_GROUPS, SLABS, seq, LANES), F32)
    return pl.pallas_call(
        _prompt_attn_kernel,
        grid=(batch, N_GROUPS),
        in_specs=[pl.BlockSpec((1, 1, 3 * SLABS, seq, LANES), lambda b, g: (b, g, 0, 0, 0))],
        out_specs=pl.BlockSpec((1, seq, GROUP_W), lambda b, g: (b, 0, 0)),
        out_shape=jax.ShapeDtypeStruct((batch, seq, GROUP_W), BF16),
        scratch_shapes=[scratch, scratch],
        compiler_params=_params(2),
        name="prompt_attn",
    )(qkv)


def _residue_allreduce(x, dil, op):
    shift = LANES // 2
    while shift >= dil:
        x = op(x, pltpu.roll(x, shift, axis=1))
        shift //= 2
    return x


def _lane_expand(rows8, dil, r0, t_new):
    assert r0 == 0 and t_new <= dil and (dil % T_PAD == 0 or T_PAD % dil == 0)
    if dil >= T_PAD:
        period = [rows8] + [jnp.zeros_like(rows8)] * (dil // T_PAD - 1)
    else:
        period = [jnp.concatenate([rows8[:dil]] * (T_PAD // dil), axis=0)]
    rows = jnp.concatenate(period * (LANES // max(dil, T_PAD)), axis=0)
    return rows.T


def _heads_to_rows(x):
    return jnp.broadcast_to(x[:, None, :], (HEADS, HEAD_DIM, LANES)).reshape(GROUP_W, LANES)


def _rows_to_heads_sum(x):
    return jnp.sum(x.reshape(HEADS, HEAD_DIM, LANES), axis=1)


def _buffer_group_lanes(c_ref, d, q8, kn8, vn8, dil, r0, t_new):
    width = c_ref.shape[2]
    tiles = width // LANES
    ql, kl, vl = (_lane_expand(a, dil, r0, t_new) for a in (q8, kn8, vn8))
    s = [_rows_to_heads_sum(c_ref[d, :GROUP_W, j * LANES:(j + 1) * LANES] * ql) for j in range(tiles)]
    s_new = _rows_to_heads_sum(kl * ql)
    top = s[0]
    for sj in s[1:]:
        top = jnp.maximum(top, sj)
    top = jnp.maximum(_residue_allreduce(top, dil, jnp.maximum), s_new)
    p = [jnp.exp(sj - top) for sj in s]
    p_new = jnp.exp(s_new - top)
    den = p[0]
    for pj in p[1:]:
        den = den + pj
    den = _residue_allreduce(den, dil, jnp.add) + p_new
    acc = c_ref[d, GROUP_W:, :LANES] * _heads_to_rows(p[0])
    for j in range(1, tiles):
        acc = acc + c_ref[d, GROUP_W:, j * LANES:(j + 1) * LANES] * _heads_to_rows(p[j])
    acc = (_residue_allreduce(acc, dil, jnp.add) + vl * _heads_to_rows(p_new)) * _heads_to_rows(1.0 / den)
    lse = _heads_to_rows(top + jnp.log(den))
    first = -(-r0 // dil) * dil
    return acc.T[first - r0:first - r0 + T_PAD], lse.T[first - r0:first - r0 + T_PAD]


def _buffer_group_dense(c_ref, d, q8, kn8, vn8, r0, t_new):
    rows = HEADS * T_PAD
    width = c_ref.shape[2]
    row = lax.broadcasted_iota(jnp.int32, (rows, GROUP_W), 0)
    lane = lax.broadcasted_iota(jnp.int32, (rows, GROUP_W), 1)
    head_sel = (row >> T_SHIFT) == (lane >> HEAD_SHIFT)
    t = (lax.broadcasted_iota(jnp.int32, (rows, width), 0) & (T_PAD - 1)) - r0
    i = lax.broadcasted_iota(jnp.int32, (rows, width), 1)
    tn = lax.broadcasted_iota(jnp.int32, (rows, T_PAD), 0) & (T_PAD - 1)
    j = lax.broadcasted_iota(jnp.int32, (rows, T_PAD), 1)
    nt = (((1,), (1,)), ((), ()))
    qm = jnp.where(head_sel, jnp.concatenate([q8] * HEADS, axis=0), 0.0).astype(BF16)
    kt = c_ref[d, :GROUP_W, :].astype(BF16)
    vt = c_ref[d, GROUP_W:, :].astype(BF16)
    sc = jnp.where(i >= t, jnp.dot(qm, kt, preferred_element_type=F32), NEG)
    own = (j <= tn) & (j >= r0) & (j < r0 + t_new)
    sn = jnp.where(own, lax.dot_general(qm, kn8.astype(BF16), nt, preferred_element_type=F32), NEG)
    m = jnp.maximum(jnp.max(sc, axis=-1, keepdims=True), jnp.max(sn, axis=-1, keepdims=True))
    pc = jnp.exp(sc - m)
    pn = jnp.exp(sn - m)
    l = jnp.sum(pc, axis=-1, keepdims=True) + jnp.sum(pn, axis=-1, keepdims=True)

    def finish():
        o = (lax.dot_general(pc.astype(BF16), vt, nt, preferred_element_type=F32)
             + jnp.dot(pn.astype(BF16), vn8.astype(BF16), preferred_element_type=F32))
        o = jnp.where(head_sel, o / l, 0.0)
        lse = jnp.where(head_sel, m + jnp.log(l), 0.0)
        fold = lambda x: sum(x[h * T_PAD:(h + 1) * T_PAD] for h in range(HEADS))
        return fold(o), fold(lse)
    return finish


def _sample_attn_start(qkv_ref, c1_ref, c2_ref, c3_ref, o_ref):
    n_db = c1_ref.shape[0]
    t_new = T_PAD // n_db
    assert qkv_ref.shape[0] == T_PAD and n_db * t_new == T_PAD
    pad_rows = jnp.zeros((T_PAD - t_new, qkv_ref.shape[1]), F32)
    pending = []
    for d in range(n_db):
        new_rows = jnp.concatenate([qkv_ref[d * t_new:(d + 1) * t_new, :], pad_rows], axis=0)
        groups = []
        for g, (c_ref, (_, dil)) in enumerate(zip((c1_ref, c2_ref, c3_ref), DIL_GROUPS)):
            cols = lambda part: slice(part * ATTN_W + g * GROUP_W, part * ATTN_W + (g + 1) * GROUP_W)
            q8 = new_rows[:, cols(0)] * SCALE
            kn8, vn8 = new_rows[:, cols(1)], new_rows[:, cols(2)]
            if dil == 1:
                groups.append(_buffer_group_dense(c_ref, d, q8, kn8, vn8, 0, t_new))
            else:
                result = _buffer_group_lanes(c_ref, d, q8, kn8, vn8, dil, 0, t_new)
                groups.append(lambda result=result: result)
        pending.append(groups)

    def finish():
        for d, groups in enumerate(pending):
            outs, lses = zip(*(group() for group in groups))
            top = jnp.maximum(jnp.maximum(lses[0], lses[1]), lses[2])
            w = [jnp.exp(x - top) for x in lses]
            o8 = (w[0] * outs[0] + w[1] * outs[1] + w[2] * outs[2]) / (w[0] + w[1] + w[2])
            o_ref[d * t_new:(d + 1) * t_new, :] = o8[:t_new]
    return finish


def _mix_value(o_a, o_b, ga_ref, gb_ref, x, wao_ref, wbo_ref, wout_ref, gpost_ref):
    a = jnp.dot(o_a.astype(BF16), wao_ref[...], preferred_element_type=F32)
    b = jnp.dot(o_b.astype(BF16), wbo_ref[...], preferred_element_type=F32)
    merged = ga_ref[...].astype(F32) * a + gb_ref[...].astype(F32) * b
    t = jnp.dot(merged.astype(BF16), wout_ref[...], preferred_element_type=F32)
    return x + _rms(t, gpost_ref[...])


def _ffn_value(x, gpre_ref, wup_ref, wdn_ref, gpost_ref):
    h = _rms(x, gpre_ref[...]).astype(BF16)
    f = jnp.zeros(x.shape, F32)
    for c in range(D_FF // FF_CHUNK):
        cs = slice(c * FF_CHUNK, (c + 1) * FF_CHUNK)
        u = jnp.maximum(jnp.dot(h, wup_ref[:, cs], preferred_element_type=F32), 0.0)
        f = f + jnp.dot((u * u).astype(BF16), wdn_ref[cs, :], preferred_element_type=F32)
    return x + _rms(f, gpost_ref[...])


def _prompt_spatial_gate(z1_ref, zn_ref, wsp_ref, bsp_ref):
    r = lax.broadcasted_iota(jnp.int32, (CHUNK, CHUNK), 0)
    c = lax.broadcasted_iota(jnp.int32, (CHUNK, CHUNK), 1)
    n_chunks = z1_ref.shape[0] // CHUNK
    mixed = []
    for g in range(SG_GROUPS):
        wt = jnp.where(r >= c, wsp_ref[g], 0.0).astype(BF16)
        zg = jnp.concatenate([zn_ref[ci * CHUNK:(ci + 1) * CHUNK, g * CHUNK:(g + 1) * CHUNK].astype(BF16)
                              for ci in range(n_chunks)], axis=1)
        mixed.append(jnp.dot(wt, zg, preferred_element_type=F32))
    chunks = [jnp.concatenate([m[:, ci * CHUNK:(ci + 1) * CHUNK] for m in mixed], axis=1) + bsp_ref[...]
              for ci in range(n_chunks)]
    return z1_ref[...].astype(F32) * jnp.concatenate(chunks, axis=0)


def _tail_prompt_kernel(oa_ref, z1_ref, zn_ref, ga_ref, gb_ref, x_ref, wsp_ref, bsp_ref,
                        wao_ref, wbo_ref, wout_ref, gmix_ref, gpre_ref, wup_ref, wdn_ref, gpost_ref,
                        qkv_ref, c1_ref, c2_ref, c3_ref, y_ref, o_ref):
    o_b = _prompt_spatial_gate(z1_ref, zn_ref, wsp_ref, bsp_ref)
    x1 = _mix_value(oa_ref[...], o_b, ga_ref, gb_ref, x_ref[...], wao_ref, wbo_ref, wout_ref, gmix_ref)
    finish_sample_attn = _sample_attn_start(qkv_ref, c1_ref, c2_ref, c3_ref, o_ref)
    y_ref[...] = _ffn_value(x1, gpre_ref, wup_ref, wdn_ref, gpost_ref)
    finish_sample_attn()


def _new_kv_feature_major(kn_ref, vn_ref, out_refs, scr):
    t_new, ndb = out_refs[0].shape[0], out_refs[0].shape[2]
    slabs = ATTN_W // LANES
    for part, src_ref in enumerate((kn_ref, vn_ref)):
        for slab in range(slabs):
            scr[part * slabs + slab] = src_ref[:, slab * LANES:(slab + 1) * LANES]
    for t in range(t_new):
        for part in range(2):
            for g, out_ref in enumerate(out_refs):
                for s in range(SLABS):
                    rows = scr[part * slabs + g * SLABS + s, pl.ds(t, ndb, stride=t_new), :]
                    out_ref[t, part * GROUP_W + s * LANES:part * GROUP_W + (s + 1) * LANES, :] = rows.T


def _tail_sample_kernel(oa_ref, z1_ref, zn_ref, ga_ref, gb_ref, x_ref, coef_ref, bsp_ref,
                        wao_ref, wbo_ref, wout_ref, gmix_ref, gpre_ref, wup_ref, wdn_ref, gpost_ref, kn_ref, vn_ref,
                        y_ref, kvs1_ref, kvs2_ref, kvs3_ref, x1_scr, h_scr, f_scr, kv_scr):
    c = pl.program_id(0)

    @pl.when(c == 0)
    def _():
        zn = zn_ref[...]
        mix = coef_ref[0] * zn
        for d in range(1, coef_ref.shape[0]):
            mix = mix + coef_ref[d] * pltpu.roll(zn, d, axis=0)
        o_b = z1_ref[...] * (mix + bsp_ref[...])
        x = x_ref[...].reshape(zn.shape[0], D_MODEL)
        x1 = _mix_value(oa_ref[...], o_b, ga_ref, gb_ref, x, wao_ref, wbo_ref, wout_ref, gmix_ref)
        x1_scr[...] = x1
        h_scr[...] = _rms(x1, gpre_ref[...]).astype(BF16)
        f_scr[...] = jnp.zeros_like(f_scr)
        _new_kv_feature_major(kn_ref, vn_ref, (kvs1_ref, kvs2_ref, kvs3_ref), kv_scr)

    u = jnp.maximum(jnp.dot(h_scr[...], wup_ref[...], preferred_element_type=F32), 0.0)
    f_scr[...] += jnp.dot((u * u).astype(BF16), wdn_ref[...], preferred_element_type=F32)

    @pl.when(c == pl.num_programs(0) - 1)
    def _():
        y_ref[...] = (x1_scr[...] + _rms(f_scr[...], gpost_ref[...])).reshape(y_ref.shape)


def _tail_weight_specs():
    return [_const_spec((GROUP_W, D_MODEL)), _const_spec((SG_W, D_MODEL)), _const_spec((D_MODEL, D_MODEL)),
            _const_spec((1, D_MODEL)), _const_spec((1, D_MODEL)), _const_spec((D_MODEL, D_FF)),
            _const_spec((D_FF, D_MODEL)), _const_spec((1, D_MODEL))]


def _tail_prompt(o_a, z1, zn, ga, gb, x, wsp, bsp, weights, qkv_s, caches):
    n, db, ns = x.shape[0], caches[0].shape[0], qkv_s.shape[0]
    tile = SAMPLE_DB_TILE
    steps = db // tile
    tm = n // steps
    assert db % tile == 0 and n % steps == 0 and tm % CHUNK == 0 and ns == steps * T_PAD
    row = lambda w: pl.BlockSpec((tm, w), lambda i: (i, 0))
    blk = lambda a: pl.BlockSpec((tile,) + a.shape[1:], lambda i: (i, 0, 0))
    new_rows = lambda w: pl.BlockSpec((T_PAD, w), lambda i: (i, 0))
    return pl.pallas_call(
        _tail_prompt_kernel,
        grid=(steps,),
        in_specs=[row(GROUP_W), row(SG_W), row(SG_W), row(D_MODEL), row(D_MODEL), row(D_MODEL),
                  _const_spec((SG_GROUPS, CHUNK, CHUNK)), _const_spec((CHUNK, SG_W))] + _tail_weight_specs()
        + [new_rows(Z_OFF)] + [blk(a) for a in caches],
        out_specs=[row(D_MODEL), new_rows(GROUP_W)],
        out_shape=[jax.ShapeDtypeStruct((n, D_MODEL), F32),
                   jax.ShapeDtypeStruct((ns, GROUP_W), F32)],
        compiler_params=_params(1),
        name="tail_prompt",
    )(o_a, z1, zn, ga, gb, x, wsp, bsp, *weights, qkv_s, *caches)


def _tail_sample(o_a, z1, zn, ga, gb, x3, coef, bsp, weights, qkv):
    n = x3.shape[0] * x3.shape[1]
    wao, wbo, wout, gmix, gpre, wup, wdn, gpost = weights
    full = lambda a: _const_spec(a.shape)
    kv_new = jax.ShapeDtypeStruct((x3.shape[1], 2 * GROUP_W, x3.shape[0]), F32)
    qkv_part = lambda part: pl.BlockSpec((n, ATTN_W), lambda c: (0, part), pipeline_mode=pl.Buffered(1))
    return pl.pallas_call(
        _tail_sample_kernel,
        grid=(D_FF // FF_CHUNK,),
        in_specs=[full(o_a), full(z1), full(zn), full(ga), full(gb), full(x3), full(coef), full(bsp),
                  full(wao), full(wbo), full(wout), full(gmix), full(gpre),
                  pl.BlockSpec((D_MODEL, FF_CHUNK), lambda c: (0, c)),
                  pl.BlockSpec((FF_CHUNK, D_MODEL), lambda c: (c, 0)), full(gpost), qkv_part(1), qkv_part(2)],
        out_specs=[pl.BlockSpec(x3.shape, lambda c: (0, 0, 0))]
        + [pl.BlockSpec(kv_new.shape, lambda c: (0, 0, 0))] * N_GROUPS,
        out_shape=[jax.ShapeDtypeStruct(x3.shape, F32)] + [kv_new] * N_GROUPS,
        scratch_shapes=[pltpu.VMEM((n, D_MODEL), F32), pltpu.VMEM((n, D_MODEL), BF16), pltpu.VMEM((n, D_MODEL), F32),
                        pltpu.VMEM((2 * ATTN_W // LANES, n, LANES), F32)],
        compiler_params=_params(1),
        name="tail_sample",
    )(o_a, z1, zn, ga, gb, x3, coef, bsp, *weights, qkv, qkv)


def _kv_from_feature_major(kv_t):
    batch, _, keep = kv_t.shape
    return kv_t.reshape(batch, 2, HEADS, HEAD_DIM, keep).transpose(0, 4, 1, 2, 3)[None]


def kernel(x_prompt, x_sample, cache_kv_w128, cache_kv_w512, cache_kv_w2048, norm_pre_mix, w_in, b_gate,
           ln_z_g, ln_z_b, w_spatial, b_spatial, w_ao, w_bo, w_out, norm_post_mix, norm_pre_ffn, w_up, w_down,
           norm_post_ffn):
    assert w_in.shape[0] == 1, "single-layer problem"
    batch, seq, _ = x_prompt.shape
    dbatch, t_new, _ = x_sample.shape
    caches_in = (cache_kv_w128[0], cache_kv_w512[0], cache_kv_w2048[0])

    wsp = w_spatial[0]
    bsp_rows = jnp.repeat(b_spatial[0].T, CHUNK, axis=1)

    ns = dbatch * t_new
    qkv_s, z1_s, zn_s, ga_s, gb_s, w_in_bf, wao, wbo, wout = _project_sample(
        x_sample, norm_pre_mix, w_in[0], b_gate, ln_z_g, ln_z_b, (w_ao[0], w_bo[0], w_out[0]))
    caches = []
    for gi, (win, dil) in enumerate(DIL_GROUPS):
        buf = caches_in[gi]
        wb = buf.shape[1]
        assert wb == win and t_new <= T_PAD, "full window buffers"
        caches.append(buf.transpose(0, 2, 3, 4, 1).reshape(dbatch, 2 * GROUP_W, wb))

    xp = x_prompt.reshape(batch * seq, D_MODEL)
    qkv, kv1_t, kv2_t, kv3_t, z1, zn, ga, gb, wup, wdn = _project_prompt(
        x_prompt, norm_pre_mix, w_in_bf, b_gate, ln_z_g, ln_z_b, (w_up[0], w_down[0]))
    weights = (wao, wbo, wout, norm_post_mix, norm_pre_ffn, wup, wdn, norm_post_ffn)
    kv_prompt = [_kv_from_feature_major(a) for a in (kv1_t, kv2_t, kv3_t)]
    o_a_p = _prompt_attention(qkv, seq).reshape(batch * seq, GROUP_W)
    assert SAMPLE_DB_TILE * t_new == T_PAD, "a tail step takes T_PAD new rows"
    y_prompt, o_a = _tail_prompt(o_a_p, z1, zn, ga, gb, xp, wsp, bsp_rows, weights, qkv_s, caches)
    y_prompt = y_prompt.reshape(batch, seq, D_MODEL)

    wt = jnp.tril(wsp)[:, :t_new, :t_new]
    coef = jnp.stack([jnp.where(jnp.arange(t_new)[None, :] >= d,
                                wt[:, jnp.arange(t_new), jnp.maximum(jnp.arange(t_new) - d, 0)], 0.0)
                      for d in range(t_new)])
    coef = jnp.repeat(coef.transpose(0, 2, 1), CHUNK, axis=2)
    reps = ns // t_new
    coef = jnp.tile(coef, (1, reps, 1))
    bsp_s = jnp.tile(bsp_rows[:t_new], (reps, 1))
    y_sample, *kv_new = _tail_sample(o_a, z1_s, zn_s, ga_s, gb_s, x_sample, coef, bsp_s, weights, qkv_s)
    kv_sample = [a.reshape(t_new, 2, HEADS, HEAD_DIM, dbatch).transpose(4, 0, 1, 2, 3)[None] for a in kv_new]
    sg_sample = zn_s.reshape(1, dbatch, t_new, SG_W)

    return (y_prompt, y_sample, kv_prompt[0], kv_prompt[1], kv_prompt[2],
            kv_sample[0], kv_sample[1], kv_sample[2], sg_sample)
```

```python
import math

import jax
import jax.numpy as jnp
from jax import lax
from jax.experimental import pallas as pl
from jax.experimental.pallas import tpu as pltpu

F32 = jnp.float32
BF16 = jnp.bfloat16

D_MODEL = 1024
HEAD_DIM = 64
HEADS = 4
GROUP_W = HEADS * HEAD_DIM
DIL_GROUPS = ((128, 1), (512, 4), (2048, 16))
N_GROUPS = len(DIL_GROUPS)
ATTN_W = N_GROUPS * GROUP_W
BLOCK = 128
CHUNK = 128
SG_GROUPS = 4
SG_W = 512
D_FF = 4 * D_MODEL
Z_OFF = 3 * ATTN_W
G_OFF = Z_OFF + 2 * SG_W
IN_W = G_OFF + 2 * D_MODEL
EPS = 1e-6
NEG = -1e30
SCALE = HEAD_DIM ** -0.5
INV_SQRT2 = 1.0 / math.sqrt(2.0)
LOG2E = 1.0 / math.log(2.0)
LANES = 128
SLABS = GROUP_W // LANES
HEADS_PER_SLAB = LANES // HEAD_DIM
HEAD_SHIFT = HEAD_DIM.bit_length() - 1
VMEM_LIMIT = 56 * 1024 * 1024

TOKEN_TILE = 512
MERGE_UNROLL = 4
BLOCK_UNROLL = 16
SAMPLE_DB_TILE = 2
T_PAD = 8
T_SHIFT = T_PAD.bit_length() - 1
BF16_SUBLANES = 16
CAST_COLS = 768
SAMPLE_TAIL_TILE = 256
FF_CHUNK = 1024


def _params(n_axes):
    return pltpu.CompilerParams(dimension_semantics=("arbitrary",) * n_axes,
                                vmem_limit_bytes=VMEM_LIMIT)


def _const_spec(shape):
    return pl.BlockSpec(shape, lambda *_: (0,) * len(shape), pipeline_mode=pl.Buffered(1))


def _rms(x, g):
    return x * lax.rsqrt(jnp.mean(x * x, axis=-1, keepdims=True) + EPS) * g


def _proj_gate_branches(h, w_ref, bg_ref, lng_ref, lnb_ref, z1_ref, zn_ref, ga_ref, gb_ref):
    z = jnp.dot(h, w_ref[:, Z_OFF:G_OFF], preferred_element_type=F32)
    z = 0.5 * z * (1.0 + lax.erf(z * INV_SQRT2))
    z1_ref[...] = z[:, :SG_W].astype(z1_ref.dtype)
    z2 = z[:, SG_W:]
    mu = jnp.mean(z2, axis=-1, keepdims=True)
    zc = z2 - mu
    var = jnp.mean(zc * zc, axis=-1, keepdims=True)
    zn_ref[...] = (zc * lax.rsqrt(var + EPS) * lng_ref[...] + lnb_ref[...]).astype(zn_ref.dtype)

    gates = jnp.dot(h, w_ref[:, G_OFF:], preferred_element_type=F32) + bg_ref[...]
    gates = 1.0 / (1.0 + jnp.exp(-gates))
    ga_ref[...] = gates[:, :D_MODEL].astype(BF16)
    gb_ref[...] = gates[:, D_MODEL:].astype(BF16)


def _cast_blocks(pairs):
    for src_ref, dst_ref in pairs:
        dst_ref[...] = src_ref[...].astype(dst_ref.dtype)


def _proj_sample_kernel(x_ref, g_ref, wf_ref, bg_ref, lng_ref, lnb_ref, wao_ref, wbo_ref, wout_ref,
                        qkv_ref, z1_ref, zn_ref, ga_ref, gb_ref, w_ref, wao_bf_ref, wbo_bf_ref, wout_bf_ref):
    @pl.when(pl.program_id(0) == 0)
    def _():
        for c in range(0, IN_W, CAST_COLS):
            w_ref[:, c:c + CAST_COLS] = wf_ref[:, c:c + CAST_COLS].astype(BF16)

    _cast_blocks(((wao_ref, wao_bf_ref), (wbo_ref, wbo_bf_ref), (wout_ref, wout_bf_ref)))
    x = x_ref[...]
    h = _rms(x.reshape(x.shape[0] * x.shape[1], x.shape[2]), g_ref[...]).astype(BF16)
    qkv_ref[...] = jnp.dot(h, w_ref[:, :Z_OFF], preferred_element_type=F32)
    _proj_gate_branches(h, w_ref, bg_ref, lng_ref, lnb_ref, z1_ref, zn_ref, ga_ref, gb_ref)


def _proj_prompt_kernel(x_ref, g_ref, w_ref, bg_ref, lng_ref, lnb_ref, wup_ref, wdn_ref,
                        qkv_ref, kv1_ref, kv2_ref, kv3_ref, z1_ref, zn_ref, ga_ref, gb_ref, wup_bf_ref, wdn_bf_ref):
    _cast_blocks(((wup_ref, wup_bf_ref), (wdn_ref, wdn_bf_ref)))
    h = _rms(x_ref[0], g_ref[...]).astype(BF16)
    _proj_gate_branches(h, w_ref, bg_ref, lng_ref, lnb_ref, z1_ref, zn_ref, ga_ref, gb_ref)
    def project(first_part, n_parts, scale=None):
        cols = jnp.dot(h, w_ref[:, first_part * ATTN_W:(first_part + n_parts) * ATTN_W], preferred_element_type=F32)
        if scale is not None:
            cols = cols * scale
        for part in range(n_parts):
            for g in range(N_GROUPS):
                for s in range(SLABS):
                    col = part * ATTN_W + g * GROUP_W + s * LANES
                    qkv_ref[0, g, (first_part + part) * SLABS + s] = cols[:, col:col + LANES]

    def kv_t(g):
        return jnp.concatenate([qkv_ref[0, g, slab].T for slab in range(SLABS, 3 * SLABS)], axis=0)

    tm = x_ref.shape[1]
    project(1, 2)
    kv3_ref[0] = kv_t(2)
    project(0, 1, scale=SCALE * LOG2E)

    @pl.when(pl.program_id(1) == pl.num_programs(1) - 1)
    def _():
        kv2_ref[0] = kv_t(1)[:, tm - kv2_ref.shape[2]:]
        kv1_ref[0] = kv_t(0)[:, tm - kv1_ref.shape[2]:]


def _proj_weight_specs():
    return [_const_spec((1, D_MODEL)), _const_spec((D_MODEL, IN_W)),
            _const_spec((1, 2 * D_MODEL)), _const_spec((1, SG_W)), _const_spec((1, SG_W))]


def _gate_branch_shapes(n, z_dtype):
    return [jax.ShapeDtypeStruct((n, SG_W), z_dtype), jax.ShapeDtypeStruct((n, SG_W), z_dtype),
            jax.ShapeDtypeStruct((n, D_MODEL), BF16), jax.ShapeDtypeStruct((n, D_MODEL), BF16)]


def _row_slices(arrays, steps):
    specs, shapes = [], []
    for a in arrays:
        rows = a.shape[0] // steps
        assert rows * steps == a.shape[0] and rows % BF16_SUBLANES == 0
        specs.append((rows, a.shape[1]))
        shapes.append(jax.ShapeDtypeStruct(a.shape, BF16))
    return specs, shapes


def _project_sample(x3, g, w_in, bg, lng, lnb, mix_weights):
    n = x3.shape[0] * x3.shape[1]
    tm = min(SAMPLE_TAIL_TILE, n)
    steps = n // tm
    row = lambda w: pl.BlockSpec((tm, w), lambda i: (i, 0))
    slices, bf_shapes = _row_slices(mix_weights, steps)
    slice_specs = [pl.BlockSpec(s, lambda i: (i, 0)) for s in slices]
    return pl.pallas_call(
        _proj_sample_kernel,
        grid=(steps,),
        in_specs=[pl.BlockSpec((tm // x3.shape[1],) + x3.shape[1:], lambda i: (i, 0, 0))] + _proj_weight_specs()
        + slice_specs,
        out_specs=[row(Z_OFF), row(SG_W), row(SG_W), row(D_MODEL), row(D_MODEL),
                   pl.BlockSpec((D_MODEL, IN_W), lambda i: (0, 0))] + slice_specs,
        out_shape=[jax.ShapeDtypeStruct((n, Z_OFF), F32)] + _gate_branch_shapes(n, F32)
        + [jax.ShapeDtypeStruct((D_MODEL, IN_W), BF16)] + bf_shapes,
        compiler_params=_params(1),
        name="proj_sample",
    )(x3, g, w_in, bg, lng, lnb, *mix_weights)


def _project_prompt(x, g, w_bf, bg, lng, lnb, ffn_weights):
    batch, seq, _ = x.shape
    tm = TOKEN_TILE
    tiles = seq // tm
    keeps = [min(win, seq) for win, _ in DIL_GROUPS]
    assert seq % tm == 0 and keeps[0] <= tm and keeps[1] <= tm and keeps[2] == seq
    row = lambda w: pl.BlockSpec((tm, w), lambda b, t: (b * tiles + t, 0))
    last = lambda keep: pl.BlockSpec((1, 2 * GROUP_W, keep), lambda b, t: (b, 0, 0))
    n = batch * seq
    slices, bf_shapes = _row_slices(ffn_weights, batch * tiles)
    slice_specs = [pl.BlockSpec(s, lambda b, t: (b * tiles + t, 0)) for s in slices]
    return pl.pallas_call(
        _proj_prompt_kernel,
        grid=(batch, tiles),
        in_specs=[pl.BlockSpec((1, tm, D_MODEL), lambda b, t: (b, t, 0))] + _proj_weight_specs() + slice_specs,
        out_specs=[pl.BlockSpec((1, N_GROUPS, 3 * SLABS, tm, LANES), lambda b, t: (b, 0, 0, t, 0)),
                   last(keeps[0]), last(keeps[1]),
                   pl.BlockSpec((1, 2 * GROUP_W, tm), lambda b, t: (b, 0, t)),
                   row(SG_W), row(SG_W), row(D_MODEL), row(D_MODEL)] + slice_specs,
        out_shape=[jax.ShapeDtypeStruct((batch, N_GROUPS, 3 * SLABS, seq, LANES), F32),
                   jax.ShapeDtypeStruct((batch, 2 * GROUP_W, keeps[0]), F32),
                   jax.ShapeDtypeStruct((batch, 2 * GROUP_W, keeps[1]), F32),
                   jax.ShapeDtypeStruct((batch, 2 * GROUP_W, keeps[2]), F32)] + _gate_branch_shapes(n, BF16)
        + bf_shapes,
        compiler_params=_params(2),
        name="proj_prompt",
    )(x, g, w_bf, bg, lng, lnb, *ffn_weights)


def _head_lane_mask(shape, hh):
    lane = lax.broadcasted_iota(jnp.int32, shape, len(shape) - 1)
    return (lane >> HEAD_SHIFT) == hh


def _band_block(q, k, v, mask):
    nt = (((1,), (1,)), ((), ()))
    assert HEADS_PER_SLAB == 2
    lm0 = _head_lane_mask((BLOCK, LANES), 0)
    mask2 = jnp.concatenate([mask, mask], axis=0)
    outs, lses = [], []
    for s in range(SLABS):
        zero = jnp.zeros_like(q[s])
        qm = jnp.concatenate([jnp.where(lm0, q[s], zero), jnp.where(lm0, zero, q[s])], axis=0)
        sc = lax.dot_general(qm, k[s], nt, preferred_element_type=F32)
        sc = jnp.where(mask2, sc, NEG)
        m = jnp.max(sc, axis=-1, keepdims=True)
        p = jnp.exp2(sc - m)
        l = jnp.sum(p, axis=-1, keepdims=True)
        o = jnp.dot(p.astype(BF16), v[s], preferred_element_type=F32) * (1.0 / l)
        lse = m + jnp.log2(l)
        outs.append(jnp.where(lm0, o[:BLOCK], o[BLOCK:]))
        lses.append(jnp.where(lm0, lse[:BLOCK], lse[BLOCK:]))
    return outs, lses


def _group_attention(qkv_ref, o_scr, lse_scr, g, dil, seq):
    n_qb = seq // (dil * BLOCK)
    qb_shift = n_qb.bit_length() - 1
    span = dil * BLOCK
    rows = (lambda start: pl.ds(start, BLOCK, stride=dil)) if dil > 1 else (lambda start: pl.ds(start, BLOCK))
    nk = 2 * BLOCK if n_qb > 1 else BLOCK
    qi = lax.broadcasted_iota(jnp.int32, (BLOCK, nk), 0)
    ki = lax.broadcasted_iota(jnp.int32, (BLOCK, nk), 1)

    def load(slab, start):
        return qkv_ref[0, 0, slab, rows(start), :]

    def body(n, carry):
        r = n >> qb_shift
        qb = n & (n_qb - 1)
        cur = r + span * qb
        if dil == 1:
            cur = pl.multiple_of(cur, BLOCK)
        q = [load(s, cur).astype(BF16) for s in range(SLABS)]
        k = [load(SLABS + s, cur).astype(BF16) for s in range(SLABS)]
        v = [load(2 * SLABS + s, cur).astype(BF16) for s in range(SLABS)]
        if n_qb > 1:
            prev = r + span * jnp.maximum(qb - 1, 0)
            if dil == 1:
                prev = pl.multiple_of(prev, BLOCK)
            k = [jnp.concatenate([load(SLABS + s, prev).astype(BF16), k[s]], axis=0) for s in range(SLABS)]
            v = [jnp.concatenate([load(2 * SLABS + s, prev).astype(BF16), v[s]], axis=0) for s in range(SLABS)]
            mask = (ki >= qi) & (ki <= qi + BLOCK) & (ki >= jnp.where(qb > 0, 0, BLOCK))
        else:
            mask = ki <= qi
        outs, lses = _band_block(q, k, v, mask)
        for s in range(SLABS):
            o_scr[g, s, rows(cur), :] = outs[s]
            lse_scr[g, s, rows(cur), :] = lses[s]
        return carry

    lax.fori_loop(0, dil * n_qb, body, 0, unroll=BLOCK_UNROLL)


def _prompt_attn_kernel(qkv_ref, o_ref, o_scr, lse_scr):
    g = pl.program_id(1)
    seq = o_ref.shape[1]
    for gi, (_, dil) in enumerate(DIL_GROUPS):
        @pl.when(g == gi)
        def _(gi=gi, dil=dil):
            _group_attention(qkv_ref, o_scr, lse_scr, gi, dil, seq)

    @pl.when(g == N_GROUPS - 1)
    def _():
        def merge(i, carry):
            rows = pl.ds(pl.multiple_of(i * BLOCK, BLOCK), BLOCK)
            for s in range(SLABS):
                ls = [lse_scr[gi, s, rows, :] for gi in range(N_GROUPS)]
                top = jnp.maximum(jnp.maximum(ls[0], ls[1]), ls[2])
                e = [jnp.exp2(x - top) for x in ls]
                num = e[0] * o_scr[0, s, rows, :] + e[1] * o_scr[1, s, rows, :] + e[2] * o_scr[2, s, rows, :]
                o_ref[0, rows, s * LANES:(s + 1) * LANES] = (num / (e[0] + e[1] + e[2])).astype(o_ref.dtype)
            return carry
        lax.fori_loop(0, seq // BLOCK, merge, 0, unroll=MERGE_UNROLL)


def _prompt_attention(qkv, seq):
    batch = qkv.shape[0]
    scratch = pltpu.VMEM((N_GROUPS, SLABS, seq, LANES), F32)
    return pl.pallas_call(
        _prompt_attn_kernel,
        grid=(batch, N_GROUPS),
        in_specs=[pl.BlockSpec((1, 1, 3 * SLABS, seq, LANES), lambda b, g: (b, g, 0, 0, 0))],
        out_specs=pl.BlockSpec((1, seq, GROUP_W), lambda b, g: (b, 0, 0)),
        out_shape=jax.ShapeDtypeStruct((batch, seq, GROUP_W), BF16),
        scratch_shapes=[scratch, scratch],
        compiler_params=_params(2),
        name="prompt_attn",
    )(qkv)


def _residue_allreduce(x, dil, op):
    shift = LANES // 2
    while shift >= dil:
        x = op(x, pltpu.roll(x, shift, axis=1))
        shift //= 2
    return x


def _lane_expand(rows8, dil, r0, t_new):
    assert r0 == 0 and t_new <= dil and (dil % T_PAD == 0 or T_PAD % dil == 0)
    if dil >= T_PAD:
        period = [rows8] + [jnp.zeros_like(rows8)] * (dil // T_PAD - 1)
    else:
        period = [jnp.concatenate([rows8[:dil]] * (T_PAD // dil), axis=0)]
    rows = jnp.concatenate(period * (LANES // max(dil, T_PAD)), axis=0)
    return rows.T


def _heads_to_rows(x):
    return jnp.broadcast_to(x[:, None, :], (HEADS, HEAD_DIM, LANES)).reshape(GROUP_W, LANES)


def _rows_to_heads_sum(x):
    return jnp.sum(x.reshape(HEADS, HEAD_DIM, LANES), axis=1)


def _buffer_group_lanes(c_ref, d, q8, kn8, vn8, dil, r0, t_new):
    width = c_ref.shape[2]
    tiles = width // LANES
    ql, kl, vl = (_lane_expand(a, dil, r0, t_new) for a in (q8, kn8, vn8))
    s = [_rows_to_heads_sum(c_ref[d, :GROUP_W, j * LANES:(j + 1) * LANES] * ql) for j in range(tiles)]
    s_new = _rows_to_heads_sum(kl * ql)
    top = s[0]
    for sj in s[1:]:
        top = jnp.maximum(top, sj)
    top = jnp.maximum(_residue_allreduce(top, dil, jnp.maximum), s_new)
    p = [jnp.exp(sj - top) for sj in s]
    p_new = jnp.exp(s_new - top)
    den = p[0]
    for pj in p[1:]:
        den = den + pj
    den = _residue_allreduce(den, dil, jnp.add) + p_new
    acc = c_ref[d, GROUP_W:, :LANES] * _heads_to_rows(p[0])
    for j in range(1, tiles):
        acc = acc + c_ref[d, GROUP_W:, j * LANES:(j + 1) * LANES] * _heads_to_rows(p[j])
    acc = (_residue_allreduce(acc, dil, jnp.add) + vl * _heads_to_rows(p_new)) * _heads_to_rows(1.0 / den)
    lse = _heads_to_rows(top + jnp.log(den))
    first = -(-r0 // dil) * dil
    return acc.T[first - r0:first - r0 + T_PAD], lse.T[first - r0:first - r0 + T_PAD]


def _buffer_group_dense(c_ref, d, q8, kn8, vn8, r0, t_new):
    rows = HEADS * T_PAD
    width = c_ref.shape[2]
    row = lax.broadcasted_iota(jnp.int32, (rows, GROUP_W), 0)
    lane = lax.broadcasted_iota(jnp.int32, (rows, GROUP_W), 1)
    head_sel = (row >> T_SHIFT) == (lane >> HEAD_SHIFT)
    t = (lax.broadcasted_iota(jnp.int32, (rows, width), 0) & (T_PAD - 1)) - r0
    i = lax.broadcasted_iota(jnp.int32, (rows, width), 1)
    tn = lax.broadcasted_iota(jnp.int32, (rows, T_PAD), 0) & (T_PAD - 1)
    j = lax.broadcasted_iota(jnp.int32, (rows, T_PAD), 1)
    nt = (((1,), (1,)), ((), ()))
    qm = jnp.where(head_sel, jnp.concatenate([q8] * HEADS, axis=0), 0.0).astype(BF16)
    kt = c_ref[d, :GROUP_W, :].astype(BF16)
    vt = c_ref[d, GROUP_W:, :].astype(BF16)
    sc = jnp.where(i >= t, jnp.dot(qm, kt, preferred_element_type=F32), NEG)
    own = (j <= tn) & (j >= r0) & (j < r0 + t_new)
    sn = jnp.where(own, lax.dot_general(qm, kn8.astype(BF16), nt, preferred_element_type=F32), NEG)
    m = jnp.maximum(jnp.max(sc, axis=-1, keepdims=True), jnp.max(sn, axis=-1, keepdims=True))
    pc = jnp.exp(sc - m)
    pn = jnp.exp(sn - m)
    l = jnp.sum(pc, axis=-1, keepdims=True) + jnp.sum(pn, axis=-1, keepdims=True)

    def finish():
        o = (lax.dot_general(pc.astype(BF16), vt, nt, preferred_element_type=F32)
             + jnp.dot(pn.astype(BF16), vn8.astype(BF16), preferred_element_type=F32))
        o = jnp.where(head_sel, o / l, 0.0)
        lse = jnp.where(head_sel, m + jnp.log(l), 0.0)
        fold = lambda x: sum(x[h * T_PAD:(h + 1) * T_PAD] for h in range(HEADS))
        return fold(o), fold(lse)
    return finish


def _sample_attn_start(qkv_ref, c1_ref, c2_ref, c3_ref, o_ref):
    n_db = c1_ref.shape[0]
    t_new = T_PAD // n_db
    assert qkv_ref.shape[0] == T_PAD and n_db * t_new == T_PAD
    pad_rows = jnp.zeros((T_PAD - t_new, qkv_ref.shape[1]), F32)
    pending = []
    for d in range(n_db):
        new_rows = jnp.concatenate([qkv_ref[d * t_new:(d + 1) * t_new, :], pad_rows], axis=0)
        groups = []
        for g, (c_ref, (_, dil)) in enumerate(zip((c1_ref, c2_ref, c3_ref), DIL_GROUPS)):
            cols = lambda part: slice(part * ATTN_W + g * GROUP_W, part * ATTN_W + (g + 1) * GROUP_W)
            q8 = new_rows[:, cols(0)] * SCALE
            kn8, vn8 = new_rows[:, cols(1)], new_rows[:, cols(2)]
            if dil == 1:
                groups.append(_buffer_group_dense(c_ref, d, q8, kn8, vn8, 0, t_new))
            else:
                result = _buffer_group_lanes(c_ref, d, q8, kn8, vn8, dil, 0, t_new)
                groups.append(lambda result=result: result)
        pending.append(groups)

    def finish():
        for d, groups in enumerate(pending):
            outs, lses = zip(*(group() for group in groups))
            top = jnp.maximum(jnp.maximum(lses[0], lses[1]), lses[2])
            w = [jnp.exp(x - top) for x in lses]
            o8 = (w[0] * outs[0] + w[1] * outs[1] + w[2] * outs[2]) / (w[0] + w[1] + w[2])
            o_ref[d * t_new:(d + 1) * t_new, :] = o8[:t_new]
    return finish


def _mix_value(o_a, o_b, ga_ref, gb_ref, x, wao_ref, wbo_ref, wout_ref, gpost_ref):
    a = jnp.dot(o_a.astype(BF16), wao_ref[...], preferred_element_type=F32)
    b = jnp.dot(o_b.astype(BF16), wbo_ref[...], preferred_element_type=F32)
    merged = ga_ref[...].astype(F32) * a + gb_ref[...].astype(F32) * b
    t = jnp.dot(merged.astype(BF16), wout_ref[...], preferred_element_type=F32)
    return x + _rms(t, gpost_ref[...])


def _ffn_value(x, gpre_ref, wup_ref, wdn_ref, gpost_ref):
    h = _rms(x, gpre_ref[...]).astype(BF16)
    f = jnp.zeros(x.shape, F32)
    for c in range(D_FF // FF_CHUNK):
        cs = slice(c * FF_CHUNK, (c + 1) * FF_CHUNK)
        u = jnp.maximum(jnp.dot(h, wup_ref[:, cs], preferred_element_type=F32), 0.0)
        f = f + jnp.dot((u * u).astype(BF16), wdn_ref[cs, :], preferred_element_type=F32)
    return x + _rms(f, gpost_ref[...])


def _prompt_spatial_gate(z1_ref, zn_ref, wsp_ref, bsp_ref):
    r = lax.broadcasted_iota(jnp.int32, (CHUNK, CHUNK), 0)
    c = lax.broadcasted_iota(jnp.int32, (CHUNK, CHUNK), 1)
    n_chunks = z1_ref.shape[0] // CHUNK
    mixed = []
    for g in range(SG_GROUPS):
        wt = jnp.where(r >= c, wsp_ref[g], 0.0).astype(BF16)
        zg = jnp.concatenate([zn_ref[ci * CHUNK:(ci + 1) * CHUNK, g * CHUNK:(g + 1) * CHUNK].astype(BF16)
                              for ci in range(n_chunks)], axis=1)
        mixed.append(jnp.dot(wt, zg, preferred_element_type=F32))
    chunks = [jnp.concatenate([m[:, ci * CHUNK:(ci + 1) * CHUNK] for m in mixed], axis=1) + bsp_ref[...]
              for ci in range(n_chunks)]
    return z1_ref[...].astype(F32) * jnp.concatenate(chunks, axis=0)


def _tail_prompt_kernel(oa_ref, z1_ref, zn_ref, ga_ref, gb_ref, x_ref, wsp_ref, bsp_ref,
                        wao_ref, wbo_ref, wout_ref, gmix_ref, gpre_ref, wup_ref, wdn_ref, gpost_ref,
                        qkv_ref, c1_ref, c2_ref, c3_ref, y_ref, o_ref):
    o_b = _prompt_spatial_gate(z1_ref, zn_ref, wsp_ref, bsp_ref)
    x1 = _mix_value(oa_ref[...], o_b, ga_ref, gb_ref, x_ref[...], wao_ref, wbo_ref, wout_ref, gmix_ref)
    finish_sample_attn = _sample_attn_start(qkv_ref, c1_ref, c2_ref, c3_ref, o_ref)
    y_ref[...] = _ffn_value(x1, gpre_ref, wup_ref, wdn_ref, gpost_ref)
    finish_sample_attn()


def _new_kv_feature_major(kn_ref, vn_ref, out_refs, scr):
    t_new, ndb = out_refs[0].shape[0], out_refs[0].shape[2]
    slabs = ATTN_W // LANES
    for part, src_ref in enumerate((kn_ref, vn_ref)):
        for slab in range(slabs):
            scr[part * slabs + slab] = src_ref[:, slab * LANES:(slab + 1) * LANES]
    for t in range(t_new):
        for part in range(2):
            for g, out_ref in enumerate(out_refs):
                for s in range(SLABS):
                    rows = scr[part * slabs + g * SLABS + s, pl.ds(t, ndb, stride=t_new), :]
                    out_ref[t, part * GROUP_W + s * LANES:part * GROUP_W + (s + 1) * LANES, :] = rows.T


def _tail_sample_kernel(oa_ref, z1_ref, zn_ref, ga_ref, gb_ref, x_ref, coef_ref, bsp_ref,
                        wao_ref, wbo_ref, wout_ref, gmix_ref, gpre_ref, wup_ref, wdn_ref, gpost_ref, kn_ref, vn_ref,
                        y_ref, kvs1_ref, kvs2_ref, kvs3_ref, x1_scr, h_scr, f_scr, kv_scr):
    c = pl.program_id(0)

    @pl.when(c == 0)
    def _():
        zn = zn_ref[...]
        periods = lambda a: a.reshape(a.shape[0] // T_PAD, T_PAD, a.shape[1])
        mix = coef_ref[0] * periods(zn)
        for d in range(1, coef_ref.shape[0]):
            mix = mix + coef_ref[d] * periods(pltpu.roll(zn, d, axis=0))
        o_b = z1_ref[...] * (mix + bsp_ref[...]).reshape(zn.shape)
        x = x_ref[...].reshape(zn.shape[0], D_MODEL)
        x1 = _mix_value(oa_ref[...], o_b, ga_ref, gb_ref, x, wao_ref, wbo_ref, wout_ref, gmix_ref)
        x1_scr[...] = x1
        h_scr[...] = _rms(x1, gpre_ref[...]).astype(BF16)
        f_scr[...] = jnp.zeros_like(f_scr)
        _new_kv_feature_major(kn_ref, vn_ref, (kvs1_ref, kvs2_ref, kvs3_ref), kv_scr)

    u = jnp.maximum(jnp.dot(h_scr[...], wup_ref[...], preferred_element_type=F32), 0.0)
    f_scr[...] += jnp.dot((u * u).astype(BF16), wdn_ref[...], preferred_element_type=F32)

    @pl.when(c == pl.num_programs(0) - 1)
    def _():
        y_ref[...] = (x1_scr[...] + _rms(f_scr[...], gpost_ref[...])).reshape(y_ref.shape)


def _tail_weight_specs():
    return [_const_spec((GROUP_W, D_MODEL)), _const_spec((SG_W, D_MODEL)), _const_spec((D_MODEL, D_MODEL)),
            _const_spec((1, D_MODEL)), _const_spec((1, D_MODEL)), _const_spec((D_MODEL, D_FF)),
            _const_spec((D_FF, D_MODEL)), _const_spec((1, D_MODEL))]


def _tail_prompt(o_a, z1, zn, ga, gb, x, wsp, bsp, weights, qkv_s, caches):
    n, db, ns = x.shape[0], caches[0].shape[0], qkv_s.shape[0]
    tile = SAMPLE_DB_TILE
    steps = db // tile
    tm = n // steps
    assert db % tile == 0 and n % steps == 0 and tm % CHUNK == 0 and ns == steps * T_PAD
    row = lambda w: pl.BlockSpec((tm, w), lambda i: (i, 0))
    blk = lambda a: pl.BlockSpec((tile,) + a.shape[1:], lambda i: (i, 0, 0))
    new_rows = lambda w: pl.BlockSpec((T_PAD, w), lambda i: (i, 0))
    return pl.pallas_call(
        _tail_prompt_kernel,
        grid=(steps,),
        in_specs=[row(GROUP_W), row(SG_W), row(SG_W), row(D_MODEL), row(D_MODEL), row(D_MODEL),
                  _const_spec((SG_GROUPS, CHUNK, CHUNK)), _const_spec((CHUNK, SG_W))] + _tail_weight_specs()
        + [new_rows(Z_OFF)] + [blk(a) for a in caches],
        out_specs=[row(D_MODEL), new_rows(GROUP_W)],
        out_shape=[jax.ShapeDtypeStruct((n, D_MODEL), F32),
                   jax.ShapeDtypeStruct((ns, GROUP_W), F32)],
        compiler_params=_params(1),
        name="tail_prompt",
    )(o_a, z1, zn, ga, gb, x, wsp, bsp, *weights, qkv_s, *caches)


def _tail_sample(o_a, z1, zn, ga, gb, x3, coef, bsp, weights, qkv):
    n = x3.shape[0] * x3.shape[1]
    wao, wbo, wout, gmix, gpre, wup, wdn, gpost = weights
    full = lambda a: _const_spec(a.shape)
    kv_new = jax.ShapeDtypeStruct((x3.shape[1], 2 * GROUP_W, x3.shape[0]), F32)
    qkv_part = lambda part: pl.BlockSpec((n, ATTN_W), lambda c: (0, part), pipeline_mode=pl.Buffered(1))
    return pl.pallas_call(
        _tail_sample_kernel,
        grid=(D_FF // FF_CHUNK,),
        in_specs=[full(o_a), full(z1), full(zn), full(ga), full(gb), full(x3), full(coef), full(bsp),
                  full(wao), full(wbo), full(wout), full(gmix), full(gpre),
                  pl.BlockSpec((D_MODEL, FF_CHUNK), lambda c: (0, c)),
                  pl.BlockSpec((FF_CHUNK, D_MODEL), lambda c: (c, 0)), full(gpost), qkv_part(1), qkv_part(2)],
        out_specs=[pl.BlockSpec(x3.shape, lambda c: (0, 0, 0))]
        + [pl.BlockSpec(kv_new.shape, lambda c: (0, 0, 0))] * N_GROUPS,
        out_shape=[jax.ShapeDtypeStruct(x3.shape, F32)] + [kv_new] * N_GROUPS,
        scratch_shapes=[pltpu.VMEM((n, D_MODEL), F32), pltpu.VMEM((n, D_MODEL), BF16), pltpu.VMEM((n, D_MODEL), F32),
                        pltpu.VMEM((2 * ATTN_W // LANES, n, LANES), F32)],
        compiler_params=_params(1),
        name="tail_sample",
    )(o_a, z1, zn, ga, gb, x3, coef, bsp, *weights, qkv, qkv)


def _kv_from_feature_major(kv_t):
    batch, _, keep = kv_t.shape
    return kv_t.reshape(batch, 2, HEADS, HEAD_DIM, keep).transpose(0, 4, 1, 2, 3)[None]


def kernel(x_prompt, x_sample, cache_kv_w128, cache_kv_w512, cache_kv_w2048, norm_pre_mix, w_in, b_gate,
           ln_z_g, ln_z_b, w_spatial, b_spatial, w_ao, w_bo, w_out, norm_post_mix, norm_pre_ffn, w_up, w_down,
           norm_post_ffn):
    assert w_in.shape[0] == 1, "single-layer problem"
    batch, seq, _ = x_prompt.shape
    dbatch, t_new, _ = x_sample.shape
    caches_in = (cache_kv_w128[0], cache_kv_w512[0], cache_kv_w2048[0])

    wsp = w_spatial[0]
    bsp_rows = jnp.repeat(b_spatial[0].T, CHUNK, axis=1)

    ns = dbatch * t_new
    qkv_s, z1_s, zn_s, ga_s, gb_s, w_in_bf, wao, wbo, wout = _project_sample(
        x_sample, norm_pre_mix, w_in[0], b_gate, ln_z_g, ln_z_b, (w_ao[0], w_bo[0], w_out[0]))
    caches = []
    for gi, (win, dil) in enumerate(DIL_GROUPS):
        buf = caches_in[gi]
        wb = buf.shape[1]
        assert wb == win and t_new <= T_PAD, "full window buffers"
        caches.append(buf.transpose(0, 2, 3, 4, 1).reshape(dbatch, 2 * GROUP_W, wb))

    xp = x_prompt.reshape(batch * seq, D_MODEL)
    qkv, kv1_t, kv2_t, kv3_t, z1, zn, ga, gb, wup, wdn = _project_prompt(
        x_prompt, norm_pre_mix, w_in_bf, b_gate, ln_z_g, ln_z_b, (w_up[0], w_down[0]))
    weights = (wao, wbo, wout, norm_post_mix, norm_pre_ffn, wup, wdn, norm_post_ffn)
    kv_prompt = [_kv_from_feature_major(a) for a in (kv1_t, kv2_t, kv3_t)]
    o_a_p = _prompt_attention(qkv, seq).reshape(batch * seq, GROUP_W)
    assert SAMPLE_DB_TILE * t_new == T_PAD, "a tail step takes T_PAD new rows"
    y_prompt, o_a = _tail_prompt(o_a_p, z1, zn, ga, gb, xp, wsp, bsp_rows, weights, qkv_s, caches)
    y_prompt = y_prompt.reshape(batch, seq, D_MODEL)

    wt = wsp[:, :t_new, :t_new]
    t_idx = jnp.arange(t_new)
    shifted = (t_idx[None, :, None] - t_idx[:, None, None] == t_idx[None, None, :]).astype(F32)
    coef = (wt[None] * shifted[:, None]).sum(-1).transpose(0, 2, 1)
    reps = T_PAD // t_new
    coef = jnp.tile(jnp.repeat(coef, CHUNK, axis=2), (1, reps, 1))
    bsp_s = jnp.tile(bsp_rows[:t_new], (reps, 1))
    y_sample, *kv_new = _tail_sample(o_a, z1_s, zn_s, ga_s, gb_s, x_sample, coef, bsp_s, weights, qkv_s)
    kv_sample = [a.reshape(t_new, 2, HEADS, HEAD_DIM, dbatch).transpose(4, 0, 1, 2, 3)[None] for a in kv_new]
    sg_sample = zn_s.reshape(1, dbatch, t_new, SG_W)

    return (y_prompt, y_sample, kv_prompt[0], kv_prompt[1], kv_prompt[2],
            kv_sample[0], kv_sample[1], kv_sample[2], sg_sample)
```

```python
import math

import jax
import jax.numpy as jnp
from jax import lax
from jax.experimental import pallas as pl
from jax.experimental.pallas import tpu as pltpu

F32 = jnp.float32
BF16 = jnp.bfloat16

D_MODEL = 1024
HEAD_DIM = 64
HEADS = 4
GROUP_W = HEADS * HEAD_DIM
DIL_GROUPS = ((128, 1), (512, 4), (2048, 16))
N_GROUPS = len(DIL_GROUPS)
ATTN_W = N_GROUPS * GROUP_W
BLOCK = 128
CHUNK = 128
SG_GROUPS = 4
SG_W = 512
D_FF = 4 * D_MODEL
Z_OFF = 3 * ATTN_W
G_OFF = Z_OFF + 2 * SG_W
IN_W = G_OFF + 2 * D_MODEL
EPS = 1e-6
NEG = -1e30
SCALE = HEAD_DIM ** -0.5
INV_SQRT2 = 1.0 / math.sqrt(2.0)
LOG2E = 1.0 / math.log(2.0)
LANES = 128
SLABS = GROUP_W // LANES
HEADS_PER_SLAB = LANES // HEAD_DIM
HEAD_SHIFT = HEAD_DIM.bit_length() - 1
VMEM_LIMIT = 56 * 1024 * 1024

TOKEN_TILE = 512
MERGE_UNROLL = 4
BLOCK_UNROLL = 16
SAMPLE_DB_TILE = 2
T_PAD = 8
T_SHIFT = T_PAD.bit_length() - 1
BF16_SUBLANES = 16
CAST_COLS = 768
SAMPLE_TAIL_TILE = 256
FF_CHUNK = 1024


def _params(n_axes):
    return pltpu.CompilerParams(dimension_semantics=("arbitrary",) * n_axes,
                                vmem_limit_bytes=VMEM_LIMIT)


def _const_spec(shape):
    return pl.BlockSpec(shape, lambda *_: (0,) * len(shape), pipeline_mode=pl.Buffered(1))


def _rms(x, g):
    return x * lax.rsqrt(jnp.mean(x * x, axis=-1, keepdims=True) + EPS) * g


def _proj_gate_branches(h, w_ref, bg_ref, lng_ref, lnb_ref, z1_ref, zn_ref, ga_ref, gb_ref):
    z = jnp.dot(h, w_ref[:, Z_OFF:G_OFF], preferred_element_type=F32)
    z = 0.5 * z * (1.0 + lax.erf(z * INV_SQRT2))
    z1_ref[...] = z[:, :SG_W].astype(z1_ref.dtype)
    z2 = z[:, SG_W:]
    mu = jnp.mean(z2, axis=-1, keepdims=True)
    zc = z2 - mu
    var = jnp.mean(zc * zc, axis=-1, keepdims=True)
    zn = (zc * lax.rsqrt(var + EPS) * lng_ref[...] + lnb_ref[...]).astype(zn_ref.dtype)
    zn_ref[...] = zn.reshape(zn_ref.shape)

    gates = jnp.dot(h, w_ref[:, G_OFF:], preferred_element_type=F32) + bg_ref[...]
    gates = 1.0 / (1.0 + jnp.exp(-gates))
    ga_ref[...] = gates[:, :D_MODEL].astype(BF16)
    gb_ref[...] = gates[:, D_MODEL:].astype(BF16)


def _cast_blocks(pairs):
    for src_ref, dst_ref in pairs:
        dst_ref[...] = src_ref[...].astype(dst_ref.dtype)


def _proj_sample_kernel(x_ref, g_ref, wf_ref, bg_ref, lng_ref, lnb_ref, wao_ref, wbo_ref, wout_ref,
                        qkv_ref, z1_ref, zn_ref, ga_ref, gb_ref, w_ref, wao_bf_ref, wbo_bf_ref, wout_bf_ref):
    @pl.when(pl.program_id(0) == 0)
    def _():
        for c in range(0, IN_W, CAST_COLS):
            w_ref[:, c:c + CAST_COLS] = wf_ref[:, c:c + CAST_COLS].astype(BF16)

    _cast_blocks(((wao_ref, wao_bf_ref), (wbo_ref, wbo_bf_ref), (wout_ref, wout_bf_ref)))
    x = x_ref[...]
    h = _rms(x.reshape(x.shape[0] * x.shape[1], x.shape[2]), g_ref[...]).astype(BF16)
    qkv_ref[...] = jnp.dot(h, w_ref[:, :Z_OFF], preferred_element_type=F32)
    _proj_gate_branches(h, w_ref, bg_ref, lng_ref, lnb_ref, z1_ref, zn_ref, ga_ref, gb_ref)


def _proj_prompt_kernel(x_ref, g_ref, w_ref, bg_ref, lng_ref, lnb_ref, wup_ref, wdn_ref,
                        qkv_ref, kv1_ref, kv2_ref, kv3_ref, z1_ref, zn_ref, ga_ref, gb_ref, wup_bf_ref, wdn_bf_ref):
    _cast_blocks(((wup_ref, wup_bf_ref), (wdn_ref, wdn_bf_ref)))
    h = _rms(x_ref[0], g_ref[...]).astype(BF16)
    _proj_gate_branches(h, w_ref, bg_ref, lng_ref, lnb_ref, z1_ref, zn_ref, ga_ref, gb_ref)
    def project(first_part, n_parts, scale=None):
        cols = jnp.dot(h, w_ref[:, first_part * ATTN_W:(first_part + n_parts) * ATTN_W], preferred_element_type=F32)
        if scale is not None:
            cols = cols * scale
        for part in range(n_parts):
            for g in range(N_GROUPS):
                for s in range(SLABS):
                    col = part * ATTN_W + g * GROUP_W + s * LANES
                    qkv_ref[0, g, (first_part + part) * SLABS + s] = cols[:, col:col + LANES]

    def kv_t(g):
        return jnp.concatenate([qkv_ref[0, g, slab].T for slab in range(SLABS, 3 * SLABS)], axis=0)

    tm = x_ref.shape[1]
    project(1, 2)
    kv3_ref[0] = kv_t(2)
    project(0, 1, scale=SCALE * LOG2E)

    @pl.when(pl.program_id(1) == pl.num_programs(1) - 1)
    def _():
        kv2_ref[0] = kv_t(1)[:, tm - kv2_ref.shape[2]:]
        kv1_ref[0] = kv_t(0)[:, tm - kv1_ref.shape[2]:]


def _proj_weight_specs():
    return [_const_spec((1, D_MODEL)), _const_spec((D_MODEL, IN_W)),
            _const_spec((1, 2 * D_MODEL)), _const_spec((1, SG_W)), _const_spec((1, SG_W))]


def _gate_branch_shapes(n, z_dtype):
    return [jax.ShapeDtypeStruct((n, SG_W), z_dtype), jax.ShapeDtypeStruct((n, SG_W), z_dtype),
            jax.ShapeDtypeStruct((n, D_MODEL), BF16), jax.ShapeDtypeStruct((n, D_MODEL), BF16)]


def _row_slices(arrays, steps):
    specs, shapes = [], []
    for a in arrays:
        rows = a.shape[0] // steps
        assert rows * steps == a.shape[0] and rows % BF16_SUBLANES == 0
        specs.append((rows, a.shape[1]))
        shapes.append(jax.ShapeDtypeStruct(a.shape, BF16))
    return specs, shapes


def _project_sample(x3, g, w_in, bg, lng, lnb, mix_weights):
    n = x3.shape[0] * x3.shape[1]
    tm = min(SAMPLE_TAIL_TILE, n)
    steps = n // tm
    row = lambda w: pl.BlockSpec((tm, w), lambda i: (i, 0))
    slices, bf_shapes = _row_slices(mix_weights, steps)
    slice_specs = [pl.BlockSpec(s, lambda i: (i, 0)) for s in slices]
    z1_shape, _, ga_shape, gb_shape = _gate_branch_shapes(n, F32)
    return pl.pallas_call(
        _proj_sample_kernel,
        grid=(steps,),
        in_specs=[pl.BlockSpec((tm // x3.shape[1],) + x3.shape[1:], lambda i: (i, 0, 0))] + _proj_weight_specs()
        + slice_specs,
        out_specs=[row(Z_OFF), row(SG_W), pl.BlockSpec((tm // x3.shape[1], x3.shape[1], SG_W), lambda i: (i, 0, 0)),
                   row(D_MODEL), row(D_MODEL), pl.BlockSpec((D_MODEL, IN_W), lambda i: (0, 0))] + slice_specs,
        out_shape=[jax.ShapeDtypeStruct((n, Z_OFF), F32), z1_shape, jax.ShapeDtypeStruct(x3.shape[:2] + (SG_W,), F32),
                   ga_shape, gb_shape, jax.ShapeDtypeStruct((D_MODEL, IN_W), BF16)] + bf_shapes,
        compiler_params=_params(1),
        name="proj_sample",
    )(x3, g, w_in, bg, lng, lnb, *mix_weights)


def _project_prompt(x, g, w_bf, bg, lng, lnb, ffn_weights):
    batch, seq, _ = x.shape
    tm = TOKEN_TILE
    tiles = seq // tm
    keeps = [min(win, seq) for win, _ in DIL_GROUPS]
    assert seq % tm == 0 and keeps[0] <= tm and keeps[1] <= tm and keeps[2] == seq
    row = lambda w: pl.BlockSpec((tm, w), lambda b, t: (b * tiles + t, 0))
    last = lambda keep: pl.BlockSpec((1, 2 * GROUP_W, keep), lambda b, t: (b, 0, 0))
    n = batch * seq
    slices, bf_shapes = _row_slices(ffn_weights, batch * tiles)
    slice_specs = [pl.BlockSpec(s, lambda b, t: (b * tiles + t, 0)) for s in slices]
    return pl.pallas_call(
        _proj_prompt_kernel,
        grid=(batch, tiles),
        in_specs=[pl.BlockSpec((1, tm, D_MODEL), lambda b, t: (b, t, 0))] + _proj_weight_specs() + slice_specs,
        out_specs=[pl.BlockSpec((1, N_GROUPS, 3 * SLABS, tm, LANES), lambda b, t: (b, 0, 0, t, 0)),
                   last(keeps[0]), last(keeps[1]),
                   pl.BlockSpec((1, 2 * GROUP_W, tm), lambda b, t: (b, 0, t)),
                   row(SG_W), row(SG_W), row(D_MODEL), row(D_MODEL)] + slice_specs,
        out_shape=[jax.ShapeDtypeStruct((batch, N_GROUPS, 3 * SLABS, seq, LANES), F32),
                   jax.ShapeDtypeStruct((batch, 2 * GROUP_W, keeps[0]), F32),
                   jax.ShapeDtypeStruct((batch, 2 * GROUP_W, keeps[1]), F32),
                   jax.ShapeDtypeStruct((batch, 2 * GROUP_W, keeps[2]), F32)] + _gate_branch_shapes(n, BF16)
        + bf_shapes,
        compiler_params=_params(2),
        name="proj_prompt",
    )(x, g, w_bf, bg, lng, lnb, *ffn_weights)


def _head_lane_mask(shape, hh):
    lane = lax.broadcasted_iota(jnp.int32, shape, len(shape) - 1)
    return (lane >> HEAD_SHIFT) == hh


def _band_block(q, k, v, mask):
    nt = (((1,), (1,)), ((), ()))
    assert HEADS_PER_SLAB == 2
    lm0 = _head_lane_mask((BLOCK, LANES), 0)
    mask2 = jnp.concatenate([mask, mask], axis=0)
    outs, lses = [], []
    for s in range(SLABS):
        zero = jnp.zeros_like(q[s])
        qm = jnp.concatenate([jnp.where(lm0, q[s], zero), jnp.where(lm0, zero, q[s])], axis=0)
        sc = lax.dot_general(qm, k[s], nt, preferred_element_type=F32)
        sc = jnp.where(mask2, sc, NEG)
        m = jnp.max(sc, axis=-1, keepdims=True)
        p = jnp.exp2(sc - m)
        l = jnp.sum(p, axis=-1, keepdims=True)
        o = jnp.dot(p.astype(BF16), v[s], preferred_element_type=F32) * (1.0 / l)
        lse = m + jnp.log2(l)
        outs.append(jnp.where(lm0, o[:BLOCK], o[BLOCK:]))
        lses.append(jnp.where(lm0, lse[:BLOCK], lse[BLOCK:]))
    return outs, lses


def _group_attention(qkv_ref, o_scr, lse_scr, g, dil, seq):
    n_qb = seq // (dil * BLOCK)
    qb_shift = n_qb.bit_length() - 1
    span = dil * BLOCK
    rows = (lambda start: pl.ds(start, BLOCK, stride=dil)) if dil > 1 else (lambda start: pl.ds(start, BLOCK))
    nk = 2 * BLOCK if n_qb > 1 else BLOCK
    qi = lax.broadcasted_iota(jnp.int32, (BLOCK, nk), 0)
    ki = lax.broadcasted_iota(jnp.int32, (BLOCK, nk), 1)

    def load(slab, start):
        return qkv_ref[0, 0, slab, rows(start), :]

    def body(n, carry):
        r = n >> qb_shift
        qb = n & (n_qb - 1)
        cur = r + span * qb
        if dil == 1:
            cur = pl.multiple_of(cur, BLOCK)
        q = [load(s, cur).astype(BF16) for s in range(SLABS)]
        k = [load(SLABS + s, cur).astype(BF16) for s in range(SLABS)]
        v = [load(2 * SLABS + s, cur).astype(BF16) for s in range(SLABS)]
        if n_qb > 1:
            prev = r + span * jnp.maximum(qb - 1, 0)
            if dil == 1:
                prev = pl.multiple_of(prev, BLOCK)
            k = [jnp.concatenate([load(SLABS + s, prev).astype(BF16), k[s]], axis=0) for s in range(SLABS)]
            v = [jnp.concatenate([load(2 * SLABS + s, prev).astype(BF16), v[s]], axis=0) for s in range(SLABS)]
            mask = (ki >= qi) & (ki <= qi + BLOCK) & (ki >= jnp.where(qb > 0, 0, BLOCK))
        else:
            mask = ki <= qi
        outs, lses = _band_block(q, k, v, mask)
        for s in range(SLABS):
            o_scr[g, s, rows(cur), :] = outs[s]
            lse_scr[g, s, rows(cur), :] = lses[s]
        return carry

    lax.fori_loop(0, dil * n_qb, body, 0, unroll=BLOCK_UNROLL)


def _prompt_attn_kernel(qkv_ref, o_ref, o_scr, lse_scr):
    g = pl.program_id(1)
    seq = o_ref.shape[1]
    for gi, (_, dil) in enumerate(DIL_GROUPS):
        @pl.when(g == gi)
        def _(gi=gi, dil=dil):
            _group_attention(qkv_ref, o_scr, lse_scr, gi, dil, seq)

    @pl.when(g == N_GROUPS - 1)
    def _():
        def merge(i, carry):
            rows = pl.ds(pl.multiple_of(i * BLOCK, BLOCK), BLOCK)
            for s in range(SLABS):
                ls = [lse_scr[gi, s, rows, :] for gi in range(N_GROUPS)]
                top = jnp.maximum(jnp.maximum(ls[0], ls[1]), ls[2])
                e = [jnp.exp2(x - top) for x in ls]
                num = e[0] * o_scr[0, s, rows, :] + e[1] * o_scr[1, s, rows, :] + e[2] * o_scr[2, s, rows, :]
                o_ref[0, rows, s * LANES:(s + 1) * LANES] = (num / (e[0] + e[1] + e[2])).astype(o_ref.dtype)
            return carry
        lax.fori_loop(0, seq // BLOCK, merge, 0, unroll=MERGE_UNROLL)


def _prompt_attention(qkv, seq):
    batch = qkv.shape[0]
    scratch = pltpu.VMEM((N_GROUPS, SLABS, seq, LANES), F32)
    return pl.pallas_call(
        _prompt_attn_kernel,
        grid=(batch, N_GROUPS),
        in_specs=[pl.BlockSpec((1, 1, 3 * SLABS, seq, LANES), lambda b, g: (b, g, 0, 0, 0))],
        out_specs=pl.BlockSpec((1, seq, GROUP_W), lambda b, g: (b, 0, 0)),
        out_shape=jax.ShapeDtypeStruct((batch, seq, GROUP_W), BF16),
        scratch_shapes=[scratch, scratch],
        compiler_params=_params(2),
        name="prompt_attn",
    )(qkv)


def _residue_allreduce(x, dil, op):
    shift = LANES // 2
    while shift >= dil:
        x = op(x, pltpu.roll(x, shift, axis=1))
        shift //= 2
    return x


def _lane_expand(rows8, dil, r0, t_new):
    assert r0 == 0 and t_new <= dil and (dil % T_PAD == 0 or T_PAD % dil == 0)
    if dil >= T_PAD:
        period = [rows8] + [jnp.zeros_like(rows8)] * (dil // T_PAD - 1)
    else:
        period = [jnp.concatenate([rows8[:dil]] * (T_PAD // dil), axis=0)]
    rows = jnp.concatenate(period * (LANES // max(dil, T_PAD)), axis=0)
    return rows.T


def _heads_to_rows(x):
    return jnp.broadcast_to(x[:, None, :], (HEADS, HEAD_DIM, LANES)).reshape(GROUP_W, LANES)


def _rows_to_heads_sum(x):
    return jnp.sum(x.reshape(HEADS, HEAD_DIM, LANES), axis=1)


def _buffer_group_lanes(c_ref, d, q8, kn8, vn8, dil, r0, t_new):
    width = c_ref.shape[2]
    tiles = width // LANES
    ql, kl, vl = (_lane_expand(a, dil, r0, t_new) for a in (q8, kn8, vn8))
    s = [_rows_to_heads_sum(c_ref[d, :GROUP_W, j * LANES:(j + 1) * LANES] * ql) for j in range(tiles)]
    s_new = _rows_to_heads_sum(kl * ql)
    top = s[0]
    for sj in s[1:]:
        top = jnp.maximum(top, sj)
    top = jnp.maximum(_residue_allreduce(top, dil, jnp.maximum), s_new)
    p = [jnp.exp(sj - top) for sj in s]
    p_new = jnp.exp(s_new - top)
    den = p[0]
    for pj in p[1:]:
        den = den + pj
    den = _residue_allreduce(den, dil, jnp.add) + p_new
    acc = c_ref[d, GROUP_W:, :LANES] * _heads_to_rows(p[0])
    for j in range(1, tiles):
        acc = acc + c_ref[d, GROUP_W:, j * LANES:(j + 1) * LANES] * _heads_to_rows(p[j])
    acc = (_residue_allreduce(acc, dil, jnp.add) + vl * _heads_to_rows(p_new)) * _heads_to_rows(1.0 / den)
    lse = _heads_to_rows(top + jnp.log(den))
    first = -(-r0 // dil) * dil
    return acc.T[first - r0:first - r0 + T_PAD], lse.T[first - r0:first - r0 + T_PAD]


def _buffer_group_dense(c_ref, d, q8, kn8, vn8, r0, t_new):
    rows = HEADS * T_PAD
    width = c_ref.shape[2]
    row = lax.broadcasted_iota(jnp.int32, (rows, GROUP_W), 0)
    lane = lax.broadcasted_iota(jnp.int32, (rows, GROUP_W), 1)
    head_sel = (row >> T_SHIFT) == (lane >> HEAD_SHIFT)
    t = (lax.broadcasted_iota(jnp.int32, (rows, width), 0) & (T_PAD - 1)) - r0
    i = lax.broadcasted_iota(jnp.int32, (rows, width), 1)
    tn = lax.broadcasted_iota(jnp.int32, (rows, T_PAD), 0) & (T_PAD - 1)
    j = lax.broadcasted_iota(jnp.int32, (rows, T_PAD), 1)
    nt = (((1,), (1,)), ((), ()))
    qm = jnp.where(head_sel, jnp.concatenate([q8] * HEADS, axis=0), 0.0).astype(BF16)
    kt = c_ref[d, :GROUP_W, :].astype(BF16)
    vt = c_ref[d, GROUP_W:, :].astype(BF16)
    sc = jnp.where(i >= t, jnp.dot(qm, kt, preferred_element_type=F32), NEG)
    own = (j <= tn) & (j >= r0) & (j < r0 + t_new)
    sn = jnp.where(own, lax.dot_general(qm, kn8.astype(BF16), nt, preferred_element_type=F32), NEG)
    m = jnp.maximum(jnp.max(sc, axis=-1, keepdims=True), jnp.max(sn, axis=-1, keepdims=True))
    pc = jnp.exp(sc - m)
    pn = jnp.exp(sn - m)
    l = jnp.sum(pc, axis=-1, keepdims=True) + jnp.sum(pn, axis=-1, keepdims=True)

    def finish():
        o = (lax.dot_general(pc.astype(BF16), vt, nt, preferred_element_type=F32)
             + jnp.dot(pn.astype(BF16), vn8.astype(BF16), preferred_element_type=F32))
        o = jnp.where(head_sel, o / l, 0.0)
        lse = jnp.where(head_sel, m + jnp.log(l), 0.0)
        fold = lambda x: sum(x[h * T_PAD:(h + 1) * T_PAD] for h in range(HEADS))
        return fold(o), fold(lse)
    return finish


def _sample_attn_start(qkv_ref, c1_ref, c2_ref, c3_ref, o_ref):
    n_db = c1_ref.shape[0]
    t_new = T_PAD // n_db
    assert qkv_ref.shape[0] == T_PAD and n_db * t_new == T_PAD
    pad_rows = jnp.zeros((T_PAD - t_new, qkv_ref.shape[1]), F32)
    pending = []
    for d in range(n_db):
        new_rows = jnp.concatenate([qkv_ref[d * t_new:(d + 1) * t_new, :], pad_rows], axis=0)
        groups = []
        for g, (c_ref, (_, dil)) in enumerate(zip((c1_ref, c2_ref, c3_ref), DIL_GROUPS)):
            cols = lambda part: slice(part * ATTN_W + g * GROUP_W, part * ATTN_W + (g + 1) * GROUP_W)
            q8 = new_rows[:, cols(0)] * SCALE
            kn8, vn8 = new_rows[:, cols(1)], new_rows[:, cols(2)]
            if dil == 1:
                groups.append(_buffer_group_dense(c_ref, d, q8, kn8, vn8, 0, t_new))
            else:
                result = _buffer_group_lanes(c_ref, d, q8, kn8, vn8, dil, 0, t_new)
                groups.append(lambda result=result: result)
        pending.append(groups)

    def finish():
        for d, groups in enumerate(pending):
            outs, lses = zip(*(group() for group in groups))
            top = jnp.maximum(jnp.maximum(lses[0], lses[1]), lses[2])
            w = [jnp.exp(x - top) for x in lses]
            o8 = (w[0] * outs[0] + w[1] * outs[1] + w[2] * outs[2]) / (w[0] + w[1] + w[2])
            o_ref[d * t_new:(d + 1) * t_new, :] = o8[:t_new]
    return finish


def _mix_value(o_a, o_b, ga_ref, gb_ref, x, wao_ref, wbo_ref, wout_ref, gpost_ref):
    a = jnp.dot(o_a.astype(BF16), wao_ref[...], preferred_element_type=F32)
    b = jnp.dot(o_b.astype(BF16), wbo_ref[...], preferred_element_type=F32)
    merged = ga_ref[...].astype(F32) * a + gb_ref[...].astype(F32) * b
    t = jnp.dot(merged.astype(BF16), wout_ref[...], preferred_element_type=F32)
    return x + _rms(t, gpost_ref[...])


def _ffn_value(x, gpre_ref, wup_ref, wdn_ref, gpost_ref):
    h = _rms(x, gpre_ref[...]).astype(BF16)
    f = jnp.zeros(x.shape, F32)
    for c in range(D_FF // FF_CHUNK):
        cs = slice(c * FF_CHUNK, (c + 1) * FF_CHUNK)
        u = jnp.maximum(jnp.dot(h, wup_ref[:, cs], preferred_element_type=F32), 0.0)
        f = f + jnp.dot((u * u).astype(BF16), wdn_ref[cs, :], preferred_element_type=F32)
    return x + _rms(f, gpost_ref[...])


def _prompt_spatial_gate(z1_ref, zn_ref, wsp_ref, bsp_ref):
    r = lax.broadcasted_iota(jnp.int32, (CHUNK, CHUNK), 0)
    c = lax.broadcasted_iota(jnp.int32, (CHUNK, CHUNK), 1)
    n_chunks = z1_ref.shape[0] // CHUNK
    mixed = []
    for g in range(SG_GROUPS):
        wt = jnp.where(r >= c, wsp_ref[g], 0.0).astype(BF16)
        zg = jnp.concatenate([zn_ref[ci * CHUNK:(ci + 1) * CHUNK, g * CHUNK:(g + 1) * CHUNK].astype(BF16)
                              for ci in range(n_chunks)], axis=1)
        mixed.append(jnp.dot(wt, zg, preferred_element_type=F32))
    chunks = [jnp.concatenate([m[:, ci * CHUNK:(ci + 1) * CHUNK] for m in mixed], axis=1) + bsp_ref[...]
              for ci in range(n_chunks)]
    return z1_ref[...].astype(F32) * jnp.concatenate(chunks, axis=0)


def _tail_prompt_kernel(oa_ref, z1_ref, zn_ref, ga_ref, gb_ref, x_ref, wsp_ref, bsp_ref,
                        wao_ref, wbo_ref, wout_ref, gmix_ref, gpre_ref, wup_ref, wdn_ref, gpost_ref,
                        qkv_ref, c1_ref, c2_ref, c3_ref, y_ref, o_ref):
    o_b = _prompt_spatial_gate(z1_ref, zn_ref, wsp_ref, bsp_ref)
    x1 = _mix_value(oa_ref[...], o_b, ga_ref, gb_ref, x_ref[...], wao_ref, wbo_ref, wout_ref, gmix_ref)
    finish_sample_attn = _sample_attn_start(qkv_ref, c1_ref, c2_ref, c3_ref, o_ref)
    y_ref[...] = _ffn_value(x1, gpre_ref, wup_ref, wdn_ref, gpost_ref)
    finish_sample_attn()


def _new_kv_feature_major(kn_ref, vn_ref, out_refs, scr):
    t_new, ndb = out_refs[0].shape[0], out_refs[0].shape[2]
    slabs = ATTN_W // LANES
    for part, src_ref in enumerate((kn_ref, vn_ref)):
        for slab in range(slabs):
            scr[part * slabs + slab] = src_ref[:, slab * LANES:(slab + 1) * LANES]
    for t in range(t_new):
        for part in range(2):
            for g, out_ref in enumerate(out_refs):
                for s in range(SLABS):
                    rows = scr[part * slabs + g * SLABS + s, pl.ds(t, ndb, stride=t_new), :]
                    out_ref[t, part * GROUP_W + s * LANES:part * GROUP_W + (s + 1) * LANES, :] = rows.T


def _tail_sample_kernel(oa_ref, z1_ref, zn_ref, ga_ref, gb_ref, x_ref, coef_ref, bsp_ref,
                        wao_ref, wbo_ref, wout_ref, gmix_ref, gpre_ref, wup_ref, wdn_ref, gpost_ref, kn_ref, vn_ref,
                        y_ref, kvs1_ref, kvs2_ref, kvs3_ref, x1_scr, h_scr, f_scr, kv_scr):
    c = pl.program_id(0)

    @pl.when(c == 0)
    def _():
        zn = zn_ref[...].reshape(z1_ref.shape)
        periods = lambda a: a.reshape(a.shape[0] // T_PAD, T_PAD, a.shape[1])
        mix = coef_ref[0] * periods(zn)
        for d in range(1, coef_ref.shape[0]):
            mix = mix + coef_ref[d] * periods(pltpu.roll(zn, d, axis=0))
        o_b = z1_ref[...] * (mix + bsp_ref[...]).reshape(zn.shape)
        x = x_ref[...].reshape(zn.shape[0], D_MODEL)
        x1 = _mix_value(oa_ref[...], o_b, ga_ref, gb_ref, x, wao_ref, wbo_ref, wout_ref, gmix_ref)
        x1_scr[...] = x1
        h_scr[...] = _rms(x1, gpre_ref[...]).astype(BF16)
        f_scr[...] = jnp.zeros_like(f_scr)
        _new_kv_feature_major(kn_ref, vn_ref, (kvs1_ref, kvs2_ref, kvs3_ref), kv_scr)

    u = jnp.maximum(jnp.dot(h_scr[...], wup_ref[...], preferred_element_type=F32), 0.0)
    f_scr[...] += jnp.dot((u * u).astype(BF16), wdn_ref[...], preferred_element_type=F32)

    @pl.when(c == pl.num_programs(0) - 1)
    def _():
        y_ref[...] = (x1_scr[...] + _rms(f_scr[...], gpost_ref[...])).reshape(y_ref.shape)


def _tail_weight_specs():
    return [_const_spec((GROUP_W, D_MODEL)), _const_spec((SG_W, D_MODEL)), _const_spec((D_MODEL, D_MODEL)),
            _const_spec((1, D_MODEL)), _const_spec((1, D_MODEL)), _const_spec((D_MODEL, D_FF)),
            _const_spec((D_FF, D_MODEL)), _const_spec((1, D_MODEL))]


def _tail_prompt(o_a, z1, zn, ga, gb, x, wsp, bsp, weights, qkv_s, caches):
    n, db, ns = x.shape[0], caches[0].shape[0], qkv_s.shape[0]
    tile = SAMPLE_DB_TILE
    steps = db // tile
    tm = n // steps
    assert db % tile == 0 and n % steps == 0 and tm % CHUNK == 0 and ns == steps * T_PAD
    row = lambda w: pl.BlockSpec((tm, w), lambda i: (i, 0))
    blk = lambda a: pl.BlockSpec((tile,) + a.shape[1:], lambda i: (i, 0, 0))
    new_rows = lambda w: pl.BlockSpec((T_PAD, w), lambda i: (i, 0))
    return pl.pallas_call(
        _tail_prompt_kernel,
        grid=(steps,),
        in_specs=[row(GROUP_W), row(SG_W), row(SG_W), row(D_MODEL), row(D_MODEL), row(D_MODEL),
                  _const_spec((SG_GROUPS, CHUNK, CHUNK)), _const_spec((CHUNK, SG_W))] + _tail_weight_specs()
        + [new_rows(Z_OFF)] + [blk(a) for a in caches],
        out_specs=[row(D_MODEL), new_rows(GROUP_W)],
        out_shape=[jax.ShapeDtypeStruct((n, D_MODEL), F32),
                   jax.ShapeDtypeStruct((ns, GROUP_W), F32)],
        compiler_params=_params(1),
        name="tail_prompt",
    )(o_a, z1, zn, ga, gb, x, wsp, bsp, *weights, qkv_s, *caches)


def _tail_sample(o_a, z1, zn, ga, gb, x3, coef, bsp, weights, qkv):
    n = x3.shape[0] * x3.shape[1]
    wao, wbo, wout, gmix, gpre, wup, wdn, gpost = weights
    full = lambda a: _const_spec(a.shape)
    kv_new = jax.ShapeDtypeStruct((x3.shape[1], 2 * GROUP_W, x3.shape[0]), F32)
    qkv_part = lambda part: pl.BlockSpec((n, ATTN_W), lambda c: (0, part), pipeline_mode=pl.Buffered(1))
    return pl.pallas_call(
        _tail_sample_kernel,
        grid=(D_FF // FF_CHUNK,),
        in_specs=[full(o_a), full(z1), full(zn), full(ga), full(gb), full(x3), full(coef), full(bsp),
                  full(wao), full(wbo), full(wout), full(gmix), full(gpre),
                  pl.BlockSpec((D_MODEL, FF_CHUNK), lambda c: (0, c)),
                  pl.BlockSpec((FF_CHUNK, D_MODEL), lambda c: (c, 0)), full(gpost), qkv_part(1), qkv_part(2)],
        out_specs=[pl.BlockSpec(x3.shape, lambda c: (0, 0, 0))]
        + [pl.BlockSpec(kv_new.shape, lambda c: (0, 0, 0))] * N_GROUPS,
        out_shape=[jax.ShapeDtypeStruct(x3.shape, F32)] + [kv_new] * N_GROUPS,
        scratch_shapes=[pltpu.VMEM((n, D_MODEL), F32), pltpu.VMEM((n, D_MODEL), BF16), pltpu.VMEM((n, D_MODEL), F32),
                        pltpu.VMEM((2 * ATTN_W // LANES, n, LANES), F32)],
        compiler_params=_params(1),
        name="tail_sample",
    )(o_a, z1, zn, ga, gb, x3, coef, bsp, *weights, qkv, qkv)


def _kv_from_feature_major(kv_t):
    batch, _, keep = kv_t.shape
    return kv_t.reshape(batch, 2, HEADS, HEAD_DIM, keep).transpose(0, 4, 1, 2, 3)[None]


def kernel(x_prompt, x_sample, cache_kv_w128, cache_kv_w512, cache_kv_w2048, norm_pre_mix, w_in, b_gate,
           ln_z_g, ln_z_b, w_spatial, b_spatial, w_ao, w_bo, w_out, norm_post_mix, norm_pre_ffn, w_up, w_down,
           norm_post_ffn):
    assert w_in.shape[0] == 1, "single-layer problem"
    batch, seq, _ = x_prompt.shape
    dbatch, t_new, _ = x_sample.shape
    caches_in = (cache_kv_w128[0], cache_kv_w512[0], cache_kv_w2048[0])

    wsp = w_spatial[0]
    bsp_rows = jnp.repeat(b_spatial[0].T, CHUNK, axis=1)

    ns = dbatch * t_new
    qkv_s, z1_s, zn_s, ga_s, gb_s, w_in_bf, wao, wbo, wout = _project_sample(
        x_sample, norm_pre_mix, w_in[0], b_gate, ln_z_g, ln_z_b, (w_ao[0], w_bo[0], w_out[0]))
    caches = []
    for gi, (win, dil) in enumerate(DIL_GROUPS):
        buf = caches_in[gi]
        wb = buf.shape[1]
        assert wb == win and t_new <= T_PAD, "full window buffers"
        caches.append(buf.transpose(0, 2, 3, 4, 1).reshape(dbatch, 2 * GROUP_W, wb))

    xp = x_prompt.reshape(batch * seq, D_MODEL)
    qkv, kv1_t, kv2_t, kv3_t, z1, zn, ga, gb, wup, wdn = _project_prompt(
        x_prompt, norm_pre_mix, w_in_bf, b_gate, ln_z_g, ln_z_b, (w_up[0], w_down[0]))
    weights = (wao, wbo, wout, norm_post_mix, norm_pre_ffn, wup, wdn, norm_post_ffn)
    kv_prompt = [_kv_from_feature_major(a) for a in (kv1_t, kv2_t, kv3_t)]
    o_a_p = _prompt_attention(qkv, seq).reshape(batch * seq, GROUP_W)
    assert SAMPLE_DB_TILE * t_new == T_PAD, "a tail step takes T_PAD new rows"
    y_prompt, o_a = _tail_prompt(o_a_p, z1, zn, ga, gb, xp, wsp, bsp_rows, weights, qkv_s, caches)
    y_prompt = y_prompt.reshape(batch, seq, D_MODEL)

    wt = wsp[:, :t_new, :t_new]
    t_idx = jnp.arange(t_new)
    shifted = (t_idx[None, :, None] - t_idx[:, None, None] == t_idx[None, None, :]).astype(F32)
    coef = (wt[None] * shifted[:, None]).sum(-1).transpose(0, 2, 1)
    reps = T_PAD // t_new
    coef = jnp.tile(jnp.repeat(coef, CHUNK, axis=2), (1, reps, 1))
    bsp_s = jnp.tile(bsp_rows[:t_new], (reps, 1))
    y_sample, *kv_new = _tail_sample(o_a, z1_s, zn_s, ga_s, gb_s, x_sample, coef, bsp_s, weights, qkv_s)
    kv_sample = [a.reshape(t_new, 2, HEADS, HEAD_DIM, dbatch).transpose(4, 0, 1, 2, 3)[None] for a in kv_new]
    sg_sample = zn_s[None]

    return (y_prompt, y_sample, kv_prompt[0], kv_prompt[1], kv_prompt[2],
            kv_sample[0], kv_sample[1], kv_sample[2], sg_sample)
```

```python
import math

import jax
import jax.numpy as jnp
from jax import lax
from jax.experimental import pallas as pl
from jax.experimental.pallas import tpu as pltpu

F32 = jnp.float32
BF16 = jnp.bfloat16

D_MODEL = 1024
HEAD_DIM = 64
HEADS = 4
GROUP_W = HEADS * HEAD_DIM
DIL_GROUPS = ((128, 1), (512, 4), (2048, 16))
N_GROUPS = len(DIL_GROUPS)
ATTN_W = N_GROUPS * GROUP_W
BLOCK = 128
CHUNK = 128
SG_GROUPS = 4
SG_W = 512
D_FF = 4 * D_MODEL
Z_OFF = 3 * ATTN_W
G_OFF = Z_OFF + 2 * SG_W
IN_W = G_OFF + 2 * D_MODEL
EPS = 1e-6
NEG = -1e30
SCALE = HEAD_DIM ** -0.5
INV_SQRT2 = 1.0 / math.sqrt(2.0)
LOG2E = 1.0 / math.log(2.0)
LANES = 128
SLABS = GROUP_W // LANES
HEADS_PER_SLAB = LANES // HEAD_DIM
HEAD_SHIFT = HEAD_DIM.bit_length() - 1
VMEM_LIMIT = 56 * 1024 * 1024

TOKEN_TILE = 512
MERGE_UNROLL = 4
BLOCK_UNROLL = 16
SAMPLE_DB_TILE = 2
T_PAD = 8
T_SHIFT = T_PAD.bit_length() - 1
BF16_SUBLANES = 16
CAST_COLS = 768
SAMPLE_TAIL_TILE = 256
FF_CHUNK = 1024


def _params(n_axes):
    return pltpu.CompilerParams(dimension_semantics=("arbitrary",) * n_axes,
                                vmem_limit_bytes=VMEM_LIMIT)


def _const_spec(shape):
    return pl.BlockSpec(shape, lambda *_: (0,) * len(shape), pipeline_mode=pl.Buffered(1))


def _rms(x, g):
    return x * lax.rsqrt(jnp.mean(x * x, axis=-1, keepdims=True) + EPS) * g


def _proj_gate_branches(h, w_ref, bg_ref, lng_ref, lnb_ref, z1_ref, zn_ref, ga_ref, gb_ref):
    z = jnp.dot(h, w_ref[:, Z_OFF:G_OFF], preferred_element_type=F32)
    z = 0.5 * z * (1.0 + lax.erf(z * INV_SQRT2))
    z1_ref[...] = z[:, :SG_W].astype(z1_ref.dtype)
    z2 = z[:, SG_W:]
    mu = jnp.mean(z2, axis=-1, keepdims=True)
    zc = z2 - mu
    var = jnp.mean(zc * zc, axis=-1, keepdims=True)
    zn = (zc * lax.rsqrt(var + EPS) * lng_ref[...] + lnb_ref[...]).astype(zn_ref.dtype)
    zn_ref[...] = zn.reshape(zn_ref.shape)

    gates = jnp.dot(h, w_ref[:, G_OFF:], preferred_element_type=F32) + bg_ref[...]
    gates = 1.0 / (1.0 + jnp.exp(-gates))
    ga_ref[...] = gates[:, :D_MODEL].astype(BF16)
    gb_ref[...] = gates[:, D_MODEL:].astype(BF16)


def _cast_blocks(pairs):
    for src_ref, dst_ref in pairs:
        dst_ref[...] = src_ref[...].astype(dst_ref.dtype)


def _proj_sample_kernel(x_ref, g_ref, wf_ref, bg_ref, lng_ref, lnb_ref, wao_ref, wbo_ref, wout_ref,
                        qkv_ref, z1_ref, zn_ref, ga_ref, gb_ref, w_ref, wao_bf_ref, wbo_bf_ref, wout_bf_ref):
    @pl.when(pl.program_id(0) == 0)
    def _():
        for c in range(0, IN_W, CAST_COLS):
            w_ref[:, c:c + CAST_COLS] = wf_ref[:, c:c + CAST_COLS].astype(BF16)

    _cast_blocks(((wao_ref, wao_bf_ref), (wbo_ref, wbo_bf_ref), (wout_ref, wout_bf_ref)))
    x = x_ref[...]
    h = _rms(x.reshape(x.shape[0] * x.shape[1], x.shape[2]), g_ref[...]).astype(BF16)
    qkv_ref[...] = jnp.dot(h, w_ref[:, :Z_OFF], preferred_element_type=F32)
    _proj_gate_branches(h, w_ref, bg_ref, lng_ref, lnb_ref, z1_ref, zn_ref, ga_ref, gb_ref)


def _proj_prompt_kernel(x_ref, g_ref, w_ref, bg_ref, lng_ref, lnb_ref, wup_ref, wdn_ref,
                        qkv_ref, kv1_ref, kv2_ref, kv3_ref, z1_ref, zn_ref, ga_ref, gb_ref, wup_bf_ref, wdn_bf_ref):
    _cast_blocks(((wup_ref, wup_bf_ref), (wdn_ref, wdn_bf_ref)))
    h = _rms(x_ref[0], g_ref[...]).astype(BF16)
    _proj_gate_branches(h, w_ref, bg_ref, lng_ref, lnb_ref, z1_ref, zn_ref, ga_ref, gb_ref)
    def project(first_part, n_parts, scale=None):
        cols = jnp.dot(h, w_ref[:, first_part * ATTN_W:(first_part + n_parts) * ATTN_W], preferred_element_type=F32)
        if scale is not None:
            cols = cols * scale
        for part in range(n_parts):
            for g in range(N_GROUPS):
                for s in range(SLABS):
                    col = part * ATTN_W + g * GROUP_W + s * LANES
                    qkv_ref[0, g, (first_part + part) * SLABS + s] = cols[:, col:col + LANES]

    def kv_t(g):
        return jnp.concatenate([qkv_ref[0, g, slab].T for slab in range(SLABS, 3 * SLABS)], axis=0)

    tm = x_ref.shape[1]
    project(1, 2)
    kv3_ref[0] = kv_t(2)
    project(0, 1, scale=SCALE * LOG2E)

    @pl.when(pl.program_id(1) == pl.num_programs(1) - 1)
    def _():
        kv2_ref[0] = kv_t(1)[:, tm - kv2_ref.shape[2]:]
        kv1_ref[0] = kv_t(0)[:, tm - kv1_ref.shape[2]:]


def _proj_weight_specs():
    return [_const_spec((1, D_MODEL)), _const_spec((D_MODEL, IN_W)),
            _const_spec((1, 2 * D_MODEL)), _const_spec((1, SG_W)), _const_spec((1, SG_W))]


def _gate_branch_shapes(n, z_dtype):
    return [jax.ShapeDtypeStruct((n, SG_W), z_dtype), jax.ShapeDtypeStruct((n, SG_W), z_dtype),
            jax.ShapeDtypeStruct((n, D_MODEL), BF16), jax.ShapeDtypeStruct((n, D_MODEL), BF16)]


def _row_slices(arrays, steps):
    specs, shapes = [], []
    for a in arrays:
        rows = a.shape[0] // steps
        assert rows * steps == a.shape[0] and rows % BF16_SUBLANES == 0
        specs.append((rows, a.shape[1]))
        shapes.append(jax.ShapeDtypeStruct(a.shape, BF16))
    return specs, shapes


def _project_sample(x3, g, w_in, bg, lng, lnb, mix_weights):
    n = x3.shape[0] * x3.shape[1]
    tm = min(SAMPLE_TAIL_TILE, n)
    steps = n // tm
    row = lambda w: pl.BlockSpec((tm, w), lambda i: (i, 0))
    slices, bf_shapes = _row_slices(mix_weights, steps)
    slice_specs = [pl.BlockSpec(s, lambda i: (i, 0)) for s in slices]
    z1_shape, _, ga_shape, gb_shape = _gate_branch_shapes(n, F32)
    return pl.pallas_call(
        _proj_sample_kernel,
        grid=(steps,),
        in_specs=[pl.BlockSpec((tm // x3.shape[1],) + x3.shape[1:], lambda i: (i, 0, 0))] + _proj_weight_specs()
        + slice_specs,
        out_specs=[row(Z_OFF), row(SG_W), pl.BlockSpec((tm // x3.shape[1], x3.shape[1], SG_W), lambda i: (i, 0, 0)),
                   row(D_MODEL), row(D_MODEL), pl.BlockSpec((D_MODEL, IN_W), lambda i: (0, 0))] + slice_specs,
        out_shape=[jax.ShapeDtypeStruct((n, Z_OFF), F32), z1_shape, jax.ShapeDtypeStruct(x3.shape[:2] + (SG_W,), F32),
                   ga_shape, gb_shape, jax.ShapeDtypeStruct((D_MODEL, IN_W), BF16)] + bf_shapes,
        compiler_params=_params(1),
        name="proj_sample",
    )(x3, g, w_in, bg, lng, lnb, *mix_weights)


def _project_prompt(x, g, w_bf, bg, lng, lnb, ffn_weights):
    batch, seq, _ = x.shape
    tm = TOKEN_TILE
    tiles = seq // tm
    keeps = [min(win, seq) for win, _ in DIL_GROUPS]
    assert seq % tm == 0 and keeps[0] <= tm and keeps[1] <= tm and keeps[2] == seq
    row = lambda w: pl.BlockSpec((tm, w), lambda b, t: (b * tiles + t, 0))
    last = lambda keep: pl.BlockSpec((1, 2 * GROUP_W, keep), lambda b, t: (b, 0, 0))
    n = batch * seq
    slices, bf_shapes = _row_slices(ffn_weights, batch * tiles)
    slice_specs = [pl.BlockSpec(s, lambda b, t: (b * tiles + t, 0)) for s in slices]
    return pl.pallas_call(
        _proj_prompt_kernel,
        grid=(batch, tiles),
        in_specs=[pl.BlockSpec((1, tm, D_MODEL), lambda b, t: (b, t, 0))] + _proj_weight_specs() + slice_specs,
        out_specs=[pl.BlockSpec((1, N_GROUPS, 3 * SLABS, tm, LANES), lambda b, t: (b, 0, 0, t, 0)),
                   last(keeps[0]), last(keeps[1]),
                   pl.BlockSpec((1, 2 * GROUP_W, tm), lambda b, t: (b, 0, t)),
                   row(SG_W), row(SG_W), row(D_MODEL), row(D_MODEL)] + slice_specs,
        out_shape=[jax.ShapeDtypeStruct((batch, N_GROUPS, 3 * SLABS, seq, LANES), F32),
                   jax.ShapeDtypeStruct((batch, 2 * GROUP_W, keeps[0]), F32),
                   jax.ShapeDtypeStruct((batch, 2 * GROUP_W, keeps[1]), F32),
                   jax.ShapeDtypeStruct((batch, 2 * GROUP_W, keeps[2]), F32)] + _gate_branch_shapes(n, BF16)
        + bf_shapes,
        compiler_params=_params(2),
        name="proj_prompt",
    )(x, g, w_bf, bg, lng, lnb, *ffn_weights)


def _head_lane_mask(shape, hh):
    lane = lax.broadcasted_iota(jnp.int32, shape, len(shape) - 1)
    return (lane >> HEAD_SHIFT) == hh


def _band_block(q, k, v, mask):
    nt = (((1,), (1,)), ((), ()))
    assert HEADS_PER_SLAB == 2
    lm0 = _head_lane_mask((BLOCK, LANES), 0)
    mask2 = jnp.concatenate([mask, mask], axis=0)
    outs, lses = [], []
    for s in range(SLABS):
        zero = jnp.zeros_like(q[s])
        qm = jnp.concatenate([jnp.where(lm0, q[s], zero), jnp.where(lm0, zero, q[s])], axis=0)
        sc = lax.dot_general(qm, k[s], nt, preferred_element_type=F32)
        sc = jnp.where(mask2, sc, NEG)
        m = jnp.max(sc, axis=-1, keepdims=True)
        p = jnp.exp2(sc - m)
        l = jnp.sum(p, axis=-1, keepdims=True)
        o = jnp.dot(p.astype(BF16), v[s], preferred_element_type=F32) * (1.0 / l)
        lse = m + jnp.log2(l)
        outs.append(jnp.where(lm0, o[:BLOCK], o[BLOCK:]))
        lses.append(jnp.where(lm0, lse[:BLOCK], lse[BLOCK:]))
    return outs, lses


def _group_attention(qkv_ref, o_scr, lse_scr, g, dil, seq):
    n_qb = seq // (dil * BLOCK)
    qb_shift = n_qb.bit_length() - 1
    span = dil * BLOCK
    rows = (lambda start: pl.ds(start, BLOCK, stride=dil)) if dil > 1 else (lambda start: pl.ds(start, BLOCK))
    nk = 2 * BLOCK if n_qb > 1 else BLOCK
    qi = lax.broadcasted_iota(jnp.int32, (BLOCK, nk), 0)
    ki = lax.broadcasted_iota(jnp.int32, (BLOCK, nk), 1)

    def load(slab, start):
        return qkv_ref[0, 0, slab, rows(start), :]

    def body(n, carry):
        k_prev, v_prev = carry
        r = n >> qb_shift
        qb = n & (n_qb - 1)
        cur = r + span * qb
        if dil == 1:
            cur = pl.multiple_of(cur, BLOCK)
        q = [load(s, cur).astype(BF16) for s in range(SLABS)]
        k = [load(SLABS + s, cur).astype(BF16) for s in range(SLABS)]
        v = [load(2 * SLABS + s, cur).astype(BF16) for s in range(SLABS)]
        carry = (k, v)
        if n_qb > 1:
            k = [jnp.concatenate([k_prev[s], k[s]], axis=0) for s in range(SLABS)]
            v = [jnp.concatenate([v_prev[s], v[s]], axis=0) for s in range(SLABS)]
            mask = (ki >= qi) & (ki <= qi + BLOCK) & (ki >= jnp.where(qb > 0, 0, BLOCK))
        else:
            mask = ki <= qi
        outs, lses = _band_block(q, k, v, mask)
        for s in range(SLABS):
            o_scr[g, s, rows(cur), :] = outs[s]
            lse_scr[g, s, rows(cur), :] = lses[s]
        return carry

    no_rows = [jnp.zeros((BLOCK, LANES), BF16)] * SLABS
    lax.fori_loop(0, dil * n_qb, body, (no_rows, no_rows), unroll=BLOCK_UNROLL)


def _prompt_attn_kernel(qkv_ref, o_ref, o_scr, lse_scr):
    g = pl.program_id(1)
    seq = o_ref.shape[1]
    for gi, (_, dil) in enumerate(DIL_GROUPS):
        @pl.when(g == gi)
        def _(gi=gi, dil=dil):
            _group_attention(qkv_ref, o_scr, lse_scr, gi, dil, seq)

    @pl.when(g == N_GROUPS - 1)
    def _():
        def merge(i, carry):
            rows = pl.ds(pl.multiple_of(i * BLOCK, BLOCK), BLOCK)
            for s in range(SLABS):
                ls = [lse_scr[gi, s, rows, :] for gi in range(N_GROUPS)]
                top = jnp.maximum(jnp.maximum(ls[0], ls[1]), ls[2])
                e = [jnp.exp2(x - top) for x in ls]
                num = e[0] * o_scr[0, s, rows, :] + e[1] * o_scr[1, s, rows, :] + e[2] * o_scr[2, s, rows, :]
                o_ref[0, rows, s * LANES:(s + 1) * LANES] = (num / (e[0] + e[1] + e[2])).astype(o_ref.dtype)
            return carry
        lax.fori_loop(0, seq // BLOCK, merge, 0, unroll=MERGE_UNROLL)


def _prompt_attention(qkv, seq):
    batch = qkv.shape[0]
    scratch = pltpu.VMEM((N_GROUPS, SLABS, seq, LANES), F32)
    return pl.pallas_call(
        _prompt_attn_kernel,
        grid=(batch, N_GROUPS),
        in_specs=[pl.BlockSpec((1, 1, 3 * SLABS, seq, LANES), lambda b, g: (b, g, 0, 0, 0))],
        out_specs=pl.BlockSpec((1, seq, GROUP_W), lambda b, g: (b, 0, 0)),
        out_shape=jax.ShapeDtypeStruct((batch, seq, GROUP_W), BF16),
        scratch_shapes=[scratch, scratch],
        compiler_params=_params(2),
        name="prompt_attn",
    )(qkv)


def _residue_allreduce(x, dil, op):
    shift = LANES // 2
    while shift >= dil:
        x = op(x, pltpu.roll(x, shift, axis=1))
        shift //= 2
    return x


def _lane_expand(rows8, dil, r0, t_new):
    assert r0 == 0 and t_new <= dil and (dil % T_PAD == 0 or T_PAD % dil == 0)
    if dil >= T_PAD:
        period = [rows8] + [jnp.zeros_like(rows8)] * (dil // T_PAD - 1)
    else:
        period = [jnp.concatenate([rows8[:dil]] * (T_PAD // dil), axis=0)]
    rows = jnp.concatenate(period * (LANES // max(dil, T_PAD)), axis=0)
    return rows.T


def _heads_to_rows(x):
    return jnp.broadcast_to(x[:, None, :], (HEADS, HEAD_DIM, LANES)).reshape(GROUP_W, LANES)


def _rows_to_heads_sum(x):
    return jnp.sum(x.reshape(HEADS, HEAD_DIM, LANES), axis=1)


def _buffer_group_lanes(c_ref, d, q8, kn8, vn8, dil, r0, t_new):
    width = c_ref.shape[2]
    tiles = width // LANES
    ql, kl, vl = (_lane_expand(a, dil, r0, t_new) for a in (q8, kn8, vn8))
    s = [_rows_to_heads_sum(c_ref[d, :GROUP_W, j * LANES:(j + 1) * LANES] * ql) for j in range(tiles)]
    s_new = _rows_to_heads_sum(kl * ql)
    top = s[0]
    for sj in s[1:]:
        top = jnp.maximum(top, sj)
    top = jnp.maximum(_residue_allreduce(top, dil, jnp.maximum), s_new)
    p = [jnp.exp(sj - top) for sj in s]
    p_new = jnp.exp(s_new - top)
    den = p[0]
    for pj in p[1:]:
        den = den + pj
    den = _residue_allreduce(den, dil, jnp.add) + p_new
    acc = c_ref[d, GROUP_W:, :LANES] * _heads_to_rows(p[0])
    for j in range(1, tiles):
        acc = acc + c_ref[d, GROUP_W:, j * LANES:(j + 1) * LANES] * _heads_to_rows(p[j])
    acc = (_residue_allreduce(acc, dil, jnp.add) + vl * _heads_to_rows(p_new)) * _heads_to_rows(1.0 / den)
    lse = _heads_to_rows(top + jnp.log(den))
    first = -(-r0 // dil) * dil
    return acc.T[first - r0:first - r0 + T_PAD], lse.T[first - r0:first - r0 + T_PAD]


def _buffer_group_dense(c_ref, d, q8, kn8, vn8, r0, t_new):
    rows = HEADS * T_PAD
    width = c_ref.shape[2]
    row = lax.broadcasted_iota(jnp.int32, (rows, GROUP_W), 0)
    lane = lax.broadcasted_iota(jnp.int32, (rows, GROUP_W), 1)
    head_sel = (row >> T_SHIFT) == (lane >> HEAD_SHIFT)
    t = (lax.broadcasted_iota(jnp.int32, (rows, width), 0) & (T_PAD - 1)) - r0
    i = lax.broadcasted_iota(jnp.int32, (rows, width), 1)
    tn = lax.broadcasted_iota(jnp.int32, (rows, T_PAD), 0) & (T_PAD - 1)
    j = lax.broadcasted_iota(jnp.int32, (rows, T_PAD), 1)
    nt = (((1,), (1,)), ((), ()))
    qm = jnp.where(head_sel, jnp.concatenate([q8] * HEADS, axis=0), 0.0).astype(BF16)
    kt = c_ref[d, :GROUP_W, :].astype(BF16)
    vt = c_ref[d, GROUP_W:, :].astype(BF16)
    sc = jnp.where(i >= t, jnp.dot(qm, kt, preferred_element_type=F32), NEG)
    own = (j <= tn) & (j >= r0) & (j < r0 + t_new)
    sn = jnp.where(own, lax.dot_general(qm, kn8.astype(BF16), nt, preferred_element_type=F32), NEG)
    m = jnp.maximum(jnp.max(sc, axis=-1, keepdims=True), jnp.max(sn, axis=-1, keepdims=True))
    pc = jnp.exp(sc - m)
    pn = jnp.exp(sn - m)
    l = jnp.sum(pc, axis=-1, keepdims=True) + jnp.sum(pn, axis=-1, keepdims=True)

    def finish():
        o = (lax.dot_general(pc.astype(BF16), vt, nt, preferred_element_type=F32)
             + jnp.dot(pn.astype(BF16), vn8.astype(BF16), preferred_element_type=F32))
        o = jnp.where(head_sel, o / l, 0.0)
        lse = jnp.where(head_sel, m + jnp.log(l), 0.0)
        fold = lambda x: sum(x[h * T_PAD:(h + 1) * T_PAD] for h in range(HEADS))
        return fold(o), fold(lse)
    return finish


def _sample_attn_start(qkv_ref, c1_ref, c2_ref, c3_ref, o_ref):
    n_db = c1_ref.shape[0]
    t_new = T_PAD // n_db
    assert qkv_ref.shape[0] == T_PAD and n_db * t_new == T_PAD
    pad_rows = jnp.zeros((T_PAD - t_new, qkv_ref.shape[1]), F32)
    pending = []
    for d in range(n_db):
        new_rows = jnp.concatenate([qkv_ref[d * t_new:(d + 1) * t_new, :], pad_rows], axis=0)
        groups = []
        for g, (c_ref, (_, dil)) in enumerate(zip((c1_ref, c2_ref, c3_ref), DIL_GROUPS)):
            cols = lambda part: slice(part * ATTN_W + g * GROUP_W, part * ATTN_W + (g + 1) * GROUP_W)
            q8 = new_rows[:, cols(0)] * SCALE
            kn8, vn8 = new_rows[:, cols(1)], new_rows[:, cols(2)]
            if dil == 1:
                groups.append(_buffer_group_dense(c_ref, d, q8, kn8, vn8, 0, t_new))
            else:
                result = _buffer_group_lanes(c_ref, d, q8, kn8, vn8, dil, 0, t_new)
                groups.append(lambda result=result: result)
        pending.append(groups)

    def finish():
        for d, groups in enumerate(pending):
            outs, lses = zip(*(group() for group in groups))
            top = jnp.maximum(jnp.maximum(lses[0], lses[1]), lses[2])
            w = [jnp.exp(x - top) for x in lses]
            o8 = (w[0] * outs[0] + w[1] * outs[1] + w[2] * outs[2]) / (w[0] + w[1] + w[2])
            o_ref[d * t_new:(d + 1) * t_new, :] = o8[:t_new]
    return finish


def _mix_value(o_a, o_b, ga_ref, gb_ref, x, wao_ref, wbo_ref, wout_ref, gpost_ref):
    a = jnp.dot(o_a.astype(BF16), wao_ref[...], preferred_element_type=F32)
    b = jnp.dot(o_b.astype(BF16), wbo_ref[...], preferred_element_type=F32)
    merged = ga_ref[...].astype(F32) * a + gb_ref[...].astype(F32) * b
    t = jnp.dot(merged.astype(BF16), wout_ref[...], preferred_element_type=F32)
    return x + _rms(t, gpost_ref[...])


def _ffn_value(x, gpre_ref, wup_ref, wdn_ref, gpost_ref):
    h = _rms(x, gpre_ref[...]).astype(BF16)
    f = jnp.zeros(x.shape, F32)
    for c in range(D_FF // FF_CHUNK):
        cs = slice(c * FF_CHUNK, (c + 1) * FF_CHUNK)
        u = jnp.maximum(jnp.dot(h, wup_ref[:, cs], preferred_element_type=F32), 0.0)
        f = f + jnp.dot((u * u).astype(BF16), wdn_ref[cs, :], preferred_element_type=F32)
    return x + _rms(f, gpost_ref[...])


def _prompt_spatial_gate(z1_ref, zn_ref, wsp_ref, bsp_ref):
    r = lax.broadcasted_iota(jnp.int32, (CHUNK, CHUNK), 0)
    c = lax.broadcasted_iota(jnp.int32, (CHUNK, CHUNK), 1)
    n_chunks = z1_ref.shape[0] // CHUNK
    mixed = []
    for g in range(SG_GROUPS):
        wt = jnp.where(r >= c, wsp_ref[g], 0.0).astype(BF16)
        zg = jnp.concatenate([zn_ref[ci * CHUNK:(ci + 1) * CHUNK, g * CHUNK:(g + 1) * CHUNK].astype(BF16)
                              for ci in range(n_chunks)], axis=1)
        mixed.append(jnp.dot(wt, zg, preferred_element_type=F32))
    chunks = [jnp.concatenate([m[:, ci * CHUNK:(ci + 1) * CHUNK] for m in mixed], axis=1) + bsp_ref[...]
              for ci in range(n_chunks)]
    return z1_ref[...].astype(F32) * jnp.concatenate(chunks, axis=0)


def _tail_prompt_kernel(oa_ref, z1_ref, zn_ref, ga_ref, gb_ref, x_ref, wsp_ref, bsp_ref,
                        wao_ref, wbo_ref, wout_ref, gmix_ref, gpre_ref, wup_ref, wdn_ref, gpost_ref,
                        qkv_ref, c1_ref, c2_ref, c3_ref, y_ref, o_ref):
    o_b = _prompt_spatial_gate(z1_ref, zn_ref, wsp_ref, bsp_ref)
    x1 = _mix_value(oa_ref[...], o_b, ga_ref, gb_ref, x_ref[...], wao_ref, wbo_ref, wout_ref, gmix_ref)
    finish_sample_attn = _sample_attn_start(qkv_ref, c1_ref, c2_ref, c3_ref, o_ref)
    y_ref[...] = _ffn_value(x1, gpre_ref, wup_ref, wdn_ref, gpost_ref)
    finish_sample_attn()


def _new_kv_feature_major(kn_ref, vn_ref, out_refs, scr):
    t_new, ndb = out_refs[0].shape[0], out_refs[0].shape[2]
    slabs = ATTN_W // LANES
    for part, src_ref in enumerate((kn_ref, vn_ref)):
        for slab in range(slabs):
            scr[part * slabs + slab] = src_ref[:, slab * LANES:(slab + 1) * LANES]
    for t in range(t_new):
        for part in range(2):
            for g, out_ref in enumerate(out_refs):
                for s in range(SLABS):
                    rows = scr[part * slabs + g * SLABS + s, pl.ds(t, ndb, stride=t_new), :]
                    out_ref[t, part * GROUP_W + s * LANES:part * GROUP_W + (s + 1) * LANES, :] = rows.T


def _tail_sample_kernel(oa_ref, z1_ref, zn_ref, ga_ref, gb_ref, x_ref, coef_ref, bsp_ref,
                        wao_ref, wbo_ref, wout_ref, gmix_ref, gpre_ref, wup_ref, wdn_ref, gpost_ref, kn_ref, vn_ref,
                        y_ref, kvs1_ref, kvs2_ref, kvs3_ref, x1_scr, h_scr, f_scr, kv_scr):
    c = pl.program_id(0)

    @pl.when(c == 0)
    def _():
        zn = zn_ref[...].reshape(z1_ref.shape)
        periods = lambda a: a.reshape(a.shape[0] // T_PAD, T_PAD, a.shape[1])
        mix = coef_ref[0] * periods(zn)
        for d in range(1, coef_ref.shape[0]):
            mix = mix + coef_ref[d] * periods(pltpu.roll(zn, d, axis=0))
        o_b = z1_ref[...] * (mix + bsp_ref[...]).reshape(zn.shape)
        x = x_ref[...].reshape(zn.shape[0], D_MODEL)
        x1 = _mix_value(oa_ref[...], o_b, ga_ref, gb_ref, x, wao_ref, wbo_ref, wout_ref, gmix_ref)
        x1_scr[...] = x1
        h_scr[...] = _rms(x1, gpre_ref[...]).astype(BF16)
        f_scr[...] = jnp.zeros_like(f_scr)
        _new_kv_feature_major(kn_ref, vn_ref, (kvs1_ref, kvs2_ref, kvs3_ref), kv_scr)

    u = jnp.maximum(jnp.dot(h_scr[...], wup_ref[...], preferred_element_type=F32), 0.0)
    f_scr[...] += jnp.dot((u * u).astype(BF16), wdn_ref[...], preferred_element_type=F32)

    @pl.when(c == pl.num_programs(0) - 1)
    def _():
        y_ref[...] = (x1_scr[...] + _rms(f_scr[...], gpost_ref[...])).reshape(y_ref.shape)


def _tail_weight_specs():
    return [_const_spec((GROUP_W, D_MODEL)), _const_spec((SG_W, D_MODEL)), _const_spec((D_MODEL, D_MODEL)),
            _const_spec((1, D_MODEL)), _const_spec((1, D_MODEL)), _const_spec((D_MODEL, D_FF)),
            _const_spec((D_FF, D_MODEL)), _const_spec((1, D_MODEL))]


def _tail_prompt(o_a, z1, zn, ga, gb, x, wsp, bsp, weights, qkv_s, caches):
    n, db, ns = x.shape[0], caches[0].shape[0], qkv_s.shape[0]
    tile = SAMPLE_DB_TILE
    steps = db // tile
    tm = n // steps
    assert db % tile == 0 and n % steps == 0 and tm % CHUNK == 0 and ns == steps * T_PAD
    row = lambda w: pl.BlockSpec((tm, w), lambda i: (i, 0))
    blk = lambda a: pl.BlockSpec((tile,) + a.shape[1:], lambda i: (i, 0, 0))
    new_rows = lambda w: pl.BlockSpec((T_PAD, w), lambda i: (i, 0))
    return pl.pallas_call(
        _tail_prompt_kernel,
        grid=(steps,),
        in_specs=[row(GROUP_W), row(SG_W), row(SG_W), row(D_MODEL), row(D_MODEL), row(D_MODEL),
                  _const_spec((SG_GROUPS, CHUNK, CHUNK)), _const_spec((CHUNK, SG_W))] + _tail_weight_specs()
        + [new_rows(Z_OFF)] + [blk(a) for a in caches],
        out_specs=[row(D_MODEL), new_rows(GROUP_W)],
        out_shape=[jax.ShapeDtypeStruct((n, D_MODEL), F32),
                   jax.ShapeDtypeStruct((ns, GROUP_W), F32)],
        compiler_params=_params(1),
        name="tail_prompt",
    )(o_a, z1, zn, ga, gb, x, wsp, bsp, *weights, qkv_s, *caches)


def _tail_sample(o_a, z1, zn, ga, gb, x3, coef, bsp, weights, qkv):
    n = x3.shape[0] * x3.shape[1]
    wao, wbo, wout, gmix, gpre, wup, wdn, gpost = weights
    full = lambda a: _const_spec(a.shape)
    kv_new = jax.ShapeDtypeStruct((x3.shape[1], 2 * GROUP_W, x3.shape[0]), F32)
    qkv_part = lambda part: pl.BlockSpec((n, ATTN_W), lambda c: (0, part), pipeline_mode=pl.Buffered(1))
    return pl.pallas_call(
        _tail_sample_kernel,
        grid=(D_FF // FF_CHUNK,),
        in_specs=[full(o_a), full(z1), full(zn), full(ga), full(gb), full(x3), full(coef), full(bsp),
                  full(wao), full(wbo), full(wout), full(gmix), full(gpre),
                  pl.BlockSpec((D_MODEL, FF_CHUNK), lambda c: (0, c)),
                  pl.BlockSpec((FF_CHUNK, D_MODEL), lambda c: (c, 0)), full(gpost), qkv_part(1), qkv_part(2)],
        out_specs=[pl.BlockSpec(x3.shape, lambda c: (0, 0, 0))]
        + [pl.BlockSpec(kv_new.shape, lambda c: (0, 0, 0))] * N_GROUPS,
        out_shape=[jax.ShapeDtypeStruct(x3.shape, F32)] + [kv_new] * N_GROUPS,
        scratch_shapes=[pltpu.VMEM((n, D_MODEL), F32), pltpu.VMEM((n, D_MODEL), BF16), pltpu.VMEM((n, D_MODEL), F32),
                        pltpu.VMEM((2 * ATTN_W // LANES, n, LANES), F32)],
        compiler_params=_params(1),
        name="tail_sample",
    )(o_a, z1, zn, ga, gb, x3, coef, bsp, *weights, qkv, qkv)


def _kv_from_feature_major(kv_t):
    batch, _, keep = kv_t.shape
    return kv_t.reshape(batch, 2, HEADS, HEAD_DIM, keep).transpose(0, 4, 1, 2, 3)[None]


def kernel(x_prompt, x_sample, cache_kv_w128, cache_kv_w512, cache_kv_w2048, norm_pre_mix, w_in, b_gate,
           ln_z_g, ln_z_b, w_spatial, b_spatial, w_ao, w_bo, w_out, norm_post_mix, norm_pre_ffn, w_up, w_down,
           norm_post_ffn):
    assert w_in.shape[0] == 1, "single-layer problem"
    batch, seq, _ = x_prompt.shape
    dbatch, t_new, _ = x_sample.shape
    caches_in = (cache_kv_w128[0], cache_kv_w512[0], cache_kv_w2048[0])

    wsp = w_spatial[0]
    bsp_rows = jnp.repeat(b_spatial[0].T, CHUNK, axis=1)

    ns = dbatch * t_new
    qkv_s, z1_s, zn_s, ga_s, gb_s, w_in_bf, wao, wbo, wout = _project_sample(
        x_sample, norm_pre_mix, w_in[0], b_gate, ln_z_g, ln_z_b, (w_ao[0], w_bo[0], w_out[0]))
    caches = []
    for gi, (win, dil) in enumerate(DIL_GROUPS):
        buf = caches_in[gi]
        wb = buf.shape[1]
        assert wb == win and t_new <= T_PAD, "full window buffers"
        caches.append(buf.transpose(0, 2, 3, 4, 1).reshape(dbatch, 2 * GROUP_W, wb))

    xp = x_prompt.reshape(batch * seq, D_MODEL)
    qkv, kv1_t, kv2_t, kv3_t, z1, zn, ga, gb, wup, wdn = _project_prompt(
        x_prompt, norm_pre_mix, w_in_bf, b_gate, ln_z_g, ln_z_b, (w_up[0], w_down[0]))
    weights = (wao, wbo, wout, norm_post_mix, norm_pre_ffn, wup, wdn, norm_post_ffn)
    kv_prompt = [_kv_from_feature_major(a) for a in (kv1_t, kv2_t, kv3_t)]
    o_a_p = _prompt_attention(qkv, seq).reshape(batch * seq, GROUP_W)
    assert SAMPLE_DB_TILE * t_new == T_PAD, "a tail step takes T_PAD new rows"
    y_prompt, o_a = _tail_prompt(o_a_p, z1, zn, ga, gb, xp, wsp, bsp_rows, weights, qkv_s, caches)
    y_prompt = y_prompt.reshape(batch, seq, D_MODEL)

    wt = wsp[:, :t_new, :t_new]
    t_idx = jnp.arange(t_new)
    shifted = (t_idx[None, :, None] - t_idx[:, None, None] == t_idx[None, None, :]).astype(F32)
    coef = (wt[None] * shifted[:, None]).sum(-1).transpose(0, 2, 1)
    reps = T_PAD // t_new
    coef = jnp.tile(jnp.repeat(coef, CHUNK, axis=2), (1, reps, 1))
    bsp_s = jnp.tile(bsp_rows[:t_new], (reps, 1))
    y_sample, *kv_new = _tail_sample(o_a, z1_s, zn_s, ga_s, gb_s, x_sample, coef, bsp_s, weights, qkv_s)
    kv_sample = [a.reshape(t_new, 2, HEADS, HEAD_DIM, dbatch).transpose(4, 0, 1, 2, 3)[None] for a in kv_new]
    sg_sample = zn_s[None]

    return (y_prompt, y_sample, kv_prompt[0], kv_prompt[1], kv_prompt[2],
            kv_sample[0], kv_sample[1], kv_sample[2], sg_sample)
```

```python
import math

import jax
import jax.numpy as jnp
from jax import lax
from jax.experimental import pallas as pl
from jax.experimental.pallas import tpu as pltpu

F32 = jnp.float32
BF16 = jnp.bfloat16

D_MODEL = 1024
HEAD_DIM = 64
HEADS = 4
GROUP_W = HEADS * HEAD_DIM
DIL_GROUPS = ((128, 1), (512, 4), (2048, 16))
N_GROUPS = len(DIL_GROUPS)
ATTN_W = N_GROUPS * GROUP_W
BLOCK = 128
CHUNK = 128
SG_GROUPS = 4
SG_W = 512
D_FF = 4 * D_MODEL
Z_OFF = 3 * ATTN_W
G_OFF = Z_OFF + 2 * SG_W
IN_W = G_OFF + 2 * D_MODEL
EPS = 1e-6
NEG = -1e30
SCALE = HEAD_DIM ** -0.5
INV_SQRT2 = 1.0 / math.sqrt(2.0)
LOG2E = 1.0 / math.log(2.0)
LANES = 128
SLABS = GROUP_W // LANES
HEADS_PER_SLAB = LANES // HEAD_DIM
HEAD_SHIFT = HEAD_DIM.bit_length() - 1
VMEM_LIMIT = 56 * 1024 * 1024

TOKEN_TILE = 512
MERGE_UNROLL = 4
BLOCK_UNROLL = 16
SAMPLE_DB_TILE = 2
T_PAD = 8
T_SHIFT = T_PAD.bit_length() - 1
BF16_SUBLANES = 16
CAST_COLS = 768
SAMPLE_TAIL_TILE = 256
FF_CHUNK = 1024


def _params(n_axes):
    return pltpu.CompilerParams(dimension_semantics=("arbitrary",) * n_axes,
                                vmem_limit_bytes=VMEM_LIMIT)


def _const_spec(shape):
    return pl.BlockSpec(shape, lambda *_: (0,) * len(shape), pipeline_mode=pl.Buffered(1))


def _rms(x, g):
    return x * lax.rsqrt(jnp.mean(x * x, axis=-1, keepdims=True) + EPS) * g


def _proj_gate_branches(h, w_ref, bg_ref, lng_ref, lnb_ref, z1_ref, zn_ref, ga_ref, gb_ref):
    z = jnp.dot(h, w_ref[:, Z_OFF:G_OFF], preferred_element_type=F32)
    z = 0.5 * z * (1.0 + lax.erf(z * INV_SQRT2))
    z1_ref[...] = z[:, :SG_W].astype(z1_ref.dtype)
    z2 = z[:, SG_W:]
    mu = jnp.mean(z2, axis=-1, keepdims=True)
    zc = z2 - mu
    var = jnp.mean(zc * zc, axis=-1, keepdims=True)
    zn = (zc * lax.rsqrt(var + EPS) * lng_ref[...] + lnb_ref[...]).astype(zn_ref.dtype)
    zn_ref[...] = zn.reshape(zn_ref.shape)

    gates = jnp.dot(h, w_ref[:, G_OFF:], preferred_element_type=F32) + bg_ref[...]
    gates = 1.0 / (1.0 + jnp.exp(-gates))
    ga_ref[...] = gates[:, :D_MODEL].astype(BF16)
    gb_ref[...] = gates[:, D_MODEL:].astype(BF16)


def _cast_blocks(pairs):
    for src_ref, dst_ref in pairs:
        dst_ref[...] = src_ref[...].astype(dst_ref.dtype)


def _proj_sample_kernel(x_ref, g_ref, wf_ref, bg_ref, lng_ref, lnb_ref, wao_ref, wbo_ref, wout_ref,
                        qkv_ref, z1_ref, zn_ref, ga_ref, gb_ref, w_ref, wao_bf_ref, wbo_bf_ref, wout_bf_ref,
                        wf_buf, wf_sem):
    @pl.when(pl.program_id(0) == 0)
    def _():
        chunks = [slice(c, c + CAST_COLS) for c in range(0, IN_W, CAST_COLS)]
        copies = [pltpu.make_async_copy(wf_ref.at[:, cs], wf_buf.at[:, cs], wf_sem.at[k])
                  for k, cs in enumerate(chunks)]
        for copy in copies:
            copy.start()
        for copy, cs in zip(copies, chunks):
            copy.wait()
            w_ref[:, cs] = wf_buf[:, cs].astype(BF16)

    _cast_blocks(((wao_ref, wao_bf_ref), (wbo_ref, wbo_bf_ref), (wout_ref, wout_bf_ref)))
    x = x_ref[...]
    h = _rms(x.reshape(x.shape[0] * x.shape[1], x.shape[2]), g_ref[...]).astype(BF16)
    qkv_ref[...] = jnp.dot(h, w_ref[:, :Z_OFF], preferred_element_type=F32)
    _proj_gate_branches(h, w_ref, bg_ref, lng_ref, lnb_ref, z1_ref, zn_ref, ga_ref, gb_ref)


def _proj_prompt_kernel(x_ref, g_ref, w_ref, bg_ref, lng_ref, lnb_ref, wup_ref, wdn_ref,
                        qkv_ref, kv1_ref, kv2_ref, kv3_ref, z1_ref, zn_ref, ga_ref, gb_ref, wup_bf_ref, wdn_bf_ref):
    _cast_blocks(((wup_ref, wup_bf_ref), (wdn_ref, wdn_bf_ref)))
    h = _rms(x_ref[0], g_ref[...]).astype(BF16)
    _proj_gate_branches(h, w_ref, bg_ref, lng_ref, lnb_ref, z1_ref, zn_ref, ga_ref, gb_ref)
    def project(first_part, n_parts, scale=None):
        cols = jnp.dot(h, w_ref[:, first_part * ATTN_W:(first_part + n_parts) * ATTN_W], preferred_element_type=F32)
        if scale is not None:
            cols = cols * scale
        for part in range(n_parts):
            for g in range(N_GROUPS):
                for s in range(SLABS):
                    col = part * ATTN_W + g * GROUP_W + s * LANES
                    qkv_ref[0, g, (first_part + part) * SLABS + s] = cols[:, col:col + LANES]

    def kv_t(g):
        return jnp.concatenate([qkv_ref[0, g, slab].T for slab in range(SLABS, 3 * SLABS)], axis=0)

    tm = x_ref.shape[1]
    project(1, 2)
    kv3_ref[0] = kv_t(2)
    project(0, 1, scale=SCALE * LOG2E)

    @pl.when(pl.program_id(1) == pl.num_programs(1) - 1)
    def _():
        kv2_ref[0] = kv_t(1)[:, tm - kv2_ref.shape[2]:]
        kv1_ref[0] = kv_t(0)[:, tm - kv1_ref.shape[2]:]


def _proj_weight_specs():
    return [_const_spec((1, D_MODEL)), _const_spec((D_MODEL, IN_W)),
            _const_spec((1, 2 * D_MODEL)), _const_spec((1, SG_W)), _const_spec((1, SG_W))]


def _gate_branch_shapes(n, z_dtype):
    return [jax.ShapeDtypeStruct((n, SG_W), z_dtype), jax.ShapeDtypeStruct((n, SG_W), z_dtype),
            jax.ShapeDtypeStruct((n, D_MODEL), BF16), jax.ShapeDtypeStruct((n, D_MODEL), BF16)]


def _row_slices(arrays, steps):
    specs, shapes = [], []
    for a in arrays:
        rows = a.shape[0] // steps
        assert rows * steps == a.shape[0] and rows % BF16_SUBLANES == 0
        specs.append((rows, a.shape[1]))
        shapes.append(jax.ShapeDtypeStruct(a.shape, BF16))
    return specs, shapes


def _project_sample(x3, g, w_in, bg, lng, lnb, mix_weights):
    n = x3.shape[0] * x3.shape[1]
    tm = min(SAMPLE_TAIL_TILE, n)
    steps = n // tm
    row = lambda w: pl.BlockSpec((tm, w), lambda i: (i, 0))
    slices, bf_shapes = _row_slices(mix_weights, steps)
    slice_specs = [pl.BlockSpec(s, lambda i: (i, 0)) for s in slices]
    z1_shape, _, ga_shape, gb_shape = _gate_branch_shapes(n, F32)
    weight_specs = _proj_weight_specs()
    weight_specs[1] = pl.BlockSpec(memory_space=pl.ANY)
    return pl.pallas_call(
        _proj_sample_kernel,
        grid=(steps,),
        in_specs=[pl.BlockSpec((tm // x3.shape[1],) + x3.shape[1:], lambda i: (i, 0, 0))] + weight_specs
        + slice_specs,
        out_specs=[row(Z_OFF), row(SG_W), pl.BlockSpec((tm // x3.shape[1], x3.shape[1], SG_W), lambda i: (i, 0, 0)),
                   row(D_MODEL), row(D_MODEL), pl.BlockSpec((D_MODEL, IN_W), lambda i: (0, 0))] + slice_specs,
        out_shape=[jax.ShapeDtypeStruct((n, Z_OFF), F32), z1_shape, jax.ShapeDtypeStruct(x3.shape[:2] + (SG_W,), F32),
                   ga_shape, gb_shape, jax.ShapeDtypeStruct((D_MODEL, IN_W), BF16)] + bf_shapes,
        scratch_shapes=[pltpu.VMEM((D_MODEL, IN_W), F32), pltpu.SemaphoreType.DMA((IN_W // CAST_COLS,))],
        compiler_params=_params(1),
        name="proj_sample",
    )(x3, g, w_in, bg, lng, lnb, *mix_weights)


def _project_prompt(x, g, w_bf, bg, lng, lnb, ffn_weights):
    batch, seq, _ = x.shape
    tm = TOKEN_TILE
    tiles = seq // tm
    keeps = [min(win, seq) for win, _ in DIL_GROUPS]
    assert seq % tm == 0 and keeps[0] <= tm and keeps[1] <= tm and keeps[2] == seq
    row = lambda w: pl.BlockSpec((tm, w), lambda b, t: (b * tiles + t, 0))
    last = lambda keep: pl.BlockSpec((1, 2 * GROUP_W, keep), lambda b, t: (b, 0, 0))
    n = batch * seq
    slices, bf_shapes = _row_slices(ffn_weights, batch * tiles)
    slice_specs = [pl.BlockSpec(s, lambda b, t: (b * tiles + t, 0)) for s in slices]
    return pl.pallas_call(
        _proj_prompt_kernel,
        grid=(batch, tiles),
        in_specs=[pl.BlockSpec((1, tm, D_MODEL), lambda b, t: (b, t, 0))] + _proj_weight_specs() + slice_specs,
        out_specs=[pl.BlockSpec((1, N_GROUPS, 3 * SLABS, tm, LANES), lambda b, t: (b, 0, 0, t, 0)),
                   last(keeps[0]), last(keeps[1]),
                   pl.BlockSpec((1, 2 * GROUP_W, tm), lambda b, t: (b, 0, t)),
                   row(SG_W), row(SG_W), row(D_MODEL), row(D_MODEL)] + slice_specs,
        out_shape=[jax.ShapeDtypeStruct((batch, N_GROUPS, 3 * SLABS, seq, LANES), F32),
                   jax.ShapeDtypeStruct((batch, 2 * GROUP_W, keeps[0]), F32),
                   jax.ShapeDtypeStruct((batch, 2 * GROUP_W, keeps[1]), F32),
                   jax.ShapeDtypeStruct((batch, 2 * GROUP_W, keeps[2]), F32)] + _gate_branch_shapes(n, BF16)
        + bf_shapes,
        compiler_params=_params(2),
        name="proj_prompt",
    )(x, g, w_bf, bg, lng, lnb, *ffn_weights)


def _head_lane_mask(shape, hh):
    lane = lax.broadcasted_iota(jnp.int32, shape, len(shape) - 1)
    return (lane >> HEAD_SHIFT) == hh


def _band_block(q, k, v, mask):
    nt = (((1,), (1,)), ((), ()))
    assert HEADS_PER_SLAB == 2
    lm0 = _head_lane_mask((BLOCK, LANES), 0)
    mask2 = jnp.concatenate([mask, mask], axis=0)
    outs, lses = [], []
    for s in range(SLABS):
        zero = jnp.zeros_like(q[s])
        qm = jnp.concatenate([jnp.where(lm0, q[s], zero), jnp.where(lm0, zero, q[s])], axis=0)
        sc = lax.dot_general(qm, k[s], nt, preferred_element_type=F32)
        sc = jnp.where(mask2, sc, NEG)
        m = jnp.max(sc, axis=-1, keepdims=True)
        p = jnp.exp2(sc - m)
        l = jnp.sum(p, axis=-1, keepdims=True)
        o = jnp.dot(p.astype(BF16), v[s], preferred_element_type=F32) * (1.0 / l)
        lse = m + jnp.log2(l)
        outs.append(jnp.where(lm0, o[:BLOCK], o[BLOCK:]))
        lses.append(jnp.where(lm0, lse[:BLOCK], lse[BLOCK:]))
    return outs, lses


def _group_attention(qkv_ref, o_scr, lse_scr, g, dil, seq):
    n_qb = seq // (dil * BLOCK)
    qb_shift = n_qb.bit_length() - 1
    span = dil * BLOCK
    rows = (lambda start: pl.ds(start, BLOCK, stride=dil)) if dil > 1 else (lambda start: pl.ds(start, BLOCK))
    nk = 2 * BLOCK if n_qb > 1 else BLOCK
    qi = lax.broadcasted_iota(jnp.int32, (BLOCK, nk), 0)
    ki = lax.broadcasted_iota(jnp.int32, (BLOCK, nk), 1)

    def load(slab, start):
        return qkv_ref[0, 0, slab, rows(start), :]

    def body(n, carry):
        k_prev, v_prev = carry
        r = n >> qb_shift
        qb = n & (n_qb - 1)
        cur = r + span * qb
        if dil == 1:
            cur = pl.multiple_of(cur, BLOCK)
        q = [load(s, cur).astype(BF16) for s in range(SLABS)]
        k = [load(SLABS + s, cur).astype(BF16) for s in range(SLABS)]
        v = [load(2 * SLABS + s, cur).astype(BF16) for s in range(SLABS)]
        carry = (k, v)
        if n_qb > 1:
            k = [jnp.concatenate([k_prev[s], k[s]], axis=0) for s in range(SLABS)]
            v = [jnp.concatenate([v_prev[s], v[s]], axis=0) for s in range(SLABS)]
            mask = (ki >= qi) & (ki <= qi + BLOCK) & (ki >= jnp.where(qb > 0, 0, BLOCK))
        else:
            mask = ki <= qi
        outs, lses = _band_block(q, k, v, mask)
        for s in range(SLABS):
            o_scr[g, s, rows(cur), :] = outs[s]
            lse_scr[g, s, rows(cur), :] = lses[s]
        return carry

    no_rows = [jnp.zeros((BLOCK, LANES), BF16)] * SLABS
    lax.fori_loop(0, dil * n_qb, body, (no_rows, no_rows), unroll=BLOCK_UNROLL)


def _prompt_attn_kernel(qkv_ref, o_ref, o_scr, lse_scr):
    g = pl.program_id(1)
    seq = o_ref.shape[1]
    for gi, (_, dil) in enumerate(DIL_GROUPS):
        @pl.when(g == gi)
        def _(gi=gi, dil=dil):
            _group_attention(qkv_ref, o_scr, lse_scr, gi, dil, seq)

    @pl.when(g == N_GROUPS - 1)
    def _():
        def merge(i, carry):
            rows = pl.ds(pl.multiple_of(i * BLOCK, BLOCK), BLOCK)
            for s in range(SLABS):
                ls = [lse_scr[gi, s, rows, :] for gi in range(N_GROUPS)]
                top = jnp.maximum(jnp.maximum(ls[0], ls[1]), ls[2])
                e = [jnp.exp2(x - top) for x in ls]
                num = e[0] * o_scr[0, s, rows, :] + e[1] * o_scr[1, s, rows, :] + e[2] * o_scr[2, s, rows, :]
                o_ref[0, rows, s * LANES:(s + 1) * LANES] = (num / (e[0] + e[1] + e[2])).astype(o_ref.dtype)
            return carry
        lax.fori_loop(0, seq // BLOCK, merge, 0, unroll=MERGE_UNROLL)


def _prompt_attention(qkv, seq):
    batch = qkv.shape[0]
    scratch = pltpu.VMEM((N_GROUPS, SLABS, seq, LANES), F32)
    return pl.pallas_call(
        _prompt_attn_kernel,
        grid=(batch, N_GROUPS),
        in_specs=[pl.BlockSpec((1, 1, 3 * SLABS, seq, LANES), lambda b, g: (b, g, 0, 0, 0))],
        out_specs=pl.BlockSpec((1, seq, GROUP_W), lambda b, g: (b, 0, 0)),
        out_shape=jax.ShapeDtypeStruct((batch, seq, GROUP_W), BF16),
        scratch_shapes=[scratch, scratch],
        compiler_params=_params(2),
        name="prompt_attn",
    )(qkv)


def _residue_allreduce(x, dil, op):
    shift = LANES // 2
    while shift >= dil:
        x = op(x, pltpu.roll(x, shift, axis=1))
        shift //= 2
    return x


def _lane_expand(rows8, dil, r0, t_new):
    assert r0 == 0 and t_new <= dil and (dil % T_PAD == 0 or T_PAD % dil == 0)
    if dil >= T_PAD:
        period = [rows8] + [jnp.zeros_like(rows8)] * (dil // T_PAD - 1)
    else:
        period = [jnp.concatenate([rows8[:dil]] * (T_PAD // dil), axis=0)]
    rows = jnp.concatenate(period * (LANES // max(dil, T_PAD)), axis=0)
    return rows.T


def _heads_to_rows(x):
    return jnp.broadcast_to(x[:, None, :], (HEADS, HEAD_DIM, LANES)).reshape(GROUP_W, LANES)


def _rows_to_heads_sum(x):
    return jnp.sum(x.reshape(HEADS, HEAD_DIM, LANES), axis=1)


def _buffer_group_lanes(c_ref, d, q8, kn8, vn8, dil, r0, t_new):
    width = c_ref.shape[2]
    tiles = width // LANES
    ql, kl, vl = (_lane_expand(a, dil, r0, t_new) for a in (q8, kn8, vn8))
    s = [_rows_to_heads_sum(c_ref[d, :GROUP_W, j * LANES:(j + 1) * LANES] * ql) for j in range(tiles)]
    s_new = _rows_to_heads_sum(kl * ql)
    top = s[0]
    for sj in s[1:]:
        top = jnp.maximum(top, sj)
    top = jnp.maximum(_residue_allreduce(top, dil, jnp.maximum), s_new)
    p = [jnp.exp(sj - top) for sj in s]
    p_new = jnp.exp(s_new - top)
    den = p[0]
    for pj in p[1:]:
        den = den + pj
    den = _residue_allreduce(den, dil, jnp.add) + p_new
    acc = c_ref[d, GROUP_W:, :LANES] * _heads_to_rows(p[0])
    for j in range(1, tiles):
        acc = acc + c_ref[d, GROUP_W:, j * LANES:(j + 1) * LANES] * _heads_to_rows(p[j])
    acc = (_residue_allreduce(acc, dil, jnp.add) + vl * _heads_to_rows(p_new)) * _heads_to_rows(1.0 / den)
    lse = _heads_to_rows(top + jnp.log(den))
    first = -(-r0 // dil) * dil
    return acc.T[first - r0:first - r0 + T_PAD], lse.T[first - r0:first - r0 + T_PAD]


def _buffer_group_dense(c_ref, d, q8, kn8, vn8, r0, t_new):
    rows = HEADS * T_PAD
    width = c_ref.shape[2]
    row = lax.broadcasted_iota(jnp.int32, (rows, GROUP_W), 0)
    lane = lax.broadcasted_iota(jnp.int32, (rows, GROUP_W), 1)
    head_sel = (row >> T_SHIFT) == (lane >> HEAD_SHIFT)
    t = (lax.broadcasted_iota(jnp.int32, (rows, width), 0) & (T_PAD - 1)) - r0
    i = lax.broadcasted_iota(jnp.int32, (rows, width), 1)
    tn = lax.broadcasted_iota(jnp.int32, (rows, T_PAD), 0) & (T_PAD - 1)
    j = lax.broadcasted_iota(jnp.int32, (rows, T_PAD), 1)
    nt = (((1,), (1,)), ((), ()))
    qm = jnp.where(head_sel, jnp.concatenate([q8] * HEADS, axis=0), 0.0).astype(BF16)
    kt = c_ref[d, :GROUP_W, :].astype(BF16)
    vt = c_ref[d, GROUP_W:, :].astype(BF16)
    sc = jnp.where(i >= t, jnp.dot(qm, kt, preferred_element_type=F32), NEG)
    own = (j <= tn) & (j >= r0) & (j < r0 + t_new)
    sn = jnp.where(own, lax.dot_general(qm, kn8.astype(BF16), nt, preferred_element_type=F32), NEG)
    m = jnp.maximum(jnp.max(sc, axis=-1, keepdims=True), jnp.max(sn, axis=-1, keepdims=True))
    pc = jnp.exp(sc - m)
    pn = jnp.exp(sn - m)
    l = jnp.sum(pc, axis=-1, keepdims=True) + jnp.sum(pn, axis=-1, keepdims=True)

    def finish():
        o = (lax.dot_general(pc.astype(BF16), vt, nt, preferred_element_type=F32)
             + jnp.dot(pn.astype(BF16), vn8.astype(BF16), preferred_element_type=F32))
        o = jnp.where(head_sel, o / l, 0.0)
        lse = jnp.where(head_sel, m + jnp.log(l), 0.0)
        fold = lambda x: sum(x[h * T_PAD:(h + 1) * T_PAD] for h in range(HEADS))
        return fold(o), fold(lse)
    return finish


def _sample_attn_start(qkv_ref, c1_ref, c2_ref, c3_ref, o_ref):
    n_db = c1_ref.shape[0]
    t_new = T_PAD // n_db
    assert qkv_ref.shape[0] == T_PAD and n_db * t_new == T_PAD
    pad_rows = jnp.zeros((T_PAD - t_new, qkv_ref.shape[1]), F32)
    pending = []
    for d in range(n_db):
        new_rows = jnp.concatenate([qkv_ref[d * t_new:(d + 1) * t_new, :], pad_rows], axis=0)
        groups = []
        for g, (c_ref, (_, dil)) in enumerate(zip((c1_ref, c2_ref, c3_ref), DIL_GROUPS)):
            cols = lambda part: slice(part * ATTN_W + g * GROUP_W, part * ATTN_W + (g + 1) * GROUP_W)
            q8 = new_rows[:, cols(0)] * SCALE
            kn8, vn8 = new_rows[:, cols(1)], new_rows[:, cols(2)]
            if dil == 1:
                groups.append(_buffer_group_dense(c_ref, d, q8, kn8, vn8, 0, t_new))
            else:
                result = _buffer_group_lanes(c_ref, d, q8, kn8, vn8, dil, 0, t_new)
                groups.append(lambda result=result: result)
        pending.append(groups)

    def finish():
        for d, groups in enumerate(pending):
            outs, lses = zip(*(group() for group in groups))
            top = jnp.maximum(jnp.maximum(lses[0], lses[1]), lses[2])
            w = [jnp.exp(x - top) for x in lses]
            o8 = (w[0] * outs[0] + w[1] * outs[1] + w[2] * outs[2]) / (w[0] + w[1] + w[2])
            o_ref[d * t_new:(d + 1) * t_new, :] = o8[:t_new]
    return finish


def _mix_value(o_a, o_b, ga_ref, gb_ref, x, wao_ref, wbo_ref, wout_ref, gpost_ref):
    a = jnp.dot(o_a.astype(BF16), wao_ref[...], preferred_element_type=F32)
    b = jnp.dot(o_b.astype(BF16), wbo_ref[...], preferred_element_type=F32)
    merged = ga_ref[...].astype(F32) * a + gb_ref[...].astype(F32) * b
    t = jnp.dot(merged.astype(BF16), wout_ref[...], preferred_element_type=F32)
    return x + _rms(t, gpost_ref[...])


def _ffn_value(x, gpre_ref, wup_ref, wdn_ref, gpost_ref):
    h = _rms(x, gpre_ref[...]).astype(BF16)
    f = jnp.zeros(x.shape, F32)
    for c in range(D_FF // FF_CHUNK):
        cs = slice(c * FF_CHUNK, (c + 1) * FF_CHUNK)
        u = jnp.maximum(jnp.dot(h, wup_ref[:, cs], preferred_element_type=F32), 0.0)
        f = f + jnp.dot((u * u).astype(BF16), wdn_ref[cs, :], preferred_element_type=F32)
    return x + _rms(f, gpost_ref[...])


def _prompt_spatial_gate(z1_ref, zn_ref, wsp_ref, bsp_ref):
    r = lax.broadcasted_iota(jnp.int32, (CHUNK, CHUNK), 0)
    c = lax.broadcasted_iota(jnp.int32, (CHUNK, CHUNK), 1)
    n_chunks = z1_ref.shape[0] // CHUNK
    mixed = []
    for g in range(SG_GROUPS):
        wt = jnp.where(r >= c, wsp_ref[g], 0.0).astype(BF16)
        zg = jnp.concatenate([zn_ref[ci * CHUNK:(ci + 1) * CHUNK, g * CHUNK:(g + 1) * CHUNK].astype(BF16)
                              for ci in range(n_chunks)], axis=1)
        mixed.append(jnp.dot(wt, zg, preferred_element_type=F32))
    chunks = [jnp.concatenate([m[:, ci * CHUNK:(ci + 1) * CHUNK] for m in mixed], axis=1) + bsp_ref[...]
              for ci in range(n_chunks)]
    return z1_ref[...].astype(F32) * jnp.concatenate(chunks, axis=0)


def _tail_prompt_kernel(oa_ref, z1_ref, zn_ref, ga_ref, gb_ref, x_ref, wsp_ref, bsp_ref,
                        wao_ref, wbo_ref, wout_ref, gmix_ref, gpre_ref, wup_ref, wdn_ref, gpost_ref,
                        qkv_ref, c1_ref, c2_ref, c3_ref, y_ref, o_ref):
    o_b = _prompt_spatial_gate(z1_ref, zn_ref, wsp_ref, bsp_ref)
    x1 = _mix_value(oa_ref[...], o_b, ga_ref, gb_ref, x_ref[...], wao_ref, wbo_ref, wout_ref, gmix_ref)
    finish_sample_attn = _sample_attn_start(qkv_ref, c1_ref, c2_ref, c3_ref, o_ref)
    y_ref[...] = _ffn_value(x1, gpre_ref, wup_ref, wdn_ref, gpost_ref)
    finish_sample_attn()


def _new_kv_feature_major(kn_ref, vn_ref, out_refs, scr):
    t_new, ndb = out_refs[0].shape[0], out_refs[0].shape[2]
    slabs = ATTN_W // LANES
    for part, src_ref in enumerate((kn_ref, vn_ref)):
        for slab in range(slabs):
            scr[part * slabs + slab] = src_ref[:, slab * LANES:(slab + 1) * LANES]
    for t in range(t_new):
        for part in range(2):
            for g, out_ref in enumerate(out_refs):
                for s in range(SLABS):
                    rows = scr[part * slabs + g * SLABS + s, pl.ds(t, ndb, stride=t_new), :]
                    out_ref[t, part * GROUP_W + s * LANES:part * GROUP_W + (s + 1) * LANES, :] = rows.T


def _tail_sample_kernel(oa_ref, z1_ref, zn_ref, ga_ref, gb_ref, x_ref, coef_ref, bsp_ref,
                        wao_ref, wbo_ref, wout_ref, gmix_ref, gpre_ref, wup_ref, wdn_ref, gpost_ref, kn_ref, vn_ref,
                        y_ref, kvs1_ref, kvs2_ref, kvs3_ref, x1_scr, h_scr, f_scr, kv_scr):
    c = pl.program_id(0)

    @pl.when(c == 0)
    def _():
        zn = zn_ref[...].reshape(z1_ref.shape)
        periods = lambda a: a.reshape(a.shape[0] // T_PAD, T_PAD, a.shape[1])
        mix = coef_ref[0] * periods(zn)
        for d in range(1, coef_ref.shape[0]):
            mix = mix + coef_ref[d] * periods(pltpu.roll(zn, d, axis=0))
        o_b = z1_ref[...] * (mix + bsp_ref[...]).reshape(zn.shape)
        x = x_ref[...].reshape(zn.shape[0], D_MODEL)
        x1 = _mix_value(oa_ref[...], o_b, ga_ref, gb_ref, x, wao_ref, wbo_ref, wout_ref, gmix_ref)
        x1_scr[...] = x1
        h_scr[...] = _rms(x1, gpre_ref[...]).astype(BF16)
        f_scr[...] = jnp.zeros_like(f_scr)
        _new_kv_feature_major(kn_ref, vn_ref, (kvs1_ref, kvs2_ref, kvs3_ref), kv_scr)

    u = jnp.maximum(jnp.dot(h_scr[...], wup_ref[...], preferred_element_type=F32), 0.0)
    f_scr[...] += jnp.dot((u * u).astype(BF16), wdn_ref[...], preferred_element_type=F32)

    @pl.when(c == pl.num_programs(0) - 1)
    def _():
        y_ref[...] = (x1_scr[...] + _rms(f_scr[...], gpost_ref[...])).reshape(y_ref.shape)


def _tail_weight_specs():
    return [_const_spec((GROUP_W, D_MODEL)), _const_spec((SG_W, D_MODEL)), _const_spec((D_MODEL, D_MODEL)),
            _const_spec((1, D_MODEL)), _const_spec((1, D_MODEL)), _const_spec((D_MODEL, D_FF)),
            _const_spec((D_FF, D_MODEL)), _const_spec((1, D_MODEL))]


def _tail_prompt(o_a, z1, zn, ga, gb, x, wsp, bsp, weights, qkv_s, caches):
    n, db, ns = x.shape[0], caches[0].shape[0], qkv_s.shape[0]
    tile = SAMPLE_DB_TILE
    steps = db // tile
    tm = n // steps
    assert db % tile == 0 and n % steps == 0 and tm % CHUNK == 0 and ns == steps * T_PAD
    row = lambda w: pl.BlockSpec((tm, w), lambda i: (i, 0))
    blk = lambda a: pl.BlockSpec((tile,) + a.shape[1:], lambda i: (i, 0, 0))
    new_rows = lambda w: pl.BlockSpec((T_PAD, w), lambda i: (i, 0))
    return pl.pallas_call(
        _tail_prompt_kernel,
        grid=(steps,),
        in_specs=[row(GROUP_W), row(SG_W), row(SG_W), row(D_MODEL), row(D_MODEL), row(D_MODEL),
                  _const_spec((SG_GROUPS, CHUNK, CHUNK)), _const_spec((CHUNK, SG_W))] + _tail_weight_specs()
        + [new_rows(Z_OFF)] + [blk(a) for a in caches],
        out_specs=[row(D_MODEL), new_rows(GROUP_W)],
        out_shape=[jax.ShapeDtypeStruct((n, D_MODEL), F32),
                   jax.ShapeDtypeStruct((ns, GROUP_W), F32)],
        compiler_params=_params(1),
        name="tail_prompt",
    )(o_a, z1, zn, ga, gb, x, wsp, bsp, *weights, qkv_s, *caches)


def _tail_sample(o_a, z1, zn, ga, gb, x3, coef, bsp, weights, qkv):
    n = x3.shape[0] * x3.shape[1]
    wao, wbo, wout, gmix, gpre, wup, wdn, gpost = weights
    full = lambda a: _const_spec(a.shape)
    kv_new = jax.ShapeDtypeStruct((x3.shape[1], 2 * GROUP_W, x3.shape[0]), F32)
    qkv_part = lambda part: pl.BlockSpec((n, ATTN_W), lambda c: (0, part), pipeline_mode=pl.Buffered(1))
    return pl.pallas_call(
        _tail_sample_kernel,
        grid=(D_FF // FF_CHUNK,),
        in_specs=[full(o_a), full(z1), full(zn), full(ga), full(gb), full(x3), full(coef), full(bsp),
                  full(wao), full(wbo), full(wout), full(gmix), full(gpre),
                  pl.BlockSpec((D_MODEL, FF_CHUNK), lambda c: (0, c)),
                  pl.BlockSpec((FF_CHUNK, D_MODEL), lambda c: (c, 0)), full(gpost), qkv_part(1), qkv_part(2)],
        out_specs=[pl.BlockSpec(x3.shape, lambda c: (0, 0, 0))]
        + [pl.BlockSpec(kv_new.shape, lambda c: (0, 0, 0))] * N_GROUPS,
        out_shape=[jax.ShapeDtypeStruct(x3.shape, F32)] + [kv_new] * N_GROUPS,
        scratch_shapes=[pltpu.VMEM((n, D_MODEL), F32), pltpu.VMEM((n, D_MODEL), BF16), pltpu.VMEM((n, D_MODEL), F32),
                        pltpu.VMEM((2 * ATTN_W // LANES, n, LANES), F32)],
        compiler_params=_params(1),
        name="tail_sample",
    )(o_a, z1, zn, ga, gb, x3, coef, bsp, *weights, qkv, qkv)


def _kv_from_feature_major(kv_t):
    batch, _, keep = kv_t.shape
    return kv_t.reshape(batch, 2, HEADS, HEAD_DIM, keep).transpose(0, 4, 1, 2, 3)[None]


def kernel(x_prompt, x_sample, cache_kv_w128, cache_kv_w512, cache_kv_w2048, norm_pre_mix, w_in, b_gate,
           ln_z_g, ln_z_b, w_spatial, b_spatial, w_ao, w_bo, w_out, norm_post_mix, norm_pre_ffn, w_up, w_down,
           norm_post_ffn):
    assert w_in.shape[0] == 1, "single-layer problem"
    batch, seq, _ = x_prompt.shape
    dbatch, t_new, _ = x_sample.shape
    caches_in = (cache_kv_w128[0], cache_kv_w512[0], cache_kv_w2048[0])

    wsp = w_spatial[0]
    bsp_rows = jnp.repeat(b_spatial[0].T, CHUNK, axis=1)

    ns = dbatch * t_new
    qkv_s, z1_s, zn_s, ga_s, gb_s, w_in_bf, wao, wbo, wout = _project_sample(
        x_sample, norm_pre_mix, w_in[0], b_gate, ln_z_g, ln_z_b, (w_ao[0], w_bo[0], w_out[0]))
    caches = []
    for gi, (win, dil) in enumerate(DIL_GROUPS):
        buf = caches_in[gi]
        wb = buf.shape[1]
        assert wb == win and t_new <= T_PAD, "full window buffers"
        caches.append(buf.transpose(0, 2, 3, 4, 1).reshape(dbatch, 2 * GROUP_W, wb))

    xp = x_prompt.reshape(batch * seq, D_MODEL)
    qkv, kv1_t, kv2_t, kv3_t, z1, zn, ga, gb, wup, wdn = _project_prompt(
        x_prompt, norm_pre_mix, w_in_bf, b_gate, ln_z_g, ln_z_b, (w_up[0], w_down[0]))
    weights = (wao, wbo, wout, norm_post_mix, norm_pre_ffn, wup, wdn, norm_post_ffn)
    kv_prompt = [_kv_from_feature_major(a) for a in (kv1_t, kv2_t, kv3_t)]
    o_a_p = _prompt_attention(qkv, seq).reshape(batch * seq, GROUP_W)
    assert SAMPLE_DB_TILE * t_new == T_PAD, "a tail step takes T_PAD new rows"
    y_prompt, o_a = _tail_prompt(o_a_p, z1, zn, ga, gb, xp, wsp, bsp_rows, weights, qkv_s, caches)
    y_prompt = y_prompt.reshape(batch, seq, D_MODEL)

    wt = wsp[:, :t_new, :t_new]
    t_idx = jnp.arange(t_new)
    shifted = (t_idx[None, :, None] - t_idx[:, None, None] == t_idx[None, None, :]).astype(F32)
    coef = (wt[None] * shifted[:, None]).sum(-1).transpose(0, 2, 1)
    reps = T_PAD // t_new
    coef = jnp.tile(jnp.repeat(coef, CHUNK, axis=2), (1, reps, 1))
    bsp_s = jnp.tile(bsp_rows[:t_new], (reps, 1))
    y_sample, *kv_new = _tail_sample(o_a, z1_s, zn_s, ga_s, gb_s, x_sample, coef, bsp_s, weights, qkv_s)
    kv_sample = [a.reshape(t_new, 2, HEADS, HEAD_DIM, dbatch).transpose(4, 0, 1, 2, 3)[None] for a in kv_new]
    sg_sample = zn_s[None]

    return (y_prompt, y_sample, kv_prompt[0], kv_prompt[1], kv_prompt[2],
            kv_sample[0], kv_sample[1], kv_sample[2], sg_sample)
```

```python
import math

import jax
import jax.numpy as jnp
from jax import lax
from jax.experimental import pallas as pl
from jax.experimental.pallas import tpu as pltpu

F32 = jnp.float32
BF16 = jnp.bfloat16

D_MODEL = 1024
HEAD_DIM = 64
HEADS = 4
GROUP_W = HEADS * HEAD_DIM
DIL_GROUPS = ((128, 1), (512, 4), (2048, 16))
N_GROUPS = len(DIL_GROUPS)
ATTN_W = N_GROUPS * GROUP_W
BLOCK = 128
CHUNK = 128
SG_GROUPS = 4
SG_W = 512
D_FF = 4 * D_MODEL
Z_OFF = 3 * ATTN_W
G_OFF = Z_OFF + 2 * SG_W
IN_W = G_OFF + 2 * D_MODEL
EPS = 1e-6
NEG = -1e30
SCALE = HEAD_DIM ** -0.5
INV_SQRT2 = 1.0 / math.sqrt(2.0)
LOG2E = 1.0 / math.log(2.0)
LANES = 128
SLABS = GROUP_W // LANES
HEADS_PER_SLAB = LANES // HEAD_DIM
HEAD_SHIFT = HEAD_DIM.bit_length() - 1
VMEM_LIMIT = 56 * 1024 * 1024

TOKEN_TILE = 512
MERGE_UNROLL = 4
BLOCK_UNROLL = 16
SAMPLE_DB_TILE = 2
T_PAD = 8
T_SHIFT = T_PAD.bit_length() - 1
BF16_SUBLANES = 16
CAST_COLS = 768
SAMPLE_TAIL_TILE = 256
FF_CHUNK = 1024


def _params(n_axes):
    return pltpu.CompilerParams(dimension_semantics=("arbitrary",) * n_axes,
                                vmem_limit_bytes=VMEM_LIMIT)


def _const_spec(shape):
    return pl.BlockSpec(shape, lambda *_: (0,) * len(shape), pipeline_mode=pl.Buffered(1))


def _rms(x, g):
    return x * lax.rsqrt(jnp.mean(x * x, axis=-1, keepdims=True) + EPS) * g


def _proj_gate_branches(h, w_ref, bg_ref, lng_ref, lnb_ref, z1_ref, zn_ref, ga_ref, gb_ref):
    z = jnp.dot(h, w_ref[:, Z_OFF:G_OFF], preferred_element_type=F32)
    z = 0.5 * z * (1.0 + lax.erf(z * INV_SQRT2))
    z1_ref[...] = z[:, :SG_W].astype(z1_ref.dtype)
    z2 = z[:, SG_W:]
    mu = jnp.mean(z2, axis=-1, keepdims=True)
    zc = z2 - mu
    var = jnp.mean(zc * zc, axis=-1, keepdims=True)
    zn = (zc * lax.rsqrt(var + EPS) * lng_ref[...] + lnb_ref[...]).astype(zn_ref.dtype)
    zn_ref[...] = zn.reshape(zn_ref.shape)

    gates = jnp.dot(h, w_ref[:, G_OFF:], preferred_element_type=F32) + bg_ref[...]
    gates = 1.0 / (1.0 + jnp.exp(-gates))
    ga_ref[...] = gates[:, :D_MODEL].astype(BF16)
    gb_ref[...] = gates[:, D_MODEL:].astype(BF16)


def _cast_blocks(pairs):
    for src_ref, dst_ref in pairs:
        dst_ref[...] = src_ref[...].astype(dst_ref.dtype)


def _proj_sample_kernel(x_ref, g_ref, wf_ref, bg_ref, lng_ref, lnb_ref, wao_ref, wbo_ref, wout_ref,
                        qkv_ref, z1_ref, zn_ref, ga_ref, gb_ref, w_ref, wao_bf_ref, wbo_bf_ref, wout_bf_ref,
                        wf_buf, wf_sem):
    @pl.when(pl.program_id(0) == 0)
    def _():
        chunks = [slice(c, c + CAST_COLS) for c in range(0, IN_W, CAST_COLS)]
        copies = [pltpu.make_async_copy(wf_ref.at[:, cs], wf_buf.at[:, cs], wf_sem.at[k])
                  for k, cs in enumerate(chunks)]
        for copy in copies:
            copy.start()
        for copy, cs in zip(copies, chunks):
            copy.wait()
            w_ref[:, cs] = wf_buf[:, cs].astype(BF16)

    _cast_blocks(((wao_ref, wao_bf_ref), (wbo_ref, wbo_bf_ref), (wout_ref, wout_bf_ref)))
    x = x_ref[...]
    h = _rms(x.reshape(x.shape[0] * x.shape[1], x.shape[2]), g_ref[...]).astype(BF16)
    qkv_ref[...] = jnp.dot(h, w_ref[:, :Z_OFF], preferred_element_type=F32)
    _proj_gate_branches(h, w_ref, bg_ref, lng_ref, lnb_ref, z1_ref, zn_ref, ga_ref, gb_ref)


def _proj_prompt_kernel(x_ref, g_ref, w_ref, bg_ref, lng_ref, lnb_ref, wup_ref, wdn_ref,
                        qkv_ref, kv1_ref, kv2_ref, kv3_ref, z1_ref, zn_ref, ga_ref, gb_ref, wup_bf_ref, wdn_bf_ref):
    _cast_blocks(((wup_ref, wup_bf_ref), (wdn_ref, wdn_bf_ref)))
    h = _rms(x_ref[0], g_ref[...]).astype(BF16)
    _proj_gate_branches(h, w_ref, bg_ref, lng_ref, lnb_ref, z1_ref, zn_ref, ga_ref, gb_ref)
    def project(first_part, n_parts, scale=None):
        cols = jnp.dot(h, w_ref[:, first_part * ATTN_W:(first_part + n_parts) * ATTN_W], preferred_element_type=F32)
        if scale is not None:
            cols = cols * scale
        for part in range(n_parts):
            for g in range(N_GROUPS):
                for s in range(SLABS):
                    col = part * ATTN_W + g * GROUP_W + s * LANES
                    qkv_ref[0, g, (first_part + part) * SLABS + s] = cols[:, col:col + LANES]

    def kv_t(g):
        return jnp.concatenate([qkv_ref[0, g, slab].T for slab in range(SLABS, 3 * SLABS)], axis=0)

    tm = x_ref.shape[1]
    project(1, 2)
    kv3_ref[0] = kv_t(2)
    project(0, 1, scale=SCALE * LOG2E)

    @pl.when(pl.program_id(1) == pl.num_programs(1) - 1)
    def _():
        kv2_ref[0] = kv_t(1)[:, tm - kv2_ref.shape[2]:]
        kv1_ref[0] = kv_t(0)[:, tm - kv1_ref.shape[2]:]


def _proj_weight_specs():
    return [_const_spec((1, D_MODEL)), _const_spec((D_MODEL, IN_W)),
            _const_spec((1, 2 * D_MODEL)), _const_spec((1, SG_W)), _const_spec((1, SG_W))]


def _gate_branch_shapes(n, z_dtype):
    return [jax.ShapeDtypeStruct((n, SG_W), z_dtype), jax.ShapeDtypeStruct((n, SG_W), z_dtype),
            jax.ShapeDtypeStruct((n, D_MODEL), BF16), jax.ShapeDtypeStruct((n, D_MODEL), BF16)]


def _row_slices(arrays, steps):
    specs, shapes = [], []
    for a in arrays:
        rows = a.shape[0] // steps
        assert rows * steps == a.shape[0] and rows % BF16_SUBLANES == 0
        specs.append((rows, a.shape[1]))
        shapes.append(jax.ShapeDtypeStruct(a.shape, BF16))
    return specs, shapes


def _project_sample(x3, g, w_in, bg, lng, lnb, mix_weights):
    n = x3.shape[0] * x3.shape[1]
    tm = min(SAMPLE_TAIL_TILE, n)
    steps = n // tm
    row = lambda w: pl.BlockSpec((tm, w), lambda i: (i, 0))
    slices, bf_shapes = _row_slices(mix_weights, steps)
    slice_specs = [pl.BlockSpec(s, lambda i: (i, 0)) for s in slices]
    z1_shape, _, ga_shape, gb_shape = _gate_branch_shapes(n, F32)
    weight_specs = _proj_weight_specs()
    weight_specs[1] = pl.BlockSpec(memory_space=pl.ANY)
    return pl.pallas_call(
        _proj_sample_kernel,
        grid=(steps,),
        in_specs=[pl.BlockSpec((tm // x3.shape[1],) + x3.shape[1:], lambda i: (i, 0, 0))] + weight_specs
        + slice_specs,
        out_specs=[row(Z_OFF), row(SG_W), pl.BlockSpec((tm // x3.shape[1], x3.shape[1], SG_W), lambda i: (i, 0, 0)),
                   row(D_MODEL), row(D_MODEL), pl.BlockSpec((D_MODEL, IN_W), lambda i: (0, 0))] + slice_specs,
        out_shape=[jax.ShapeDtypeStruct((n, Z_OFF), F32), z1_shape, jax.ShapeDtypeStruct(x3.shape[:2] + (SG_W,), F32),
                   ga_shape, gb_shape, jax.ShapeDtypeStruct((D_MODEL, IN_W), BF16)] + bf_shapes,
        scratch_shapes=[pltpu.VMEM((D_MODEL, IN_W), F32), pltpu.SemaphoreType.DMA((IN_W // CAST_COLS,))],
        compiler_params=_params(1),
        name="proj_sample",
    )(x3, g, w_in, bg, lng, lnb, *mix_weights)


def _project_prompt(x, g, w_bf, bg, lng, lnb, ffn_weights):
    batch, seq, _ = x.shape
    tm = TOKEN_TILE
    tiles = seq // tm
    keeps = [min(win, seq) for win, _ in DIL_GROUPS]
    assert seq % tm == 0 and keeps[0] <= tm and keeps[1] <= tm and keeps[2] == seq
    row = lambda w: pl.BlockSpec((tm, w), lambda b, t: (b * tiles + t, 0))
    last = lambda keep: pl.BlockSpec((1, 2 * GROUP_W, keep), lambda b, t: (b, 0, 0))
    n = batch * seq
    slices, bf_shapes = _row_slices(ffn_weights, batch * tiles)
    slice_specs = [pl.BlockSpec(s, lambda b, t: (b * tiles + t, 0)) for s in slices]
    return pl.pallas_call(
        _proj_prompt_kernel,
        grid=(batch, tiles),
        in_specs=[pl.BlockSpec((1, tm, D_MODEL), lambda b, t: (b, t, 0))] + _proj_weight_specs() + slice_specs,
        out_specs=[pl.BlockSpec((1, N_GROUPS, 3 * SLABS, tm, LANES), lambda b, t: (b, 0, 0, t, 0)),
                   last(keeps[0]), last(keeps[1]),
                   pl.BlockSpec((1, 2 * GROUP_W, tm), lambda b, t: (b, 0, t)),
                   row(SG_W), row(SG_W), row(D_MODEL), row(D_MODEL)] + slice_specs,
        out_shape=[jax.ShapeDtypeStruct((batch, N_GROUPS, 3 * SLABS, seq, LANES), F32),
                   jax.ShapeDtypeStruct((batch, 2 * GROUP_W, keeps[0]), F32),
                   jax.ShapeDtypeStruct((batch, 2 * GROUP_W, keeps[1]), F32),
                   jax.ShapeDtypeStruct((batch, 2 * GROUP_W, keeps[2]), F32)] + _gate_branch_shapes(n, BF16)
        + bf_shapes,
        compiler_params=_params(2),
        name="proj_prompt",
    )(x, g, w_bf, bg, lng, lnb, *ffn_weights)


def _head_lane_mask(shape, hh):
    lane = lax.broadcasted_iota(jnp.int32, shape, len(shape) - 1)
    return (lane >> HEAD_SHIFT) == hh


def _band_block(q, k, v, mask):
    nt = (((1,), (1,)), ((), ()))
    assert HEADS_PER_SLAB == 2
    lm0 = _head_lane_mask((BLOCK, LANES), 0)
    mask2 = jnp.concatenate([mask, mask], axis=0)
    outs, lses = [], []
    for s in range(SLABS):
        zero = jnp.zeros_like(q[s])
        qm = jnp.concatenate([jnp.where(lm0, q[s], zero), jnp.where(lm0, zero, q[s])], axis=0)
        sc = lax.dot_general(qm, k[s], nt, preferred_element_type=F32)
        sc = jnp.where(mask2, sc, NEG)
        m = jnp.max(sc, axis=-1, keepdims=True)
        p = jnp.exp2(sc - m)
        l = jnp.sum(p, axis=-1, keepdims=True)
        o = jnp.dot(p.astype(BF16), v[s], preferred_element_type=F32) * (1.0 / l)
        lse = m + jnp.log2(l)
        outs.append(jnp.where(lm0, o[:BLOCK], o[BLOCK:]))
        lses.append(jnp.where(lm0, lse[:BLOCK], lse[BLOCK:]))
    return outs, lses


def _group_attention(qkv_ref, o_scr, lse_scr, g, dil, seq):
    n_qb = seq // (dil * BLOCK)
    qb_shift = n_qb.bit_length() - 1
    span = dil * BLOCK
    rows = (lambda start: pl.ds(start, BLOCK, stride=dil)) if dil > 1 else (lambda start: pl.ds(start, BLOCK))
    nk = 2 * BLOCK if n_qb > 1 else BLOCK
    qi = lax.broadcasted_iota(jnp.int32, (BLOCK, nk), 0)
    ki = lax.broadcasted_iota(jnp.int32, (BLOCK, nk), 1)

    def load(slab, start):
        return qkv_ref[0, 0, slab, rows(start), :]

    def body(n, carry):
        k_prev, v_prev = carry
        r = n >> qb_shift
        qb = n & (n_qb - 1)
        cur = r + span * qb
        if dil == 1:
            cur = pl.multiple_of(cur, BLOCK)
        q = [load(s, cur).astype(BF16) for s in range(SLABS)]
        k = [load(SLABS + s, cur).astype(BF16) for s in range(SLABS)]
        v = [load(2 * SLABS + s, cur).astype(BF16) for s in range(SLABS)]
        carry = (k, v)
        if n_qb > 1:
            k = [jnp.concatenate([k_prev[s], k[s]], axis=0) for s in range(SLABS)]
            v = [jnp.concatenate([v_prev[s], v[s]], axis=0) for s in range(SLABS)]
            mask = (ki >= qi) & (ki <= qi + BLOCK) & (ki >= jnp.where(qb > 0, 0, BLOCK))
        else:
            mask = ki <= qi
        outs, lses = _band_block(q, k, v, mask)
        for s in range(SLABS):
            o_scr[g, s, rows(cur), :] = outs[s]
            lse_scr[g, s, rows(cur), :] = lses[s]
        return carry

    no_rows = [jnp.zeros((BLOCK, LANES), BF16)] * SLABS
    lax.fori_loop(0, dil * n_qb, body, (no_rows, no_rows), unroll=BLOCK_UNROLL)


def _prompt_attn_kernel(qkv_ref, o_ref, o_scr, lse_scr):
    g = pl.program_id(1)
    seq = o_ref.shape[1]
    for gi, (_, dil) in enumerate(DIL_GROUPS):
        @pl.when(g == gi)
        def _(gi=gi, dil=dil):
            _group_attention(qkv_ref, o_scr, lse_scr, gi, dil, seq)

    @pl.when(g == N_GROUPS - 1)
    def _():
        def merge(i, carry):
            rows = pl.ds(pl.multiple_of(i * BLOCK, BLOCK), BLOCK)
            for s in range(SLABS):
                ls = [lse_scr[gi, s, rows, :] for gi in range(N_GROUPS)]
                top = jnp.maximum(jnp.maximum(ls[0], ls[1]), ls[2])
                e = [jnp.exp2(x - top) for x in ls]
                num = e[0] * o_scr[0, s, rows, :] + e[1] * o_scr[1, s, rows, :] + e[2] * o_scr[2, s, rows, :]
                o_ref[0, rows, s * LANES:(s + 1) * LANES] = (num / (e[0] + e[1] + e[2])).astype(o_ref.dtype)
            return carry
        lax.fori_loop(0, seq // BLOCK, merge, 0, unroll=MERGE_UNROLL)


def _prompt_attention(qkv, seq):
    batch = qkv.shape[0]
    scratch = pltpu.VMEM((N_GROUPS, SLABS, seq, LANES), F32)
    return pl.pallas_call(
        _prompt_attn_kernel,
        grid=(batch, N_GROUPS),
        in_specs=[pl.BlockSpec((1, 1, 3 * SLABS, seq, LANES), lambda b, g: (b, g, 0, 0, 0))],
        out_specs=pl.BlockSpec((1, seq, GROUP_W), lambda b, g: (b, 0, 0)),
        out_shape=jax.ShapeDtypeStruct((batch, seq, GROUP_W), BF16),
        scratch_shapes=[scratch, scratch],
        compiler_params=_params(2),
        name="prompt_attn",
    )(qkv)


def _residue_allreduce(x, dil, op):
    shift = LANES // 2
    while shift >= dil:
        x = op(x, pltpu.roll(x, shift, axis=1))
        shift //= 2
    return x


def _lane_expand(rows8, dil, r0, t_new):
    assert r0 == 0 and t_new <= dil and (dil % T_PAD == 0 or T_PAD % dil == 0)
    if dil >= T_PAD:
        period = [rows8] + [jnp.zeros_like(rows8)] * (dil // T_PAD - 1)
    else:
        period = [jnp.concatenate([rows8[:dil]] * (T_PAD // dil), axis=0)]
    rows = jnp.concatenate(period * (LANES // max(dil, T_PAD)), axis=0)
    return rows.T


def _heads_to_rows(x):
    return jnp.broadcast_to(x[:, None, :], (HEADS, HEAD_DIM, LANES)).reshape(GROUP_W, LANES)


def _rows_to_heads_sum(x):
    return jnp.sum(x.reshape(HEADS, HEAD_DIM, LANES), axis=1)


def _buffer_group_lanes(c_ref, d, q8, kn8, vn8, dil, r0, t_new):
    width = c_ref.shape[2]
    tiles = width // LANES
    ql, kl, vl = (_lane_expand(a, dil, r0, t_new) for a in (q8, kn8, vn8))
    s = [_rows_to_heads_sum(c_ref[d, :GROUP_W, j * LANES:(j + 1) * LANES] * ql) for j in range(tiles)]
    s_new = _rows_to_heads_sum(kl * ql)
    top = s[0]
    for sj in s[1:]:
        top = jnp.maximum(top, sj)
    top = jnp.maximum(_residue_allreduce(top, dil, jnp.maximum), s_new)
    p = [jnp.exp(sj - top) for sj in s]
    p_new = jnp.exp(s_new - top)
    den = p[0]
    for pj in p[1:]:
        den = den + pj
    den = _residue_allreduce(den, dil, jnp.add) + p_new
    acc = c_ref[d, GROUP_W:, :LANES] * _heads_to_rows(p[0])
    for j in range(1, tiles):
        acc = acc + c_ref[d, GROUP_W:, j * LANES:(j + 1) * LANES] * _heads_to_rows(p[j])
    acc = (_residue_allreduce(acc, dil, jnp.add) + vl * _heads_to_rows(p_new)) * _heads_to_rows(1.0 / den)
    lse = _heads_to_rows(top + jnp.log(den))
    first = -(-r0 // dil) * dil
    return acc.T[first - r0:first - r0 + T_PAD], lse.T[first - r0:first - r0 + T_PAD]


def _buffer_group_dense(c_ref, d, q8, kn8, vn8, r0, t_new):
    rows = HEADS * T_PAD
    width = c_ref.shape[2]
    row = lax.broadcasted_iota(jnp.int32, (rows, GROUP_W), 0)
    lane = lax.broadcasted_iota(jnp.int32, (rows, GROUP_W), 1)
    head_sel = (row >> T_SHIFT) == (lane >> HEAD_SHIFT)
    t = (lax.broadcasted_iota(jnp.int32, (rows, width), 0) & (T_PAD - 1)) - r0
    i = lax.broadcasted_iota(jnp.int32, (rows, width), 1)
    tn = lax.broadcasted_iota(jnp.int32, (rows, T_PAD), 0) & (T_PAD - 1)
    j = lax.broadcasted_iota(jnp.int32, (rows, T_PAD), 1)
    nt = (((1,), (1,)), ((), ()))
    qm = jnp.where(head_sel, jnp.concatenate([q8] * HEADS, axis=0), 0.0).astype(BF16)
    kt = c_ref[d, :GROUP_W, :].astype(BF16)
    vt = c_ref[d, GROUP_W:, :].astype(BF16)
    sc = jnp.where(i >= t, jnp.dot(qm, kt, preferred_element_type=F32), NEG)
    own = (j <= tn) & (j >= r0) & (j < r0 + t_new)
    sn = jnp.where(own, lax.dot_general(qm, kn8.astype(BF16), nt, preferred_element_type=F32), NEG)
    m = jnp.maximum(jnp.max(sc, axis=-1, keepdims=True), jnp.max(sn, axis=-1, keepdims=True))
    pc = jnp.exp(sc - m)
    pn = jnp.exp(sn - m)
    l = jnp.sum(pc, axis=-1, keepdims=True) + jnp.sum(pn, axis=-1, keepdims=True)

    def finish():
        o = (lax.dot_general(pc.astype(BF16), vt, nt, preferred_element_type=F32)
             + jnp.dot(pn.astype(BF16), vn8.astype(BF16), preferred_element_type=F32))
        o = jnp.where(head_sel, o / l, 0.0)
        lse = jnp.where(head_sel, m + jnp.log(l), 0.0)
        fold = lambda x: sum(x[h * T_PAD:(h + 1) * T_PAD] for h in range(HEADS))
        return fold(o), fold(lse)
    return finish


def _sample_attn_start(qkv_ref, c1_ref, c2_ref, c3_ref, o_ref):
    n_db = c1_ref.shape[0]
    t_new = T_PAD // n_db
    assert qkv_ref.shape[0] == T_PAD and n_db * t_new == T_PAD
    pad_rows = jnp.zeros((T_PAD - t_new, qkv_ref.shape[1]), F32)
    pending = []
    for d in range(n_db):
        new_rows = jnp.concatenate([qkv_ref[d * t_new:(d + 1) * t_new, :], pad_rows], axis=0)
        groups = []
        for g, (c_ref, (_, dil)) in enumerate(zip((c1_ref, c2_ref, c3_ref), DIL_GROUPS)):
            cols = lambda part: slice(part * ATTN_W + g * GROUP_W, part * ATTN_W + (g + 1) * GROUP_W)
            q8 = new_rows[:, cols(0)] * SCALE
            kn8, vn8 = new_rows[:, cols(1)], new_rows[:, cols(2)]
            if dil == 1:
                groups.append(_buffer_group_dense(c_ref, d, q8, kn8, vn8, 0, t_new))
            else:
                result = _buffer_group_lanes(c_ref, d, q8, kn8, vn8, dil, 0, t_new)
                groups.append(lambda result=result: result)
        pending.append(groups)

    def finish():
        for d, groups in enumerate(pending):
            outs, lses = zip(*(group() for group in groups))
            top = jnp.maximum(jnp.maximum(lses[0], lses[1]), lses[2])
            w = [jnp.exp(x - top) for x in lses]
            o8 = (w[0] * outs[0] + w[1] * outs[1] + w[2] * outs[2]) / (w[0] + w[1] + w[2])
            o_ref[d * t_new:(d + 1) * t_new, :] = o8[:t_new]
    return finish


def _mix_value(o_a, o_b, ga_ref, gb_ref, x, wao_ref, wbo_ref, wout_ref, gpost_ref):
    a = jnp.dot(o_a.astype(BF16), wao_ref[...], preferred_element_type=F32)
    b = jnp.dot(o_b.astype(BF16), wbo_ref[...], preferred_element_type=F32)
    merged = ga_ref[...].astype(F32) * a + gb_ref[...].astype(F32) * b
    t = jnp.dot(merged.astype(BF16), wout_ref[...], preferred_element_type=F32)
    return x + _rms(t, gpost_ref[...])


def _ffn_value(x, gpre_ref, wup_ref, wdn_ref, gpost_ref):
    h = _rms(x, gpre_ref[...]).astype(BF16)
    f = jnp.zeros(x.shape, F32)
    for c in range(D_FF // FF_CHUNK):
        cs = slice(c * FF_CHUNK, (c + 1) * FF_CHUNK)
        u = jnp.maximum(jnp.dot(h, wup_ref[:, cs], preferred_element_type=F32), 0.0)
        f = f + jnp.dot((u * u).astype(BF16), wdn_ref[cs, :], preferred_element_type=F32)
    return x + _rms(f, gpost_ref[...])


def _prompt_spatial_gate(z1_ref, zn_ref, wsp_ref, bsp_ref):
    r = lax.broadcasted_iota(jnp.int32, (CHUNK, CHUNK), 0)
    c = lax.broadcasted_iota(jnp.int32, (CHUNK, CHUNK), 1)
    n_chunks = z1_ref.shape[0] // CHUNK
    mixed = []
    for g in range(SG_GROUPS):
        wt = jnp.where(r >= c, wsp_ref[g], 0.0).astype(BF16)
        zg = jnp.concatenate([zn_ref[ci * CHUNK:(ci + 1) * CHUNK, g * CHUNK:(g + 1) * CHUNK].astype(BF16)
                              for ci in range(n_chunks)], axis=1)
        mixed.append(jnp.dot(wt, zg, preferred_element_type=F32))
    chunks = [jnp.concatenate([m[:, ci * CHUNK:(ci + 1) * CHUNK] for m in mixed], axis=1) + bsp_ref[...]
              for ci in range(n_chunks)]
    return z1_ref[...].astype(F32) * jnp.concatenate(chunks, axis=0)


def _tail_prompt_kernel(oa_ref, z1_ref, zn_ref, ga_ref, gb_ref, x_ref, wsp_ref, bsp_ref,
                        wao_ref, wbo_ref, wout_ref, gmix_ref, gpre_ref, wup_ref, wdn_ref, gpost_ref,
                        qkv_ref, c1_ref, c2_ref, c3_ref, y_ref, o_ref):
    o_b = _prompt_spatial_gate(z1_ref, zn_ref, wsp_ref, bsp_ref)
    x1 = _mix_value(oa_ref[...], o_b, ga_ref, gb_ref, x_ref[...], wao_ref, wbo_ref, wout_ref, gmix_ref)
    finish_sample_attn = _sample_attn_start(qkv_ref, c1_ref, c2_ref, c3_ref, o_ref)
    y_ref[...] = _ffn_value(x1, gpre_ref, wup_ref, wdn_ref, gpost_ref)
    finish_sample_attn()


def _new_kv_slab_copies(qkv_ref, scr, sem):
    return [pltpu.make_async_copy(qkv_ref.at[:, ATTN_W + j * LANES:ATTN_W + (j + 1) * LANES], scr.at[j], sem.at[j])
            for j in range(scr.shape[0])]


def _new_kv_feature_major(copies, out_refs, scr):
    t_new, ndb = out_refs[0].shape[0], out_refs[0].shape[2]
    slabs = ATTN_W // LANES
    for copy in copies:
        copy.wait()
    for t in range(t_new):
        for part in range(2):
            for g, out_ref in enumerate(out_refs):
                for s in range(SLABS):
                    rows = scr[part * slabs + g * SLABS + s, pl.ds(t, ndb, stride=t_new), :]
                    out_ref[t, part * GROUP_W + s * LANES:part * GROUP_W + (s + 1) * LANES, :] = rows.T


def _tail_sample_kernel(oa_ref, z1_ref, zn_ref, ga_ref, gb_ref, x_ref, coef_ref, bsp_ref,
                        wao_ref, wbo_ref, wout_ref, gmix_ref, gpre_ref, wup_ref, wdn_ref, gpost_ref, qkv_ref,
                        y_ref, kvs1_ref, kvs2_ref, kvs3_ref, x1_scr, h_scr, f_scr, kv_scr, kv_sem):
    c = pl.program_id(0)

    @pl.when(c == 0)
    def _():
        kv_copies = _new_kv_slab_copies(qkv_ref, kv_scr, kv_sem)
        for copy in kv_copies:
            copy.start()
        zn = zn_ref[...].reshape(z1_ref.shape)
        periods = lambda a: a.reshape(a.shape[0] // T_PAD, T_PAD, a.shape[1])
        mix = coef_ref[0] * periods(zn)
        for d in range(1, coef_ref.shape[0]):
            mix = mix + coef_ref[d] * periods(pltpu.roll(zn, d, axis=0))
        o_b = z1_ref[...] * (mix + bsp_ref[...]).reshape(zn.shape)
        x = x_ref[...].reshape(zn.shape[0], D_MODEL)
        x1 = _mix_value(oa_ref[...], o_b, ga_ref, gb_ref, x, wao_ref, wbo_ref, wout_ref, gmix_ref)
        x1_scr[...] = x1
        h_scr[...] = _rms(x1, gpre_ref[...]).astype(BF16)
        f_scr[...] = jnp.zeros_like(f_scr)
        _new_kv_feature_major(kv_copies, (kvs1_ref, kvs2_ref, kvs3_ref), kv_scr)

    u = jnp.maximum(jnp.dot(h_scr[...], wup_ref[...], preferred_element_type=F32), 0.0)
    f_scr[...] += jnp.dot((u * u).astype(BF16), wdn_ref[...], preferred_element_type=F32)

    @pl.when(c == pl.num_programs(0) - 1)
    def _():
        y_ref[...] = (x1_scr[...] + _rms(f_scr[...], gpost_ref[...])).reshape(y_ref.shape)


def _tail_weight_specs():
    return [_const_spec((GROUP_W, D_MODEL)), _const_spec((SG_W, D_MODEL)), _const_spec((D_MODEL, D_MODEL)),
            _const_spec((1, D_MODEL)), _const_spec((1, D_MODEL)), _const_spec((D_MODEL, D_FF)),
            _const_spec((D_FF, D_MODEL)), _const_spec((1, D_MODEL))]


def _tail_prompt(o_a, z1, zn, ga, gb, x, wsp, bsp, weights, qkv_s, caches):
    n, db, ns = x.shape[0], caches[0].shape[0], qkv_s.shape[0]
    tile = SAMPLE_DB_TILE
    steps = db // tile
    tm = n // steps
    assert db % tile == 0 and n % steps == 0 and tm % CHUNK == 0 and ns == steps * T_PAD
    row = lambda w: pl.BlockSpec((tm, w), lambda i: (i, 0))
    blk = lambda a: pl.BlockSpec((tile,) + a.shape[1:], lambda i: (i, 0, 0))
    new_rows = lambda w: pl.BlockSpec((T_PAD, w), lambda i: (i, 0))
    return pl.pallas_call(
        _tail_prompt_kernel,
        grid=(steps,),
        in_specs=[row(GROUP_W), row(SG_W), row(SG_W), row(D_MODEL), row(D_MODEL), row(D_MODEL),
                  _const_spec((SG_GROUPS, CHUNK, CHUNK)), _const_spec((CHUNK, SG_W))] + _tail_weight_specs()
        + [new_rows(Z_OFF)] + [blk(a) for a in caches],
        out_specs=[row(D_MODEL), new_rows(GROUP_W)],
        out_shape=[jax.ShapeDtypeStruct((n, D_MODEL), F32),
                   jax.ShapeDtypeStruct((ns, GROUP_W), F32)],
        compiler_params=_params(1),
        name="tail_prompt",
    )(o_a, z1, zn, ga, gb, x, wsp, bsp, *weights, qkv_s, *caches)


def _tail_sample(o_a, z1, zn, ga, gb, x3, coef, bsp, weights, qkv):
    n = x3.shape[0] * x3.shape[1]
    wao, wbo, wout, gmix, gpre, wup, wdn, gpost = weights
    full = lambda a: _const_spec(a.shape)
    kv_new = jax.ShapeDtypeStruct((x3.shape[1], 2 * GROUP_W, x3.shape[0]), F32)
    kv_slabs = 2 * ATTN_W // LANES
    return pl.pallas_call(
        _tail_sample_kernel,
        grid=(D_FF // FF_CHUNK,),
        in_specs=[full(o_a), full(z1), full(zn), full(ga), full(gb), full(x3), full(coef), full(bsp),
                  full(wao), full(wbo), full(wout), full(gmix), full(gpre),
                  pl.BlockSpec((D_MODEL, FF_CHUNK), lambda c: (0, c)),
                  pl.BlockSpec((FF_CHUNK, D_MODEL), lambda c: (c, 0)), full(gpost),
                  pl.BlockSpec(memory_space=pl.ANY)],
        out_specs=[pl.BlockSpec(x3.shape, lambda c: (0, 0, 0))]
        + [pl.BlockSpec(kv_new.shape, lambda c: (0, 0, 0))] * N_GROUPS,
        out_shape=[jax.ShapeDtypeStruct(x3.shape, F32)] + [kv_new] * N_GROUPS,
        scratch_shapes=[pltpu.VMEM((n, D_MODEL), F32), pltpu.VMEM((n, D_MODEL), BF16), pltpu.VMEM((n, D_MODEL), F32),
                        pltpu.VMEM((kv_slabs, n, LANES), F32), pltpu.SemaphoreType.DMA((kv_slabs,))],
        compiler_params=_params(1),
        name="tail_sample",
    )(o_a, z1, zn, ga, gb, x3, coef, bsp, *weights, qkv)


def _kv_from_feature_major(kv_t):
    batch, _, keep = kv_t.shape
    return kv_t.reshape(batch, 2, HEADS, HEAD_DIM, keep).transpose(0, 4, 1, 2, 3)[None]


def kernel(x_prompt, x_sample, cache_kv_w128, cache_kv_w512, cache_kv_w2048, norm_pre_mix, w_in, b_gate,
           ln_z_g, ln_z_b, w_spatial, b_spatial, w_ao, w_bo, w_out, norm_post_mix, norm_pre_ffn, w_up, w_down,
           norm_post_ffn):
    assert w_in.shape[0] == 1, "single-layer problem"
    batch, seq, _ = x_prompt.shape
    dbatch, t_new, _ = x_sample.shape
    caches_in = (cache_kv_w128[0], cache_kv_w512[0], cache_kv_w2048[0])

    wsp = w_spatial[0]
    bsp_rows = jnp.repeat(b_spatial[0].T, CHUNK, axis=1)

    ns = dbatch * t_new
    qkv_s, z1_s, zn_s, ga_s, gb_s, w_in_bf, wao, wbo, wout = _project_sample(
        x_sample, norm_pre_mix, w_in[0], b_gate, ln_z_g, ln_z_b, (w_ao[0], w_bo[0], w_out[0]))
    caches = []
    for gi, (win, dil) in enumerate(DIL_GROUPS):
        buf = caches_in[gi]
        wb = buf.shape[1]
        assert wb == win and t_new <= T_PAD, "full window buffers"
        caches.append(buf.transpose(0, 2, 3, 4, 1).reshape(dbatch, 2 * GROUP_W, wb))

    xp = x_prompt.reshape(batch * seq, D_MODEL)
    qkv, kv1_t, kv2_t, kv3_t, z1, zn, ga, gb, wup, wdn = _project_prompt(
        x_prompt, norm_pre_mix, w_in_bf, b_gate, ln_z_g, ln_z_b, (w_up[0], w_down[0]))
    weights = (wao, wbo, wout, norm_post_mix, norm_pre_ffn, wup, wdn, norm_post_ffn)
    kv_prompt = [_kv_from_feature_major(a) for a in (kv1_t, kv2_t, kv3_t)]
    o_a_p = _prompt_attention(qkv, seq).reshape(batch * seq, GROUP_W)
    assert SAMPLE_DB_TILE * t_new == T_PAD, "a tail step takes T_PAD new rows"
    y_prompt, o_a = _tail_prompt(o_a_p, z1, zn, ga, gb, xp, wsp, bsp_rows, weights, qkv_s, caches)
    y_prompt = y_prompt.reshape(batch, seq, D_MODEL)

    wt = wsp[:, :t_new, :t_new]
    t_idx = jnp.arange(t_new)
    shifted = (t_idx[None, :, None] - t_idx[:, None, None] == t_idx[None, None, :]).astype(F32)
    coef = (wt[None] * shifted[:, None]).sum(-1).transpose(0, 2, 1)
    reps = T_PAD // t_new
    coef = jnp.tile(jnp.repeat(coef, CHUNK, axis=2), (1, reps, 1))
    bsp_s = jnp.tile(bsp_rows[:t_new], (reps, 1))
    y_sample, *kv_new = _tail_sample(o_a, z1_s, zn_s, ga_s, gb_s, x_sample, coef, bsp_s, weights, qkv_s)
    kv_sample = [a.reshape(t_new, 2, HEADS, HEAD_DIM, dbatch).transpose(4, 0, 1, 2, 3)[None] for a in kv_new]
    sg_sample = zn_s[None]

    return (y_prompt, y_sample, kv_prompt[0], kv_prompt[1], kv_prompt[2],
            kv_sample[0], kv_sample[1], kv_sample[2], sg_sample)
```
